```python
import jax, jax.numpy as jnp
from jax import lax
import numpy as np

D_MODEL = 2048
BATCH = 8
SEQ = 8192
DEPTH = 4

CHUNK = 64
N_MIXERS = 2
HEAD_DIM = 128
N_HEADS = D_MODEL // HEAD_DIM
Q_BLOCK = 128
POOL_WINDOWS = (2, 4, 8, 16)
N_POOL_GROUPS = len(POOL_WINDOWS)
POOL_GROUP = D_MODEL // N_POOL_GROUPS
D_FF = -(-8 * D_MODEL // (3 * 256)) * 256
N_FOX_LAYERS = (DEPTH + N_MIXERS - 1) // N_MIXERS
N_POOL_LAYERS = DEPTH // N_MIXERS
RMS_EPS = 1e-6
NEG_INF = -1e30

kernel_name = "fox_pool_hybrid_encoder"


def rmsnorm(x, g):
    xf = x.astype(jnp.float32)
    y = xf * lax.rsqrt(jnp.mean(xf * xf, axis=-1, keepdims=True) + RMS_EPS)
    return (y * g.astype(jnp.float32)).astype(x.dtype)


def forgetting_attention(h, w_in, b_f, g_q, g_k, w_out):
    B, S, D = h.shape
    proj = jnp.einsum('bsd,de->bse', h, w_in)
    q = proj[..., :D].reshape(B, S, N_HEADS, HEAD_DIM)
    k = proj[..., D:2 * D].reshape(B, S, N_HEADS, HEAD_DIM)
    v = proj[..., 2 * D:3 * D].reshape(B, S, N_HEADS, HEAD_DIM)
    f_logit = proj[..., 3 * D:]
    q = rmsnorm(q, g_q)
    k = rmsnorm(k, g_k)
    log_f = jax.nn.log_sigmoid((f_logit + b_f).astype(jnp.float32))
    c = jnp.cumsum(log_f, axis=1).transpose(0, 2, 1)
    nb = S // Q_BLOCK
    q_blocks = q.reshape(B, nb, Q_BLOCK, N_HEADS, HEAD_DIM).transpose(1, 0, 2, 3, 4)
    c_blocks = c.reshape(B, N_HEADS, nb, Q_BLOCK).transpose(2, 0, 1, 3)
    starts = jnp.arange(nb) * Q_BLOCK
    k_pos = jnp.arange(S)
    scale = HEAD_DIM ** -0.5

    def block(args):
        qb, cb, start = args
        s = jnp.einsum('bqhd,bkhd->bhqk', qb, k).astype(jnp.float32) * scale
        s = s + (cb[..., :, None] - c[..., None, :])
        q_pos = start + jnp.arange(Q_BLOCK)
        s = jnp.where(q_pos[:, None] >= k_pos[None, :], s, NEG_INF)
        p = jax.nn.softmax(s, axis=-1).astype(v.dtype)
        return jnp.einsum('bhqk,bkhd->bqhd', p, v)

    o = lax.map(block, (q_blocks, c_blocks, starts))
    o = o.transpose(1, 0, 2, 3, 4).reshape(B, S, D)
    return jnp.einsum('bsd,de->bse', o, w_out)


def multiscale_pool(h, w, b, scale):
    B, S, D = h.shape
    hf = h.astype(jnp.float32)
    cs = jnp.concatenate([jnp.zeros((B, 1, D), jnp.float32), jnp.cumsum(hf, axis=1)], axis=1)
    t = jnp.arange(S)
    means = []
    for g, win in enumerate(POOL_WINDOWS):
        csg = cs[..., g * POOL_GROUP:(g + 1) * POOL_GROUP]
        lo = jnp.maximum(t + 1 - win, 0)
        cnt = (t + 1 - lo).astype(jnp.float32)
        means.append((csg[:, t + 1] - csg[:, lo]) / cnt[None, :, None])
    y = (jnp.concatenate(means, axis=-1) - hf).astype(h.dtype)
    y = jnp.einsum('bsgc,gce->bsge', y.reshape(B, S, N_POOL_GROUPS, POOL_GROUP), w).reshape(B, S, D)
    return (y + b) * scale


def swiglu(h, w_gu, w_down):
    gu = jnp.einsum('bsd,df->bsf', h, w_gu)
    gate, up = gu[..., :D_FF], gu[..., D_FF:]
    return jnp.einsum('bsf,fd->bsd', jax.nn.silu(gate) * up, w_down)


def _fwd_setup_inputs(seed: int = 0) -> dict:
    key = jax.random.key(seed)
    ks = jax.random.split(key, 13)
    f32 = jnp.float32
    D = D_MODEL
    x = jax.random.normal(ks[0], (BATCH, SEQ, D), f32)
    mix_norm_g = 1.0 + 0.02 * jax.random.normal(ks[1], (DEPTH, D), f32)
    ffn_norm_g = 1.0 + 0.02 * jax.random.normal(ks[2], (DEPTH, D), f32)
    fox_w_in = jax.random.normal(ks[3], (N_FOX_LAYERS, D, 3 * D + N_HEADS), f32) * D ** -0.5
    fox_b_f = 2.0 + 0.5 * jax.random.normal(ks[4], (N_FOX_LAYERS, N_HEADS), f32)
    fox_q_norm_g = 1.0 + 0.02 * jax.random.normal(ks[5], (N_FOX_LAYERS, HEAD_DIM), f32)
    fox_k_norm_g = 1.0 + 0.02 * jax.random.normal(ks[6], (N_FOX_LAYERS, HEAD_DIM), f32)
    fox_w_out = jax.random.normal(ks[7], (N_FOX_LAYERS, D, D), f32) * D ** -0.5
    pool_w = jax.random.normal(ks[8], (N_POOL_LAYERS, N_POOL_GROUPS, POOL_GROUP, POOL_GROUP), f32) * POOL_GROUP ** -0.5
    pool_b = 0.01 * jax.random.normal(ks[9], (N_POOL_LAYERS, D), f32)
    pool_scale = 1.0 + 0.02 * jax.random.normal(ks[10], (N_POOL_LAYERS, D), f32)
    ffn_w_gate_up = jax.random.normal(ks[11], (DEPTH, D, 2 * D_FF), f32) * D ** -0.5
    ffn_w_down = jax.random.normal(ks[12], (DEPTH, D_FF, D), f32) * D_FF ** -0.5
    return {"x": x, "mix_norm_g": mix_norm_g, "ffn_norm_g": ffn_norm_g,
            "fox_w_in": fox_w_in, "fox_b_f": fox_b_f, "fox_q_norm_g": fox_q_norm_g,
            "fox_k_norm_g": fox_k_norm_g, "fox_w_out": fox_w_out,
            "pool_w": pool_w, "pool_b": pool_b, "pool_scale": pool_scale,
            "ffn_w_gate_up": ffn_w_gate_up, "ffn_w_down": ffn_w_down}


def _fwd_reference(x, mix_norm_g, ffn_norm_g, fox_w_in, fox_b_f, fox_q_norm_g, fox_k_norm_g,
              fox_w_out, pool_w, pool_b, pool_scale, ffn_w_gate_up, ffn_w_down):
    for i in range(DEPTH):
        j = i // N_MIXERS
        h = rmsnorm(x, mix_norm_g[i])
        if i % N_MIXERS == 0:
            x = x + forgetting_attention(h, fox_w_in[j], fox_b_f[j], fox_q_norm_g[j],
                                         fox_k_norm_g[j], fox_w_out[j])
        else:
            x = x + multiscale_pool(h, pool_w[j], pool_b[j], pool_scale[j])
        h = rmsnorm(x, ffn_norm_g[i])
        x = x + swiglu(h, ffn_w_gate_up[i], ffn_w_down[i])
    return x


import jax as _jax
import jax.numpy as _jnp

TWIN_FORMAT = 'train_step'
FWD_PARAMS = ['x', 'mix_norm_g', 'ffn_norm_g', 'fox_w_in', 'fox_b_f', 'fox_q_norm_g', 'fox_k_norm_g', 'fox_w_out', 'pool_w', 'pool_b', 'pool_scale', 'ffn_w_gate_up', 'ffn_w_down']
TWIN_WEIGHTS = ['mix_norm_g', 'ffn_norm_g', 'fox_w_in', 'fox_b_f', 'fox_q_norm_g', 'fox_k_norm_g', 'fox_w_out', 'pool_w', 'pool_b', 'pool_scale', 'ffn_w_gate_up', 'ffn_w_down']
TWIN_DIFF_INPUT = 'x'
TWIN_INPUTS = ['x', 'mix_norm_g', 'ffn_norm_g', 'fox_w_in', 'fox_b_f', 'fox_q_norm_g', 'fox_k_norm_g', 'fox_w_out', 'pool_w', 'pool_b', 'pool_scale', 'ffn_w_gate_up', 'ffn_w_down', 'loss_target', 'm_mix_norm_g', 'm_ffn_norm_g', 'm_fox_w_in', 'm_fox_b_f', 'm_fox_q_norm_g', 'm_fox_k_norm_g', 'm_fox_w_out', 'm_pool_w', 'm_pool_b', 'm_pool_scale', 'm_ffn_w_gate_up', 'm_ffn_w_down', 'v_mix_norm_g', 'v_ffn_norm_g', 'v_fox_w_in', 'v_fox_b_f', 'v_fox_q_norm_g', 'v_fox_k_norm_g', 'v_fox_w_out', 'v_pool_w', 'v_pool_b', 'v_pool_scale', 'v_ffn_w_gate_up', 'v_ffn_w_down']
TWIN_OUTPUTS = ['loss', 'grad_x', 'grad_mix_norm_g', 'grad_ffn_norm_g', 'grad_fox_w_in', 'grad_fox_b_f', 'grad_fox_q_norm_g', 'grad_fox_k_norm_g', 'grad_fox_w_out', 'grad_pool_w', 'grad_pool_b', 'grad_pool_scale', 'grad_ffn_w_gate_up', 'grad_ffn_w_down', 'delta_mix_norm_g', 'delta_ffn_norm_g', 'delta_fox_w_in', 'delta_fox_b_f', 'delta_fox_q_norm_g', 'delta_fox_k_norm_g', 'delta_fox_w_out', 'delta_pool_w', 'delta_pool_b', 'delta_pool_scale', 'delta_ffn_w_gate_up', 'delta_ffn_w_down', 'new_m_mix_norm_g', 'new_m_ffn_norm_g', 'new_m_fox_w_in', 'new_m_fox_b_f', 'new_m_fox_q_norm_g', 'new_m_fox_k_norm_g', 'new_m_fox_w_out', 'new_m_pool_w', 'new_m_pool_b', 'new_m_pool_scale', 'new_m_ffn_w_gate_up', 'new_m_ffn_w_down', 'new_v_mix_norm_g', 'new_v_ffn_norm_g', 'new_v_fox_w_in', 'new_v_fox_b_f', 'new_v_fox_q_norm_g', 'new_v_fox_k_norm_g', 'new_v_fox_w_out', 'new_v_pool_w', 'new_v_pool_b', 'new_v_pool_scale', 'new_v_ffn_w_gate_up', 'new_v_ffn_w_down']
TWIN_LEAF_KINDS = {'loss': 'loss', 'grad_x': 'grad_x', 'grad_mix_norm_g': 'grad_w', 'grad_ffn_norm_g': 'grad_w', 'grad_fox_w_in': 'grad_w', 'grad_fox_b_f': 'grad_w', 'grad_fox_q_norm_g': 'grad_w', 'grad_fox_k_norm_g': 'grad_w', 'grad_fox_w_out': 'grad_w', 'grad_pool_w': 'grad_w', 'grad_pool_b': 'grad_w', 'grad_pool_scale': 'grad_w', 'grad_ffn_w_gate_up': 'grad_w', 'grad_ffn_w_down': 'grad_w', 'delta_mix_norm_g': 'delta_w', 'delta_ffn_norm_g': 'delta_w', 'delta_fox_w_in': 'delta_w', 'delta_fox_b_f': 'delta_w', 'delta_fox_q_norm_g': 'delta_w', 'delta_fox_k_norm_g': 'delta_w', 'delta_fox_w_out': 'delta_w', 'delta_pool_w': 'delta_w', 'delta_pool_b': 'delta_w', 'delta_pool_scale': 'delta_w', 'delta_ffn_w_gate_up': 'delta_w', 'delta_ffn_w_down': 'delta_w', 'new_m_mix_norm_g': 'new_m', 'new_m_ffn_norm_g': 'new_m', 'new_m_fox_w_in': 'new_m', 'new_m_fox_b_f': 'new_m', 'new_m_fox_q_norm_g': 'new_m', 'new_m_fox_k_norm_g': 'new_m', 'new_m_fox_w_out': 'new_m', 'new_m_pool_w': 'new_m', 'new_m_pool_b': 'new_m', 'new_m_pool_scale': 'new_m', 'new_m_ffn_w_gate_up': 'new_m', 'new_m_ffn_w_down': 'new_m', 'new_v_mix_norm_g': 'new_v', 'new_v_ffn_norm_g': 'new_v', 'new_v_fox_w_in': 'new_v', 'new_v_fox_b_f': 'new_v', 'new_v_fox_q_norm_g': 'new_v', 'new_v_fox_k_norm_g': 'new_v', 'new_v_fox_w_out': 'new_v', 'new_v_pool_w': 'new_v', 'new_v_pool_b': 'new_v', 'new_v_pool_scale': 'new_v', 'new_v_ffn_w_gate_up': 'new_v', 'new_v_ffn_w_down': 'new_v'}


def _forward(args):
    return _fwd_reference(*[args[k] for k in FWD_PARAMS])


def _output_shape():
    def fwd():
        inp = _fwd_setup_inputs(0)
        return _fwd_reference(*[inp[k] for k in FWD_PARAMS])
    out = _jax.eval_shape(fwd)
    return out.shape, out.dtype

N_MICROBATCH = 1
ADAM_LR = 0.001
ADAM_B1 = 0.9
ADAM_B2 = 0.999
ADAM_EPS = 1e-08
ADAM_WD = 0.01
ADAM_STEP = 10
PER_EXAMPLE_BATCH_AXIS = {'x': 0, 'loss_target': 0}
SHARED_INPUTS = []
_WEIGHT_DTYPES = {'mix_norm_g': _jnp.float32, 'ffn_norm_g': _jnp.float32, 'fox_w_in': _jnp.float32, 'fox_b_f': _jnp.float32, 'fox_q_norm_g': _jnp.float32, 'fox_k_norm_g': _jnp.float32, 'fox_w_out': _jnp.float32, 'pool_w': _jnp.float32, 'pool_b': _jnp.float32, 'pool_scale': _jnp.float32, 'ffn_w_gate_up': _jnp.float32, 'ffn_w_down': _jnp.float32}
MOMENT_SCALE = {'mix_norm_g': 1.861248e+01, 'ffn_norm_g': 2.465553e+01, 'fox_w_in': 2.925589e-01, 'fox_b_f': 2.019361e+02, 'fox_q_norm_g': 3.163139e+01, 'fox_k_norm_g': 3.159122e+01, 'fox_w_out': 3.134622e-01, 'pool_w': 2.047683e+00, 'pool_b': 9.621562e+00, 'pool_scale': 2.554501e+01, 'ffn_w_gate_up': 2.048787e-01, 'ffn_w_down': 3.613071e-01}


def _to_microbatches(a, axis):
    t = _jnp.moveaxis(a, axis, 0)
    t = t.reshape((N_MICROBATCH, t.shape[0] // N_MICROBATCH) + t.shape[1:])
    return _jnp.moveaxis(t, 1, axis + 1)


def setup_inputs(seed: int = 0) -> dict:
    inp = _fwd_setup_inputs(seed)
    key = _jax.random.fold_in(_jax.random.key(seed), 7919)
    shape, _ = _output_shape()
    out = dict(inp)
    out["loss_target"] = _jax.random.normal(_jax.random.fold_in(key, 0), shape, _jnp.float32)
    for i, name in enumerate(TWIN_WEIGHTS):
        w = inp[name].astype(_jnp.float32)
        if MOMENT_SCALE is None:
            s = _jnp.sqrt(_jnp.mean(_jnp.square(w)) + 1e-30)
        else:
            s = MOMENT_SCALE[name]
        km, kv = _jax.random.split(_jax.random.fold_in(key, i + 1))
        out[name] = w
        out["m_" + name] = s * _jax.random.normal(km, w.shape, _jnp.float32)
        out["v_" + name] = (s * s) * _jax.random.uniform(kv, w.shape, _jnp.float32, 0.5, 1.5)
    if N_MICROBATCH > 1:
        for name, axis in PER_EXAMPLE_BATCH_AXIS.items():
            out[name] = _to_microbatches(out[name], axis)
    return {'x': out['x'], 'mix_norm_g': out['mix_norm_g'], 'ffn_norm_g': out['ffn_norm_g'], 'fox_w_in': out['fox_w_in'], 'fox_b_f': out['fox_b_f'], 'fox_q_norm_g': out['fox_q_norm_g'], 'fox_k_norm_g': out['fox_k_norm_g'], 'fox_w_out': out['fox_w_out'], 'pool_w': out['pool_w'], 'pool_b': out['pool_b'], 'pool_scale': out['pool_scale'], 'ffn_w_gate_up': out['ffn_w_gate_up'], 'ffn_w_down': out['ffn_w_down'], 'loss_target': out['loss_target'], 'm_mix_norm_g': out['m_mix_norm_g'], 'm_ffn_norm_g': out['m_ffn_norm_g'], 'm_fox_w_in': out['m_fox_w_in'], 'm_fox_b_f': out['m_fox_b_f'], 'm_fox_q_norm_g': out['m_fox_q_norm_g'], 'm_fox_k_norm_g': out['m_fox_k_norm_g'], 'm_fox_w_out': out['m_fox_w_out'], 'm_pool_w': out['m_pool_w'], 'm_pool_b': out['m_pool_b'], 'm_pool_scale': out['m_pool_scale'], 'm_ffn_w_gate_up': out['m_ffn_w_gate_up'], 'm_ffn_w_down': out['m_ffn_w_down'], 'v_mix_norm_g': out['v_mix_norm_g'], 'v_ffn_norm_g': out['v_ffn_norm_g'], 'v_fox_w_in': out['v_fox_w_in'], 'v_fox_b_f': out['v_fox_b_f'], 'v_fox_q_norm_g': out['v_fox_q_norm_g'], 'v_fox_k_norm_g': out['v_fox_k_norm_g'], 'v_fox_w_out': out['v_fox_w_out'], 'v_pool_w': out['v_pool_w'], 'v_pool_b': out['v_pool_b'], 'v_pool_scale': out['v_pool_scale'], 'v_ffn_w_gate_up': out['v_ffn_w_gate_up'], 'v_ffn_w_down': out['v_ffn_w_down']}


def _loss(weights, diff, rest, loss_target):
    with _jax.named_scope("forward"):
        args = {**rest, TWIN_DIFF_INPUT: diff, **{k: w.astype(_WEIGHT_DTYPES[k]) for k, w in weights.items()}}
        y = _forward(args)
    with _jax.named_scope("loss_head"):
        err = _jnp.square(y.astype(_jnp.float32) - loss_target)
        return 0.5 * _jnp.sum(_jnp.mean(err, axis=-1)) if err.ndim else 0.5 * err


def _adamw(w, g, m, v):
    m = ADAM_B1 * m + (1.0 - ADAM_B1) * g
    v = ADAM_B2 * v + (1.0 - ADAM_B2) * _jnp.square(g)
    m_hat = m / (1.0 - ADAM_B1 ** ADAM_STEP)
    v_hat = v / (1.0 - ADAM_B2 ** ADAM_STEP)
    delta = -ADAM_LR * (m_hat / (_jnp.sqrt(v_hat) + ADAM_EPS) + ADAM_WD * w)
    return delta, m, v


def reference(x, mix_norm_g, ffn_norm_g, fox_w_in, fox_b_f, fox_q_norm_g, fox_k_norm_g, fox_w_out, pool_w, pool_b, pool_scale, ffn_w_gate_up, ffn_w_down, loss_target, m_mix_norm_g, m_ffn_norm_g, m_fox_w_in, m_fox_b_f, m_fox_q_norm_g, m_fox_k_norm_g, m_fox_w_out, m_pool_w, m_pool_b, m_pool_scale, m_ffn_w_gate_up, m_ffn_w_down, v_mix_norm_g, v_ffn_norm_g, v_fox_w_in, v_fox_b_f, v_fox_q_norm_g, v_fox_k_norm_g, v_fox_w_out, v_pool_w, v_pool_b, v_pool_scale, v_ffn_w_gate_up, v_ffn_w_down):
    given = dict(x=x, mix_norm_g=mix_norm_g, ffn_norm_g=ffn_norm_g, fox_w_in=fox_w_in, fox_b_f=fox_b_f, fox_q_norm_g=fox_q_norm_g, fox_k_norm_g=fox_k_norm_g, fox_w_out=fox_w_out, pool_w=pool_w, pool_b=pool_b, pool_scale=pool_scale, ffn_w_gate_up=ffn_w_gate_up, ffn_w_down=ffn_w_down, loss_target=loss_target, m_mix_norm_g=m_mix_norm_g, m_ffn_norm_g=m_ffn_norm_g, m_fox_w_in=m_fox_w_in, m_fox_b_f=m_fox_b_f, m_fox_q_norm_g=m_fox_q_norm_g, m_fox_k_norm_g=m_fox_k_norm_g, m_fox_w_out=m_fox_w_out, m_pool_w=m_pool_w, m_pool_b=m_pool_b, m_pool_scale=m_pool_scale, m_ffn_w_gate_up=m_ffn_w_gate_up, m_ffn_w_down=m_ffn_w_down, v_mix_norm_g=v_mix_norm_g, v_ffn_norm_g=v_ffn_norm_g, v_fox_w_in=v_fox_w_in, v_fox_b_f=v_fox_b_f, v_fox_q_norm_g=v_fox_q_norm_g, v_fox_k_norm_g=v_fox_k_norm_g, v_fox_w_out=v_fox_w_out, v_pool_w=v_pool_w, v_pool_b=v_pool_b, v_pool_scale=v_pool_scale, v_ffn_w_gate_up=v_ffn_w_gate_up, v_ffn_w_down=v_ffn_w_down)
    weights = {n: given[n] for n in TWIN_WEIGHTS}
    shared = {n: given[n] for n in SHARED_INPUTS}
    per_example = {n: given[n] for n in ['x']}
    grad_fn = _jax.value_and_grad(_loss, argnums=(0, 1))

    def one_microbatch(ex, loss_target):
        ex = dict(ex)
        diff = ex.pop(TWIN_DIFF_INPUT)
        return grad_fn(weights, diff, {**shared, **ex}, loss_target)

    if N_MICROBATCH == 1:
        loss, (grad_w, grad_x) = one_microbatch(per_example, given["loss_target"])
    else:
        def body(carry, xs):
            loss_sum, grad_sum = carry
            l_k, (gw_k, gx_k) = one_microbatch(xs[0], xs[1])
            with _jax.named_scope("update"):
                return (loss_sum + l_k, _jax.tree.map(_jnp.add, grad_sum, gw_k)), gx_k

        init = (_jnp.zeros((), _jnp.float32), _jax.tree.map(_jnp.zeros_like, weights))
        (loss, grad_w), grad_x = _jax.lax.scan(body, init, (per_example, given["loss_target"]))
    with _jax.named_scope("update"):
        delta_w, new_m, new_v = {}, {}, {}
        for n in TWIN_WEIGHTS:
            delta_w[n], new_m[n], new_v[n] = _adamw(weights[n], grad_w[n], given["m_" + n], given["v_" + n])
    return (loss, grad_x, *[grad_w[n] for n in TWIN_WEIGHTS], *[delta_w[n] for n in TWIN_WEIGHTS],
            *[new_m[n] for n in TWIN_WEIGHTS], *[new_v[n] for n in TWIN_WEIGHTS])
```

```python
import functools

import jax
import jax.numpy as jnp
from jax import lax
from jax.experimental import pallas as pl
from jax.experimental.pallas import tpu as pltpu

F32 = jnp.float32
BF16 = jnp.bfloat16
MESH = pl.DeviceIdType.MESH

N_DEV = 8
HEAD_DIM = 128
LANES = 128
POOL_WINDOWS = (2, 4, 8, 16)
POOL_HALO = 16
RMS_EPS = 1e-6
NEG_INF = -1e30
ADAM_LR = 0.001
ADAM_B1 = 0.9
ADAM_B2 = 0.999
ADAM_EPS = 1e-08
ADAM_WD = 0.01
ADAM_STEP = 10
VMEM_LIMIT = 52 * 1024 * 1024

NN = (((1,), (0,)), ((), ()))
NT = (((1,), (1,)), ((), ()))
TN = (((0,), (0,)), ((), ()))


def _tile(n, pref, align):
    best = None
    d = align
    while d <= min(n, pref):
        if n % d == 0:
            best = d
        d += align
    return n if best is None else best


def _params(*sem):
    return pltpu.CompilerParams(dimension_semantics=sem, vmem_limit_bytes=VMEM_LIMIT)


def _position():
    x, y, c = lax.axis_index("x"), lax.axis_index("y"), lax.axis_index("c")
    return x, y, c, 4 * x + 2 * y + c


def _peer(x, y, c, k):
    px = 1 - x if k & 4 else x
    py = 1 - y if k & 2 else y
    pc = 1 - c if k & 1 else c
    return (px, py, pc), 4 * px + 2 * py + pc


def _exchange(name, ins, out_shapes, copies):
    n_in, n_cp = len(ins), len(copies)

    def body(*refs):
        in_refs = refs[:n_in]
        out_refs = refs[n_in:n_in + len(out_shapes)]
        send_sems, recv_sems, loc_sems = refs[n_in + len(out_shapes):]
        x, y, c, me = _position()
        local = []
        for ci, (ii, src_of, oi, dst_of) in enumerate(copies):
            cp = pltpu.make_async_copy(src_of(in_refs[ii], me), dst_of(out_refs[oi], me), loc_sems.at[ci])
            cp.start()
            local.append(cp)
        sends, recvs = [], []
        for k in range(1, N_DEV):
            pid, p = _peer(x, y, c, k)
            for ci, (ii, src_of, oi, dst_of) in enumerate(copies):
                sem = ci * (N_DEV - 1) + k - 1
                send = pltpu.make_async_remote_copy(
                    src_ref=src_of(in_refs[ii], p), dst_ref=dst_of(out_refs[oi], me),
                    send_sem=send_sems.at[sem], recv_sem=recv_sems.at[sem],
                    device_id=pid, device_id_type=MESH)
                send.start()
                sends.append(send)
                recvs.append(pltpu.make_async_remote_copy(
                    src_ref=src_of(in_refs[ii], p), dst_ref=dst_of(out_refs[oi], p),
                    send_sem=send_sems.at[sem], recv_sem=recv_sems.at[sem],
                    device_id=pid, device_id_type=MESH))
        for r in recvs:
            r.wait_recv()
        for s in sends:
            s.wait_send()
        for cp in local:
            cp.wait()

    any_spec = pl.BlockSpec(memory_space=pl.ANY)
    return pl.pallas_call(
        body, name=name,
        out_shape=tuple(out_shapes),
        in_specs=[any_spec] * n_in,
        out_specs=tuple([any_spec] * len(out_shapes)),
        scratch_shapes=[pltpu.SemaphoreType.DMA((n_cp * (N_DEV - 1),)),
                        pltpu.SemaphoreType.DMA((n_cp * (N_DEV - 1),)),
                        pltpu.SemaphoreType.DMA((n_cp,))],
    )(*ins)


def _all_gather_layers(name, stacked):
    ins, outs, copies = [], [], []
    for t in stacked:
        ii = len(ins)
        ins.append(t)
        for l in range(t.shape[0]):
            oi = len(outs)
            outs.append(jax.ShapeDtypeStruct((N_DEV,) + t.shape[1:], t.dtype))
            copies.append((ii, (lambda ref, p, l=l: ref.at[l]), oi, (lambda ref, s: ref.at[s])))
    res = _exchange(name, ins, outs, copies)
    out, pos = [], 0
    for t in stacked:
        out.append(list(res[pos:pos + t.shape[0]]))
        pos += t.shape[0]
    return out


def _all_to_all_layers(name, blocked):
    ins, outs, copies = [], [], []
    for layers in blocked:
        oi = len(outs)
        t0 = layers[0]
        outs.append(jax.ShapeDtypeStruct((N_DEV, len(layers)) + t0.shape[1:], t0.dtype))
        for l, t in enumerate(layers):
            ii = len(ins)
            ins.append(t)
            copies.append((ii, (lambda ref, p: ref.at[p]), oi, (lambda ref, s, l=l: ref.at[s, l])))
    return list(_exchange(name, ins, outs, copies))


def _mm(name, mode, a, b, out_shape, *, grid, a_spec, b_spec, o_spec, acc_shape, add=None, add_spec=None):
    nk = grid[2]
    dn = {"nn": NN, "nt": NT, "tn": TN}[mode]
    has_add = add is not None

    def body(*refs):
        if has_add:
            a_ref, b_ref, add_ref, o_ref = refs[:4]
        else:
            a_ref, b_ref, o_ref = refs[:3]
            add_ref = None
        prod = lax.dot_general(a_ref[...].astype(BF16), b_ref[...].astype(BF16), dn,
                               preferred_element_type=F32)

        def finish(r):
            if has_add:
                r = r + add_ref[...]
            o_ref[...] = r.astype(o_ref.dtype)

        if nk == 1:
            finish(prod)
        else:
            acc_ref = refs[-1]
            k = pl.program_id(2)

            @pl.when(k == 0)
            def _():
                acc_ref[...] = prod

            @pl.when(k > 0)
            def _():
                acc_ref[...] += prod

            @pl.when(k == nk - 1)
            def _():
                finish(acc_ref[...])

    ins = [a, b] + ([add] if has_add else [])
    in_specs = [a_spec, b_spec] + ([add_spec] if has_add else [])
    scratch = [] if nk == 1 else [pltpu.VMEM(acc_shape, F32)]
    return pl.pallas_call(
        body, name=name, grid=grid, out_shape=out_shape,
        in_specs=in_specs, out_specs=o_spec, scratch_shapes=scratch,
        compiler_params=_params("parallel", "parallel", "arbitrary"),
    )(*ins)


def _mm_nn(name, a, b, out_dtype, add=None, tm=1024, tn=1024, tk=2048):
    m, kd = a.shape
    n = b.shape[1]
    tm, tn, tk = _tile(m, tm, 16), _tile(n, tn, LANES), _tile(kd, tk, LANES)
    return _mm(name, "nn", a, b, jax.ShapeDtypeStruct((m, n), out_dtype),
               grid=(m // tm, n // tn, kd // tk),
               a_spec=pl.BlockSpec((tm, tk), lambda i, j, k: (i, k)),
               b_spec=pl.BlockSpec((tk, tn), lambda i, j, k: (k, j)),
               o_spec=pl.BlockSpec((tm, tn), lambda i, j, k: (i, j)),
               acc_shape=(tm, tn), add=add,
               add_spec=pl.BlockSpec((tm, tn), lambda i, j, k: (i, j)))


def _mm_nt(name, a, b, out_dtype, tm=1024, tn=1024, tk=2048):
    m, kd = a.shape
    n = b.shape[0]
    tm, tn, tk = _tile(m, tm, 16), _tile(n, tn, LANES), _tile(kd, tk, LANES)
    return _mm(name, "nt", a, b, jax.ShapeDtypeStruct((m, n), out_dtype),
               grid=(m // tm, n // tn, kd // tk),
               a_spec=pl.BlockSpec((tm, tk), lambda i, j, k: (i, k)),
               b_spec=pl.BlockSpec((tn, tk), lambda i, j, k: (j, k)),
               o_spec=pl.BlockSpec((tm, tn), lambda i, j, k: (i, j)),
               acc_shape=(tm, tn))


def _mm_tn(name, a, b, out_dtype, tm=1024, tn=1024, ts=1024):
    s, m = a.shape
    n = b.shape[1]
    tm, tn, ts = _tile(m, tm, LANES), _tile(n, tn, LANES), _tile(s, ts, 16)
    return _mm(name, "tn", a, b, jax.ShapeDtypeStruct((m, n), out_dtype),
               grid=(m // tm, n // tn, s // ts),
               a_spec=pl.BlockSpec((ts, tm), lambda i, j, k: (k, i)),
               b_spec=pl.BlockSpec((ts, tn), lambda i, j, k: (k, j)),
               o_spec=pl.BlockSpec((tm, tn), lambda i, j, k: (i, j)),
               acc_shape=(tm, tn))


def _rms_fwd(name, x, g):
    s, d = x.shape
    tm = _tile(s, 512, 16)

    def body(x_ref, g_ref, h_ref):
        xv = x_ref[...]
        r = lax.rsqrt(jnp.mean(xv * xv, axis=-1, keepdims=True) + RMS_EPS)
        h_ref[...] = ((xv * r) * g_ref[...]).astype(BF16)

    return pl.pallas_call(
        body, name=name, grid=(s // tm,), out_shape=jax.ShapeDtypeStruct((s, d), BF16),
        in_specs=[pl.BlockSpec((tm, d), lambda i: (i, 0)), pl.BlockSpec((1, d), lambda i: (0, 0))],
        out_specs=pl.BlockSpec((tm, d), lambda i: (i, 0)),
        compiler_params=_params("parallel"),
    )(x, g)


def _rms_bwd(name, dh, x, g, dres):
    s, d = x.shape
    tm = _tile(s, 256, 16)

    def body(dh_ref, x_ref, g_ref, dres_ref, dx_ref, dg_ref):
        i = pl.program_id(0)
        xv = x_ref[...]
        r = lax.rsqrt(jnp.mean(xv * xv, axis=-1, keepdims=True) + RMS_EPS)
        xhat = xv * r
        dhv = dh_ref[...].astype(F32)
        gdh = dhv * g_ref[...]
        dx_ref[...] = dres_ref[...] + r * (gdh - xhat * jnp.mean(gdh * xhat, axis=-1, keepdims=True))
        part = jnp.sum(dhv * xhat, axis=0, keepdims=True)

        @pl.when(i == 0)
        def _():
            dg_ref[...] = part

        @pl.when(i > 0)
        def _():
            dg_ref[...] += part

    row = pl.BlockSpec((tm, d), lambda i: (i, 0))
    vec = pl.BlockSpec((1, d), lambda i: (0, 0))
    return pl.pallas_call(
        body, name=name, grid=(s // tm,),
        out_shape=(jax.ShapeDtypeStruct((s, d), F32), jax.ShapeDtypeStruct((1, d), F32)),
        in_specs=[row, row, vec, row], out_specs=(row, vec),
        compiler_params=_params("arbitrary"),
    )(dh, x, g, dres)


def _split3(v):
    hi = v.astype(BF16)
    r1 = v - hi.astype(F32)
    mid = r1.astype(BF16)
    lo = (r1 - mid.astype(F32)).astype(BF16)
    return hi, mid, lo


def _tri_sum(tri, v):
    hi, mid, lo = _split3(v)
    dot = functools.partial(lax.dot_general, dimension_numbers=NN, preferred_element_type=F32)
    return dot(tri, hi) + dot(tri, mid) + dot(tri, lo)


def _gate_fwd(name, flog, b_pad):
    s = flog.shape[0]
    tb = _tile(s, 256, 16)

    def body(f_ref, b_ref, c_ref, carry_ref):
        i = pl.program_id(0)

        @pl.when(i == 0)
        def _():
            carry_ref[...] = jnp.zeros_like(carry_ref)

        z = f_ref[...] + b_ref[...]
        lf = jnp.minimum(z, 0.0) - jnp.log(1.0 + jnp.exp(-jnp.abs(z)))
        rows = lax.broadcasted_iota(jnp.int32, (tb, tb), 0)
        cols = lax.broadcasted_iota(jnp.int32, (tb, tb), 1)
        tri = (rows >= cols).astype(BF16)
        c_ref[...] = _tri_sum(tri, lf) + carry_ref[...]
        carry_ref[...] = c_ref[pl.ds(tb - 1, 1), :]

    return pl.pallas_call(
        body, name=name, grid=(s // tb,), out_shape=jax.ShapeDtypeStruct((s, LANES), F32),
        in_specs=[pl.BlockSpec((tb, LANES), lambda i: (i, 0)), pl.BlockSpec((1, LANES), lambda i: (0, 0))],
        out_specs=pl.BlockSpec((tb, LANES), lambda i: (i, 0)),
        scratch_shapes=[pltpu.VMEM((1, LANES), F32)],
        compiler_params=_params("arbitrary"),
    )(flog, b_pad)


def _gate_bwd(name, dck, dcq, flog, b_pad, n_heads):
    s = flog.shape[0]
    tb = _tile(s, 256, 16)
    nb = s // tb

    def body(dck_ref, dcq_ref, f_ref, b_ref, df_ref, db_ref, carry_ref, tmp_ref):
        i = pl.program_id(0)

        @pl.when(i == 0)
        def _():
            carry_ref[...] = jnp.zeros_like(carry_ref)

        rows = lax.broadcasted_iota(jnp.int32, (tb, tb), 0)
        cols = lax.broadcasted_iota(jnp.int32, (tb, tb), 1)
        tri = (rows <= cols).astype(BF16)
        tmp_ref[...] = _tri_sum(tri, dck_ref[...] + dcq_ref[...]) + carry_ref[...]
        carry_ref[...] = tmp_ref[pl.ds(0, 1), :]
        z = f_ref[...] + b_ref[...]
        lane = lax.broadcasted_iota(jnp.int32, (tb, LANES), 1)
        df = jnp.where(lane < n_heads, tmp_ref[...] / (1.0 + jnp.exp(z)), 0.0)
        df_ref[...] = df.astype(BF16)
        part = jnp.sum(df, axis=0, keepdims=True)

        @pl.when(i == 0)
        def _():
            db_ref[...] = part

        @pl.when(i > 0)
        def _():
            db_ref[...] += part

    rev = pl.BlockSpec((tb, LANES), lambda i: (nb - 1 - i, 0))
    vec = pl.BlockSpec((1, LANES), lambda i: (0, 0))
    return pl.pallas_call(
        body, name=name, grid=(nb,),
        out_shape=(jax.ShapeDtypeStruct((s, LANES), BF16), jax.ShapeDtypeStruct((1, LANES), F32)),
        in_specs=[rev, rev, rev, vec], out_specs=(rev, vec),
        scratch_shapes=[pltpu.VMEM((1, LANES), F32), pltpu.VMEM((tb, LANES), F32)],
        compiler_params=_params("arbitrary"),
    )(dck, dcq, flog, b_pad)


def _head_rms(v, g):
    r = lax.rsqrt(jnp.mean(v * v, axis=-1, keepdims=True) + RMS_EPS)
    return (v * r) * g


def _qkv_fwd(name, proj, gq, gk, d):
    s = proj.shape[0]
    tm = _tile(s, 256, 16)
    n_heads = d // HEAD_DIM

    def body(q_ref, k_ref, v_ref, gq_ref, gk_ref, qn_ref, kn_ref, vb_ref):
        for h in range(n_heads):
            sl = slice(h * HEAD_DIM, (h + 1) * HEAD_DIM)
            qn_ref[:, sl] = _head_rms(q_ref[:, sl], gq_ref[...]).astype(BF16)
            kn_ref[:, sl] = _head_rms(k_ref[:, sl], gk_ref[...]).astype(BF16)
        vb_ref[...] = v_ref[...].astype(BF16)

    col = lambda c: pl.BlockSpec((tm, d), lambda i, c=c: (i, c))
    vec = pl.BlockSpec((1, HEAD_DIM), lambda i: (0, 0))
    out = jax.ShapeDtypeStruct((s, d), BF16)
    return pl.pallas_call(
        body, name=name, grid=(s // tm,), out_shape=(out, out, out),
        in_specs=[col(0), col(1), col(2), vec, vec], out_specs=(col(0), col(0), col(0)),
        compiler_params=_params("parallel"),
    )(proj, proj, proj, gq, gk)


def _qkv_bwd(name, proj, dqn, dkn, dv, dflog, gq, gk, d, n_pad):
    s = proj.shape[0]
    tm = _tile(s, 256, 16)
    n_heads = d // HEAD_DIM

    def head_bwd(raw, dy, g):
        r = lax.rsqrt(jnp.mean(raw * raw, axis=-1, keepdims=True) + RMS_EPS)
        hat = raw * r
        gdy = dy * g
        dx = r * (gdy - hat * jnp.mean(gdy * hat, axis=-1, keepdims=True))
        return dx, jnp.sum(dy * hat, axis=0, keepdims=True)

    def body(q_ref, k_ref, dqn_ref, dkn_ref, dv_ref, df_ref, gq_ref, gk_ref, dp_ref, dgq_ref, dgk_ref):
        i = pl.program_id(0)
        accq = jnp.zeros((1, HEAD_DIM), F32)
        acck = jnp.zeros((1, HEAD_DIM), F32)
        for h in range(n_heads):
            sl = slice(h * HEAD_DIM, (h + 1) * HEAD_DIM)
            dq, pq = head_bwd(q_ref[:, sl], dqn_ref[:, sl], gq_ref[...])
            dk, pk = head_bwd(k_ref[:, sl], dkn_ref[:, sl], gk_ref[...])
            dp_ref[:, sl] = dq.astype(BF16)
            dp_ref[:, d + h * HEAD_DIM:d + (h + 1) * HEAD_DIM] = dk.astype(BF16)
            accq, acck = accq + pq, acck + pk
        dp_ref[:, 2 * d:3 * d] = dv_ref[...]
        dp_ref[:, 3 * d:] = df_ref[...]

        @pl.when(i == 0)
        def _():
            dgq_ref[...] = accq
            dgk_ref[...] = acck

        @pl.when(i > 0)
        def _():
            dgq_ref[...] += accq
            dgk_ref[...] += acck

    col = lambda c: pl.BlockSpec((tm, d), lambda i, c=c: (i, c))
    vec = pl.BlockSpec((1, HEAD_DIM), lambda i: (0, 0))
    return pl.pallas_call(
        body, name=name, grid=(s // tm,),
        out_shape=(jax.ShapeDtypeStruct((s, n_pad), BF16), jax.ShapeDtypeStruct((1, HEAD_DIM), F32),
                   jax.ShapeDtypeStruct((1, HEAD_DIM), F32)),
        in_specs=[col(0), col(1), col(0), col(0), col(0), pl.BlockSpec((tm, LANES), lambda i: (i, 0)), vec, vec],
        out_specs=(pl.BlockSpec((tm, n_pad), lambda i: (i, 0)), vec, vec),
        compiler_params=_params("arbitrary"),
    )(proj, proj, dqn, dkn, dv, dflog, gq, gk)


def _attn_fwd(name, qn, kn, vb, c_col, c_row):
    s, d = qn.shape
    n_heads = d // HEAD_DIM
    t = _tile(s, 512, LANES)
    scale = HEAD_DIM ** -0.5

    def body(q_ref, k_ref, v_ref, cq_ref, ck_ref, o_ref, lse_ref, m_ref, l_ref, acc_ref):
        i = pl.program_id(1)
        q = q_ref[...]
        cq = cq_ref[...]
        m_ref[...] = jnp.full(m_ref.shape, NEG_INF, F32)
        l_ref[...] = jnp.zeros_like(l_ref)
        acc_ref[...] = jnp.zeros_like(acc_ref)

        def step(j, masked):
            start = pl.multiple_of(j * t, t)
            kj = k_ref[pl.ds(start, t), :]
            vj = v_ref[pl.ds(start, t), :]
            ckj = ck_ref[:, pl.ds(start, t)]
            sc = lax.dot_general(q, kj, NT, preferred_element_type=F32) * scale + (cq - ckj)
            if masked:
                rows = lax.broadcasted_iota(jnp.int32, (t, t), 0)
                cols = lax.broadcasted_iota(jnp.int32, (t, t), 1)
                sc = jnp.where(rows >= cols, sc, NEG_INF)
            m_prev = m_ref[...]
            m_new = jnp.maximum(m_prev, jnp.max(sc, axis=-1, keepdims=True))
            p = jnp.exp(sc - m_new)
            alpha = jnp.exp(m_prev - m_new)
            l_ref[...] = alpha * l_ref[...] + jnp.sum(p, axis=-1, keepdims=True)
            acc_ref[...] = alpha * acc_ref[...] + lax.dot_general(
                p.astype(BF16), vj, NN, preferred_element_type=F32)
            m_ref[...] = m_new

        def loop_body(j, carry):
            step(j, False)
            return carry

        lax.fori_loop(0, i, loop_body, 0)
        step(i, True)
        o_ref[...] = (acc_ref[...] / l_ref[...]).astype(BF16)
        lse_ref[...] = m_ref[...] + jnp.log(l_ref[...])

    head_all = pl.BlockSpec((s, HEAD_DIM), lambda h, i: (0, h))
    return pl.pallas_call(
        body, name=name, grid=(n_heads, s // t),
        out_shape=(jax.ShapeDtypeStruct((s, d), BF16), jax.ShapeDtypeStruct((n_heads, s, 1), F32)),
        in_specs=[pl.BlockSpec((t, HEAD_DIM), lambda h, i: (i, h)), head_all, head_all,
                  pl.BlockSpec((None, t, 1), lambda h, i: (h, i, 0)),
                  pl.BlockSpec((None, 1, s), lambda h, i: (h, 0, 0))],
        out_specs=(pl.BlockSpec((t, HEAD_DIM), lambda h, i: (i, h)),
                   pl.BlockSpec((None, t, 1), lambda h, i: (h, i, 0))),
        scratch_shapes=[pltpu.VMEM((t, 1), F32), pltpu.VMEM((t, 1), F32), pltpu.VMEM((t, HEAD_DIM), F32)],
        compiler_params=_params("parallel", "arbitrary"),
    )(qn, kn, vb, c_col, c_row)


def _attn_delta(name, o, do, n_heads):
    s, d = o.shape
    tm = _tile(s, 256, 16)

    def body(o_ref, do_ref, dl_ref):
        lane = lax.broadcasted_iota(jnp.int32, (tm, LANES), 1)
        acc = jnp.zeros((tm, LANES), F32)
        for h in range(n_heads):
            sl = slice(h * HEAD_DIM, (h + 1) * HEAD_DIM)
            col = jnp.sum(o_ref[:, sl].astype(F32) * do_ref[:, sl].astype(F32), axis=-1, keepdims=True)
            acc = jnp.where(lane == h, col, acc)
        dl_ref[...] = acc

    row = pl.BlockSpec((tm, d), lambda i: (i, 0))
    return pl.pallas_call(
        body, name=name, grid=(s // tm,), out_shape=jax.ShapeDtypeStruct((s, LANES), F32),
        in_specs=[row, row], out_specs=pl.BlockSpec((tm, LANES), lambda i: (i, 0)),
        compiler_params=_params("parallel"),
    )(o, do)


def _attn_bwd(name, qn, kn, vb, do, c_row, lse_row, delta_row, c_col):
    s, d = qn.shape
    n_heads = d // HEAD_DIM
    t = _tile(s, 512, LANES)
    nq = s // t
    scale = HEAD_DIM ** -0.5

    def body(q_ref, do_ref, cr_ref, lse_ref, dl_ref, k_ref, v_ref, ck_ref,
             dq_ref, dk_ref, dv_ref, dc_ref, dcq_ref, dk_acc, dv_acc, dc_acc):
        j = pl.program_id(1)

        @pl.when(j == 0)
        def _():
            dq_ref[...] = jnp.zeros_like(dq_ref)
            dcq_ref[...] = jnp.zeros_like(dcq_ref)

        kj = k_ref[...]
        vj = v_ref[...]
        ckj = ck_ref[...]
        dk_acc[...] = jnp.zeros_like(dk_acc)
        dv_acc[...] = jnp.zeros_like(dv_acc)
        dc_acc[...] = jnp.zeros_like(dc_acc)

        def step(i, masked):
            start = pl.multiple_of(i * t, t)
            qi = q_ref[pl.ds(start, t), :]
            doi = do_ref[pl.ds(start, t), :]
            bias = cr_ref[:, pl.ds(start, t)] - lse_ref[:, pl.ds(start, t)]
            dli = dl_ref[:, pl.ds(start, t)]
            st = lax.dot_general(kj, qi, NT, preferred_element_type=F32) * scale + (bias - ckj)
            if masked:
                rows = lax.broadcasted_iota(jnp.int32, (t, t), 0)
                cols = lax.broadcasted_iota(jnp.int32, (t, t), 1)
                st = jnp.where(cols >= rows, st, NEG_INF)
            pt = jnp.exp(st)
            dpt = lax.dot_general(vj, doi, NT, preferred_element_type=F32)
            dst = pt * (dpt - dli)
            dsb = dst.astype(BF16)
            dv_acc[...] += lax.dot_general(pt.astype(BF16), doi, NN, preferred_element_type=F32)
            dk_acc[...] += lax.dot_general(dsb, qi, NN, preferred_element_type=F32)
            dq_ref[pl.ds(start, t), :] += lax.dot_general(dsb, kj, TN, preferred_element_type=F32) * scale
            dc_acc[...] += jnp.sum(dst, axis=1, keepdims=True)
            dcq_ref[:, pl.ds(start, t)] += jnp.sum(dst, axis=0, keepdims=True)

        step(j, True)

        def loop_body(i, carry):
            step(i, False)
            return carry

        lax.fori_loop(j + 1, nq, loop_body, 0)
        dk_ref[...] = dk_acc[...] * scale
        dv_ref[...] = dv_acc[...].astype(BF16)
        dc_ref[...] = -dc_acc[...]

    head_all = pl.BlockSpec((s, HEAD_DIM), lambda h, j: (0, h))
    row_all = pl.BlockSpec((None, 1, s), lambda h, j: (h, 0, 0))
    blk = pl.BlockSpec((t, HEAD_DIM), lambda h, j: (j, h))
    col_blk = pl.BlockSpec((None, t, 1), lambda h, j: (h, j, 0))
    return pl.pallas_call(
        body, name=name, grid=(n_heads, nq),
        out_shape=(jax.ShapeDtypeStruct((s, d), F32), jax.ShapeDtypeStruct((s, d), F32),
                   jax.ShapeDtypeStruct((s, d), BF16), jax.ShapeDtypeStruct((n_heads, s, 1), F32),
                   jax.ShapeDtypeStruct((n_heads, 1, s), F32)),
        in_specs=[head_all, head_all, row_all, row_all, row_all, blk, blk, col_blk],
        out_specs=(head_all, blk, blk, col_blk, row_all),
        scratch_shapes=[pltpu.VMEM((t, HEAD_DIM), F32), pltpu.VMEM((t, HEAD_DIM), F32), pltpu.VMEM((t, 1), F32)],
        compiler_params=_params("parallel", "arbitrary"),
    )(qn, do, c_row, lse_row, delta_row, kn, vb, c_col)


def _ffn_up(name, h, w_gu):
    s, d = h.shape
    fs = w_gu.shape[2]
    half = N_DEV // 2
    tm = _tile(s, 512, 16)

    def body(h_ref, wg_ref, wu_ref, g_ref, u_ref, a_ref):
        hv = h_ref[...]
        g = lax.dot_general(hv, wg_ref[...], NN, preferred_element_type=F32)
        u = lax.dot_general(hv, wu_ref[...], NN, preferred_element_type=F32)
        g_ref[...] = g.astype(BF16)
        u_ref[...] = u.astype(BF16)
        a_ref[...] = (g * jax.nn.sigmoid(g) * u).astype(BF16)

    out = jax.ShapeDtypeStruct((s, half * fs), BF16)
    ospec = pl.BlockSpec((tm, fs), lambda j, i: (i, j))
    return pl.pallas_call(
        body, name=name, grid=(half, s // tm), out_shape=(out, out, out),
        in_specs=[pl.BlockSpec((tm, d), lambda j, i: (i, 0)),
                  pl.BlockSpec((None, d, fs), lambda j, i: (j, 0, 0)),
                  pl.BlockSpec((None, d, fs), lambda j, i: (j + half, 0, 0))],
        out_specs=(ospec, ospec, ospec),
        compiler_params=_params("parallel", "parallel"),
    )(h, w_gu, w_gu)


def _ffn_dact(name, dx, w_dn4, g, u):
    s, d = dx.shape
    half, fs = w_dn4.shape[0], w_dn4.shape[1]
    tm = _tile(s, 512, 16)

    def body(dx_ref, w_ref, g_ref, u_ref, dgu_ref):
        da = lax.dot_general(dx_ref[...].astype(BF16), w_ref[...], NT, preferred_element_type=F32)
        gv = g_ref[...].astype(F32)
        uv = u_ref[...].astype(F32)
        sig = jax.nn.sigmoid(gv)
        dgu_ref[0] = (da * uv * (sig * (1.0 + gv * (1.0 - sig)))).astype(BF16)
        dgu_ref[1] = (da * (gv * sig)).astype(BF16)

    blk = pl.BlockSpec((tm, fs), lambda j, i: (i, j))
    return pl.pallas_call(
        body, name=name, grid=(half, s // tm),
        out_shape=jax.ShapeDtypeStruct((2, s, half * fs), BF16),
        in_specs=[pl.BlockSpec((tm, d), lambda j, i: (i, 0)),
                  pl.BlockSpec((None, fs, d), lambda j, i: (j, 0, 0)), blk, blk],
        out_specs=pl.BlockSpec((2, tm, fs), lambda j, i: (0, i, j)),
        compiler_params=_params("parallel", "parallel"),
    )(dx, w_dn4, g, u)


def _ffn_dw_gu(name, h, dgu):
    s, d = h.shape
    half, fs = N_DEV // 2, dgu.shape[2] // (N_DEV // 2)
    tm, ts = _tile(d, 1024, LANES), _tile(s, 1024, 16)
    return _mm(name, "tn", h, dgu, jax.ShapeDtypeStruct((N_DEV, d, fs), BF16),
               grid=(d // tm, N_DEV, s // ts),
               a_spec=pl.BlockSpec((ts, tm), lambda i, j, k: (k, i)),
               b_spec=pl.BlockSpec((None, ts, fs), lambda i, j, k: (j // half, k, j % half)),
               o_spec=pl.BlockSpec((None, tm, fs), lambda i, j, k: (j, i, 0)),
               acc_shape=(tm, fs))


def _ffn_dh(name, dgu, w_gu):
    s = dgu.shape[1]
    d, fs = w_gu.shape[1], w_gu.shape[2]
    half = N_DEV // 2
    tm = _tile(s, 512, 16)
    return _mm(name, "nt", dgu, w_gu, jax.ShapeDtypeStruct((s, d), F32),
               grid=(s // tm, 1, N_DEV),
               a_spec=pl.BlockSpec((None, tm, fs), lambda i, j, k: (k // half, i, k % half)),
               b_spec=pl.BlockSpec((None, d, fs), lambda i, j, k: (k, 0, 0)),
               o_spec=pl.BlockSpec((tm, d), lambda i, j, k: (i, 0)),
               acc_shape=(tm, d))


def _pool_fwd(name, x, g, w, b, sc):
    s, d = x.shape
    dg = d // len(POOL_WINDOWS)
    tm = _tile(s, 256, POOL_HALO)
    per = tm // POOL_HALO

    def body(x_ref, xh_ref, g_ref, w_ref, b_ref, sc_ref, xo_ref, y_ref, zb_ref):
        i = pl.program_id(0)
        gv = g_ref[...]

        def norm(v):
            return (v * lax.rsqrt(jnp.mean(v * v, axis=-1, keepdims=True) + RMS_EPS)) * gv

        h = norm(x_ref[...])
        halo = norm(xh_ref[...]) * (i > 0).astype(F32)
        ext = jnp.concatenate([halo, h], axis=0)
        t = i * tm + lax.broadcasted_iota(jnp.int32, (tm, 1), 0)
        for gi, win in enumerate(POOL_WINDOWS):
            sl = slice(gi * dg, (gi + 1) * dg)
            acc = ext[:, sl]
            step = 1
            while step < win:
                acc = acc + pltpu.roll(acc, step, 0)
                step *= 2
            inv = 1.0 / jnp.minimum(t + 1, win).astype(F32)
            yg = (acc[POOL_HALO:, :] * inv - h[:, sl]).astype(BF16)
            y_ref[:, sl] = yg
            zb = lax.dot_general(yg, w_ref[gi], NN, preferred_element_type=F32) + b_ref[:, sl]
            zb_ref[:, sl] = zb
            xo_ref[:, sl] = x_ref[:, sl] + zb * sc_ref[:, sl]

    row = pl.BlockSpec((tm, d), lambda i: (i, 0))
    vec = pl.BlockSpec((1, d), lambda i: (0, 0))
    return pl.pallas_call(
        body, name=name, grid=(s // tm,),
        out_shape=(jax.ShapeDtypeStruct((s, d), F32), jax.ShapeDtypeStruct((s, d), BF16),
                   jax.ShapeDtypeStruct((s, d), F32)),
        in_specs=[row, pl.BlockSpec((POOL_HALO, d), lambda i: (jnp.maximum(i * per - 1, 0), 0)),
                  vec, pl.BlockSpec(w.shape, lambda i: (0, 0, 0)), vec, vec],
        out_specs=(row, row, row),
        compiler_params=_params("parallel"),
    )(x, x, g, w, b, sc)


def _pool_bwd(name, dout, x, zb, g, w, sc):
    s, d = x.shape
    dg = d // len(POOL_WINDOWS)
    tm = _tile(s, 256, POOL_HALO)
    per = tm // POOL_HALO
    nb = s // tm
    ext_rows = tm + POOL_HALO

    def body(do_ref, doh_ref, x_ref, zb_ref, g_ref, w_ref, sc_ref, dx_ref, dz_ref, dgn_ref, dsc_ref, db_ref):
        i = pl.program_id(0)
        scv = sc_ref[...]
        dov = do_ref[...]
        dz = dov * scv
        dz_ref[...] = dz.astype(BF16)
        halo = doh_ref[...] * scv * (i < nb - 1).astype(F32)
        ext = jnp.concatenate([dz, halo], axis=0).astype(BF16)
        t = i * tm + lax.broadcasted_iota(jnp.int32, (ext_rows, 1), 0)
        parts = []
        for gi, win in enumerate(POOL_WINDOWS):
            sl = slice(gi * dg, (gi + 1) * dg)
            dy = lax.dot_general(ext[:, sl], w_ref[gi], NT, preferred_element_type=F32)
            acc = dy * (1.0 / jnp.minimum(t + 1, win).astype(F32))
            step = 1
            while step < win:
                acc = acc + pltpu.roll(acc, ext_rows - step, 0)
                step *= 2
            parts.append(acc[:tm, :] - dy[:tm, :])
        dh = jnp.concatenate(parts, axis=1)
        xv = x_ref[...]
        r = lax.rsqrt(jnp.mean(xv * xv, axis=-1, keepdims=True) + RMS_EPS)
        xhat = xv * r
        gdh = dh * g_ref[...]
        dx_ref[...] = dov + r * (gdh - xhat * jnp.mean(gdh * xhat, axis=-1, keepdims=True))
        pgn = jnp.sum(dh * xhat, axis=0, keepdims=True)
        psc = jnp.sum(dov * zb_ref[...], axis=0, keepdims=True)
        pb = jnp.sum(dz, axis=0, keepdims=True)

        @pl.when(i == 0)
        def _():
            dgn_ref[...] = pgn
            dsc_ref[...] = psc
            db_ref[...] = pb

        @pl.when(i > 0)
        def _():
            dgn_ref[...] += pgn
            dsc_ref[...] += psc
            db_ref[...] += pb

    row = pl.BlockSpec((tm, d), lambda i: (i, 0))
    vec = pl.BlockSpec((1, d), lambda i: (0, 0))
    vshape = jax.ShapeDtypeStruct((1, d), F32)
    return pl.pallas_call(
        body, name=name, grid=(nb,),
        out_shape=(jax.ShapeDtypeStruct((s, d), F32), jax.ShapeDtypeStruct((s, d), BF16), vshape, vshape, vshape),
        in_specs=[row, pl.BlockSpec((POOL_HALO, d), lambda i: (jnp.minimum((i + 1) * per, s // POOL_HALO - 1), 0)),
                  row, row, vec, pl.BlockSpec(w.shape, lambda i: (0, 0, 0)), vec],
        out_specs=(row, row, vec, vec, vec),
        compiler_params=_params("arbitrary"),
    )(dout, dout, x, zb, g, w, sc)


def _pool_dw(name, y, dz, n_groups):
    s, d = y.shape
    dg = d // n_groups
    ts = _tile(s, 1024, 16)
    return _mm(name, "tn", y, dz, jax.ShapeDtypeStruct((n_groups, dg, dg), F32),
               grid=(n_groups, 1, s // ts),
               a_spec=pl.BlockSpec((ts, dg), lambda i, j, k: (k, i)),
               b_spec=pl.BlockSpec((ts, dg), lambda i, j, k: (k, i)),
               o_spec=pl.BlockSpec((None, dg, dg), lambda i, j, k: (i, 0, 0)),
               acc_shape=(dg, dg))


def _loss_head(name, y, tgt):
    s, d = y.shape
    tm = _tile(s, 512, 16)

    def body(y_ref, t_ref, dy_ref, l_ref):
        i = pl.program_id(0)
        e = y_ref[...] - t_ref[...]
        dy_ref[...] = e * (1.0 / d)
        part = jnp.sum(jnp.mean(e * e, axis=-1, keepdims=True), axis=0, keepdims=True)
        part = jnp.broadcast_to(part, l_ref.shape)

        @pl.when(i == 0)
        def _():
            l_ref[...] = part

        @pl.when(i > 0)
        def _():
            l_ref[...] += part

    row = pl.BlockSpec((tm, d), lambda i: (i, 0))
    return pl.pallas_call(
        body, name=name, grid=(s // tm,),
        out_shape=(jax.ShapeDtypeStruct((s, d), F32), jax.ShapeDtypeStruct((8, LANES), F32)),
        in_specs=[row, row], out_specs=(row, pl.BlockSpec((8, LANES), lambda i: (0, 0))),
        compiler_params=_params("arbitrary"),
    )(y, tgt)


def _adamw(name, w, m, v, pieces):
    r, c = w.shape
    tr = _tile(r, 128, 16)

    def body(w_ref, m_ref, v_ref, p_ref, g_ref, d_ref, nm_ref, nv_ref):
        g = p_ref[0].astype(F32)
        for k in range(1, N_DEV):
            g = g + p_ref[k].astype(F32)
        mn = ADAM_B1 * m_ref[...] + (1.0 - ADAM_B1) * g
        vn = ADAM_B2 * v_ref[...] + (1.0 - ADAM_B2) * (g * g)
        m_hat = mn / (1.0 - ADAM_B1 ** ADAM_STEP)
        v_hat = vn / (1.0 - ADAM_B2 ** ADAM_STEP)
        g_ref[...] = g
        d_ref[...] = -ADAM_LR * (m_hat / (jnp.sqrt(v_hat) + ADAM_EPS) + ADAM_WD * w_ref[...])
        nm_ref[...] = mn
        nv_ref[...] = vn

    blk = pl.BlockSpec((tr, c), lambda i: (i, 0))
    out = jax.ShapeDtypeStruct((r, c), F32)
    return pl.pallas_call(
        body, name=name, grid=(r // tr,), out_shape=(out, out, out, out),
        in_specs=[blk, blk, blk, pl.BlockSpec((N_DEV, tr, c), lambda i: (0, i, 0))],
        out_specs=(blk, blk, blk, blk),
        compiler_params=_params("parallel"),
    )(w, m, v, pieces)


def _adamw_nd(name, w, m, v, pieces):
    c = w.shape[-1]
    outs = _adamw(name, w.reshape(-1, c), m.reshape(-1, c), v.reshape(-1, c), pieces.reshape(N_DEV, -1, c))
    return [o.reshape(w.shape) for o in outs]


def _pack_small(mix, ffn, b_f, gq, gk):
    def rows(a):
        a = a.reshape(-1, LANES) if a.shape[-1] >= LANES else jnp.pad(a, ((0, 0), (0, LANES - a.shape[-1])))
        return jnp.pad(a, ((0, -a.shape[0] % 8), (0, 0)))
    return jnp.concatenate([rows(mix), rows(ffn), rows(b_f), rows(gq), rows(gk)], axis=0)


def _unpack_small(p, mix, ffn, b_f, gq, gk):
    out, pos = [], 0
    for a in (mix, ffn, b_f, gq, gk):
        n = a.size // LANES if a.shape[-1] >= LANES else a.shape[0]
        blk = p[pos:pos + n]
        out.append(blk.reshape(a.shape) if a.shape[-1] >= LANES else blk[:, :a.shape[-1]])
        pos += n + (-n % 8)
    return out


def kernel(x, mix_norm_g, ffn_norm_g, fox_w_in, fox_b_f, fox_q_norm_g, fox_k_norm_g, fox_w_out, pool_w, pool_b, pool_scale, ffn_w_gate_up, ffn_w_down, loss_target, m_mix_norm_g, m_ffn_norm_g, m_fox_w_in, m_fox_b_f, m_fox_q_norm_g, m_fox_k_norm_g, m_fox_w_out, m_pool_w, m_pool_b, m_pool_scale, m_ffn_w_gate_up, m_ffn_w_down, v_mix_norm_g, v_ffn_norm_g, v_fox_w_in, v_fox_b_f, v_fox_q_norm_g, v_fox_k_norm_g, v_fox_w_out, v_pool_w, v_pool_b, v_pool_scale, v_ffn_w_gate_up, v_ffn_w_down):
    xs, tgt = x[0], loss_target[0]
    s, d = xs.shape
    depth = mix_norm_g.shape[0]
    n_fox, n_pool = fox_w_in.shape[0], pool_w.shape[0]
    n_heads = d // HEAD_DIM
    n_in = fox_w_in.shape[2] * N_DEV
    n_pad = 3 * d + LANES
    n_groups = pool_w.shape[1]
    dsh = d // N_DEV
    half = N_DEV // 2
    axes = ("x", "y", "c")

    pool_bs = jnp.stack([pool_b, pool_scale])[None]
    w_in_g, w_out_g, pool_w_g, pool_bs_g = _all_gather_layers(
        "gather_mixer", [fox_w_in.astype(BF16), fox_w_out.astype(BF16), pool_w.astype(BF16), pool_bs])
    w_gu_g, w_dn_g = _all_gather_layers(
        "gather_ffn", [ffn_w_gate_up.astype(BF16), ffn_w_down.astype(BF16)])
    w_in = [jnp.pad(jnp.transpose(g, (1, 0, 2)).reshape(d, n_in), ((0, 0), (0, n_pad - n_in))) for g in w_in_g]
    w_out = [g.reshape(d, d) for g in w_out_g]
    w_pool = [jnp.transpose(g, (1, 0, 2, 3)).reshape(n_groups, d // n_groups, d // n_groups) for g in pool_w_g]
    pool_bs_full = jnp.transpose(pool_bs_g[0], (1, 2, 0, 3)).reshape(2, n_pool, 1, d)
    w_dn = [g.reshape(-1, d) for g in w_dn_g]
    b_pad = [jnp.pad(fox_b_f[j], (0, LANES - n_heads))[None] for j in range(n_fox)]

    saved = []
    cur = xs
    for i in range(depth):
        j = i // 2
        gm = mix_norm_g[i][None]
        if i % 2 == 0:
            h = _rms_fwd(f"norm_mix{i}", cur, gm)
            proj = _mm_nn(f"proj_in{i}", h, w_in[j], F32, tn=896)
            gq, gk = fox_q_norm_g[j][None], fox_k_norm_g[j][None]
            qn, kn, vb = _qkv_fwd(f"qk_norm{i}", proj, gq, gk, d)
            flog = proj[:, 3 * d:]
            c = _gate_fwd(f"gate{i}", flog, b_pad[j])
            c_t = c[:, :n_heads].T
            c_col, c_row = c_t[:, :, None], c_t[:, None, :]
            o, lse = _attn_fwd(f"attn{i}", qn, kn, vb, c_col, c_row)
            mid = _mm_nn(f"proj_out{i}", o, w_out[j], F32, add=cur)
            mix_saved = (cur, h, proj, flog, qn, kn, vb, c_col, c_row, o, lse)
        else:
            mid, y, zb = _pool_fwd(f"pool{i}", cur, gm, w_pool[j], pool_bs_full[0, j], pool_bs_full[1, j])
            mix_saved = (cur, y, zb)
        h2 = _rms_fwd(f"norm_ffn{i}", mid, ffn_norm_g[i][None])
        gate, up, act = _ffn_up(f"ffn_up{i}", h2, w_gu_g[i])
        nxt = _mm_nn(f"ffn_down{i}", act, w_dn[i], F32, add=mid, tk=1408)
        saved.append((mix_saved, mid, h2, gate, up, act))
        cur = nxt

    dcur, lpart = _loss_head("loss_head", cur, tgt)
    loss = lax.psum(0.5 * lpart[0, 0], axes)

    d_mix, d_ffn = [None] * depth, [None] * depth
    d_bf, d_gq, d_gk = [None] * n_fox, [None] * n_fox, [None] * n_fox
    g_in, g_out = [None] * n_fox, [None] * n_fox
    g_pw, g_pb, g_ps = [None] * n_pool, [None] * n_pool, [None] * n_pool
    g_gu, g_dn = [None] * depth, [None] * depth
    for i in reversed(range(depth)):
        j = i // 2
        mix_saved, mid, h2, gate, up, act = saved[i]
        dgu = _ffn_dact(f"ffn_dact{i}", dcur, w_dn[i].reshape(half, -1, d), gate, up)
        g_dn[i] = _mm_tn(f"ffn_dw_down{i}", act, dcur, BF16, tm=1408).reshape(N_DEV, -1, d)
        g_gu[i] = _ffn_dw_gu(f"ffn_dw_up{i}", h2, dgu)
        dh2 = _ffn_dh(f"ffn_dh{i}", dgu, w_gu_g[i])
        dmid, d_ffn[i] = _rms_bwd(f"norm_ffn_bwd{i}", dh2, mid, ffn_norm_g[i][None], dcur)
        gm = mix_norm_g[i][None]
        if i % 2 == 0:
            xin, h, proj, flog, qn, kn, vb, c_col, c_row, o, lse = mix_saved
            do = _mm_nt(f"proj_out_dx{i}", dmid, w_out[j], BF16)
            g_out[j] = _mm_tn(f"proj_out_dw{i}", o, dmid, BF16).reshape(N_DEV, dsh, d)
            delta = _attn_delta(f"attn_delta{i}", o, do, n_heads)
            delta_row = delta[:, :n_heads].T[:, None, :]
            dqn, dkn, dv, dck, dcq = _attn_bwd(f"attn_bwd{i}", qn, kn, vb, do, c_row,
                                               lse.reshape(n_heads, 1, s), delta_row, c_col)
            lane_pad = ((0, 0), (0, LANES - n_heads))
            dflog, d_bf[j] = _gate_bwd(f"gate_bwd{i}", jnp.pad(dck[:, :, 0].T, lane_pad),
                                       jnp.pad(dcq[:, 0, :].T, lane_pad), flog, b_pad[j], n_heads)
            gq, gk = fox_q_norm_g[j][None], fox_k_norm_g[j][None]
            dproj, d_gq[j], d_gk[j] = _qkv_bwd(f"qk_norm_bwd{i}", proj, dqn, dkn, dv, dflog, gq, gk, d, n_pad)
            dw_in = _mm_tn(f"proj_in_dw{i}", h, dproj, BF16, tn=896)
            g_in[j] = jnp.transpose(dw_in[:, :n_in].reshape(d, N_DEV, n_in // N_DEV), (1, 0, 2))
            dh = _mm_nt(f"proj_in_dx{i}", dproj, w_in[j], F32, tk=896)
            dcur, d_mix[i] = _rms_bwd(f"norm_mix_bwd{i}", dh, xin, gm, dmid)
        else:
            xin, y, zb = mix_saved
            dcur, dz, d_mix[i], dsc, db = _pool_bwd(f"pool_bwd{i}", dmid, xin, zb, gm, w_pool[j], pool_bs_full[1, j])
            dwp = _pool_dw(f"pool_dw{i}", y, dz, n_groups)
            dg = d // n_groups
            g_pw[j] = jnp.transpose(dwp.reshape(n_groups, N_DEV, dg // N_DEV, dg), (1, 0, 2, 3)).astype(BF16)
            g_pb[j], g_ps[j] = db.reshape(N_DEV, dsh), dsc.reshape(N_DEV, dsh)
    grad_x = dcur[None]

    g_pbs = jnp.stack([jnp.stack(g_pb, axis=1), jnp.stack(g_ps, axis=1)], axis=1)
    r_in, r_out, r_pw, r_pbs = _all_to_all_layers("scatter_mixer", [g_in, g_out, g_pw, [g_pbs]])
    r_gu, r_dn = _all_to_all_layers("scatter_ffn", [g_gu, g_dn])
    upd = {}
    upd["fox_w_in"] = _adamw_nd("adamw_w_in", fox_w_in, m_fox_w_in, v_fox_w_in, r_in)
    upd["fox_w_out"] = _adamw_nd("adamw_w_out", fox_w_out, m_fox_w_out, v_fox_w_out, r_out)
    upd["pool_w"] = _adamw_nd("adamw_pool_w", pool_w, m_pool_w, v_pool_w, r_pw)
    pbs = _adamw_nd("adamw_pool_bs", pool_bs[0], jnp.stack([m_pool_b, m_pool_scale]),
                    jnp.stack([v_pool_b, v_pool_scale]), r_pbs[:, 0])
    upd["pool_b"] = [o[0] for o in pbs]
    upd["pool_scale"] = [o[1] for o in pbs]
    upd["ffn_w_gate_up"] = _adamw_nd("adamw_gate_up", ffn_w_gate_up, m_ffn_w_gate_up, v_ffn_w_gate_up, r_gu)
    upd["ffn_w_down"] = _adamw_nd("adamw_down", ffn_w_down, m_ffn_w_down, v_ffn_w_down, r_dn)

    small_w = (mix_norm_g, ffn_norm_g, fox_b_f, fox_q_norm_g, fox_k_norm_g)
    small_g = _pack_small(jnp.concatenate(d_mix), jnp.concatenate(d_ffn),
                          jnp.concatenate(d_bf)[:, :n_heads], jnp.concatenate(d_gq), jnp.concatenate(d_gk))
    (small_pieces,), = _all_gather_layers("gather_small", [small_g[None]])
    small = _adamw("adamw_small", _pack_small(*small_w),
                   _pack_small(m_mix_norm_g, m_ffn_norm_g, m_fox_b_f, m_fox_q_norm_g, m_fox_k_norm_g),
                   _pack_small(v_mix_norm_g, v_ffn_norm_g, v_fox_b_f, v_fox_q_norm_g, v_fox_k_norm_g),
                   small_pieces)
    small = [_unpack_small(o, *small_w) for o in small]
    for n, name in enumerate(("mix_norm_g", "ffn_norm_g", "fox_b_f", "fox_q_norm_g", "fox_k_norm_g")):
        upd[name] = [o[n] for o in small]

    order = ("mix_norm_g", "ffn_norm_g", "fox_w_in", "fox_b_f", "fox_q_norm_g", "fox_k_norm_g", "fox_w_out",
             "pool_w", "pool_b", "pool_scale", "ffn_w_gate_up", "ffn_w_down")
    return (loss, grad_x) + tuple(upd[name][q] for q in range(4) for name in order)
```

```python
import functools

import jax
import jax.numpy as jnp
from jax import lax
from jax.experimental import pallas as pl
from jax.experimental.pallas import tpu as pltpu

F32 = jnp.float32
BF16 = jnp.bfloat16
MESH = pl.DeviceIdType.MESH

N_DEV = 8
HEAD_DIM = 128
LANES = 128
POOL_WINDOWS = (2, 4, 8, 16)
POOL_HALO = 16
RMS_EPS = 1e-6
NEG_INF = -1e30
ADAM_LR = 0.001
ADAM_B1 = 0.9
ADAM_B2 = 0.999
ADAM_EPS = 1e-08
ADAM_WD = 0.01
ADAM_STEP = 10
VMEM_LIMIT = 52 * 1024 * 1024

NN = (((1,), (0,)), ((), ()))
NT = (((1,), (1,)), ((), ()))
TN = (((0,), (0,)), ((), ()))


def _tile(n, pref, align):
    best = None
    d = align
    while d <= min(n, pref):
        if n % d == 0:
            best = d
        d += align
    return n if best is None else best


def _params(*sem):
    return pltpu.CompilerParams(dimension_semantics=sem, vmem_limit_bytes=VMEM_LIMIT)


def _position():
    x, y, c = lax.axis_index("x"), lax.axis_index("y"), lax.axis_index("c")
    return x, y, c, 4 * x + 2 * y + c


def _peer(x, y, c, k):
    px = 1 - x if k & 4 else x
    py = 1 - y if k & 2 else y
    pc = 1 - c if k & 1 else c
    return (px, py, pc), 4 * px + 2 * py + pc


def _exchange(name, ins, out_shapes, copies):
    n_in, n_cp = len(ins), len(copies)

    def body(*refs):
        in_refs = refs[:n_in]
        out_refs = refs[n_in:n_in + len(out_shapes)]
        send_sems, recv_sems, loc_sems = refs[n_in + len(out_shapes):]
        x, y, c, me = _position()
        local = []
        for ci, (ii, src_of, oi, dst_of) in enumerate(copies):
            cp = pltpu.make_async_copy(src_of(in_refs[ii], me), dst_of(out_refs[oi], me), loc_sems.at[ci])
            cp.start()
            local.append(cp)
        sends, recvs = [], []
        for k in range(1, N_DEV):
            pid, p = _peer(x, y, c, k)
            for ci, (ii, src_of, oi, dst_of) in enumerate(copies):
                sem = ci * (N_DEV - 1) + k - 1
                send = pltpu.make_async_remote_copy(
                    src_ref=src_of(in_refs[ii], p), dst_ref=dst_of(out_refs[oi], me),
                    send_sem=send_sems.at[sem], recv_sem=recv_sems.at[sem],
                    device_id=pid, device_id_type=MESH)
                send.start()
                sends.append(send)
                recvs.append(pltpu.make_async_remote_copy(
                    src_ref=src_of(in_refs[ii], p), dst_ref=dst_of(out_refs[oi], p),
                    send_sem=send_sems.at[sem], recv_sem=recv_sems.at[sem],
                    device_id=pid, device_id_type=MESH))
        for r in recvs:
            r.wait_recv()
        for s in sends:
            s.wait_send()
        for cp in local:
            cp.wait()

    any_spec = pl.BlockSpec(memory_space=pl.ANY)
    return pl.pallas_call(
        body, name=name,
        out_shape=tuple(out_shapes),
        in_specs=[any_spec] * n_in,
        out_specs=tuple([any_spec] * len(out_shapes)),
        scratch_shapes=[pltpu.SemaphoreType.DMA((n_cp * (N_DEV - 1),)),
                        pltpu.SemaphoreType.DMA((n_cp * (N_DEV - 1),)),
                        pltpu.SemaphoreType.DMA((n_cp,))],
    )(*ins)


def _exchange_start(name, ins, out_shapes, copies):
    n_in, n_out, n_cp = len(ins), len(out_shapes), len(copies)

    def body(*refs):
        in_refs = refs[:n_in]
        land_refs = refs[n_in:n_in + n_out]
        send_sems, recv_sems = refs[n_in + n_out:n_in + n_out + 2]
        token_ref = refs[2 * (n_in + n_out) + 2]
        loc_sems = refs[2 * (n_in + n_out) + 3]
        x, y, c, me = _position()
        local = []
        for ci, (ii, src_of, oi, dst_of) in enumerate(copies):
            cp = pltpu.make_async_copy(src_of(in_refs[ii], me), dst_of(land_refs[oi], me), loc_sems.at[ci])
            cp.start()
            local.append(cp)
        for k in range(1, N_DEV):
            pid, p = _peer(x, y, c, k)
            for ci, (ii, src_of, oi, dst_of) in enumerate(copies):
                sem = ci * (N_DEV - 1) + k - 1
                pltpu.make_async_remote_copy(
                    src_ref=src_of(in_refs[ii], p), dst_ref=dst_of(land_refs[oi], me),
                    send_sem=send_sems.at[sem], recv_sem=recv_sems.at[sem],
                    device_id=pid, device_id_type=MESH).start()
        for cp in local:
            cp.wait()
        token_ref[...] = jnp.zeros_like(token_ref)

    hbm = pl.BlockSpec(memory_space=pltpu.HBM)
    sem = pl.BlockSpec(memory_space=pltpu.SEMAPHORE)
    n_sem = n_cp * (N_DEV - 1)
    lands = [pltpu.with_memory_space_constraint(lax.empty(o.shape, o.dtype), pltpu.HBM) for o in out_shapes]
    srcs = [pltpu.with_memory_space_constraint(a, pltpu.HBM) for a in ins]
    res = pl.pallas_call(
        body, name=name,
        out_shape=(pltpu.SemaphoreType.DMA((n_sem,)), pltpu.SemaphoreType.DMA((n_sem,)),
                   *[pltpu.HBM(a.shape, a.dtype) for a in ins],
                   *[pltpu.HBM(o.shape, o.dtype) for o in out_shapes],
                   jax.ShapeDtypeStruct((8, LANES), F32)),
        in_specs=[hbm] * (n_in + n_out),
        out_specs=(sem, sem, *([hbm] * (n_in + n_out)), pl.BlockSpec(memory_space=pltpu.VMEM)),
        input_output_aliases={i: 2 + i for i in range(n_in + n_out)},
        scratch_shapes=[pltpu.SemaphoreType.DMA((n_cp,))],
        compiler_params=pltpu.CompilerParams(has_side_effects=pltpu.SideEffectType.DATAFLOW_SIDE_EFFECTING),
    )(*srcs, *lands)
    return dict(name=name, copies=copies, send=res[0], recv=res[1], srcs=list(res[2:2 + n_in]),
                lands=list(res[2 + n_in:2 + n_in + n_out]), token=res[-1])


def _exchange_wait(handle, after):
    copies, srcs, lands = handle["copies"], handle["srcs"], handle["lands"]
    n_in, n_out = len(srcs), len(lands)

    def body(*refs):
        in_refs = refs[:n_in]
        land_refs = refs[n_in:n_in + n_out]
        send_sems, recv_sems = refs[n_in + n_out:n_in + n_out + 2]
        x, y, c, me = _position()
        waits = []
        for k in range(1, N_DEV):
            pid, p = _peer(x, y, c, k)
            for ci, (ii, src_of, oi, dst_of) in enumerate(copies):
                sem = ci * (N_DEV - 1) + k - 1
                waits.append(pltpu.make_async_remote_copy(
                    src_ref=src_of(in_refs[ii], p), dst_ref=dst_of(land_refs[oi], p),
                    send_sem=send_sems.at[sem], recv_sem=recv_sems.at[sem],
                    device_id=pid, device_id_type=MESH))
        for w in waits:
            w.wait_send()
        for w in waits:
            w.wait_recv()

    hbm = pl.BlockSpec(memory_space=pltpu.HBM)
    sem = pl.BlockSpec(memory_space=pltpu.SEMAPHORE)
    res = pl.pallas_call(
        body, name=handle["name"] + "_wait",
        out_shape=tuple(pltpu.HBM(a.shape, a.dtype) for a in srcs + lands),
        in_specs=[hbm] * (n_in + n_out) + [sem, sem, pl.BlockSpec(memory_space=pl.ANY)],
        out_specs=tuple([hbm] * (n_in + n_out)),
        input_output_aliases={i: i for i in range(n_in + n_out)},
        compiler_params=pltpu.CompilerParams(has_side_effects=pltpu.SideEffectType.DATAFLOW_SIDE_EFFECTING),
    )(*srcs, *lands, handle["send"], handle["recv"], after)
    return list(res[n_in:])


def _gather_plan(stacked):
    ins, outs, copies = [], [], []
    for t in stacked:
        ii = len(ins)
        ins.append(t)
        for l in range(t.shape[0]):
            oi = len(outs)
            outs.append(jax.ShapeDtypeStruct((N_DEV,) + t.shape[1:], t.dtype))
            copies.append((ii, (lambda ref, p, l=l: ref.at[l]), oi, (lambda ref, s: ref.at[s])))
    return ins, outs, copies


def _scatter_plan(blocked):
    outs = [jax.ShapeDtypeStruct(t.shape, t.dtype) for t in blocked]
    copies = [(n, (lambda ref, p: ref.at[p]), n, (lambda ref, s: ref.at[s])) for n in range(len(blocked))]
    return list(blocked), outs, copies


def _all_gather_layers(name, stacked):
    ins, outs, copies = [], [], []
    for t in stacked:
        ii = len(ins)
        ins.append(t)
        for l in range(t.shape[0]):
            oi = len(outs)
            outs.append(jax.ShapeDtypeStruct((N_DEV,) + t.shape[1:], t.dtype))
            copies.append((ii, (lambda ref, p, l=l: ref.at[l]), oi, (lambda ref, s: ref.at[s])))
    res = _exchange(name, ins, outs, copies)
    out, pos = [], 0
    for t in stacked:
        out.append(list(res[pos:pos + t.shape[0]]))
        pos += t.shape[0]
    return out


def _all_to_all_layers(name, blocked):
    ins, outs, copies = [], [], []
    for layers in blocked:
        oi = len(outs)
        t0 = layers[0]
        outs.append(jax.ShapeDtypeStruct((N_DEV, len(layers)) + t0.shape[1:], t0.dtype))
        for l, t in enumerate(layers):
            ii = len(ins)
            ins.append(t)
            copies.append((ii, (lambda ref, p: ref.at[p]), oi, (lambda ref, s, l=l: ref.at[s, l])))
    return list(_exchange(name, ins, outs, copies))


def _mm(name, mode, a, b, out_shape, *, grid, a_spec, b_spec, o_spec, acc_shape, add=None, add_spec=None):
    nk = grid[2]
    dn = {"nn": NN, "nt": NT, "tn": TN}[mode]
    has_add = add is not None

    def body(*refs):
        if has_add:
            a_ref, b_ref, add_ref, o_ref = refs[:4]
        else:
            a_ref, b_ref, o_ref = refs[:3]
            add_ref = None
        prod = lax.dot_general(a_ref[...].astype(BF16), b_ref[...].astype(BF16), dn,
                               preferred_element_type=F32)

        def finish(r):
            if has_add:
                r = r + add_ref[...]
            o_ref[...] = r.astype(o_ref.dtype)

        if nk == 1:
            finish(prod)
        else:
            acc_ref = refs[-1]
            k = pl.program_id(2)

            @pl.when(k == 0)
            def _():
                acc_ref[...] = prod

            @pl.when(k > 0)
            def _():
                acc_ref[...] += prod

            @pl.when(k == nk - 1)
            def _():
                finish(acc_ref[...])

    ins = [a, b] + ([add] if has_add else [])
    in_specs = [a_spec, b_spec] + ([add_spec] if has_add else [])
    scratch = [] if nk == 1 else [pltpu.VMEM(acc_shape, F32)]
    return pl.pallas_call(
        body, name=name, grid=grid, out_shape=out_shape,
        in_specs=in_specs, out_specs=o_spec, scratch_shapes=scratch,
        compiler_params=_params("parallel", "parallel", "arbitrary"),
    )(*ins)


def _mm_nn(name, a, b, out_dtype, add=None, tm=1024, tn=1024, tk=2048):
    m, kd = a.shape
    n = b.shape[1]
    tm, tn, tk = _tile(m, tm, 16), _tile(n, tn, LANES), _tile(kd, tk, LANES)
    return _mm(name, "nn", a, b, jax.ShapeDtypeStruct((m, n), out_dtype),
               grid=(m // tm, n // tn, kd // tk),
               a_spec=pl.BlockSpec((tm, tk), lambda i, j, k: (i, k)),
               b_spec=pl.BlockSpec((tk, tn), lambda i, j, k: (k, j)),
               o_spec=pl.BlockSpec((tm, tn), lambda i, j, k: (i, j)),
               acc_shape=(tm, tn), add=add,
               add_spec=pl.BlockSpec((tm, tn), lambda i, j, k: (i, j)))


def _mm_nt(name, a, b, out_dtype, tm=1024, tn=1024, tk=2048):
    m, kd = a.shape
    n = b.shape[0]
    tm, tn, tk = _tile(m, tm, 16), _tile(n, tn, LANES), _tile(kd, tk, LANES)
    return _mm(name, "nt", a, b, jax.ShapeDtypeStruct((m, n), out_dtype),
               grid=(m // tm, n // tn, kd // tk),
               a_spec=pl.BlockSpec((tm, tk), lambda i, j, k: (i, k)),
               b_spec=pl.BlockSpec((tn, tk), lambda i, j, k: (j, k)),
               o_spec=pl.BlockSpec((tm, tn), lambda i, j, k: (i, j)),
               acc_shape=(tm, tn))


def _mm_tn(name, a, b, out_dtype, tm=1024, tn=1024, ts=1024):
    s, m = a.shape
    n = b.shape[1]
    tm, tn, ts = _tile(m, tm, LANES), _tile(n, tn, LANES), _tile(s, ts, 16)
    return _mm(name, "tn", a, b, jax.ShapeDtypeStruct((m, n), out_dtype),
               grid=(m // tm, n // tn, s // ts),
               a_spec=pl.BlockSpec((ts, tm), lambda i, j, k: (k, i)),
               b_spec=pl.BlockSpec((ts, tn), lambda i, j, k: (k, j)),
               o_spec=pl.BlockSpec((tm, tn), lambda i, j, k: (i, j)),
               acc_shape=(tm, tn))


def _rms_fwd(name, x, g):
    s, d = x.shape
    tm = _tile(s, 512, 16)

    def body(x_ref, g_ref, h_ref):
        xv = x_ref[...]
        r = lax.rsqrt(jnp.mean(xv * xv, axis=-1, keepdims=True) + RMS_EPS)
        h_ref[...] = ((xv * r) * g_ref[...]).astype(BF16)

    return pl.pallas_call(
        body, name=name, grid=(s // tm,), out_shape=jax.ShapeDtypeStruct((s, d), BF16),
        in_specs=[pl.BlockSpec((tm, d), lambda i: (i, 0)), pl.BlockSpec((1, d), lambda i: (0, 0))],
        out_specs=pl.BlockSpec((tm, d), lambda i: (i, 0)),
        compiler_params=_params("parallel"),
    )(x, g)


def _rms_bwd(name, dh, x, g, dres):
    s, d = x.shape
    tm = _tile(s, 256, 16)

    def body(dh_ref, x_ref, g_ref, dres_ref, dx_ref, dg_ref):
        i = pl.program_id(0)
        xv = x_ref[...]
        r = lax.rsqrt(jnp.mean(xv * xv, axis=-1, keepdims=True) + RMS_EPS)
        xhat = xv * r
        dhv = dh_ref[...].astype(F32)
        gdh = dhv * g_ref[...]
        dx_ref[...] = dres_ref[...] + r * (gdh - xhat * jnp.mean(gdh * xhat, axis=-1, keepdims=True))
        part = jnp.sum(dhv * xhat, axis=0, keepdims=True)

        @pl.when(i == 0)
        def _():
            dg_ref[...] = part

        @pl.when(i > 0)
        def _():
            dg_ref[...] += part

    row = pl.BlockSpec((tm, d), lambda i: (i, 0))
    vec = pl.BlockSpec((1, d), lambda i: (0, 0))
    return pl.pallas_call(
        body, name=name, grid=(s // tm,),
        out_shape=(jax.ShapeDtypeStruct((s, d), F32), jax.ShapeDtypeStruct((1, d), F32)),
        in_specs=[row, row, vec, row], out_specs=(row, vec),
        compiler_params=_params("arbitrary"),
    )(dh, x, g, dres)


def _split3(v):
    hi = v.astype(BF16)
    r1 = v - hi.astype(F32)
    mid = r1.astype(BF16)
    lo = (r1 - mid.astype(F32)).astype(BF16)
    return hi, mid, lo


def _tri_sum(tri, v):
    hi, mid, lo = _split3(v)
    dot = functools.partial(lax.dot_general, dimension_numbers=NN, preferred_element_type=F32)
    return dot(tri, hi) + dot(tri, mid) + dot(tri, lo)


def _gate_fwd(name, flog, b_pad):
    s = flog.shape[0]
    tb = _tile(s, 256, 16)

    def body(f_ref, b_ref, c_ref, carry_ref):
        i = pl.program_id(0)

        @pl.when(i == 0)
        def _():
            carry_ref[...] = jnp.zeros_like(carry_ref)

        z = f_ref[...] + b_ref[...]
        lf = jnp.minimum(z, 0.0) - jnp.log(1.0 + jnp.exp(-jnp.abs(z)))
        rows = lax.broadcasted_iota(jnp.int32, (tb, tb), 0)
        cols = lax.broadcasted_iota(jnp.int32, (tb, tb), 1)
        tri = (rows >= cols).astype(BF16)
        c_ref[...] = _tri_sum(tri, lf) + carry_ref[...]
        carry_ref[...] = c_ref[pl.ds(tb - 1, 1), :]

    return pl.pallas_call(
        body, name=name, grid=(s // tb,), out_shape=jax.ShapeDtypeStruct((s, LANES), F32),
        in_specs=[pl.BlockSpec((tb, LANES), lambda i: (i, 0)), pl.BlockSpec((1, LANES), lambda i: (0, 0))],
        out_specs=pl.BlockSpec((tb, LANES), lambda i: (i, 0)),
        scratch_shapes=[pltpu.VMEM((1, LANES), F32)],
        compiler_params=_params("arbitrary"),
    )(flog, b_pad)


def _gate_bwd(name, dck, dcq, flog, b_pad, n_heads):
    s = flog.shape[0]
    tb = _tile(s, 256, 16)
    nb = s // tb

    def body(dck_ref, dcq_ref, f_ref, b_ref, df_ref, db_ref, carry_ref, tmp_ref):
        i = pl.program_id(0)

        @pl.when(i == 0)
        def _():
            carry_ref[...] = jnp.zeros_like(carry_ref)

        rows = lax.broadcasted_iota(jnp.int32, (tb, tb), 0)
        cols = lax.broadcasted_iota(jnp.int32, (tb, tb), 1)
        tri = (rows <= cols).astype(BF16)
        tmp_ref[...] = _tri_sum(tri, dck_ref[...] + dcq_ref[...]) + carry_ref[...]
        carry_ref[...] = tmp_ref[pl.ds(0, 1), :]
        z = f_ref[...] + b_ref[...]
        lane = lax.broadcasted_iota(jnp.int32, (tb, LANES), 1)
        df = jnp.where(lane < n_heads, tmp_ref[...] / (1.0 + jnp.exp(z)), 0.0)
        df_ref[...] = df.astype(BF16)
        part = jnp.sum(df, axis=0, keepdims=True)

        @pl.when(i == 0)
        def _():
            db_ref[...] = part

        @pl.when(i > 0)
        def _():
            db_ref[...] += part

    rev = pl.BlockSpec((tb, LANES), lambda i: (nb - 1 - i, 0))
    vec = pl.BlockSpec((1, LANES), lambda i: (0, 0))
    return pl.pallas_call(
        body, name=name, grid=(nb,),
        out_shape=(jax.ShapeDtypeStruct((s, LANES), BF16), jax.ShapeDtypeStruct((1, LANES), F32)),
        in_specs=[rev, rev, rev, vec], out_specs=(rev, vec),
        scratch_shapes=[pltpu.VMEM((1, LANES), F32), pltpu.VMEM((tb, LANES), F32)],
        compiler_params=_params("arbitrary"),
    )(dck, dcq, flog, b_pad)


def _head_rms(v, g):
    r = lax.rsqrt(jnp.mean(v * v, axis=-1, keepdims=True) + RMS_EPS)
    return (v * r) * g


def _qkv_fwd(name, proj, gq, gk, d):
    s = proj.shape[0]
    tm = _tile(s, 256, 16)
    n_heads = d // HEAD_DIM

    def body(q_ref, k_ref, v_ref, gq_ref, gk_ref, qn_ref, kn_ref, vb_ref):
        for h in range(n_heads):
            sl = slice(h * HEAD_DIM, (h + 1) * HEAD_DIM)
            qn_ref[:, sl] = _head_rms(q_ref[:, sl], gq_ref[...]).astype(BF16)
            kn_ref[:, sl] = _head_rms(k_ref[:, sl], gk_ref[...]).astype(BF16)
        vb_ref[...] = v_ref[...].astype(BF16)

    col = lambda c: pl.BlockSpec((tm, d), lambda i, c=c: (i, c))
    vec = pl.BlockSpec((1, HEAD_DIM), lambda i: (0, 0))
    out = jax.ShapeDtypeStruct((s, d), BF16)
    return pl.pallas_call(
        body, name=name, grid=(s // tm,), out_shape=(out, out, out),
        in_specs=[col(0), col(1), col(2), vec, vec], out_specs=(col(0), col(0), col(0)),
        compiler_params=_params("parallel"),
    )(proj, proj, proj, gq, gk)


def _qkv_bwd(name, proj, dqn, dkn, dv, dflog, gq, gk, d, n_pad):
    s = proj.shape[0]
    tm = _tile(s, 256, 16)
    n_heads = d // HEAD_DIM

    def head_bwd(raw, dy, g):
        r = lax.rsqrt(jnp.mean(raw * raw, axis=-1, keepdims=True) + RMS_EPS)
        hat = raw * r
        gdy = dy * g
        dx = r * (gdy - hat * jnp.mean(gdy * hat, axis=-1, keepdims=True))
        return dx, jnp.sum(dy * hat, axis=0, keepdims=True)

    def body(q_ref, k_ref, dqn_ref, dkn_ref, dv_ref, df_ref, gq_ref, gk_ref, dp_ref, dgq_ref, dgk_ref):
        i = pl.program_id(0)
        accq = jnp.zeros((1, HEAD_DIM), F32)
        acck = jnp.zeros((1, HEAD_DIM), F32)
        for h in range(n_heads):
            sl = slice(h * HEAD_DIM, (h + 1) * HEAD_DIM)
            dq, pq = head_bwd(q_ref[:, sl], dqn_ref[:, sl], gq_ref[...])
            dk, pk = head_bwd(k_ref[:, sl], dkn_ref[:, sl], gk_ref[...])
            dp_ref[:, sl] = dq.astype(BF16)
            dp_ref[:, d + h * HEAD_DIM:d + (h + 1) * HEAD_DIM] = dk.astype(BF16)
            accq, acck = accq + pq, acck + pk
        dp_ref[:, 2 * d:3 * d] = dv_ref[...]
        dp_ref[:, 3 * d:] = df_ref[...]

        @pl.when(i == 0)
        def _():
            dgq_ref[...] = accq
            dgk_ref[...] = acck

        @pl.when(i > 0)
        def _():
            dgq_ref[...] += accq
            dgk_ref[...] += acck

    col = lambda c: pl.BlockSpec((tm, d), lambda i, c=c: (i, c))
    vec = pl.BlockSpec((1, HEAD_DIM), lambda i: (0, 0))
    return pl.pallas_call(
        body, name=name, grid=(s // tm,),
        out_shape=(jax.ShapeDtypeStruct((s, n_pad), BF16), jax.ShapeDtypeStruct((1, HEAD_DIM), F32),
                   jax.ShapeDtypeStruct((1, HEAD_DIM), F32)),
        in_specs=[col(0), col(1), col(0), col(0), col(0), pl.BlockSpec((tm, LANES), lambda i: (i, 0)), vec, vec],
        out_specs=(pl.BlockSpec((tm, n_pad), lambda i: (i, 0)), vec, vec),
        compiler_params=_params("arbitrary"),
    )(proj, proj, dqn, dkn, dv, dflog, gq, gk)


def _attn_fwd(name, qn, kn, vt, c_row, c_col):
    s, d = qn.shape
    n_heads = d // HEAD_DIM
    t = _tile(s, 512, LANES)
    scale = HEAD_DIM ** -0.5

    def body(q_ref, k_ref, vt_ref, cq_ref, ck_ref, o_ref, lse_ref, m_ref, l_ref, acc_ref):
        i = pl.program_id(1)
        q = q_ref[...]
        cq = cq_ref[...]
        m_ref[...] = jnp.full(m_ref.shape, NEG_INF, F32)
        l_ref[...] = jnp.zeros_like(l_ref)
        acc_ref[...] = jnp.zeros_like(acc_ref)

        def step(j, masked):
            start = pl.multiple_of(j * t, t)
            kj = k_ref[pl.ds(start, t), :]
            vtj = vt_ref[:, pl.ds(start, t)]
            ckj = ck_ref[pl.ds(start, t), :]
            st = lax.dot_general(kj, q, NT, preferred_element_type=F32) * scale + (cq - ckj)
            if masked:
                rows = lax.broadcasted_iota(jnp.int32, (t, t), 0)
                cols = lax.broadcasted_iota(jnp.int32, (t, t), 1)
                st = jnp.where(cols >= rows, st, NEG_INF)
            m_prev = m_ref[...]
            m_new = jnp.maximum(m_prev, jnp.max(st, axis=0, keepdims=True))
            pt = jnp.exp(st - m_new)
            alpha = jnp.exp(m_prev - m_new)
            l_ref[...] = alpha * l_ref[...] + jnp.sum(pt, axis=0, keepdims=True)
            acc_ref[...] = alpha * acc_ref[...] + lax.dot_general(
                vtj, pt.astype(BF16), NN, preferred_element_type=F32)
            m_ref[...] = m_new

        def loop_body(j, carry):
            step(j, False)
            return carry

        lax.fori_loop(0, i, loop_body, 0)
        step(i, True)
        o_ref[...] = (acc_ref[...] / l_ref[...]).T.astype(BF16)
        lse_ref[...] = m_ref[...] + jnp.log(l_ref[...])

    row_blk = pl.BlockSpec((None, 1, t), lambda h, i: (h, 0, i))
    return pl.pallas_call(
        body, name=name, grid=(n_heads, s // t),
        out_shape=(jax.ShapeDtypeStruct((s, d), BF16), jax.ShapeDtypeStruct((n_heads, 1, s), F32)),
        in_specs=[pl.BlockSpec((t, HEAD_DIM), lambda h, i: (i, h)),
                  pl.BlockSpec((s, HEAD_DIM), lambda h, i: (0, h)),
                  pl.BlockSpec((HEAD_DIM, s), lambda h, i: (h, 0)),
                  row_blk, pl.BlockSpec((None, s, 1), lambda h, i: (h, 0, 0))],
        out_specs=(pl.BlockSpec((t, HEAD_DIM), lambda h, i: (i, h)), row_blk),
        scratch_shapes=[pltpu.VMEM((1, t), F32), pltpu.VMEM((1, t), F32), pltpu.VMEM((HEAD_DIM, t), F32)],
        compiler_params=_params("parallel", "arbitrary"),
    )(qn, kn, vt, c_row, c_col)


def _attn_delta(name, o, do, n_heads):
    s, d = o.shape
    tm = _tile(s, 256, 16)

    def body(o_ref, do_ref, dl_ref):
        lane = lax.broadcasted_iota(jnp.int32, (tm, LANES), 1)
        acc = jnp.zeros((tm, LANES), F32)
        for h in range(n_heads):
            sl = slice(h * HEAD_DIM, (h + 1) * HEAD_DIM)
            col = jnp.sum(o_ref[:, sl].astype(F32) * do_ref[:, sl].astype(F32), axis=-1, keepdims=True)
            acc = jnp.where(lane == h, col, acc)
        dl_ref[...] = acc

    row = pl.BlockSpec((tm, d), lambda i: (i, 0))
    return pl.pallas_call(
        body, name=name, grid=(s // tm,), out_shape=jax.ShapeDtypeStruct((s, LANES), F32),
        in_specs=[row, row], out_specs=pl.BlockSpec((tm, LANES), lambda i: (i, 0)),
        compiler_params=_params("parallel"),
    )(o, do)


def _attn_bwd(name, qn, kn, vb, do, c_row, lse_row, delta_row, c_col):
    s, d = qn.shape
    n_heads = d // HEAD_DIM
    t = _tile(s, 512, LANES)
    nq = s // t
    scale = HEAD_DIM ** -0.5

    def body(q_ref, do_ref, cr_ref, lse_ref, dl_ref, k_ref, v_ref, ck_ref,
             dq_ref, dk_ref, dv_ref, dc_ref, dcq_ref, dk_acc, dv_acc, dc_acc):
        j = pl.program_id(1)

        @pl.when(j == 0)
        def _():
            dq_ref[...] = jnp.zeros_like(dq_ref)
            dcq_ref[...] = jnp.zeros_like(dcq_ref)

        kj = k_ref[...]
        vj = v_ref[...]
        ckj = ck_ref[...]
        dk_acc[...] = jnp.zeros_like(dk_acc)
        dv_acc[...] = jnp.zeros_like(dv_acc)
        dc_acc[...] = jnp.zeros_like(dc_acc)

        def step(i, masked):
            start = pl.multiple_of(i * t, t)
            qi = q_ref[pl.ds(start, t), :]
            doi = do_ref[pl.ds(start, t), :]
            bias = cr_ref[:, pl.ds(start, t)] - lse_ref[:, pl.ds(start, t)]
            dli = dl_ref[:, pl.ds(start, t)]
            st = lax.dot_general(kj, qi, NT, preferred_element_type=F32) * scale + (bias - ckj)
            if masked:
                rows = lax.broadcasted_iota(jnp.int32, (t, t), 0)
                cols = lax.broadcasted_iota(jnp.int32, (t, t), 1)
                st = jnp.where(cols >= rows, st, NEG_INF)
            pt = jnp.exp(st)
            dpt = lax.dot_general(vj, doi, NT, preferred_element_type=F32)
            dst = pt * (dpt - dli)
            dsb = dst.astype(BF16)
            dv_acc[...] += lax.dot_general(pt.astype(BF16), doi, NN, preferred_element_type=F32)
            dk_acc[...] += lax.dot_general(dsb, qi, NN, preferred_element_type=F32)
            dq_ref[pl.ds(start, t), :] += lax.dot_general(dsb, kj, TN, preferred_element_type=F32) * scale
            dc_acc[...] += jnp.sum(dst, axis=1, keepdims=True)
            dcq_ref[:, pl.ds(start, t)] += jnp.sum(dst, axis=0, keepdims=True)

        step(j, True)

        def loop_body(i, carry):
            step(i, False)
            return carry

        lax.fori_loop(j + 1, nq, loop_body, 0)
        dk_ref[...] = dk_acc[...] * scale
        dv_ref[...] = dv_acc[...].astype(BF16)
        dc_ref[...] = -dc_acc[...]

    head_all = pl.BlockSpec((s, HEAD_DIM), lambda h, j: (0, h))
    row_all = pl.BlockSpec((None, 1, s), lambda h, j: (h, 0, 0))
    blk = pl.BlockSpec((t, HEAD_DIM), lambda h, j: (j, h))
    col_blk = pl.BlockSpec((None, t, 1), lambda h, j: (h, j, 0))
    return pl.pallas_call(
        body, name=name, grid=(n_heads, nq),
        out_shape=(jax.ShapeDtypeStruct((s, d), F32), jax.ShapeDtypeStruct((s, d), F32),
                   jax.ShapeDtypeStruct((s, d), BF16), jax.ShapeDtypeStruct((n_heads, s, 1), F32),
                   jax.ShapeDtypeStruct((n_heads, 1, s), F32)),
        in_specs=[head_all, head_all, row_all, row_all, row_all, blk, blk, col_blk],
        out_specs=(head_all, blk, blk, col_blk, row_all),
        scratch_shapes=[pltpu.VMEM((t, HEAD_DIM), F32), pltpu.VMEM((t, HEAD_DIM), F32), pltpu.VMEM((t, 1), F32)],
        compiler_params=_params("parallel", "arbitrary"),
    )(qn, do, c_row, lse_row, delta_row, kn, vb, c_col)


def _ffn_up(name, h, w_gu):
    s, d = h.shape
    fs = w_gu.shape[2]
    half = N_DEV // 2
    tm = _tile(s, 512, 16)

    def body(h_ref, wg_ref, wu_ref, g_ref, u_ref, a_ref):
        hv = h_ref[...]
        g = lax.dot_general(hv, wg_ref[...], NN, preferred_element_type=F32)
        u = lax.dot_general(hv, wu_ref[...], NN, preferred_element_type=F32)
        g_ref[...] = g.astype(BF16)
        u_ref[...] = u.astype(BF16)
        a_ref[...] = (g * jax.nn.sigmoid(g) * u).astype(BF16)

    out = jax.ShapeDtypeStruct((s, half * fs), BF16)
    ospec = pl.BlockSpec((tm, fs), lambda j, i: (i, j))
    return pl.pallas_call(
        body, name=name, grid=(half, s // tm), out_shape=(out, out, out),
        in_specs=[pl.BlockSpec((tm, d), lambda j, i: (i, 0)),
                  pl.BlockSpec((None, d, fs), lambda j, i: (j, 0, 0)),
                  pl.BlockSpec((None, d, fs), lambda j, i: (j + half, 0, 0))],
        out_specs=(ospec, ospec, ospec),
        compiler_params=_params("parallel", "parallel"),
    )(h, w_gu, w_gu)


def _ffn_dact(name, dx, w_dn4, g, u):
    s, d = dx.shape
    half, fs = w_dn4.shape[0], w_dn4.shape[1]
    tm = _tile(s, 512, 16)

    def body(dx_ref, w_ref, g_ref, u_ref, dgu_ref):
        da = lax.dot_general(dx_ref[...].astype(BF16), w_ref[...], NT, preferred_element_type=F32)
        gv = g_ref[...].astype(F32)
        uv = u_ref[...].astype(F32)
        sig = jax.nn.sigmoid(gv)
        dgu_ref[0] = (da * uv * (sig * (1.0 + gv * (1.0 - sig)))).astype(BF16)
        dgu_ref[1] = (da * (gv * sig)).astype(BF16)

    blk = pl.BlockSpec((tm, fs), lambda j, i: (i, j))
    return pl.pallas_call(
        body, name=name, grid=(half, s // tm),
        out_shape=jax.ShapeDtypeStruct((2, s, half * fs), BF16),
        in_specs=[pl.BlockSpec((tm, d), lambda j, i: (i, 0)),
                  pl.BlockSpec((None, fs, d), lambda j, i: (j, 0, 0)), blk, blk],
        out_specs=pl.BlockSpec((2, tm, fs), lambda j, i: (0, i, j)),
        compiler_params=_params("parallel", "parallel"),
    )(dx, w_dn4, g, u)


def _ffn_dw_gu(name, h, dgu):
    s, d = h.shape
    half, fs = N_DEV // 2, dgu.shape[2] // (N_DEV // 2)
    tm, ts = _tile(d, 1024, LANES), _tile(s, 1024, 16)
    return _mm(name, "tn", h, dgu, jax.ShapeDtypeStruct((N_DEV, d, fs), BF16),
               grid=(d // tm, N_DEV, s // ts),
               a_spec=pl.BlockSpec((ts, tm), lambda i, j, k: (k, i)),
               b_spec=pl.BlockSpec((None, ts, fs), lambda i, j, k: (j // half, k, j % half)),
               o_spec=pl.BlockSpec((None, tm, fs), lambda i, j, k: (j, i, 0)),
               acc_shape=(tm, fs))


def _ffn_dh(name, dgu, w_gu):
    s = dgu.shape[1]
    d, fs = w_gu.shape[1], w_gu.shape[2]
    half = N_DEV // 2
    tm = _tile(s, 512, 16)
    return _mm(name, "nt", dgu, w_gu, jax.ShapeDtypeStruct((s, d), F32),
               grid=(s // tm, 1, N_DEV),
               a_spec=pl.BlockSpec((None, tm, fs), lambda i, j, k: (k // half, i, k % half)),
               b_spec=pl.BlockSpec((None, d, fs), lambda i, j, k: (k, 0, 0)),
               o_spec=pl.BlockSpec((tm, d), lambda i, j, k: (i, 0)),
               acc_shape=(tm, d))


def _pool_fwd(name, x, g, w, b, sc):
    s, d = x.shape
    dg = d // len(POOL_WINDOWS)
    tm = _tile(s, 256, POOL_HALO)
    per = tm // POOL_HALO

    def body(x_ref, xh_ref, g_ref, w_ref, b_ref, sc_ref, xo_ref, y_ref, zb_ref):
        i = pl.program_id(0)
        gv = g_ref[...]

        def norm(v):
            return (v * lax.rsqrt(jnp.mean(v * v, axis=-1, keepdims=True) + RMS_EPS)) * gv

        h = norm(x_ref[...])
        halo = norm(xh_ref[...]) * (i > 0).astype(F32)
        ext = jnp.concatenate([halo, h], axis=0)
        t = i * tm + lax.broadcasted_iota(jnp.int32, (tm, 1), 0)
        for gi, win in enumerate(POOL_WINDOWS):
            sl = slice(gi * dg, (gi + 1) * dg)
            acc = ext[:, sl]
            step = 1
            while step < win:
                acc = acc + pltpu.roll(acc, step, 0)
                step *= 2
            inv = 1.0 / jnp.minimum(t + 1, win).astype(F32)
            yg = (acc[POOL_HALO:, :] * inv - h[:, sl]).astype(BF16)
            y_ref[:, sl] = yg
            zb = lax.dot_general(yg, w_ref[gi], NN, preferred_element_type=F32) + b_ref[:, sl]
            zb_ref[:, sl] = zb
            xo_ref[:, sl] = x_ref[:, sl] + zb * sc_ref[:, sl]

    row = pl.BlockSpec((tm, d), lambda i: (i, 0))
    vec = pl.BlockSpec((1, d), lambda i: (0, 0))
    return pl.pallas_call(
        body, name=name, grid=(s // tm,),
        out_shape=(jax.ShapeDtypeStruct((s, d), F32), jax.ShapeDtypeStruct((s, d), BF16),
                   jax.ShapeDtypeStruct((s, d), F32)),
        in_specs=[row, pl.BlockSpec((POOL_HALO, d), lambda i: (jnp.maximum(i * per - 1, 0), 0)),
                  vec, pl.BlockSpec(w.shape, lambda i: (0, 0, 0)), vec, vec],
        out_specs=(row, row, row),
        compiler_params=_params("parallel"),
    )(x, x, g, w, b, sc)


def _pool_bwd(name, dout, x, zb, g, w, sc):
    s, d = x.shape
    dg = d // len(POOL_WINDOWS)
    tm = _tile(s, 256, POOL_HALO)
    per = tm // POOL_HALO
    nb = s // tm
    ext_rows = tm + POOL_HALO

    def body(do_ref, doh_ref, x_ref, zb_ref, g_ref, w_ref, sc_ref, dx_ref, dz_ref, dgn_ref, dsc_ref, db_ref):
        i = pl.program_id(0)
        scv = sc_ref[...]
        dov = do_ref[...]
        dz = dov * scv
        dz_ref[...] = dz.astype(BF16)
        halo = doh_ref[...] * scv * (i < nb - 1).astype(F32)
        ext = jnp.concatenate([dz, halo], axis=0).astype(BF16)
        t = i * tm + lax.broadcasted_iota(jnp.int32, (ext_rows, 1), 0)
        parts = []
        for gi, win in enumerate(POOL_WINDOWS):
            sl = slice(gi * dg, (gi + 1) * dg)
            dy = lax.dot_general(ext[:, sl], w_ref[gi], NT, preferred_element_type=F32)
            acc = dy * (1.0 / jnp.minimum(t + 1, win).astype(F32))
            step = 1
            while step < win:
                acc = acc + pltpu.roll(acc, ext_rows - step, 0)
                step *= 2
            parts.append(acc[:tm, :] - dy[:tm, :])
        dh = jnp.concatenate(parts, axis=1)
        xv = x_ref[...]
        r = lax.rsqrt(jnp.mean(xv * xv, axis=-1, keepdims=True) + RMS_EPS)
        xhat = xv * r
        gdh = dh * g_ref[...]
        dx_ref[...] = dov + r * (gdh - xhat * jnp.mean(gdh * xhat, axis=-1, keepdims=True))
        pgn = jnp.sum(dh * xhat, axis=0, keepdims=True)
        psc = jnp.sum(dov * zb_ref[...], axis=0, keepdims=True)
        pb = jnp.sum(dz, axis=0, keepdims=True)

        @pl.when(i == 0)
        def _():
            dgn_ref[...] = pgn
            dsc_ref[...] = psc
            db_ref[...] = pb

        @pl.when(i > 0)
        def _():
            dgn_ref[...] += pgn
            dsc_ref[...] += psc
            db_ref[...] += pb

    row = pl.BlockSpec((tm, d), lambda i: (i, 0))
    vec = pl.BlockSpec((1, d), lambda i: (0, 0))
    vshape = jax.ShapeDtypeStruct((1, d), F32)
    return pl.pallas_call(
        body, name=name, grid=(nb,),
        out_shape=(jax.ShapeDtypeStruct((s, d), F32), jax.ShapeDtypeStruct((s, d), BF16), vshape, vshape, vshape),
        in_specs=[row, pl.BlockSpec((POOL_HALO, d), lambda i: (jnp.minimum((i + 1) * per, s // POOL_HALO - 1), 0)),
                  row, row, vec, pl.BlockSpec(w.shape, lambda i: (0, 0, 0)), vec],
        out_specs=(row, row, vec, vec, vec),
        compiler_params=_params("arbitrary"),
    )(dout, dout, x, zb, g, w, sc)


def _pool_dw(name, y, dz, n_groups):
    s, d = y.shape
    dg = d // n_groups
    ts = _tile(s, 1024, 16)
    return _mm(name, "tn", y, dz, jax.ShapeDtypeStruct((n_groups, dg, dg), F32),
               grid=(n_groups, 1, s // ts),
               a_spec=pl.BlockSpec((ts, dg), lambda i, j, k: (k, i)),
               b_spec=pl.BlockSpec((ts, dg), lambda i, j, k: (k, i)),
               o_spec=pl.BlockSpec((None, dg, dg), lambda i, j, k: (i, 0, 0)),
               acc_shape=(dg, dg))


def _loss_head(name, y, tgt):
    s, d = y.shape
    tm = _tile(s, 512, 16)

    def body(y_ref, t_ref, dy_ref, l_ref):
        i = pl.program_id(0)
        e = y_ref[...] - t_ref[...]
        dy_ref[...] = e * (1.0 / d)
        part = jnp.sum(jnp.mean(e * e, axis=-1, keepdims=True), axis=0, keepdims=True)
        part = jnp.broadcast_to(part, l_ref.shape)

        @pl.when(i == 0)
        def _():
            l_ref[...] = part

        @pl.when(i > 0)
        def _():
            l_ref[...] += part

    row = pl.BlockSpec((tm, d), lambda i: (i, 0))
    return pl.pallas_call(
        body, name=name, grid=(s // tm,),
        out_shape=(jax.ShapeDtypeStruct((s, d), F32), jax.ShapeDtypeStruct((8, LANES), F32)),
        in_specs=[row, row], out_specs=(row, pl.BlockSpec((8, LANES), lambda i: (0, 0))),
        compiler_params=_params("arbitrary"),
    )(y, tgt)


def _adam_update(w_ref, m_ref, v_ref, p_ref, g_ref, d_ref, nm_ref, nv_ref):
    g = p_ref[0].astype(F32)
    for k in range(1, N_DEV):
        g = g + p_ref[k].astype(F32)
    mn = ADAM_B1 * m_ref[...] + (1.0 - ADAM_B1) * g
    vn = ADAM_B2 * v_ref[...] + (1.0 - ADAM_B2) * (g * g)
    m_hat = mn / (1.0 - ADAM_B1 ** ADAM_STEP)
    v_hat = vn / (1.0 - ADAM_B2 ** ADAM_STEP)
    g_ref[...] = g
    d_ref[...] = -ADAM_LR * (m_hat / (jnp.sqrt(v_hat) + ADAM_EPS) + ADAM_WD * w_ref[...])
    nm_ref[...] = mn
    nv_ref[...] = vn


def _adamw_layers(name, w, m, v, pieces):
    n_layers, r, c = w.shape
    tr = _tile(r, 128, 16)

    def body(w_ref, m_ref, v_ref, *rest):
        p_refs, outs = rest[:n_layers], rest[n_layers:]
        layer = pl.program_id(0)
        for l in range(n_layers):
            @pl.when(layer == l)
            def _(l=l):
                _adam_update(w_ref, m_ref, v_ref, p_refs[l], *outs)

    blk = pl.BlockSpec((None, tr, c), lambda l, i: (l, i, 0))
    terms = [pl.BlockSpec((N_DEV, tr, c), lambda l, i, n=n: (0, jnp.where(l == n, i, 0), 0))
             for n in range(n_layers)]
    out = jax.ShapeDtypeStruct(w.shape, F32)
    return list(pl.pallas_call(
        body, name=name, grid=(n_layers, r // tr), out_shape=(out, out, out, out),
        in_specs=[blk, blk, blk] + terms, out_specs=(blk, blk, blk, blk),
        compiler_params=_params("parallel", "parallel"),
    )(w, m, v, *pieces))


def _adamw(name, w, m, v, pieces):
    r, c = w.shape
    tr = _tile(r, 128, 16)

    def body(w_ref, m_ref, v_ref, p_ref, g_ref, d_ref, nm_ref, nv_ref):
        _adam_update(w_ref, m_ref, v_ref, p_ref, g_ref, d_ref, nm_ref, nv_ref)

    blk = pl.BlockSpec((tr, c), lambda i: (i, 0))
    out = jax.ShapeDtypeStruct((r, c), F32)
    return pl.pallas_call(
        body, name=name, grid=(r // tr,), out_shape=(out, out, out, out),
        in_specs=[blk, blk, blk, pl.BlockSpec((N_DEV, tr, c), lambda i: (0, i, 0))],
        out_specs=(blk, blk, blk, blk),
        compiler_params=_params("parallel"),
    )(w, m, v, pieces)


def _adamw_nd(name, w, m, v, pieces):
    c = w.shape[-1]
    outs = _adamw(name, w.reshape(-1, c), m.reshape(-1, c), v.reshape(-1, c), pieces.reshape(N_DEV, -1, c))
    return [o.reshape(w.shape) for o in outs]


def _pack_small(mix, ffn, b_f, gq, gk):
    def rows(a):
        a = a.reshape(-1, LANES) if a.shape[-1] >= LANES else jnp.pad(a, ((0, 0), (0, LANES - a.shape[-1])))
        return jnp.pad(a, ((0, -a.shape[0] % 8), (0, 0)))
    return jnp.concatenate([rows(mix), rows(ffn), rows(b_f), rows(gq), rows(gk)], axis=0)


def _unpack_small(p, mix, ffn, b_f, gq, gk):
    out, pos = [], 0
    for a in (mix, ffn, b_f, gq, gk):
        n = a.size // LANES if a.shape[-1] >= LANES else a.shape[0]
        blk = p[pos:pos + n]
        out.append(blk.reshape(a.shape) if a.shape[-1] >= LANES else blk[:, :a.shape[-1]])
        pos += n + (-n % 8)
    return out


def kernel(x, mix_norm_g, ffn_norm_g, fox_w_in, fox_b_f, fox_q_norm_g, fox_k_norm_g, fox_w_out, pool_w, pool_b, pool_scale, ffn_w_gate_up, ffn_w_down, loss_target, m_mix_norm_g, m_ffn_norm_g, m_fox_w_in, m_fox_b_f, m_fox_q_norm_g, m_fox_k_norm_g, m_fox_w_out, m_pool_w, m_pool_b, m_pool_scale, m_ffn_w_gate_up, m_ffn_w_down, v_mix_norm_g, v_ffn_norm_g, v_fox_w_in, v_fox_b_f, v_fox_q_norm_g, v_fox_k_norm_g, v_fox_w_out, v_pool_w, v_pool_b, v_pool_scale, v_ffn_w_gate_up, v_ffn_w_down):
    xs, tgt = x[0], loss_target[0]
    s, d = xs.shape
    depth = mix_norm_g.shape[0]
    n_fox, n_pool = fox_w_in.shape[0], pool_w.shape[0]
    n_heads = d // HEAD_DIM
    n_in = fox_w_in.shape[2] * N_DEV
    n_pad = 3 * d + LANES
    n_groups = pool_w.shape[1]
    dsh = d // N_DEV
    half = N_DEV // 2
    axes = ("x", "y", "c")

    pool_bs = jnp.stack([pool_b, pool_scale])[None]
    w_in_g, w_out_g, pool_w_g, pool_bs_g = _all_gather_layers(
        "gather_mixer", [fox_w_in.astype(BF16), fox_w_out.astype(BF16), pool_w.astype(BF16), pool_bs])
    gu_bf, dn_bf = ffn_w_gate_up.astype(BF16), ffn_w_down.astype(BF16)
    ffn_gather = [_exchange_start(f"gather_ffn{l}", *_gather_plan([gu_bf[l:l + 1], dn_bf[l:l + 1]]))
                  for l in range(depth)]
    started = sum(hd["token"][:1, :1] for hd in ffn_gather)
    w_gu_g, w_dn_g, w_dn = [None] * depth, [None] * depth, [None] * depth
    w_in =[jnp.pad(jnp.transpose(g, (1, 0, 2)).reshape(d, n_in), ((0, 0), (0, n_pad - n_in))) for g in w_in_g]
    w_out = [g.reshape(d, d) for g in w_out_g]
    w_pool = [jnp.transpose(g, (1, 0, 2, 3)).reshape(n_groups, d // n_groups, d // n_groups) for g in pool_w_g]
    pool_bs_full = jnp.transpose(pool_bs_g[0], (1, 2, 0, 3)).reshape(2, n_pool, 1, d)
    b_pad =[jnp.pad(fox_b_f[j], (0, LANES - n_heads))[None] for j in range(n_fox)]

    saved = []
    cur = xs
    for i in range(depth):
        j = i // 2
        gm = mix_norm_g[i][None]
        if i == 0:
            gm = gm + started
        if i % 2 == 0:
            h = _rms_fwd(f"norm_mix{i}", cur, gm)
            proj =_mm_nn(f"proj_in{i}", h, w_in[j], F32, tn=896)
            gq, gk = fox_q_norm_g[j][None], fox_k_norm_g[j][None]
            qn, kn, vb = _qkv_fwd(f"qk_norm{i}", proj, gq, gk, d)
            flog = proj[:, 3 * d:]
            c = _gate_fwd(f"gate{i}", flog, b_pad[j])
            c_t = c[:, :n_heads].T
            c_col, c_row = c_t[:, :, None], c_t[:, None, :]
            o, lse = _attn_fwd(f"attn{i}", qn, kn, vb.T, c_row, c_col)
            mid = _mm_nn(f"proj_out{i}", o, w_out[j], F32, add=cur)
            mix_saved = (cur, h, proj, flog, qn, kn, vb, c_col, c_row, o, lse)
        else:
            mid, y, zb = _pool_fwd(f"pool{i}", cur, gm, w_pool[j], pool_bs_full[0, j], pool_bs_full[1, j])
            mix_saved = (cur, y, zb)
        h2 = _rms_fwd(f"norm_ffn{i}", mid, ffn_norm_g[i][None])
        w_gu_g[i], dn_g = _exchange_wait(ffn_gather[i], h2)
        w_dn[i] = dn_g.reshape(-1, d)
        gate, up, act =_ffn_up(f"ffn_up{i}", h2, w_gu_g[i])
        nxt = _mm_nn(f"ffn_down{i}", act, w_dn[i], F32, add=mid, tk=1408)
        saved.append((mix_saved, mid, h2, gate, up, act))
        cur = nxt

    dcur, lpart = _loss_head("loss_head", cur, tgt)
    loss = lax.psum(0.5 * lpart[0, 0], axes)

    d_mix, d_ffn = [None] * depth, [None] * depth
    d_bf, d_gq, d_gk = [None] * n_fox, [None] * n_fox, [None] * n_fox
    g_in, g_out = [None] * n_fox, [None] * n_fox
    g_pw, g_pb, g_ps = [None] * n_pool, [None] * n_pool, [None] * n_pool
    ffn_scatter = [None] * depth
    for i in reversed(range(depth)):
        j = i // 2
        mix_saved, mid, h2, gate, up, act = saved[i]
        dgu = _ffn_dact(f"ffn_dact{i}", dcur, w_dn[i].reshape(half, -1, d), gate, up)
        g_dn = _mm_tn(f"ffn_dw_down{i}", act, dcur, BF16, tm=1408).reshape(N_DEV, -1, d)
        g_gu = _ffn_dw_gu(f"ffn_dw_up{i}", h2, dgu)
        ffn_scatter[i] = _exchange_start(f"scatter_ffn{i}", *_scatter_plan([g_gu, g_dn]))
        dh2 = _ffn_dh(f"ffn_dh{i}", dgu, w_gu_g[i])
        g_ffn = ffn_norm_g[i][None] + ffn_scatter[i]["token"][:1, :1]
        dmid, d_ffn[i] = _rms_bwd(f"norm_ffn_bwd{i}", dh2, mid, g_ffn, dcur)
        gm = mix_norm_g[i][None]
        if i % 2 == 0:
            xin, h, proj, flog, qn, kn, vb, c_col, c_row, o, lse = mix_saved
            do = _mm_nt(f"proj_out_dx{i}", dmid, w_out[j], BF16)
            g_out[j] = _mm_tn(f"proj_out_dw{i}", o, dmid, BF16).reshape(N_DEV, dsh, d)
            delta = _attn_delta(f"attn_delta{i}", o, do, n_heads)
            delta_row = delta[:, :n_heads].T[:, None, :]
            dqn, dkn, dv, dck, dcq = _attn_bwd(f"attn_bwd{i}", qn, kn, vb, do, c_row,
                                               lse, delta_row, c_col)
            lane_pad = ((0, 0), (0, LANES - n_heads))
            dflog, d_bf[j] = _gate_bwd(f"gate_bwd{i}", jnp.pad(dck[:, :, 0].T, lane_pad),
                                       jnp.pad(dcq[:, 0, :].T, lane_pad), flog, b_pad[j], n_heads)
            gq, gk = fox_q_norm_g[j][None], fox_k_norm_g[j][None]
            dproj, d_gq[j], d_gk[j] = _qkv_bwd(f"qk_norm_bwd{i}", proj, dqn, dkn, dv, dflog, gq, gk, d, n_pad)
            dw_in = _mm_tn(f"proj_in_dw{i}", h, dproj, BF16, tn=896)
            g_in[j] = jnp.transpose(dw_in[:, :n_in].reshape(d, N_DEV, n_in // N_DEV), (1, 0, 2))
            dh = _mm_nt(f"proj_in_dx{i}", dproj, w_in[j], F32, tk=896)
            dcur, d_mix[i] = _rms_bwd(f"norm_mix_bwd{i}", dh, xin, gm, dmid)
        else:
            xin, y, zb = mix_saved
            dcur, dz, d_mix[i], dsc, db = _pool_bwd(f"pool_bwd{i}", dmid, xin, zb, gm, w_pool[j], pool_bs_full[1, j])
            dwp = _pool_dw(f"pool_dw{i}", y, dz, n_groups)
            dg = d // n_groups
            g_pw[j] = jnp.transpose(dwp.reshape(n_groups, N_DEV, dg // N_DEV, dg), (1, 0, 2, 3)).astype(BF16)
            g_pb[j], g_ps[j] = db.reshape(N_DEV, dsh), dsc.reshape(N_DEV, dsh)
    grad_x = dcur[None]

    g_pbs = jnp.stack([jnp.stack(g_pb, axis=1), jnp.stack(g_ps, axis=1)], axis=1)
    r_in, r_out, r_pw, r_pbs = _all_to_all_layers("scatter_mixer", [g_in, g_out, g_pw, [g_pbs]])
    landed = [_exchange_wait(ffn_scatter[l], dcur) for l in range(depth)]
    r_gu, r_dn = [t[0] for t in landed], [t[1] for t in landed]
    upd = {}
    upd["fox_w_in"] = _adamw_nd("adamw_w_in", fox_w_in, m_fox_w_in, v_fox_w_in, r_in)
    upd["fox_w_out"] = _adamw_nd("adamw_w_out", fox_w_out, m_fox_w_out, v_fox_w_out, r_out)
    upd["pool_w"] = _adamw_nd("adamw_pool_w", pool_w, m_pool_w, v_pool_w, r_pw)
    pbs = _adamw_nd("adamw_pool_bs", pool_bs[0], jnp.stack([m_pool_b, m_pool_scale]),
                    jnp.stack([v_pool_b, v_pool_scale]), r_pbs[:, 0])
    upd["pool_b"] = [o[0] for o in pbs]
    upd["pool_scale"] = [o[1] for o in pbs]
    upd["ffn_w_gate_up"] = _adamw_layers("adamw_gate_up", ffn_w_gate_up, m_ffn_w_gate_up, v_ffn_w_gate_up, r_gu)
    upd["ffn_w_down"] = _adamw_layers("adamw_down", ffn_w_down, m_ffn_w_down, v_ffn_w_down, r_dn)

    small_w = (mix_norm_g, ffn_norm_g, fox_b_f, fox_q_norm_g, fox_k_norm_g)
    small_g = _pack_small(jnp.concatenate(d_mix), jnp.concatenate(d_ffn),
                          jnp.concatenate(d_bf)[:, :n_heads], jnp.concatenate(d_gq), jnp.concatenate(d_gk))
    (small_pieces,), = _all_gather_layers("gather_small", [small_g[None]])
    small = _adamw("adamw_small", _pack_small(*small_w),
                   _pack_small(m_mix_norm_g, m_ffn_norm_g, m_fox_b_f, m_fox_q_norm_g, m_fox_k_norm_g),
                   _pack_small(v_mix_norm_g, v_ffn_norm_g, v_fox_b_f, v_fox_q_norm_g, v_fox_k_norm_g),
                   small_pieces)
    small = [_unpack_small(o, *small_w) for o in small]
    for n, name in enumerate(("mix_norm_g", "ffn_norm_g", "fox_b_f", "fox_q_norm_g", "fox_k_norm_g")):
        upd[name] = [o[n] for o in small]

    order = ("mix_norm_g", "ffn_norm_g", "fox_w_in", "fox_b_f", "fox_q_norm_g", "fox_k_norm_g", "fox_w_out",
             "pool_w", "pool_b", "pool_scale", "ffn_w_gate_up", "ffn_w_down")
    return (loss, grad_x) + tuple(upd[name][q] for q in range(4) for name in order)
```

```python
import functools

import jax
import jax.numpy as jnp
from jax import lax
from jax.experimental import pallas as pl
from jax.experimental.pallas import tpu as pltpu

F32 = jnp.float32
BF16 = jnp.bfloat16
MESH = pl.DeviceIdType.MESH

N_DEV = 8
HEAD_DIM = 128
LANES = 128
POOL_WINDOWS = (2, 4, 8, 16)
POOL_HALO = 16
RMS_EPS = 1e-6
NEG_INF = -1e30
ADAM_LR = 0.001
ADAM_B1 = 0.9
ADAM_B2 = 0.999
ADAM_EPS = 1e-08
ADAM_WD = 0.01
ADAM_STEP = 10
VMEM_LIMIT = 52 * 1024 * 1024

NN = (((1,), (0,)), ((), ()))
NT = (((1,), (1,)), ((), ()))
TN = (((0,), (0,)), ((), ()))


def _tile(n, pref, align):
    best = None
    d = align
    while d <= min(n, pref):
        if n % d == 0:
            best = d
        d += align
    return n if best is None else best


def _params(*sem):
    return pltpu.CompilerParams(dimension_semantics=sem, vmem_limit_bytes=VMEM_LIMIT)


def _position():
    x, y, c = lax.axis_index("x"), lax.axis_index("y"), lax.axis_index("c")
    return x, y, c, 4 * x + 2 * y + c


def _peer(x, y, c, k):
    px = 1 - x if k & 4 else x
    py = 1 - y if k & 2 else y
    pc = 1 - c if k & 1 else c
    return (px, py, pc), 4 * px + 2 * py + pc


def _exchange(name, ins, out_shapes, copies):
    n_in, n_cp = len(ins), len(copies)

    def body(*refs):
        in_refs = refs[:n_in]
        out_refs = refs[n_in:n_in + len(out_shapes)]
        send_sems, recv_sems, loc_sems = refs[n_in + len(out_shapes):]
        x, y, c, me = _position()
        local = []
        for ci, (ii, src_of, oi, dst_of) in enumerate(copies):
            cp = pltpu.make_async_copy(src_of(in_refs[ii], me), dst_of(out_refs[oi], me), loc_sems.at[ci])
            cp.start()
            local.append(cp)
        sends, recvs = [], []
        for k in range(1, N_DEV):
            pid, p = _peer(x, y, c, k)
            for ci, (ii, src_of, oi, dst_of) in enumerate(copies):
                sem = ci * (N_DEV - 1) + k - 1
                send = pltpu.make_async_remote_copy(
                    src_ref=src_of(in_refs[ii], p), dst_ref=dst_of(out_refs[oi], me),
                    send_sem=send_sems.at[sem], recv_sem=recv_sems.at[sem],
                    device_id=pid, device_id_type=MESH)
                send.start()
                sends.append(send)
                recvs.append(pltpu.make_async_remote_copy(
                    src_ref=src_of(in_refs[ii], p), dst_ref=dst_of(out_refs[oi], p),
                    send_sem=send_sems.at[sem], recv_sem=recv_sems.at[sem],
                    device_id=pid, device_id_type=MESH))
        for r in recvs:
            r.wait_recv()
        for s in sends:
            s.wait_send()
        for cp in local:
            cp.wait()

    any_spec = pl.BlockSpec(memory_space=pl.ANY)
    return pl.pallas_call(
        body, name=name,
        out_shape=tuple(out_shapes),
        in_specs=[any_spec] * n_in,
        out_specs=tuple([any_spec] * len(out_shapes)),
        scratch_shapes=[pltpu.SemaphoreType.DMA((n_cp * (N_DEV - 1),)),
                        pltpu.SemaphoreType.DMA((n_cp * (N_DEV - 1),)),
                        pltpu.SemaphoreType.DMA((n_cp,))],
    )(*ins)


def _exchange_start(name, ins, out_shapes, copies):
    n_in, n_out, n_cp = len(ins), len(out_shapes), len(copies)

    def body(*refs):
        in_refs = refs[:n_in]
        land_refs = refs[n_in:n_in + n_out]
        send_sems, recv_sems = refs[n_in + n_out:n_in + n_out + 2]
        token_ref = refs[2 * (n_in + n_out) + 2]
        loc_sems = refs[2 * (n_in + n_out) + 3]
        x, y, c, me = _position()
        local = []
        for ci, (ii, src_of, oi, dst_of) in enumerate(copies):
            cp = pltpu.make_async_copy(src_of(in_refs[ii], me), dst_of(land_refs[oi], me), loc_sems.at[ci])
            cp.start()
            local.append(cp)
        for cp in local:
            cp.wait()
        for k in range(1, N_DEV):
            pid, p = _peer(x, y, c, k)
            for ci, (ii, src_of, oi, dst_of) in enumerate(copies):
                sem = ci * (N_DEV - 1) + k - 1
                pltpu.make_async_remote_copy(
                    src_ref=src_of(in_refs[ii], p), dst_ref=dst_of(land_refs[oi], me),
                    send_sem=send_sems.at[sem], recv_sem=recv_sems.at[sem],
                    device_id=pid, device_id_type=MESH).start()
        token_ref[...] = jnp.zeros_like(token_ref)

    hbm = pl.BlockSpec(memory_space=pltpu.HBM)
    sem = pl.BlockSpec(memory_space=pltpu.SEMAPHORE)
    n_sem = n_cp * (N_DEV - 1)
    lands = [pltpu.with_memory_space_constraint(lax.empty(o.shape, o.dtype), pltpu.HBM) for o in out_shapes]
    srcs = [pltpu.with_memory_space_constraint(a, pltpu.HBM) for a in ins]
    res = pl.pallas_call(
        body, name=name,
        out_shape=(pltpu.SemaphoreType.DMA((n_sem,)), pltpu.SemaphoreType.DMA((n_sem,)),
                   *[pltpu.HBM(a.shape, a.dtype) for a in ins],
                   *[pltpu.HBM(o.shape, o.dtype) for o in out_shapes],
                   jax.ShapeDtypeStruct((8, LANES), F32)),
        in_specs=[hbm] * (n_in + n_out),
        out_specs=(sem, sem, *([hbm] * (n_in + n_out)), pl.BlockSpec(memory_space=pltpu.VMEM)),
        input_output_aliases={i: 2 + i for i in range(n_in + n_out)},
        scratch_shapes=[pltpu.SemaphoreType.DMA((n_cp,))],
        compiler_params=pltpu.CompilerParams(has_side_effects=pltpu.SideEffectType.DATAFLOW_SIDE_EFFECTING),
    )(*srcs, *lands)
    return dict(name=name, copies=copies, send=res[0], recv=res[1], srcs=list(res[2:2 + n_in]),
                lands=list(res[2 + n_in:2 + n_in + n_out]), token=res[-1])


def _exchange_wait(handle, after):
    copies, srcs, lands = handle["copies"], handle["srcs"], handle["lands"]
    n_in, n_out = len(srcs), len(lands)

    def body(*refs):
        in_refs = refs[:n_in]
        land_refs = refs[n_in:n_in + n_out]
        send_sems, recv_sems = refs[n_in + n_out:n_in + n_out + 2]
        x, y, c, me = _position()
        waits = []
        for k in range(1, N_DEV):
            pid, p = _peer(x, y, c, k)
            for ci, (ii, src_of, oi, dst_of) in enumerate(copies):
                sem = ci * (N_DEV - 1) + k - 1
                waits.append(pltpu.make_async_remote_copy(
                    src_ref=src_of(in_refs[ii], p), dst_ref=dst_of(land_refs[oi], p),
                    send_sem=send_sems.at[sem], recv_sem=recv_sems.at[sem],
                    device_id=pid, device_id_type=MESH))
        for w in waits:
            w.wait_send()
        for w in waits:
            w.wait_recv()

    hbm = pl.BlockSpec(memory_space=pltpu.HBM)
    sem = pl.BlockSpec(memory_space=pltpu.SEMAPHORE)
    res = pl.pallas_call(
        body, name=handle["name"] + "_wait",
        out_shape=tuple(pltpu.HBM(a.shape, a.dtype) for a in srcs + lands),
        in_specs=[hbm] * (n_in + n_out) + [sem, sem, pl.BlockSpec(memory_space=pl.ANY)],
        out_specs=tuple([hbm] * (n_in + n_out)),
        input_output_aliases={i: i for i in range(n_in + n_out)},
        compiler_params=pltpu.CompilerParams(has_side_effects=pltpu.SideEffectType.DATAFLOW_SIDE_EFFECTING),
    )(*srcs, *lands, handle["send"], handle["recv"], after)
    return list(res[n_in:])


def _gather_plan(stacked):
    ins, outs, copies = [], [], []
    for t in stacked:
        ii = len(ins)
        ins.append(t)
        for l in range(t.shape[0]):
            oi = len(outs)
            outs.append(jax.ShapeDtypeStruct((N_DEV,) + t.shape[1:], t.dtype))
            copies.append((ii, (lambda ref, p, l=l: ref.at[l]), oi, (lambda ref, s: ref.at[s])))
    return ins, outs, copies


def _scatter_plan(blocked):
    outs = [jax.ShapeDtypeStruct(t.shape, t.dtype) for t in blocked]
    copies = [(n, (lambda ref, p: ref.at[p]), n, (lambda ref, s: ref.at[s])) for n in range(len(blocked))]
    return list(blocked), outs, copies


def _all_gather_layers(name, stacked):
    ins, outs, copies = [], [], []
    for t in stacked:
        ii = len(ins)
        ins.append(t)
        for l in range(t.shape[0]):
            oi = len(outs)
            outs.append(jax.ShapeDtypeStruct((N_DEV,) + t.shape[1:], t.dtype))
            copies.append((ii, (lambda ref, p, l=l: ref.at[l]), oi, (lambda ref, s: ref.at[s])))
    res = _exchange(name, ins, outs, copies)
    out, pos = [], 0
    for t in stacked:
        out.append(list(res[pos:pos + t.shape[0]]))
        pos += t.shape[0]
    return out


def _mm(name, mode, a, b, out_shape, *, grid, a_spec, b_spec, o_spec, acc_shape, add=None, add_spec=None):
    nk = grid[2]
    dn = {"nn": NN, "nt": NT, "tn": TN}[mode]
    has_add = add is not None

    def body(*refs):
        if has_add:
            a_ref, b_ref, add_ref, o_ref = refs[:4]
        else:
            a_ref, b_ref, o_ref = refs[:3]
            add_ref = None
        prod = lax.dot_general(a_ref[...].astype(BF16), b_ref[...].astype(BF16), dn,
                               preferred_element_type=F32)

        def finish(r):
            if has_add:
                r = r + add_ref[...]
            o_ref[...] = r.astype(o_ref.dtype)

        if nk == 1:
            finish(prod)
        else:
            acc_ref = refs[-1]
            k = pl.program_id(2)

            @pl.when(k == 0)
            def _():
                acc_ref[...] = prod

            @pl.when(k > 0)
            def _():
                acc_ref[...] += prod

            @pl.when(k == nk - 1)
            def _():
                finish(acc_ref[...])

    ins = [a, b] + ([add] if has_add else [])
    in_specs = [a_spec, b_spec] + ([add_spec] if has_add else [])
    scratch = [] if nk == 1 else [pltpu.VMEM(acc_shape, F32)]
    return pl.pallas_call(
        body, name=name, grid=grid, out_shape=out_shape,
        in_specs=in_specs, out_specs=o_spec, scratch_shapes=scratch,
        compiler_params=_params("parallel", "parallel", "arbitrary"),
    )(*ins)


def _mm_nn(name, a, b, out_dtype, add=None, tm=1024, tn=1024, tk=2048):
    m, kd = a.shape
    n = b.shape[1]
    tm, tn, tk = _tile(m, tm, 16), _tile(n, tn, LANES), _tile(kd, tk, LANES)
    return _mm(name, "nn", a, b, jax.ShapeDtypeStruct((m, n), out_dtype),
               grid=(m // tm, n // tn, kd // tk),
               a_spec=pl.BlockSpec((tm, tk), lambda i, j, k: (i, k)),
               b_spec=pl.BlockSpec((tk, tn), lambda i, j, k: (k, j)),
               o_spec=pl.BlockSpec((tm, tn), lambda i, j, k: (i, j)),
               acc_shape=(tm, tn), add=add,
               add_spec=pl.BlockSpec((tm, tn), lambda i, j, k: (i, j)))


def _mm_nt(name, a, b, out_dtype, tm=1024, tn=1024, tk=2048):
    m, kd = a.shape
    n = b.shape[0]
    tm, tn, tk = _tile(m, tm, 16), _tile(n, tn, LANES), _tile(kd, tk, LANES)
    return _mm(name, "nt", a, b, jax.ShapeDtypeStruct((m, n), out_dtype),
               grid=(m // tm, n // tn, kd // tk),
               a_spec=pl.BlockSpec((tm, tk), lambda i, j, k: (i, k)),
               b_spec=pl.BlockSpec((tn, tk), lambda i, j, k: (j, k)),
               o_spec=pl.BlockSpec((tm, tn), lambda i, j, k: (i, j)),
               acc_shape=(tm, tn))


def _mm_tn(name, a, b, out_dtype, tm=1024, tn=1024, ts=1024):
    s, m = a.shape
    n = b.shape[1]
    tm, tn, ts = _tile(m, tm, LANES), _tile(n, tn, LANES), _tile(s, ts, 16)
    return _mm(name, "tn", a, b, jax.ShapeDtypeStruct((m, n), out_dtype),
               grid=(m // tm, n // tn, s // ts),
               a_spec=pl.BlockSpec((ts, tm), lambda i, j, k: (k, i)),
               b_spec=pl.BlockSpec((ts, tn), lambda i, j, k: (k, j)),
               o_spec=pl.BlockSpec((tm, tn), lambda i, j, k: (i, j)),
               acc_shape=(tm, tn))


def _rms_fwd(name, x, g):
    s, d = x.shape
    tm = _tile(s, 512, 16)

    def body(x_ref, g_ref, h_ref):
        xv = x_ref[...]
        r = lax.rsqrt(jnp.mean(xv * xv, axis=-1, keepdims=True) + RMS_EPS)
        h_ref[...] = ((xv * r) * g_ref[...]).astype(BF16)

    return pl.pallas_call(
        body, name=name, grid=(s // tm,), out_shape=jax.ShapeDtypeStruct((s, d), BF16),
        in_specs=[pl.BlockSpec((tm, d), lambda i: (i, 0)), pl.BlockSpec((1, d), lambda i: (0, 0))],
        out_specs=pl.BlockSpec((tm, d), lambda i: (i, 0)),
        compiler_params=_params("parallel"),
    )(x, g)


def _rms_bwd(name, dh, x, g, dres):
    s, d = x.shape
    tm = _tile(s, 256, 16)

    def body(dh_ref, x_ref, g_ref, dres_ref, dx_ref, dg_ref):
        i = pl.program_id(0)
        xv = x_ref[...]
        r = lax.rsqrt(jnp.mean(xv * xv, axis=-1, keepdims=True) + RMS_EPS)
        xhat = xv * r
        dhv = dh_ref[...].astype(F32)
        gdh = dhv * g_ref[...]
        dx_ref[...] = dres_ref[...] + r * (gdh - xhat * jnp.mean(gdh * xhat, axis=-1, keepdims=True))
        part = jnp.sum(dhv * xhat, axis=0, keepdims=True)

        @pl.when(i == 0)
        def _():
            dg_ref[...] = part

        @pl.when(i > 0)
        def _():
            dg_ref[...] += part

    row = pl.BlockSpec((tm, d), lambda i: (i, 0))
    vec = pl.BlockSpec((1, d), lambda i: (0, 0))
    return pl.pallas_call(
        body, name=name, grid=(s // tm,),
        out_shape=(jax.ShapeDtypeStruct((s, d), F32), jax.ShapeDtypeStruct((1, d), F32)),
        in_specs=[row, row, vec, row], out_specs=(row, vec),
        compiler_params=_params("arbitrary"),
    )(dh, x, g, dres)


def _split3(v):
    hi = v.astype(BF16)
    r1 = v - hi.astype(F32)
    mid = r1.astype(BF16)
    lo = (r1 - mid.astype(F32)).astype(BF16)
    return hi, mid, lo


def _tri_sum(tri, v):
    hi, mid, lo = _split3(v)
    dot = functools.partial(lax.dot_general, dimension_numbers=NN, preferred_element_type=F32)
    return dot(tri, hi) + dot(tri, mid) + dot(tri, lo)


def _gate_fwd(name, flog, b_pad):
    s = flog.shape[0]
    tb = _tile(s, 256, 16)

    def body(f_ref, b_ref, c_ref, carry_ref):
        i = pl.program_id(0)

        @pl.when(i == 0)
        def _():
            carry_ref[...] = jnp.zeros_like(carry_ref)

        z = f_ref[...] + b_ref[...]
        lf = jnp.minimum(z, 0.0) - jnp.log(1.0 + jnp.exp(-jnp.abs(z)))
        rows = lax.broadcasted_iota(jnp.int32, (tb, tb), 0)
        cols = lax.broadcasted_iota(jnp.int32, (tb, tb), 1)
        tri = (rows >= cols).astype(BF16)
        c_ref[...] = _tri_sum(tri, lf) + carry_ref[...]
        carry_ref[...] = c_ref[pl.ds(tb - 1, 1), :]

    return pl.pallas_call(
        body, name=name, grid=(s // tb,), out_shape=jax.ShapeDtypeStruct((s, LANES), F32),
        in_specs=[pl.BlockSpec((tb, LANES), lambda i: (i, 0)), pl.BlockSpec((1, LANES), lambda i: (0, 0))],
        out_specs=pl.BlockSpec((tb, LANES), lambda i: (i, 0)),
        scratch_shapes=[pltpu.VMEM((1, LANES), F32)],
        compiler_params=_params("arbitrary"),
    )(flog, b_pad)


def _gate_bwd(name, dck, dcq, flog, b_pad, n_heads):
    s = flog.shape[0]
    tb = _tile(s, 256, 16)
    nb = s // tb

    def body(dck_ref, dcq_ref, f_ref, b_ref, df_ref, db_ref, carry_ref, tmp_ref):
        i = pl.program_id(0)

        @pl.when(i == 0)
        def _():
            carry_ref[...] = jnp.zeros_like(carry_ref)

        rows = lax.broadcasted_iota(jnp.int32, (tb, tb), 0)
        cols = lax.broadcasted_iota(jnp.int32, (tb, tb), 1)
        tri = (rows <= cols).astype(BF16)
        tmp_ref[...] = _tri_sum(tri, dck_ref[...] + dcq_ref[...]) + carry_ref[...]
        carry_ref[...] = tmp_ref[pl.ds(0, 1), :]
        z = f_ref[...] + b_ref[...]
        lane = lax.broadcasted_iota(jnp.int32, (tb, LANES), 1)
        df = jnp.where(lane < n_heads, tmp_ref[...] / (1.0 + jnp.exp(z)), 0.0)
        df_ref[...] = df.astype(BF16)
        part = jnp.sum(df, axis=0, keepdims=True)

        @pl.when(i == 0)
        def _():
            db_ref[...] = part

        @pl.when(i > 0)
        def _():
            db_ref[...] += part

    rev = pl.BlockSpec((tb, LANES), lambda i: (nb - 1 - i, 0))
    vec = pl.BlockSpec((1, LANES), lambda i: (0, 0))
    return pl.pallas_call(
        body, name=name, grid=(nb,),
        out_shape=(jax.ShapeDtypeStruct((s, LANES), BF16), jax.ShapeDtypeStruct((1, LANES), F32)),
        in_specs=[rev, rev, rev, vec], out_specs=(rev, vec),
        scratch_shapes=[pltpu.VMEM((1, LANES), F32), pltpu.VMEM((tb, LANES), F32)],
        compiler_params=_params("arbitrary"),
    )(dck, dcq, flog, b_pad)


def _head_rms(v, g):
    r = lax.rsqrt(jnp.mean(v * v, axis=-1, keepdims=True) + RMS_EPS)
    return (v * r) * g


def _qkv_fwd(name, proj, gq, gk, d):
    s = proj.shape[0]
    tm = _tile(s, 256, 16)
    n_heads = d // HEAD_DIM

    def body(q_ref, k_ref, v_ref, gq_ref, gk_ref, qn_ref, kn_ref, vb_ref):
        for h in range(n_heads):
            sl = slice(h * HEAD_DIM, (h + 1) * HEAD_DIM)
            qn_ref[:, sl] = _head_rms(q_ref[:, sl], gq_ref[...]).astype(BF16)
            kn_ref[:, sl] = _head_rms(k_ref[:, sl], gk_ref[...]).astype(BF16)
        vb_ref[...] = v_ref[...].astype(BF16)

    col = lambda c: pl.BlockSpec((tm, d), lambda i, c=c: (i, c))
    vec = pl.BlockSpec((1, HEAD_DIM), lambda i: (0, 0))
    out = jax.ShapeDtypeStruct((s, d), BF16)
    return pl.pallas_call(
        body, name=name, grid=(s // tm,), out_shape=(out, out, out),
        in_specs=[col(0), col(1), col(2), vec, vec], out_specs=(col(0), col(0), col(0)),
        compiler_params=_params("parallel"),
    )(proj, proj, proj, gq, gk)


def _qkv_bwd(name, proj, dqn, dkn, dv, dflog, gq, gk, d, n_pad):
    s = proj.shape[0]
    tm = _tile(s, 256, 16)
    n_heads = d // HEAD_DIM

    def head_bwd(raw, dy, g):
        r = lax.rsqrt(jnp.mean(raw * raw, axis=-1, keepdims=True) + RMS_EPS)
        hat = raw * r
        gdy = dy * g
        dx = r * (gdy - hat * jnp.mean(gdy * hat, axis=-1, keepdims=True))
        return dx, jnp.sum(dy * hat, axis=0, keepdims=True)

    def body(q_ref, k_ref, dqn_ref, dkn_ref, dv_ref, df_ref, gq_ref, gk_ref, dp_ref, dgq_ref, dgk_ref):
        i = pl.program_id(0)
        accq = jnp.zeros((1, HEAD_DIM), F32)
        acck = jnp.zeros((1, HEAD_DIM), F32)
        for h in range(n_heads):
            sl = slice(h * HEAD_DIM, (h + 1) * HEAD_DIM)
            dq, pq = head_bwd(q_ref[:, sl], dqn_ref[:, sl], gq_ref[...])
            dk, pk = head_bwd(k_ref[:, sl], dkn_ref[:, sl], gk_ref[...])
            dp_ref[:, sl] = dq.astype(BF16)
            dp_ref[:, d + h * HEAD_DIM:d + (h + 1) * HEAD_DIM] = dk.astype(BF16)
            accq, acck = accq + pq, acck + pk
        dp_ref[:, 2 * d:3 * d] = dv_ref[...]
        dp_ref[:, 3 * d:] = df_ref[...]

        @pl.when(i == 0)
        def _():
            dgq_ref[...] = accq
            dgk_ref[...] = acck

        @pl.when(i > 0)
        def _():
            dgq_ref[...] += accq
            dgk_ref[...] += acck

    col = lambda c: pl.BlockSpec((tm, d), lambda i, c=c: (i, c))
    vec = pl.BlockSpec((1, HEAD_DIM), lambda i: (0, 0))
    return pl.pallas_call(
        body, name=name, grid=(s // tm,),
        out_shape=(jax.ShapeDtypeStruct((s, n_pad), BF16), jax.ShapeDtypeStruct((1, HEAD_DIM), F32),
                   jax.ShapeDtypeStruct((1, HEAD_DIM), F32)),
        in_specs=[col(0), col(1), col(0), col(0), col(0), pl.BlockSpec((tm, LANES), lambda i: (i, 0)), vec, vec],
        out_specs=(pl.BlockSpec((tm, n_pad), lambda i: (i, 0)), vec, vec),
        compiler_params=_params("arbitrary"),
    )(proj, proj, dqn, dkn, dv, dflog, gq, gk)


def _attn_fwd(name, qn, kn, vt, c_row, c_col):
    s, d = qn.shape
    n_heads = d // HEAD_DIM
    t = _tile(s, 512, LANES)
    scale = HEAD_DIM ** -0.5

    hp = 2 if n_heads % 2 == 0 else 1
    log2e = 1.4426950408889634

    def body(q_ref, k_ref, vt_ref, cq_ref, ck_ref, o_ref, lse_ref, m_ref, l_ref, acc_ref):
        i = pl.program_id(1)
        m_ref[...] = jnp.full(m_ref.shape, NEG_INF, F32)
        l_ref[...] = jnp.zeros_like(l_ref)
        acc_ref[...] = jnp.zeros_like(acc_ref)

        def step(j, masked):
            start = pl.multiple_of(j * t, t)
            for hh in range(hp):
                sl = slice(hh * HEAD_DIM, (hh + 1) * HEAD_DIM)
                kj = k_ref[pl.ds(start, t), sl]
                vtj = vt_ref[sl, pl.ds(start, t)]
                bias = cq_ref[hh] * log2e - ck_ref[hh, pl.ds(start, t), :] * log2e
                st = lax.dot_general(kj, q_ref[:, sl], NT, preferred_element_type=F32) * (scale * log2e) + bias
                if masked:
                    rows = lax.broadcasted_iota(jnp.int32, (t, t), 0)
                    cols = lax.broadcasted_iota(jnp.int32, (t, t), 1)
                    st = jnp.where(cols >= rows, st, NEG_INF)
                m_prev = m_ref[hh]
                m_new = jnp.maximum(m_prev, jnp.max(st, axis=0, keepdims=True))
                pt = jnp.exp2(st - m_new)
                alpha = jnp.exp2(m_prev - m_new)
                l_ref[hh] = alpha * l_ref[hh] + jnp.sum(pt, axis=0, keepdims=True)
                acc_ref[hh] = alpha * acc_ref[hh] + lax.dot_general(
                    vtj, pt.astype(BF16), NN, preferred_element_type=F32)
                m_ref[hh] = m_new

        def loop_body(j, carry):
            step(j, False)
            return carry

        lax.fori_loop(0, i, loop_body, 0)
        step(i, True)
        for hh in range(hp):
            sl = slice(hh * HEAD_DIM, (hh + 1) * HEAD_DIM)
            o_ref[:, sl] = (acc_ref[hh] / l_ref[hh]).T.astype(BF16)
            lse_ref[hh] = (m_ref[hh] + jnp.log2(l_ref[hh])) * (1.0 / log2e)

    wide = hp * HEAD_DIM
    row_blk = pl.BlockSpec((hp, 1, t), lambda h, i: (h, 0, i))
    return pl.pallas_call(
        body, name=name, grid=(n_heads // hp, s // t),
        out_shape=(jax.ShapeDtypeStruct((s, d), BF16), jax.ShapeDtypeStruct((n_heads, 1, s), F32)),
        in_specs=[pl.BlockSpec((t, wide), lambda h, i: (i, h)),
                  pl.BlockSpec((s, wide), lambda h, i: (0, h)),
                  pl.BlockSpec((wide, s), lambda h, i: (h, 0)),
                  row_blk, pl.BlockSpec((hp, s, 1), lambda h, i: (h, 0, 0))],
        out_specs=(pl.BlockSpec((t, wide), lambda h, i: (i, h)), row_blk),
        scratch_shapes=[pltpu.VMEM((hp, 1, t), F32), pltpu.VMEM((hp, 1, t), F32),
                        pltpu.VMEM((hp, HEAD_DIM, t), F32)],
        compiler_params=_params("parallel", "arbitrary"),
    )(qn, kn, vt, c_row, c_col)


def _attn_delta(name, o, do, n_heads):
    s, d = o.shape
    tm = _tile(s, 256, 16)

    def body(o_ref, do_ref, dl_ref):
        lane = lax.broadcasted_iota(jnp.int32, (tm, LANES), 1)
        acc = jnp.zeros((tm, LANES), F32)
        for h in range(n_heads):
            sl = slice(h * HEAD_DIM, (h + 1) * HEAD_DIM)
            col = jnp.sum(o_ref[:, sl].astype(F32) * do_ref[:, sl].astype(F32), axis=-1, keepdims=True)
            acc = jnp.where(lane == h, col, acc)
        dl_ref[...] = acc

    row = pl.BlockSpec((tm, d), lambda i: (i, 0))
    return pl.pallas_call(
        body, name=name, grid=(s // tm,), out_shape=jax.ShapeDtypeStruct((s, LANES), F32),
        in_specs=[row, row], out_specs=pl.BlockSpec((tm, LANES), lambda i: (i, 0)),
        compiler_params=_params("parallel"),
    )(o, do)


def _attn_bwd(name, qn, kn, vb, do, c_row, lse_row, delta_row, c_col):
    s, d = qn.shape
    n_heads = d // HEAD_DIM
    t = _tile(s, 512, LANES)
    nq = s // t
    scale = HEAD_DIM ** -0.5

    def body(q_ref, do_ref, cr_ref, lse_ref, dl_ref, k_ref, v_ref, ck_ref,
             dq_ref, dk_ref, dv_ref, dc_ref, dcq_ref, dk_acc, dv_acc, dc_acc):
        j = pl.program_id(1)

        @pl.when(j == 0)
        def _():
            dq_ref[...] = jnp.zeros_like(dq_ref)
            dcq_ref[...] = jnp.zeros_like(dcq_ref)

        kj = k_ref[...]
        vj = v_ref[...]
        ckj = ck_ref[...]
        dk_acc[...] = jnp.zeros_like(dk_acc)
        dv_acc[...] = jnp.zeros_like(dv_acc)
        dc_acc[...] = jnp.zeros_like(dc_acc)

        def step(i, masked):
            start = pl.multiple_of(i * t, t)
            qi = q_ref[pl.ds(start, t), :]
            doi = do_ref[pl.ds(start, t), :]
            bias = cr_ref[:, pl.ds(start, t)] - lse_ref[:, pl.ds(start, t)]
            dli = dl_ref[:, pl.ds(start, t)]
            st = lax.dot_general(kj, qi, NT, preferred_element_type=F32) * scale + (bias - ckj)
            if masked:
                rows = lax.broadcasted_iota(jnp.int32, (t, t), 0)
                cols = lax.broadcasted_iota(jnp.int32, (t, t), 1)
                st = jnp.where(cols >= rows, st, NEG_INF)
            pt = jnp.exp(st)
            dpt = lax.dot_general(vj, doi, NT, preferred_element_type=F32)
            dst = pt * (dpt - dli)
            dsb = dst.astype(BF16)
            dv_acc[...] += lax.dot_general(pt.astype(BF16), doi, NN, preferred_element_type=F32)
            dk_acc[...] += lax.dot_general(dsb, qi, NN, preferred_element_type=F32)
            dq_ref[pl.ds(start, t), :] += lax.dot_general(dsb, kj, TN, preferred_element_type=F32) * scale
            dc_acc[...] += jnp.sum(dst, axis=1, keepdims=True)
            dcq_ref[:, pl.ds(start, t)] += jnp.sum(dst, axis=0, keepdims=True)

        step(j, True)

        def loop_body(i, carry):
            step(i, False)
            return carry

        lax.fori_loop(j + 1, nq, loop_body, 0)
        dk_ref[...] = dk_acc[...] * scale
        dv_ref[...] = dv_acc[...].astype(BF16)
        dc_ref[...] = -dc_acc[...]

    head_all = pl.BlockSpec((s, HEAD_DIM), lambda h, j: (0, h))
    row_all = pl.BlockSpec((None, 1, s), lambda h, j: (h, 0, 0))
    blk = pl.BlockSpec((t, HEAD_DIM), lambda h, j: (j, h))
    col_blk = pl.BlockSpec((None, t, 1), lambda h, j: (h, j, 0))
    return pl.pallas_call(
        body, name=name, grid=(n_heads, nq),
        out_shape=(jax.ShapeDtypeStruct((s, d), F32), jax.ShapeDtypeStruct((s, d), F32),
                   jax.ShapeDtypeStruct((s, d), BF16), jax.ShapeDtypeStruct((n_heads, s, 1), F32),
                   jax.ShapeDtypeStruct((n_heads, 1, s), F32)),
        in_specs=[head_all, head_all, row_all, row_all, row_all, blk, blk, col_blk],
        out_specs=(head_all, blk, blk, col_blk, row_all),
        scratch_shapes=[pltpu.VMEM((t, HEAD_DIM), F32), pltpu.VMEM((t, HEAD_DIM), F32), pltpu.VMEM((t, 1), F32)],
        compiler_params=_params("parallel", "arbitrary"),
    )(qn, do, c_row, lse_row, delta_row, kn, vb, c_col)


def _ffn_up(name, h, w_gu):
    s, d = h.shape
    fs = w_gu.shape[2]
    half = N_DEV // 2
    tm = _tile(s, 512, 16)

    def body(h_ref, wg_ref, wu_ref, g_ref, u_ref, a_ref):
        hv = h_ref[...]
        g = lax.dot_general(hv, wg_ref[...], NN, preferred_element_type=F32)
        u = lax.dot_general(hv, wu_ref[...], NN, preferred_element_type=F32)
        g_ref[...] = g.astype(BF16)
        u_ref[...] = u.astype(BF16)
        a_ref[...] = (g * jax.nn.sigmoid(g) * u).astype(BF16)

    out = jax.ShapeDtypeStruct((s, half * fs), BF16)
    ospec = pl.BlockSpec((tm, fs), lambda j, i: (i, j))
    return pl.pallas_call(
        body, name=name, grid=(half, s // tm), out_shape=(out, out, out),
        in_specs=[pl.BlockSpec((tm, d), lambda j, i: (i, 0)),
                  pl.BlockSpec((None, d, fs), lambda j, i: (j, 0, 0)),
                  pl.BlockSpec((None, d, fs), lambda j, i: (j + half, 0, 0))],
        out_specs=(ospec, ospec, ospec),
        compiler_params=_params("parallel", "parallel"),
    )(h, w_gu, w_gu)


def _ffn_dact(name, dx, w_dn4, g, u):
    s, d = dx.shape
    half, fs = w_dn4.shape[0], w_dn4.shape[1]
    tm = _tile(s, 512, 16)

    def body(dx_ref, w_ref, g_ref, u_ref, dgu_ref):
        da = lax.dot_general(dx_ref[...].astype(BF16), w_ref[...], NT, preferred_element_type=F32)
        gv = g_ref[...].astype(F32)
        uv = u_ref[...].astype(F32)
        sig = jax.nn.sigmoid(gv)
        dgu_ref[0] = (da * uv * (sig * (1.0 + gv * (1.0 - sig)))).astype(BF16)
        dgu_ref[1] = (da * (gv * sig)).astype(BF16)

    blk = pl.BlockSpec((tm, fs), lambda j, i: (i, j))
    return pl.pallas_call(
        body, name=name, grid=(half, s // tm),
        out_shape=jax.ShapeDtypeStruct((2, s, half * fs), BF16),
        in_specs=[pl.BlockSpec((tm, d), lambda j, i: (i, 0)),
                  pl.BlockSpec((None, fs, d), lambda j, i: (j, 0, 0)), blk, blk],
        out_specs=pl.BlockSpec((2, tm, fs), lambda j, i: (0, i, j)),
        compiler_params=_params("parallel", "parallel"),
    )(dx, w_dn4, g, u)


def _ffn_dw_gu(name, h, dgu):
    s, d = h.shape
    half, fs = N_DEV // 2, dgu.shape[2] // (N_DEV // 2)
    tm, ts = _tile(d, 1024, LANES), _tile(s, 1024, 16)
    return _mm(name, "tn", h, dgu, jax.ShapeDtypeStruct((N_DEV, d, fs), BF16),
               grid=(d // tm, N_DEV, s // ts),
               a_spec=pl.BlockSpec((ts, tm), lambda i, j, k: (k, i)),
               b_spec=pl.BlockSpec((None, ts, fs), lambda i, j, k: (j // half, k, j % half)),
               o_spec=pl.BlockSpec((None, tm, fs), lambda i, j, k: (j, i, 0)),
               acc_shape=(tm, fs))


def _ffn_dh(name, dgu, w_gu):
    s = dgu.shape[1]
    d, fs = w_gu.shape[1], w_gu.shape[2]
    half = N_DEV // 2
    tm = _tile(s, 512, 16)
    return _mm(name, "nt", dgu, w_gu, jax.ShapeDtypeStruct((s, d), F32),
               grid=(s // tm, 1, N_DEV),
               a_spec=pl.BlockSpec((None, tm, fs), lambda i, j, k: (k // half, i, k % half)),
               b_spec=pl.BlockSpec((None, d, fs), lambda i, j, k: (k, 0, 0)),
               o_spec=pl.BlockSpec((tm, d), lambda i, j, k: (i, 0)),
               acc_shape=(tm, d))


def _pool_fwd(name, x, g, w, b, sc):
    s, d = x.shape
    dg = d // len(POOL_WINDOWS)
    tm = _tile(s, 256, POOL_HALO)
    per = tm // POOL_HALO

    def body(x_ref, xh_ref, g_ref, w_ref, b_ref, sc_ref, xo_ref, y_ref, zb_ref):
        i = pl.program_id(0)
        gv = g_ref[...]

        def norm(v):
            return (v * lax.rsqrt(jnp.mean(v * v, axis=-1, keepdims=True) + RMS_EPS)) * gv

        h = norm(x_ref[...])
        halo = norm(xh_ref[...]) * (i > 0).astype(F32)
        ext = jnp.concatenate([halo, h], axis=0)
        t = i * tm + lax.broadcasted_iota(jnp.int32, (tm, 1), 0)
        for gi, win in enumerate(POOL_WINDOWS):
            sl = slice(gi * dg, (gi + 1) * dg)
            acc = ext[:, sl]
            step = 1
            while step < win:
                acc = acc + pltpu.roll(acc, step, 0)
                step *= 2
            inv = 1.0 / jnp.minimum(t + 1, win).astype(F32)
            yg = (acc[POOL_HALO:, :] * inv - h[:, sl]).astype(BF16)
            y_ref[:, sl] = yg
            zb = lax.dot_general(yg, w_ref[gi], NN, preferred_element_type=F32) + b_ref[:, sl]
            zb_ref[:, sl] = zb
            xo_ref[:, sl] = x_ref[:, sl] + zb * sc_ref[:, sl]

    row = pl.BlockSpec((tm, d), lambda i: (i, 0))
    vec = pl.BlockSpec((1, d), lambda i: (0, 0))
    return pl.pallas_call(
        body, name=name, grid=(s // tm,),
        out_shape=(jax.ShapeDtypeStruct((s, d), F32), jax.ShapeDtypeStruct((s, d), BF16),
                   jax.ShapeDtypeStruct((s, d), F32)),
        in_specs=[row, pl.BlockSpec((POOL_HALO, d), lambda i: (jnp.maximum(i * per - 1, 0), 0)),
                  vec, pl.BlockSpec(w.shape, lambda i: (0, 0, 0)), vec, vec],
        out_specs=(row, row, row),
        compiler_params=_params("parallel"),
    )(x, x, g, w, b, sc)


def _pool_bwd(name, dout, x, zb, g, w, sc):
    s, d = x.shape
    dg = d // len(POOL_WINDOWS)
    tm = _tile(s, 256, POOL_HALO)
    per = tm // POOL_HALO
    nb = s // tm
    ext_rows = tm + POOL_HALO

    def body(do_ref, doh_ref, x_ref, zb_ref, g_ref, w_ref, sc_ref, dx_ref, dz_ref, dgn_ref, dsc_ref, db_ref):
        i = pl.program_id(0)
        scv = sc_ref[...]
        dov = do_ref[...]
        dz = dov * scv
        dz_ref[...] = dz.astype(BF16)
        halo = doh_ref[...] * scv * (i < nb - 1).astype(F32)
        ext = jnp.concatenate([dz, halo], axis=0).astype(BF16)
        t = i * tm + lax.broadcasted_iota(jnp.int32, (ext_rows, 1), 0)
        parts = []
        for gi, win in enumerate(POOL_WINDOWS):
            sl = slice(gi * dg, (gi + 1) * dg)
            dy = lax.dot_general(ext[:, sl], w_ref[gi], NT, preferred_element_type=F32)
            acc = dy * (1.0 / jnp.minimum(t + 1, win).astype(F32))
            step = 1
            while step < win:
                acc = acc + pltpu.roll(acc, ext_rows - step, 0)
                step *= 2
            parts.append(acc[:tm, :] - dy[:tm, :])
        dh = jnp.concatenate(parts, axis=1)
        xv = x_ref[...]
        r = lax.rsqrt(jnp.mean(xv * xv, axis=-1, keepdims=True) + RMS_EPS)
        xhat = xv * r
        gdh = dh * g_ref[...]
        dx_ref[...] = dov + r * (gdh - xhat * jnp.mean(gdh * xhat, axis=-1, keepdims=True))
        pgn = jnp.sum(dh * xhat, axis=0, keepdims=True)
        psc = jnp.sum(dov * zb_ref[...], axis=0, keepdims=True)
        pb = jnp.sum(dz, axis=0, keepdims=True)

        @pl.when(i == 0)
        def _():
            dgn_ref[...] = pgn
            dsc_ref[...] = psc
            db_ref[...] = pb

        @pl.when(i > 0)
        def _():
            dgn_ref[...] += pgn
            dsc_ref[...] += psc
            db_ref[...] += pb

    row = pl.BlockSpec((tm, d), lambda i: (i, 0))
    vec = pl.BlockSpec((1, d), lambda i: (0, 0))
    vshape = jax.ShapeDtypeStruct((1, d), F32)
    return pl.pallas_call(
        body, name=name, grid=(nb,),
        out_shape=(jax.ShapeDtypeStruct((s, d), F32), jax.ShapeDtypeStruct((s, d), BF16), vshape, vshape, vshape),
        in_specs=[row, pl.BlockSpec((POOL_HALO, d), lambda i: (jnp.minimum((i + 1) * per, s // POOL_HALO - 1), 0)),
                  row, row, vec, pl.BlockSpec(w.shape, lambda i: (0, 0, 0)), vec],
        out_specs=(row, row, vec, vec, vec),
        compiler_params=_params("arbitrary"),
    )(dout, dout, x, zb, g, w, sc)


def _pool_dw(name, y, dz, n_groups):
    s, d = y.shape
    dg = d // n_groups
    ts = _tile(s, 1024, 16)
    return _mm(name, "tn", y, dz, jax.ShapeDtypeStruct((n_groups, dg, dg), F32),
               grid=(n_groups, 1, s // ts),
               a_spec=pl.BlockSpec((ts, dg), lambda i, j, k: (k, i)),
               b_spec=pl.BlockSpec((ts, dg), lambda i, j, k: (k, i)),
               o_spec=pl.BlockSpec((None, dg, dg), lambda i, j, k: (i, 0, 0)),
               acc_shape=(dg, dg))


def _loss_head(name, y, tgt):
    s, d = y.shape
    tm = _tile(s, 512, 16)

    def body(y_ref, t_ref, dy_ref, l_ref):
        i = pl.program_id(0)
        e = y_ref[...] - t_ref[...]
        dy_ref[...] = e * (1.0 / d)
        part = jnp.sum(jnp.mean(e * e, axis=-1, keepdims=True), axis=0, keepdims=True)
        part = jnp.broadcast_to(part, l_ref.shape)

        @pl.when(i == 0)
        def _():
            l_ref[...] = part

        @pl.when(i > 0)
        def _():
            l_ref[...] += part

    row = pl.BlockSpec((tm, d), lambda i: (i, 0))
    return pl.pallas_call(
        body, name=name, grid=(s // tm,),
        out_shape=(jax.ShapeDtypeStruct((s, d), F32), jax.ShapeDtypeStruct((8, LANES), F32)),
        in_specs=[row, row], out_specs=(row, pl.BlockSpec((8, LANES), lambda i: (0, 0))),
        compiler_params=_params("arbitrary"),
    )(y, tgt)


def _adam_update(w_ref, m_ref, v_ref, p_ref, g_ref, d_ref, nm_ref, nv_ref):
    g = p_ref[0].astype(F32)
    for k in range(1, N_DEV):
        g = g + p_ref[k].astype(F32)
    mn = ADAM_B1 * m_ref[...] + (1.0 - ADAM_B1) * g
    vn = ADAM_B2 * v_ref[...] + (1.0 - ADAM_B2) * (g * g)
    m_hat = mn / (1.0 - ADAM_B1 ** ADAM_STEP)
    v_hat = vn / (1.0 - ADAM_B2 ** ADAM_STEP)
    g_ref[...] = g
    d_ref[...] = -ADAM_LR * (m_hat / (jnp.sqrt(v_hat) + ADAM_EPS) + ADAM_WD * w_ref[...])
    nm_ref[...] = mn
    nv_ref[...] = vn


def _adamw_layers(name, w, m, v, pieces):
    n_layers, r, c = w.shape
    tr = _tile(r, 128, 16)

    def body(w_ref, m_ref, v_ref, *rest):
        p_refs, outs = rest[:n_layers], rest[n_layers:]
        layer = pl.program_id(0)
        for l in range(n_layers):
            @pl.when(layer == l)
            def _(l=l):
                _adam_update(w_ref, m_ref, v_ref, p_refs[l], *outs)

    blk = pl.BlockSpec((None, tr, c), lambda l, i: (l, i, 0))
    terms = [pl.BlockSpec((N_DEV, tr, c), lambda l, i, n=n: (0, jnp.where(l == n, i, 0), 0))
             for n in range(n_layers)]
    out = jax.ShapeDtypeStruct(w.shape, F32)
    return list(pl.pallas_call(
        body, name=name, grid=(n_layers, r // tr), out_shape=(out, out, out, out),
        in_specs=[blk, blk, blk] + terms, out_specs=(blk, blk, blk, blk),
        compiler_params=_params("parallel", "parallel"),
    )(w, m, v, *pieces))


def _adamw(name, w, m, v, pieces):
    r, c = w.shape
    tr = _tile(r, 128, 16)

    def body(w_ref, m_ref, v_ref, p_ref, g_ref, d_ref, nm_ref, nv_ref):
        _adam_update(w_ref, m_ref, v_ref, p_ref, g_ref, d_ref, nm_ref, nv_ref)

    blk = pl.BlockSpec((tr, c), lambda i: (i, 0))
    out = jax.ShapeDtypeStruct((r, c), F32)
    return pl.pallas_call(
        body, name=name, grid=(r // tr,), out_shape=(out, out, out, out),
        in_specs=[blk, blk, blk, pl.BlockSpec((N_DEV, tr, c), lambda i: (0, i, 0))],
        out_specs=(blk, blk, blk, blk),
        compiler_params=_params("parallel"),
    )(w, m, v, pieces)


def _pack_small(mix, ffn, b_f, gq, gk):
    def rows(a):
        a = a.reshape(-1, LANES) if a.shape[-1] >= LANES else jnp.pad(a, ((0, 0), (0, LANES - a.shape[-1])))
        return jnp.pad(a, ((0, -a.shape[0] % 8), (0, 0)))
    return jnp.concatenate([rows(mix), rows(ffn), rows(b_f), rows(gq), rows(gk)], axis=0)


def _unpack_small(p, mix, ffn, b_f, gq, gk):
    out, pos = [], 0
    for a in (mix, ffn, b_f, gq, gk):
        n = a.size // LANES if a.shape[-1] >= LANES else a.shape[0]
        blk = p[pos:pos + n]
        out.append(blk.reshape(a.shape) if a.shape[-1] >= LANES else blk[:, :a.shape[-1]])
        pos += n + (-n % 8)
    return out


def kernel(x, mix_norm_g, ffn_norm_g, fox_w_in, fox_b_f, fox_q_norm_g, fox_k_norm_g, fox_w_out, pool_w, pool_b, pool_scale, ffn_w_gate_up, ffn_w_down, loss_target, m_mix_norm_g, m_ffn_norm_g, m_fox_w_in, m_fox_b_f, m_fox_q_norm_g, m_fox_k_norm_g, m_fox_w_out, m_pool_w, m_pool_b, m_pool_scale, m_ffn_w_gate_up, m_ffn_w_down, v_mix_norm_g, v_ffn_norm_g, v_fox_w_in, v_fox_b_f, v_fox_q_norm_g, v_fox_k_norm_g, v_fox_w_out, v_pool_w, v_pool_b, v_pool_scale, v_ffn_w_gate_up, v_ffn_w_down):
    xs, tgt = x[0], loss_target[0]
    s, d = xs.shape
    depth = mix_norm_g.shape[0]
    n_fox, n_pool = fox_w_in.shape[0], pool_w.shape[0]
    n_heads = d // HEAD_DIM
    n_in = fox_w_in.shape[2] * N_DEV
    n_pad = 3 * d + LANES
    n_groups = pool_w.shape[1]
    dsh = d // N_DEV
    half = N_DEV // 2
    axes = ("x", "y", "c")

    w_in_bf, w_out_bf, pool_w_bf = fox_w_in.astype(BF16), fox_w_out.astype(BF16), pool_w.astype(BF16)
    gu_bf, dn_bf = ffn_w_gate_up.astype(BF16), ffn_w_down.astype(BF16)
    pool_bs = jnp.stack([pool_b, pool_scale], axis=1)
    (w_in0,), (w_out0,) = _all_gather_layers("gather_mixer0", [w_in_bf[:1], w_out_bf[:1]])
    mix_gather, ffn_gather = [None] * depth, [None] * depth
    for l in range(depth):
        j = l // 2
        if l > 0:
            shards = [w_in_bf[j:j + 1], w_out_bf[j:j + 1]] if l % 2 == 0 else [pool_w_bf[j:j + 1], pool_bs[j:j + 1]]
            mix_gather[l] = _exchange_start(f"gather_mixer{l}", *_gather_plan(shards))
        ffn_gather[l] = _exchange_start(f"gather_ffn{l}", *_gather_plan([gu_bf[l:l + 1], dn_bf[l:l + 1]]))
    started = sum(hd["token"][:1, :1] for hd in mix_gather[1:] + ffn_gather)
    w_gu_g, w_dn = [None] * depth, [None] * depth
    w_in, w_out = [None] * n_fox, [None] * n_fox
    w_pool, pool_b_full, pool_s_full = [None] * n_pool, [None] * n_pool, [None] * n_pool
    b_pad =[jnp.pad(fox_b_f[j], (0, LANES - n_heads))[None] for j in range(n_fox)]

    saved = []
    cur = xs
    for i in range(depth):
        j = i // 2
        gm = mix_norm_g[i][None]
        if i == 0:
            gm = gm + started
        if i % 2 == 0:
            w_in_g, w_out_g = (w_in0, w_out0) if i == 0 else _exchange_wait(mix_gather[i], cur)
            w_in[j] = jnp.pad(jnp.transpose(w_in_g, (1, 0, 2)).reshape(d, n_in), ((0, 0), (0, n_pad - n_in)))
            w_out[j] = w_out_g.reshape(d, d)
            h = _rms_fwd(f"norm_mix{i}", cur, gm)
            proj =_mm_nn(f"proj_in{i}", h, w_in[j], F32, tn=896)
            gq, gk = fox_q_norm_g[j][None], fox_k_norm_g[j][None]
            qn, kn, vb = _qkv_fwd(f"qk_norm{i}", proj, gq, gk, d)
            flog = proj[:, 3 * d:]
            c = _gate_fwd(f"gate{i}", flog, b_pad[j])
            c_t = c[:, :n_heads].T
            c_col, c_row = c_t[:, :, None], c_t[:, None, :]
            o, lse = _attn_fwd(f"attn{i}", qn, kn, vb.T, c_row, c_col)
            mid = _mm_nn(f"proj_out{i}", o, w_out[j], F32, add=cur)
            mix_saved = (cur, h, proj, flog, qn, kn, vb, c_col, c_row, o, lse)
        else:
            pw_g, pbs_g = _exchange_wait(mix_gather[i], cur)
            w_pool[j] = jnp.transpose(pw_g, (1, 0, 2, 3)).reshape(n_groups, d // n_groups, d // n_groups)
            pbs_full = jnp.transpose(pbs_g, (1, 0, 2)).reshape(2, 1, d)
            pool_b_full[j], pool_s_full[j] = pbs_full[0], pbs_full[1]
            mid, y, zb = _pool_fwd(f"pool{i}", cur, gm, w_pool[j], pool_b_full[j], pool_s_full[j])
            mix_saved = (cur, y, zb)
        h2 = _rms_fwd(f"norm_ffn{i}", mid, ffn_norm_g[i][None])
        w_gu_g[i], dn_g = _exchange_wait(ffn_gather[i], h2)
        w_dn[i] = dn_g.reshape(-1, d)
        gate, up, act =_ffn_up(f"ffn_up{i}", h2, w_gu_g[i])
        nxt = _mm_nn(f"ffn_down{i}", act, w_dn[i], F32, add=mid, tk=1408)
        saved.append((mix_saved, mid, h2, gate, up, act))
        cur = nxt

    dcur, lpart = _loss_head("loss_head", cur, tgt)
    loss = lax.psum(0.5 * lpart[0, 0], axes)

    d_mix, d_ffn = [None] * depth, [None] * depth
    d_bf, d_gq, d_gk = [None] * n_fox, [None] * n_fox, [None] * n_fox
    mix_scatter, ffn_scatter = [None] * depth, [None] * depth
    pending = jnp.zeros((1, 1), F32)
    for i in reversed(range(depth)):
        j = i // 2
        mix_saved, mid, h2, gate, up, act = saved[i]
        dgu = _ffn_dact(f"ffn_dact{i}", dcur, w_dn[i].reshape(half, -1, d), gate, up)
        g_dn = _mm_tn(f"ffn_dw_down{i}", act, dcur, BF16, tm=1408).reshape(N_DEV, -1, d)
        g_gu = _ffn_dw_gu(f"ffn_dw_up{i}", h2, dgu)
        ffn_scatter[i] = _exchange_start(f"scatter_ffn{i}", *_scatter_plan([g_gu, g_dn]))
        dh2 = _ffn_dh(f"ffn_dh{i}", dgu, w_gu_g[i])
        g_ffn = ffn_norm_g[i][None] + ffn_scatter[i]["token"][:1, :1] + pending
        dmid, d_ffn[i] = _rms_bwd(f"norm_ffn_bwd{i}", dh2, mid, g_ffn, dcur)
        gm = mix_norm_g[i][None]
        if i % 2 == 0:
            xin, h, proj, flog, qn, kn, vb, c_col, c_row, o, lse = mix_saved
            do = _mm_nt(f"proj_out_dx{i}", dmid, w_out[j], BF16)
            g_out = _mm_tn(f"proj_out_dw{i}", o, dmid, BF16).reshape(N_DEV, dsh, d)
            delta = _attn_delta(f"attn_delta{i}", o, do, n_heads)
            delta_row = delta[:, :n_heads].T[:, None, :]
            dqn, dkn, dv, dck, dcq = _attn_bwd(f"attn_bwd{i}", qn, kn, vb, do, c_row,
                                               lse, delta_row, c_col)
            lane_pad = ((0, 0), (0, LANES - n_heads))
            dflog, d_bf[j] = _gate_bwd(f"gate_bwd{i}", jnp.pad(dck[:, :, 0].T, lane_pad),
                                       jnp.pad(dcq[:, 0, :].T, lane_pad), flog, b_pad[j], n_heads)
            gq, gk = fox_q_norm_g[j][None], fox_k_norm_g[j][None]
            dproj, d_gq[j], d_gk[j] = _qkv_bwd(f"qk_norm_bwd{i}", proj, dqn, dkn, dv, dflog, gq, gk, d, n_pad)
            dw_in = _mm_tn(f"proj_in_dw{i}", h, dproj, BF16, tn=896)
            g_in = jnp.transpose(dw_in[:, :n_in].reshape(d, N_DEV, n_in // N_DEV), (1, 0, 2))
            mix_scatter[i] = _exchange_start(f"scatter_mixer{i}", *_scatter_plan([g_in, g_out]))
            dh = _mm_nt(f"proj_in_dx{i}", dproj, w_in[j], F32, tk=896)
            gm = gm + mix_scatter[i]["token"][:1, :1]
            dcur, d_mix[i] = _rms_bwd(f"norm_mix_bwd{i}", dh, xin, gm, dmid)
        else:
            xin, y, zb = mix_saved
            dcur, dz, d_mix[i], dsc, db = _pool_bwd(f"pool_bwd{i}", dmid, xin, zb, gm, w_pool[j], pool_s_full[j])
            dwp = _pool_dw(f"pool_dw{i}", y, dz, n_groups)
            dg = d // n_groups
            g_pw = jnp.transpose(dwp.reshape(n_groups, N_DEV, dg // N_DEV, dg), (1, 0, 2, 3)).astype(BF16)
            g_pbs = jnp.stack([db.reshape(N_DEV, dsh), dsc.reshape(N_DEV, dsh)], axis=1)
            mix_scatter[i] = _exchange_start(f"scatter_mixer{i}", *_scatter_plan([g_pw, g_pbs]))
            pending = mix_scatter[i]["token"][:1, :1]
    grad_x = dcur[None]

    mix_landed = [_exchange_wait(mix_scatter[l], dcur) for l in range(depth)]
    landed = [_exchange_wait(ffn_scatter[l], dcur) for l in range(depth)]
    r_in, r_out = [t[0] for t in mix_landed[0::2]], [t[1] for t in mix_landed[0::2]]
    r_pw = [t[0].reshape(N_DEV, -1, t[0].shape[-1]) for t in mix_landed[1::2]]
    r_pbs = [t[1] for t in mix_landed[1::2]]
    r_gu, r_dn = [t[0] for t in landed], [t[1] for t in landed]
    upd = {}
    upd["fox_w_in"] = _adamw_layers("adamw_w_in", fox_w_in, m_fox_w_in, v_fox_w_in, r_in)
    upd["fox_w_out"] = _adamw_layers("adamw_w_out", fox_w_out, m_fox_w_out, v_fox_w_out, r_out)
    fold = lambda a: a.reshape(n_pool, -1, a.shape[-1])
    upd["pool_w"] = [o.reshape(pool_w.shape) for o in
                     _adamw_layers("adamw_pool_w", fold(pool_w), fold(m_pool_w), fold(v_pool_w), r_pw)]
    pbs = _adamw_layers("adamw_pool_bs", pool_bs, jnp.stack([m_pool_b, m_pool_scale], axis=1),
                        jnp.stack([v_pool_b, v_pool_scale], axis=1), r_pbs)
    upd["pool_b"] = [o[:, 0] for o in pbs]
    upd["pool_scale"] = [o[:, 1] for o in pbs]
    upd["ffn_w_gate_up"] = _adamw_layers("adamw_gate_up", ffn_w_gate_up, m_ffn_w_gate_up, v_ffn_w_gate_up, r_gu)
    upd["ffn_w_down"] = _adamw_layers("adamw_down", ffn_w_down, m_ffn_w_down, v_ffn_w_down, r_dn)

    small_w = (mix_norm_g, ffn_norm_g, fox_b_f, fox_q_norm_g, fox_k_norm_g)
    small_g = _pack_small(jnp.concatenate(d_mix), jnp.concatenate(d_ffn),
                          jnp.concatenate(d_bf)[:, :n_heads], jnp.concatenate(d_gq), jnp.concatenate(d_gk))
    (small_pieces,), = _all_gather_layers("gather_small", [small_g[None]])
    small = _adamw("adamw_small", _pack_small(*small_w),
                   _pack_small(m_mix_norm_g, m_ffn_norm_g, m_fox_b_f, m_fox_q_norm_g, m_fox_k_norm_g),
                   _pack_small(v_mix_norm_g, v_ffn_norm_g, v_fox_b_f, v_fox_q_norm_g, v_fox_k_norm_g),
                   small_pieces)
    small = [_unpack_small(o, *small_w) for o in small]
    for n, name in enumerate(("mix_norm_g", "ffn_norm_g", "fox_b_f", "fox_q_norm_g", "fox_k_norm_g")):
        upd[name] = [o[n] for o in small]

    order = ("mix_norm_g", "ffn_norm_g", "fox_w_in", "fox_b_f", "fox_q_norm_g", "fox_k_norm_g", "fox_w_out",
             "pool_w", "pool_b", "pool_scale", "ffn_w_gate_up", "ffn_w_down")
    return (loss, grad_x) + tuple(upd[name][q] for q in range(4) for name in order)
```

```python
import functools

import jax
import jax.numpy as jnp
from jax import lax
from jax.experimental import pallas as pl
from jax.experimental.pallas import tpu as pltpu

F32 = jnp.float32
BF16 = jnp.bfloat16
MESH = pl.DeviceIdType.MESH

N_DEV = 8
HEAD_DIM = 128
LANES = 128
POOL_WINDOWS = (2, 4, 8, 16)
POOL_HALO = 16
RMS_EPS = 1e-6
NEG_INF = -1e30
ADAM_LR = 0.001
ADAM_B1 = 0.9
ADAM_B2 = 0.999
ADAM_EPS = 1e-08
ADAM_WD = 0.01
ADAM_STEP = 10
VMEM_LIMIT = 52 * 1024 * 1024

NN = (((1,), (0,)), ((), ()))
NT = (((1,), (1,)), ((), ()))
TN = (((0,), (0,)), ((), ()))


def _tile(n, pref, align):
    best = None
    d = align
    while d <= min(n, pref):
        if n % d == 0:
            best = d
        d += align
    return n if best is None else best


def _params(*sem):
    return pltpu.CompilerParams(dimension_semantics=sem, vmem_limit_bytes=VMEM_LIMIT)


def _position():
    x, y, c = lax.axis_index("x"), lax.axis_index("y"), lax.axis_index("c")
    return x, y, c, 4 * x + 2 * y + c


def _peer(x, y, c, k):
    px = 1 - x if k & 4 else x
    py = 1 - y if k & 2 else y
    pc = 1 - c if k & 1 else c
    return (px, py, pc), 4 * px + 2 * py + pc


def _exchange(name, ins, out_shapes, copies):
    n_in, n_cp = len(ins), len(copies)

    def body(*refs):
        in_refs = refs[:n_in]
        out_refs = refs[n_in:n_in + len(out_shapes)]
        send_sems, recv_sems, loc_sems = refs[n_in + len(out_shapes):]
        x, y, c, me = _position()
        local = []
        for ci, (ii, src_of, oi, dst_of) in enumerate(copies):
            cp = pltpu.make_async_copy(src_of(in_refs[ii], me), dst_of(out_refs[oi], me), loc_sems.at[ci])
            cp.start()
            local.append(cp)
        sends, recvs = [], []
        for k in range(1, N_DEV):
            pid, p = _peer(x, y, c, k)
            for ci, (ii, src_of, oi, dst_of) in enumerate(copies):
                sem = ci * (N_DEV - 1) + k - 1
                send = pltpu.make_async_remote_copy(
                    src_ref=src_of(in_refs[ii], p), dst_ref=dst_of(out_refs[oi], me),
                    send_sem=send_sems.at[sem], recv_sem=recv_sems.at[sem],
                    device_id=pid, device_id_type=MESH)
                send.start()
                sends.append(send)
                recvs.append(pltpu.make_async_remote_copy(
                    src_ref=src_of(in_refs[ii], p), dst_ref=dst_of(out_refs[oi], p),
                    send_sem=send_sems.at[sem], recv_sem=recv_sems.at[sem],
                    device_id=pid, device_id_type=MESH))
        for r in recvs:
            r.wait_recv()
        for s in sends:
            s.wait_send()
        for cp in local:
            cp.wait()

    any_spec = pl.BlockSpec(memory_space=pl.ANY)
    return pl.pallas_call(
        body, name=name,
        out_shape=tuple(out_shapes),
        in_specs=[any_spec] * n_in,
        out_specs=tuple([any_spec] * len(out_shapes)),
        scratch_shapes=[pltpu.SemaphoreType.DMA((n_cp * (N_DEV - 1),)),
                        pltpu.SemaphoreType.DMA((n_cp * (N_DEV - 1),)),
                        pltpu.SemaphoreType.DMA((n_cp,))],
    )(*ins)


def _exchange_start(name, ins, out_shapes, copies):
    n_in, n_out, n_cp = len(ins), len(out_shapes), len(copies)

    def body(*refs):
        in_refs = refs[:n_in]
        land_refs = refs[n_in:n_in + n_out]
        send_sems, recv_sems = refs[n_in + n_out:n_in + n_out + 2]
        token_ref = refs[2 * (n_in + n_out) + 2]
        loc_sems = refs[2 * (n_in + n_out) + 3]
        x, y, c, me = _position()
        local = []
        for ci, (ii, src_of, oi, dst_of) in enumerate(copies):
            cp = pltpu.make_async_copy(src_of(in_refs[ii], me), dst_of(land_refs[oi], me), loc_sems.at[ci])
            cp.start()
            local.append(cp)
        for cp in local:
            cp.wait()
        for k in range(1, N_DEV):
            pid, p = _peer(x, y, c, k)
            for ci, (ii, src_of, oi, dst_of) in enumerate(copies):
                sem = ci * (N_DEV - 1) + k - 1
                pltpu.make_async_remote_copy(
                    src_ref=src_of(in_refs[ii], p), dst_ref=dst_of(land_refs[oi], me),
                    send_sem=send_sems.at[sem], recv_sem=recv_sems.at[sem],
                    device_id=pid, device_id_type=MESH).start()
        token_ref[...] = jnp.zeros_like(token_ref)

    hbm = pl.BlockSpec(memory_space=pltpu.HBM)
    sem = pl.BlockSpec(memory_space=pltpu.SEMAPHORE)
    n_sem = n_cp * (N_DEV - 1)
    lands = [pltpu.with_memory_space_constraint(lax.empty(o.shape, o.dtype), pltpu.HBM) for o in out_shapes]
    srcs = [pltpu.with_memory_space_constraint(a, pltpu.HBM) for a in ins]
    res = pl.pallas_call(
        body, name=name,
        out_shape=(pltpu.SemaphoreType.DMA((n_sem,)), pltpu.SemaphoreType.DMA((n_sem,)),
                   *[pltpu.HBM(a.shape, a.dtype) for a in ins],
                   *[pltpu.HBM(o.shape, o.dtype) for o in out_shapes],
                   jax.ShapeDtypeStruct((8, LANES), F32)),
        in_specs=[hbm] * (n_in + n_out),
        out_specs=(sem, sem, *([hbm] * (n_in + n_out)), pl.BlockSpec(memory_space=pltpu.VMEM)),
        input_output_aliases={i: 2 + i for i in range(n_in + n_out)},
        scratch_shapes=[pltpu.SemaphoreType.DMA((n_cp,))],
        compiler_params=pltpu.CompilerParams(has_side_effects=pltpu.SideEffectType.DATAFLOW_SIDE_EFFECTING),
    )(*srcs, *lands)
    return dict(name=name, copies=copies, send=res[0], recv=res[1], srcs=list(res[2:2 + n_in]),
                lands=list(res[2 + n_in:2 + n_in + n_out]), token=res[-1])


def _exchange_wait(handle, after):
    copies, srcs, lands = handle["copies"], handle["srcs"], handle["lands"]
    n_in, n_out = len(srcs), len(lands)

    def body(*refs):
        in_refs = refs[:n_in]
        land_refs = refs[n_in:n_in + n_out]
        send_sems, recv_sems = refs[n_in + n_out:n_in + n_out + 2]
        x, y, c, me = _position()
        waits = []
        for k in range(1, N_DEV):
            pid, p = _peer(x, y, c, k)
            for ci, (ii, src_of, oi, dst_of) in enumerate(copies):
                sem = ci * (N_DEV - 1) + k - 1
                waits.append(pltpu.make_async_remote_copy(
                    src_ref=src_of(in_refs[ii], p), dst_ref=dst_of(land_refs[oi], p),
                    send_sem=send_sems.at[sem], recv_sem=recv_sems.at[sem],
                    device_id=pid, device_id_type=MESH))
        for w in waits:
            w.wait_send()
        for w in waits:
            w.wait_recv()

    hbm = pl.BlockSpec(memory_space=pltpu.HBM)
    sem = pl.BlockSpec(memory_space=pltpu.SEMAPHORE)
    res = pl.pallas_call(
        body, name=handle["name"] + "_wait",
        out_shape=tuple(pltpu.HBM(a.shape, a.dtype) for a in srcs + lands),
        in_specs=[hbm] * (n_in + n_out) + [sem, sem, pl.BlockSpec(memory_space=pl.ANY)],
        out_specs=tuple([hbm] * (n_in + n_out)),
        input_output_aliases={i: i for i in range(n_in + n_out)},
        compiler_params=pltpu.CompilerParams(has_side_effects=pltpu.SideEffectType.DATAFLOW_SIDE_EFFECTING),
    )(*srcs, *lands, handle["send"], handle["recv"], after)
    return list(res[n_in:])


def _gather_plan(stacked):
    ins, outs, copies = [], [], []
    for t in stacked:
        ii = len(ins)
        ins.append(t)
        for l in range(t.shape[0]):
            oi = len(outs)
            outs.append(jax.ShapeDtypeStruct((N_DEV,) + t.shape[1:], t.dtype))
            copies.append((ii, (lambda ref, p, l=l: ref.at[l]), oi, (lambda ref, s: ref.at[s])))
    return ins, outs, copies


def _scatter_plan(blocked):
    outs = [jax.ShapeDtypeStruct(t.shape, t.dtype) for t in blocked]
    copies = [(n, (lambda ref, p: ref.at[p]), n, (lambda ref, s: ref.at[s])) for n in range(len(blocked))]
    return list(blocked), outs, copies


def _all_gather_layers(name, stacked):
    ins, outs, copies = [], [], []
    for t in stacked:
        ii = len(ins)
        ins.append(t)
        for l in range(t.shape[0]):
            oi = len(outs)
            outs.append(jax.ShapeDtypeStruct((N_DEV,) + t.shape[1:], t.dtype))
            copies.append((ii, (lambda ref, p, l=l: ref.at[l]), oi, (lambda ref, s: ref.at[s])))
    res = _exchange(name, ins, outs, copies)
    out, pos = [], 0
    for t in stacked:
        out.append(list(res[pos:pos + t.shape[0]]))
        pos += t.shape[0]
    return out


def _mm(name, mode, a, b, out_shape, *, grid, a_spec, b_spec, o_spec, acc_shape, add=None, add_spec=None, dep=None):
    nk = grid[2]
    dn = {"nn": NN, "nt": NT, "tn": TN}[mode]
    has_add, has_dep = add is not None, dep is not None
    own_acc = nk > 1 and out_shape.dtype != F32

    def body(*refs):
        a_ref, b_ref = refs[:2]
        add_ref = refs[2] if has_add else None
        o_ref = refs[2 + has_add + has_dep]

        def product():
            return lax.dot_general(a_ref[...].astype(BF16), b_ref[...].astype(BF16), dn,
                                   preferred_element_type=F32)

        if nk == 1:
            r = product() + add_ref[...] if has_add else product()
            o_ref[...] = r.astype(o_ref.dtype)
        else:
            acc_ref = refs[-1] if own_acc else o_ref
            k = pl.program_id(2)

            @pl.when(k == 0)
            def _():
                acc_ref[...] = add_ref[...] if has_add else jnp.zeros_like(acc_ref)

            acc_ref[...] += product()
            if own_acc:
                @pl.when(k == nk - 1)
                def _():
                    o_ref[...] = acc_ref[...].astype(o_ref.dtype)

    ins = [a, b] + ([add] if has_add else []) + ([dep] if has_dep else [])
    in_specs = ([a_spec, b_spec] + ([add_spec] if has_add else [])
                + ([pl.BlockSpec(memory_space=pl.ANY)] if has_dep else []))
    scratch = [pltpu.VMEM(acc_shape, F32)] if own_acc else []
    return pl.pallas_call(
        body, name=name, grid=grid, out_shape=out_shape,
        in_specs=in_specs, out_specs=o_spec, scratch_shapes=scratch,
        compiler_params=_params("parallel", "parallel", "arbitrary"),
    )(*ins)


def _mm_nn(name, a, b, out_dtype, add=None, tm=1024, tn=1024, tk=2048):
    m, kd = a.shape
    n = b.shape[1]
    tm, tn, tk = _tile(m, tm, 16), _tile(n, tn, LANES), _tile(kd, tk, LANES)
    return _mm(name, "nn", a, b, jax.ShapeDtypeStruct((m, n), out_dtype),
               grid=(m // tm, n // tn, kd // tk),
               a_spec=pl.BlockSpec((tm, tk), lambda i, j, k: (i, k)),
               b_spec=pl.BlockSpec((tk, tn), lambda i, j, k: (k, j)),
               o_spec=pl.BlockSpec((tm, tn), lambda i, j, k: (i, j)),
               acc_shape=(tm, tn), add=add,
               add_spec=pl.BlockSpec((tm, tn), lambda i, j, k: (i, j)))


def _mm_nt(name, a, b, out_dtype, tm=1024, tn=1024, tk=2048, dep=None):
    m, kd = a.shape
    n = b.shape[0]
    tm, tn, tk = _tile(m, tm, 16), _tile(n, tn, LANES), _tile(kd, tk, LANES)
    return _mm(name, "nt", a, b, jax.ShapeDtypeStruct((m, n), out_dtype),
               grid=(m // tm, n // tn, kd // tk),
               a_spec=pl.BlockSpec((tm, tk), lambda i, j, k: (i, k)),
               b_spec=pl.BlockSpec((tn, tk), lambda i, j, k: (j, k)),
               o_spec=pl.BlockSpec((tm, tn), lambda i, j, k: (i, j)),
               acc_shape=(tm, tn), dep=dep)


def _mm_tn(name, a, b, out_dtype, tm=1024, tn=1024, ts=1024, dep=None):
    s, m = a.shape
    n = b.shape[1]
    tm, tn, ts = _tile(m, tm, LANES), _tile(n, tn, LANES), _tile(s, ts, 16)
    return _mm(name, "tn", a, b, jax.ShapeDtypeStruct((m, n), out_dtype),
               grid=(m // tm, n // tn, s // ts),
               a_spec=pl.BlockSpec((ts, tm), lambda i, j, k: (k, i)),
               b_spec=pl.BlockSpec((ts, tn), lambda i, j, k: (k, j)),
               o_spec=pl.BlockSpec((tm, tn), lambda i, j, k: (i, j)),
               acc_shape=(tm, tn), dep=dep)


def _rms_fwd(name, x, g):
    s, d = x.shape
    tm = _tile(s, 512, 16)

    def body(x_ref, g_ref, h_ref):
        xv = x_ref[...]
        r = lax.rsqrt(jnp.mean(xv * xv, axis=-1, keepdims=True) + RMS_EPS)
        h_ref[...] = ((xv * r) * g_ref[...]).astype(BF16)

    return pl.pallas_call(
        body, name=name, grid=(s // tm,), out_shape=jax.ShapeDtypeStruct((s, d), BF16),
        in_specs=[pl.BlockSpec((tm, d), lambda i: (i, 0)), pl.BlockSpec((1, d), lambda i: (0, 0))],
        out_specs=pl.BlockSpec((tm, d), lambda i: (i, 0)),
        compiler_params=_params("parallel"),
    )(x, g)


def _rms_bwd(name, dh, x, g, dres):
    s, d = x.shape
    tm = _tile(s, 256, 16)

    def body(dh_ref, x_ref, g_ref, dres_ref, dx_ref, dg_ref):
        i = pl.program_id(0)
        xv = x_ref[...]
        r = lax.rsqrt(jnp.mean(xv * xv, axis=-1, keepdims=True) + RMS_EPS)
        xhat = xv * r
        dhv = dh_ref[...].astype(F32)
        gdh = dhv * g_ref[...]
        dx_ref[...] = dres_ref[...] + r * (gdh - xhat * jnp.mean(gdh * xhat, axis=-1, keepdims=True))
        part = jnp.sum(dhv * xhat, axis=0, keepdims=True)

        @pl.when(i == 0)
        def _():
            dg_ref[...] = part

        @pl.when(i > 0)
        def _():
            dg_ref[...] += part

    row = pl.BlockSpec((tm, d), lambda i: (i, 0))
    vec = pl.BlockSpec((1, d), lambda i: (0, 0))
    return pl.pallas_call(
        body, name=name, grid=(s // tm,),
        out_shape=(jax.ShapeDtypeStruct((s, d), F32), jax.ShapeDtypeStruct((1, d), F32)),
        in_specs=[row, row, vec, row], out_specs=(row, vec),
        compiler_params=_params("arbitrary"),
    )(dh, x, g, dres)


def _split3(v):
    hi = v.astype(BF16)
    r1 = v - hi.astype(F32)
    mid = r1.astype(BF16)
    lo = (r1 - mid.astype(F32)).astype(BF16)
    return hi, mid, lo


def _tri_sum(tri, v):
    hi, mid, lo = _split3(v)
    dot = functools.partial(lax.dot_general, dimension_numbers=NN, preferred_element_type=F32)
    return dot(tri, hi) + dot(tri, mid) + dot(tri, lo)


def _gate_fwd(name, flog, b_pad):
    s = flog.shape[0]
    tb = _tile(s, 256, 16)

    def body(f_ref, b_ref, c_ref, carry_ref):
        i = pl.program_id(0)

        @pl.when(i == 0)
        def _():
            carry_ref[...] = jnp.zeros_like(carry_ref)

        z = f_ref[...] + b_ref[...]
        lf = jnp.minimum(z, 0.0) - jnp.log(1.0 + jnp.exp(-jnp.abs(z)))
        rows = lax.broadcasted_iota(jnp.int32, (tb, tb), 0)
        cols = lax.broadcasted_iota(jnp.int32, (tb, tb), 1)
        tri = (rows >= cols).astype(BF16)
        c_ref[...] = _tri_sum(tri, lf) + carry_ref[...]
        carry_ref[...] = c_ref[pl.ds(tb - 1, 1), :]

    return pl.pallas_call(
        body, name=name, grid=(s // tb,), out_shape=jax.ShapeDtypeStruct((s, LANES), F32),
        in_specs=[pl.BlockSpec((tb, LANES), lambda i: (i, 0)), pl.BlockSpec((1, LANES), lambda i: (0, 0))],
        out_specs=pl.BlockSpec((tb, LANES), lambda i: (i, 0)),
        scratch_shapes=[pltpu.VMEM((1, LANES), F32)],
        compiler_params=_params("arbitrary"),
    )(flog, b_pad)


def _gate_bwd(name, dck, dcq, flog, b_pad, n_heads):
    s = flog.shape[0]
    tb = _tile(s, 256, 16)
    nb = s // tb

    def body(dck_ref, dcq_ref, f_ref, b_ref, df_ref, db_ref, carry_ref, tmp_ref):
        i = pl.program_id(0)

        @pl.when(i == 0)
        def _():
            carry_ref[...] = jnp.zeros_like(carry_ref)

        rows = lax.broadcasted_iota(jnp.int32, (tb, tb), 0)
        cols = lax.broadcasted_iota(jnp.int32, (tb, tb), 1)
        tri = (rows <= cols).astype(BF16)
        tmp_ref[...] = _tri_sum(tri, dck_ref[...] + dcq_ref[...]) + carry_ref[...]
        carry_ref[...] = tmp_ref[pl.ds(0, 1), :]
        z = f_ref[...] + b_ref[...]
        lane = lax.broadcasted_iota(jnp.int32, (tb, LANES), 1)
        df = jnp.where(lane < n_heads, tmp_ref[...] / (1.0 + jnp.exp(z)), 0.0)
        df_ref[...] = df.astype(BF16)
        part = jnp.sum(df, axis=0, keepdims=True)

        @pl.when(i == 0)
        def _():
            db_ref[...] = part

        @pl.when(i > 0)
        def _():
            db_ref[...] += part

    rev = pl.BlockSpec((tb, LANES), lambda i: (nb - 1 - i, 0))
    vec = pl.BlockSpec((1, LANES), lambda i: (0, 0))
    return pl.pallas_call(
        body, name=name, grid=(nb,),
        out_shape=(jax.ShapeDtypeStruct((s, LANES), BF16), jax.ShapeDtypeStruct((1, LANES), F32)),
        in_specs=[rev, rev, rev, vec], out_specs=(rev, vec),
        scratch_shapes=[pltpu.VMEM((1, LANES), F32), pltpu.VMEM((tb, LANES), F32)],
        compiler_params=_params("arbitrary"),
    )(dck, dcq, flog, b_pad)


def _head_rms(v, g):
    r = lax.rsqrt(jnp.mean(v * v, axis=-1, keepdims=True) + RMS_EPS)
    return (v * r) * g


def _qkv_fwd(name, proj, gq, gk, d):
    s = proj.shape[0]
    tm = _tile(s, 256, 16)
    n_heads = d // HEAD_DIM

    def body(q_ref, k_ref, v_ref, gq_ref, gk_ref, qn_ref, kn_ref, vb_ref):
        for h in range(n_heads):
            sl = slice(h * HEAD_DIM, (h + 1) * HEAD_DIM)
            qn_ref[:, sl] = _head_rms(q_ref[:, sl], gq_ref[...]).astype(BF16)
            kn_ref[:, sl] = _head_rms(k_ref[:, sl], gk_ref[...]).astype(BF16)
        vb_ref[...] = v_ref[...].astype(BF16)

    col = lambda c: pl.BlockSpec((tm, d), lambda i, c=c: (i, c))
    vec = pl.BlockSpec((1, HEAD_DIM), lambda i: (0, 0))
    out = jax.ShapeDtypeStruct((s, d), BF16)
    return pl.pallas_call(
        body, name=name, grid=(s // tm,), out_shape=(out, out, out),
        in_specs=[col(0), col(1), col(2), vec, vec], out_specs=(col(0), col(0), col(0)),
        compiler_params=_params("parallel"),
    )(proj, proj, proj, gq, gk)


def _qkv_bwd(name, proj, dqn, dkn, dv, dflog, gq, gk, d, n_pad):
    s = proj.shape[0]
    tm = _tile(s, 256, 16)
    n_heads = d // HEAD_DIM

    def head_bwd(raw, dy, g):
        r = lax.rsqrt(jnp.mean(raw * raw, axis=-1, keepdims=True) + RMS_EPS)
        hat = raw * r
        gdy = dy * g
        dx = r * (gdy - hat * jnp.mean(gdy * hat, axis=-1, keepdims=True))
        return dx, jnp.sum(dy * hat, axis=0, keepdims=True)

    def body(q_ref, k_ref, dqn_ref, dkn_ref, dv_ref, df_ref, gq_ref, gk_ref, dp_ref, dgq_ref, dgk_ref):
        i = pl.program_id(0)
        accq = jnp.zeros((1, HEAD_DIM), F32)
        acck = jnp.zeros((1, HEAD_DIM), F32)
        for h in range(n_heads):
            sl = slice(h * HEAD_DIM, (h + 1) * HEAD_DIM)
            dq, pq = head_bwd(q_ref[:, sl], dqn_ref[:, sl], gq_ref[...])
            dk, pk = head_bwd(k_ref[:, sl], dkn_ref[:, sl], gk_ref[...])
            dp_ref[:, sl] = dq.astype(BF16)
            dp_ref[:, d + h * HEAD_DIM:d + (h + 1) * HEAD_DIM] = dk.astype(BF16)
            accq, acck = accq + pq, acck + pk
        dp_ref[:, 2 * d:3 * d] = dv_ref[...]
        dp_ref[:, 3 * d:] = df_ref[...]

        @pl.when(i == 0)
        def _():
            dgq_ref[...] = accq
            dgk_ref[...] = acck

        @pl.when(i > 0)
        def _():
            dgq_ref[...] += accq
            dgk_ref[...] += acck

    col = lambda c: pl.BlockSpec((tm, d), lambda i, c=c: (i, c))
    vec = pl.BlockSpec((1, HEAD_DIM), lambda i: (0, 0))
    return pl.pallas_call(
        body, name=name, grid=(s // tm,),
        out_shape=(jax.ShapeDtypeStruct((s, n_pad), BF16), jax.ShapeDtypeStruct((1, HEAD_DIM), F32),
                   jax.ShapeDtypeStruct((1, HEAD_DIM), F32)),
        in_specs=[col(0), col(1), col(0), col(0), col(0), pl.BlockSpec((tm, LANES), lambda i: (i, 0)), vec, vec],
        out_specs=(pl.BlockSpec((tm, n_pad), lambda i: (i, 0)), vec, vec),
        compiler_params=_params("arbitrary"),
    )(proj, proj, dqn, dkn, dv, dflog, gq, gk)


def _attn_fwd(name, qn, kn, vt, c_row, c_col):
    s, d = qn.shape
    n_heads = d // HEAD_DIM
    t = _tile(s, 512, LANES)
    scale = HEAD_DIM ** -0.5

    hp = 2 if n_heads % 2 == 0 else 1
    log2e = 1.4426950408889634

    def body(q_ref, k_ref, vt_ref, cq_ref, ck_ref, o_ref, lse_ref, m_ref, l_ref, acc_ref):
        i = pl.program_id(1)
        m_ref[...] = jnp.full(m_ref.shape, NEG_INF, F32)
        l_ref[...] = jnp.zeros_like(l_ref)
        acc_ref[...] = jnp.zeros_like(acc_ref)

        def step(j, masked):
            start = pl.multiple_of(j * t, t)
            for hh in range(hp):
                sl = slice(hh * HEAD_DIM, (hh + 1) * HEAD_DIM)
                kj = k_ref[pl.ds(start, t), sl]
                vtj = vt_ref[sl, pl.ds(start, t)]
                bias = cq_ref[hh] * log2e - ck_ref[hh, pl.ds(start, t), :] * log2e
                st = lax.dot_general(kj, q_ref[:, sl], NT, preferred_element_type=F32) * (scale * log2e) + bias
                if masked:
                    rows = lax.broadcasted_iota(jnp.int32, (t, t), 0)
                    cols = lax.broadcasted_iota(jnp.int32, (t, t), 1)
                    st = jnp.where(cols >= rows, st, NEG_INF)
                m_prev = m_ref[hh]
                m_new = jnp.maximum(m_prev, jnp.max(st, axis=0, keepdims=True))
                pt = jnp.exp2(st - m_new)
                alpha = jnp.exp2(m_prev - m_new)
                l_ref[hh] = alpha * l_ref[hh] + jnp.sum(pt, axis=0, keepdims=True)
                acc_ref[hh] = alpha * acc_ref[hh] + lax.dot_general(
                    vtj, pt.astype(BF16), NN, preferred_element_type=F32)
                m_ref[hh] = m_new

        def loop_body(j, carry):
            step(j, False)
            return carry

        lax.fori_loop(0, i, loop_body, 0)
        step(i, True)
        for hh in range(hp):
            sl = slice(hh * HEAD_DIM, (hh + 1) * HEAD_DIM)
            o_ref[:, sl] = (acc_ref[hh] / l_ref[hh]).T.astype(BF16)
            lse_ref[hh] = (m_ref[hh] + jnp.log2(l_ref[hh])) * (1.0 / log2e)

    wide = hp * HEAD_DIM
    row_blk = pl.BlockSpec((hp, 1, t), lambda h, i: (h, 0, i))
    return pl.pallas_call(
        body, name=name, grid=(n_heads // hp, s // t),
        out_shape=(jax.ShapeDtypeStruct((s, d), BF16), jax.ShapeDtypeStruct((n_heads, 1, s), F32)),
        in_specs=[pl.BlockSpec((t, wide), lambda h, i: (i, h)),
                  pl.BlockSpec((s, wide), lambda h, i: (0, h)),
                  pl.BlockSpec((wide, s), lambda h, i: (h, 0)),
                  row_blk, pl.BlockSpec((hp, s, 1), lambda h, i: (h, 0, 0))],
        out_specs=(pl.BlockSpec((t, wide), lambda h, i: (i, h)), row_blk),
        scratch_shapes=[pltpu.VMEM((hp, 1, t), F32), pltpu.VMEM((hp, 1, t), F32),
                        pltpu.VMEM((hp, HEAD_DIM, t), F32)],
        compiler_params=_params("parallel", "arbitrary"),
    )(qn, kn, vt, c_row, c_col)


def _attn_delta(name, o, do, n_heads):
    s, d = o.shape
    tm = _tile(s, 256, 16)

    def body(o_ref, do_ref, dl_ref):
        lane = lax.broadcasted_iota(jnp.int32, (tm, LANES), 1)
        acc = jnp.zeros((tm, LANES), F32)
        for h in range(n_heads):
            sl = slice(h * HEAD_DIM, (h + 1) * HEAD_DIM)
            col = jnp.sum(o_ref[:, sl].astype(F32) * do_ref[:, sl].astype(F32), axis=-1, keepdims=True)
            acc = jnp.where(lane == h, col, acc)
        dl_ref[...] = acc

    row = pl.BlockSpec((tm, d), lambda i: (i, 0))
    return pl.pallas_call(
        body, name=name, grid=(s // tm,), out_shape=jax.ShapeDtypeStruct((s, LANES), F32),
        in_specs=[row, row], out_specs=pl.BlockSpec((tm, LANES), lambda i: (i, 0)),
        compiler_params=_params("parallel"),
    )(o, do)


def _attn_bwd(name, qn, kn, vb, do, c_row, lse_row, delta_row, c_col):
    s, d = qn.shape
    n_heads = d // HEAD_DIM
    t = _tile(s, 512, LANES)
    nq = s // t
    scale = HEAD_DIM ** -0.5

    def body(q_ref, do_ref, cr_ref, lse_ref, dl_ref, k_ref, v_ref, ck_ref,
             dq_ref, dk_ref, dv_ref, dc_ref, dcq_ref, dk_acc, dv_acc, dc_acc):
        j = pl.program_id(1)

        @pl.when(j == 0)
        def _():
            dq_ref[...] = jnp.zeros_like(dq_ref)
            dcq_ref[...] = jnp.zeros_like(dcq_ref)

        kj = k_ref[...]
        vj = v_ref[...]
        ckj = ck_ref[...]
        dk_acc[...] = jnp.zeros_like(dk_acc)
        dv_acc[...] = jnp.zeros_like(dv_acc)
        dc_acc[...] = jnp.zeros_like(dc_acc)

        def step(i, masked):
            start = pl.multiple_of(i * t, t)
            qi = q_ref[pl.ds(start, t), :]
            doi = do_ref[pl.ds(start, t), :]
            bias = cr_ref[:, pl.ds(start, t)] - lse_ref[:, pl.ds(start, t)]
            dli = dl_ref[:, pl.ds(start, t)]
            st = lax.dot_general(kj, qi, NT, preferred_element_type=F32) * scale + (bias - ckj)
            if masked:
                rows = lax.broadcasted_iota(jnp.int32, (t, t), 0)
                cols = lax.broadcasted_iota(jnp.int32, (t, t), 1)
                st = jnp.where(cols >= rows, st, NEG_INF)
            pt = jnp.exp(st)
            dpt = lax.dot_general(vj, doi, NT, preferred_element_type=F32)
            dst = pt * (dpt - dli)
            dsb = dst.astype(BF16)
            dv_acc[...] += lax.dot_general(pt.astype(BF16), doi, NN, preferred_element_type=F32)
            dk_acc[...] += lax.dot_general(dsb, qi, NN, preferred_element_type=F32)
            dq_ref[pl.ds(start, t), :] += lax.dot_general(dsb, kj, TN, preferred_element_type=F32) * scale
            dc_acc[...] += jnp.sum(dst, axis=1, keepdims=True)
            dcq_ref[:, pl.ds(start, t)] += jnp.sum(dst, axis=0, keepdims=True)

        step(j, True)

        def loop_body(i, carry):
            step(i, False)
            return carry

        lax.fori_loop(j + 1, nq, loop_body, 0)
        dk_ref[...] = dk_acc[...] * scale
        dv_ref[...] = dv_acc[...].astype(BF16)
        dc_ref[...] = -dc_acc[...]

    head_all = pl.BlockSpec((s, HEAD_DIM), lambda h, j: (0, h))
    row_all = pl.BlockSpec((None, 1, s), lambda h, j: (h, 0, 0))
    blk = pl.BlockSpec((t, HEAD_DIM), lambda h, j: (j, h))
    col_blk = pl.BlockSpec((None, t, 1), lambda h, j: (h, j, 0))
    return pl.pallas_call(
        body, name=name, grid=(n_heads, nq),
        out_shape=(jax.ShapeDtypeStruct((s, d), F32), jax.ShapeDtypeStruct((s, d), F32),
                   jax.ShapeDtypeStruct((s, d), BF16), jax.ShapeDtypeStruct((n_heads, s, 1), F32),
                   jax.ShapeDtypeStruct((n_heads, 1, s), F32)),
        in_specs=[head_all, head_all, row_all, row_all, row_all, blk, blk, col_blk],
        out_specs=(head_all, blk, blk, col_blk, row_all),
        scratch_shapes=[pltpu.VMEM((t, HEAD_DIM), F32), pltpu.VMEM((t, HEAD_DIM), F32), pltpu.VMEM((t, 1), F32)],
        compiler_params=_params("parallel", "arbitrary"),
    )(qn, do, c_row, lse_row, delta_row, kn, vb, c_col)


def _ffn_up(name, h, w_gu):
    s, d = h.shape
    fs = w_gu.shape[2]
    half = N_DEV // 2
    tm = _tile(s, 512, 16)

    def body(h_ref, wg_ref, wu_ref, g_ref, u_ref, a_ref):
        hv = h_ref[...]
        g = lax.dot_general(hv, wg_ref[...], NN, preferred_element_type=F32)
        u = lax.dot_general(hv, wu_ref[...], NN, preferred_element_type=F32)
        g_ref[...] = g.astype(BF16)
        u_ref[...] = u.astype(BF16)
        a_ref[...] = (g * jax.nn.sigmoid(g) * u).astype(BF16)

    out = jax.ShapeDtypeStruct((s, half * fs), BF16)
    ospec = pl.BlockSpec((tm, fs), lambda j, i: (i, j))
    return pl.pallas_call(
        body, name=name, grid=(half, s // tm), out_shape=(out, out, out),
        in_specs=[pl.BlockSpec((tm, d), lambda j, i: (i, 0)),
                  pl.BlockSpec((None, d, fs), lambda j, i: (j, 0, 0)),
                  pl.BlockSpec((None, d, fs), lambda j, i: (j + half, 0, 0))],
        out_specs=(ospec, ospec, ospec),
        compiler_params=_params("parallel", "parallel"),
    )(h, w_gu, w_gu)


def _ffn_dact(name, dx, w_dn4, g, u):
    s, d = dx.shape
    half, fs = w_dn4.shape[0], w_dn4.shape[1]
    tm = _tile(s, 512, 16)

    cut = (fs // (2 * LANES)) * LANES

    def body(dx_ref, w_ref, g_ref, u_ref, dgu_ref):
        dxv = dx_ref[...].astype(BF16)
        for lo, hi in ((0, cut), (cut, fs)) if cut else ((0, fs),):
            da = lax.dot_general(dxv, w_ref[lo:hi, :], NT, preferred_element_type=F32)
            gv = g_ref[:, lo:hi].astype(F32)
            uv = u_ref[:, lo:hi].astype(F32)
            sig = jax.nn.sigmoid(gv)
            dgu_ref[0, :, lo:hi] = (da * uv * (sig * (1.0 + gv * (1.0 - sig)))).astype(BF16)
            dgu_ref[1, :, lo:hi] = (da * (gv * sig)).astype(BF16)

    blk = pl.BlockSpec((tm, fs), lambda j, i: (i, j))
    return pl.pallas_call(
        body, name=name, grid=(half, s // tm),
        out_shape=jax.ShapeDtypeStruct((2, s, half * fs), BF16),
        in_specs=[pl.BlockSpec((tm, d), lambda j, i: (i, 0)),
                  pl.BlockSpec((None, fs, d), lambda j, i: (j, 0, 0)), blk, blk],
        out_specs=pl.BlockSpec((2, tm, fs), lambda j, i: (0, i, j)),
        compiler_params=_params("parallel", "parallel"),
    )(dx, w_dn4, g, u)


def _ffn_dw_gu(name, h, dgu, dep=None):
    s, d = h.shape
    half, fs = N_DEV // 2, dgu.shape[2] // (N_DEV // 2)
    tm, ts = _tile(d, 1024, LANES), _tile(s, 1024, 16)
    return _mm(name, "tn", h, dgu, jax.ShapeDtypeStruct((N_DEV, d, fs), BF16),
               grid=(d // tm, N_DEV, s // ts),
               a_spec=pl.BlockSpec((ts, tm), lambda i, j, k: (k, i)),
               b_spec=pl.BlockSpec((None, ts, fs), lambda i, j, k: (j // half, k, j % half)),
               o_spec=pl.BlockSpec((None, tm, fs), lambda i, j, k: (j, i, 0)),
               acc_shape=(tm, fs), dep=dep)


def _ffn_dh(name, dgu, w_gu, dep=None):
    s = dgu.shape[1]
    d, fs = w_gu.shape[1], w_gu.shape[2]
    half = N_DEV // 2
    tm = _tile(s, 512, 16)
    return _mm(name, "nt", dgu, w_gu, jax.ShapeDtypeStruct((s, d), F32),
               grid=(s // tm, 1, N_DEV),
               a_spec=pl.BlockSpec((None, tm, fs), lambda i, j, k: (k // half, i, k % half)),
               b_spec=pl.BlockSpec((None, d, fs), lambda i, j, k: (k, 0, 0)),
               o_spec=pl.BlockSpec((tm, d), lambda i, j, k: (i, 0)),
               acc_shape=(tm, d), dep=dep)


def _pool_fwd(name, x, g, w, b, sc):
    s, d = x.shape
    dg = d // len(POOL_WINDOWS)
    tm = _tile(s, 256, POOL_HALO)
    per = tm // POOL_HALO

    def body(x_ref, xh_ref, g_ref, w_ref, b_ref, sc_ref, xo_ref, y_ref, zb_ref):
        i = pl.program_id(0)
        gv = g_ref[...]

        def norm(v):
            return (v * lax.rsqrt(jnp.mean(v * v, axis=-1, keepdims=True) + RMS_EPS)) * gv

        h = norm(x_ref[...])
        halo = norm(xh_ref[...]) * (i > 0).astype(F32)
        ext = jnp.concatenate([halo, h], axis=0)
        t = i * tm + lax.broadcasted_iota(jnp.int32, (tm, 1), 0)
        for gi, win in enumerate(POOL_WINDOWS):
            sl = slice(gi * dg, (gi + 1) * dg)
            acc = ext[:, sl]
            step = 1
            while step < win:
                acc = acc + pltpu.roll(acc, step, 0)
                step *= 2
            inv = 1.0 / jnp.minimum(t + 1, win).astype(F32)
            yg = (acc[POOL_HALO:, :] * inv - h[:, sl]).astype(BF16)
            y_ref[:, sl] = yg
            zb = lax.dot_general(yg, w_ref[gi], NN, preferred_element_type=F32) + b_ref[:, sl]
            zb_ref[:, sl] = zb
            xo_ref[:, sl] = x_ref[:, sl] + zb * sc_ref[:, sl]

    row = pl.BlockSpec((tm, d), lambda i: (i, 0))
    vec = pl.BlockSpec((1, d), lambda i: (0, 0))
    return pl.pallas_call(
        body, name=name, grid=(s // tm,),
        out_shape=(jax.ShapeDtypeStruct((s, d), F32), jax.ShapeDtypeStruct((s, d), BF16),
                   jax.ShapeDtypeStruct((s, d), F32)),
        in_specs=[row, pl.BlockSpec((POOL_HALO, d), lambda i: (jnp.maximum(i * per - 1, 0), 0)),
                  vec, pl.BlockSpec(w.shape, lambda i: (0, 0, 0)), vec, vec],
        out_specs=(row, row, row),
        compiler_params=_params("parallel"),
    )(x, x, g, w, b, sc)


def _pool_bwd(name, dout, x, zb, g, w, sc):
    s, d = x.shape
    dg = d // len(POOL_WINDOWS)
    tm = _tile(s, 256, POOL_HALO)
    per = tm // POOL_HALO
    nb = s // tm
    ext_rows = tm + POOL_HALO

    def body(do_ref, doh_ref, x_ref, zb_ref, g_ref, w_ref, sc_ref, dx_ref, dz_ref, dgn_ref, dsc_ref, db_ref):
        i = pl.program_id(0)
        scv = sc_ref[...]
        dov = do_ref[...]
        dz = dov * scv
        dz_ref[...] = dz.astype(BF16)
        halo = doh_ref[...] * scv * (i < nb - 1).astype(F32)
        ext = jnp.concatenate([dz, halo], axis=0).astype(BF16)
        t = i * tm + lax.broadcasted_iota(jnp.int32, (ext_rows, 1), 0)
        parts = []
        for gi, win in enumerate(POOL_WINDOWS):
            sl = slice(gi * dg, (gi + 1) * dg)
            dy = lax.dot_general(ext[:, sl], w_ref[gi], NT, preferred_element_type=F32)
            acc = dy * (1.0 / jnp.minimum(t + 1, win).astype(F32))
            step = 1
            while step < win:
                acc = acc + pltpu.roll(acc, ext_rows - step, 0)
                step *= 2
            parts.append(acc[:tm, :] - dy[:tm, :])
        dh = jnp.concatenate(parts, axis=1)
        xv = x_ref[...]
        r = lax.rsqrt(jnp.mean(xv * xv, axis=-1, keepdims=True) + RMS_EPS)
        xhat = xv * r
        gdh = dh * g_ref[...]
        dx_ref[...] = dov + r * (gdh - xhat * jnp.mean(gdh * xhat, axis=-1, keepdims=True))
        pgn = jnp.sum(dh * xhat, axis=0, keepdims=True)
        psc = jnp.sum(dov * zb_ref[...], axis=0, keepdims=True)
        pb = jnp.sum(dz, axis=0, keepdims=True)

        @pl.when(i == 0)
        def _():
            dgn_ref[...] = pgn
            dsc_ref[...] = psc
            db_ref[...] = pb

        @pl.when(i > 0)
        def _():
            dgn_ref[...] += pgn
            dsc_ref[...] += psc
            db_ref[...] += pb

    row = pl.BlockSpec((tm, d), lambda i: (i, 0))
    vec = pl.BlockSpec((1, d), lambda i: (0, 0))
    vshape = jax.ShapeDtypeStruct((1, d), F32)
    return pl.pallas_call(
        body, name=name, grid=(nb,),
        out_shape=(jax.ShapeDtypeStruct((s, d), F32), jax.ShapeDtypeStruct((s, d), BF16), vshape, vshape, vshape),
        in_specs=[row, pl.BlockSpec((POOL_HALO, d), lambda i: (jnp.minimum((i + 1) * per, s // POOL_HALO - 1), 0)),
                  row, row, vec, pl.BlockSpec(w.shape, lambda i: (0, 0, 0)), vec],
        out_specs=(row, row, vec, vec, vec),
        compiler_params=_params("arbitrary"),
    )(dout, dout, x, zb, g, w, sc)


def _pool_dw(name, y, dz, n_groups):
    s, d = y.shape
    dg = d // n_groups
    ts = _tile(s, 1024, 16)
    return _mm(name, "tn", y, dz, jax.ShapeDtypeStruct((n_groups, dg, dg), F32),
               grid=(n_groups, 1, s // ts),
               a_spec=pl.BlockSpec((ts, dg), lambda i, j, k: (k, i)),
               b_spec=pl.BlockSpec((ts, dg), lambda i, j, k: (k, i)),
               o_spec=pl.BlockSpec((None, dg, dg), lambda i, j, k: (i, 0, 0)),
               acc_shape=(dg, dg))


def _loss_head(name, y, tgt):
    s, d = y.shape
    tm = _tile(s, 512, 16)

    def body(y_ref, t_ref, dy_ref, l_ref):
        i = pl.program_id(0)
        e = y_ref[...] - t_ref[...]
        dy_ref[...] = e * (1.0 / d)
        part = jnp.sum(jnp.mean(e * e, axis=-1, keepdims=True), axis=0, keepdims=True)
        part = jnp.broadcast_to(part, l_ref.shape)

        @pl.when(i == 0)
        def _():
            l_ref[...] = part

        @pl.when(i > 0)
        def _():
            l_ref[...] += part

    row = pl.BlockSpec((tm, d), lambda i: (i, 0))
    return pl.pallas_call(
        body, name=name, grid=(s // tm,),
        out_shape=(jax.ShapeDtypeStruct((s, d), F32), jax.ShapeDtypeStruct((8, LANES), F32)),
        in_specs=[row, row], out_specs=(row, pl.BlockSpec((8, LANES), lambda i: (0, 0))),
        compiler_params=_params("arbitrary"),
    )(y, tgt)


def _adam_update(w_ref, m_ref, v_ref, p_ref, g_ref, d_ref, nm_ref, nv_ref):
    g = p_ref[0].astype(F32)
    for k in range(1, N_DEV):
        g = g + p_ref[k].astype(F32)
    mn = ADAM_B1 * m_ref[...] + (1.0 - ADAM_B1) * g
    vn = ADAM_B2 * v_ref[...] + (1.0 - ADAM_B2) * (g * g)
    m_hat = mn / (1.0 - ADAM_B1 ** ADAM_STEP)
    v_hat = vn / (1.0 - ADAM_B2 ** ADAM_STEP)
    g_ref[...] = g
    d_ref[...] = -ADAM_LR * (m_hat / (jnp.sqrt(v_hat) + ADAM_EPS) + ADAM_WD * w_ref[...])
    nm_ref[...] = mn
    nv_ref[...] = vn


def _adamw_layers(name, w, m, v, pieces):
    n_layers, r, c = w.shape
    tr = _tile(r, 128, 16)

    def body(w_ref, m_ref, v_ref, *rest):
        p_refs, outs = rest[:n_layers], rest[n_layers:]
        layer = pl.program_id(0)
        for l in range(n_layers):
            @pl.when(layer == l)
            def _(l=l):
                _adam_update(w_ref, m_ref, v_ref, p_refs[l], *outs)

    blk = pl.BlockSpec((None, tr, c), lambda l, i: (l, i, 0))
    terms = [pl.BlockSpec((N_DEV, tr, c), lambda l, i, n=n: (0, jnp.where(l == n, i, 0), 0))
             for n in range(n_layers)]
    out = jax.ShapeDtypeStruct(w.shape, F32)
    return list(pl.pallas_call(
        body, name=name, grid=(n_layers, r // tr), out_shape=(out, out, out, out),
        in_specs=[blk, blk, blk] + terms, out_specs=(blk, blk, blk, blk),
        compiler_params=_params("parallel", "parallel"),
    )(w, m, v, *pieces))


def _adamw(name, w, m, v, pieces):
    r, c = w.shape
    tr = _tile(r, 128, 16)

    def body(w_ref, m_ref, v_ref, p_ref, g_ref, d_ref, nm_ref, nv_ref):
        _adam_update(w_ref, m_ref, v_ref, p_ref, g_ref, d_ref, nm_ref, nv_ref)

    blk = pl.BlockSpec((tr, c), lambda i: (i, 0))
    out = jax.ShapeDtypeStruct((r, c), F32)
    return pl.pallas_call(
        body, name=name, grid=(r // tr,), out_shape=(out, out, out, out),
        in_specs=[blk, blk, blk, pl.BlockSpec((N_DEV, tr, c), lambda i: (0, i, 0))],
        out_specs=(blk, blk, blk, blk),
        compiler_params=_params("parallel"),
    )(w, m, v, pieces)


def _pack_small(mix, ffn, b_f, gq, gk):
    def rows(a):
        a = a.reshape(-1, LANES) if a.shape[-1] >= LANES else jnp.pad(a, ((0, 0), (0, LANES - a.shape[-1])))
        return jnp.pad(a, ((0, -a.shape[0] % 8), (0, 0)))
    return jnp.concatenate([rows(mix), rows(ffn), rows(b_f), rows(gq), rows(gk)], axis=0)


def _unpack_small(p, mix, ffn, b_f, gq, gk):
    out, pos = [], 0
    for a in (mix, ffn, b_f, gq, gk):
        n = a.size // LANES if a.shape[-1] >= LANES else a.shape[0]
        blk = p[pos:pos + n]
        out.append(blk.reshape(a.shape) if a.shape[-1] >= LANES else blk[:, :a.shape[-1]])
        pos += n + (-n % 8)
    return out


def kernel(x, mix_norm_g, ffn_norm_g, fox_w_in, fox_b_f, fox_q_norm_g, fox_k_norm_g, fox_w_out, pool_w, pool_b, pool_scale, ffn_w_gate_up, ffn_w_down, loss_target, m_mix_norm_g, m_ffn_norm_g, m_fox_w_in, m_fox_b_f, m_fox_q_norm_g, m_fox_k_norm_g, m_fox_w_out, m_pool_w, m_pool_b, m_pool_scale, m_ffn_w_gate_up, m_ffn_w_down, v_mix_norm_g, v_ffn_norm_g, v_fox_w_in, v_fox_b_f, v_fox_q_norm_g, v_fox_k_norm_g, v_fox_w_out, v_pool_w, v_pool_b, v_pool_scale, v_ffn_w_gate_up, v_ffn_w_down):
    xs, tgt = x[0], loss_target[0]
    s, d = xs.shape
    depth = mix_norm_g.shape[0]
    n_fox, n_pool = fox_w_in.shape[0], pool_w.shape[0]
    n_heads = d // HEAD_DIM
    n_in = fox_w_in.shape[2] * N_DEV
    n_pad = 3 * d + LANES
    n_groups = pool_w.shape[1]
    dsh = d // N_DEV
    half = N_DEV // 2
    axes = ("x", "y", "c")

    w_in_bf, w_out_bf, pool_w_bf = fox_w_in.astype(BF16), fox_w_out.astype(BF16), pool_w.astype(BF16)
    gu_bf, dn_bf = ffn_w_gate_up.astype(BF16), ffn_w_down.astype(BF16)
    pool_bs = jnp.stack([pool_b, pool_scale], axis=1)
    (w_in0,), (w_out0,) = _all_gather_layers("gather_mixer0", [w_in_bf[:1], w_out_bf[:1]])
    mix_gather, ffn_gather = [None] * depth, [None] * depth
    for l in range(depth):
        j = l // 2
        if l > 0:
            shards = [w_in_bf[j:j + 1], w_out_bf[j:j + 1]] if l % 2 == 0 else [pool_w_bf[j:j + 1], pool_bs[j:j + 1]]
            mix_gather[l] = _exchange_start(f"gather_mixer{l}", *_gather_plan(shards))
        ffn_gather[l] = _exchange_start(f"gather_ffn{l}", *_gather_plan([gu_bf[l:l + 1], dn_bf[l:l + 1]]))
    started = sum(hd["token"][:1, :1] for hd in mix_gather[1:] + ffn_gather)
    w_gu_g, w_dn = [None] * depth, [None] * depth
    w_in, w_out = [None] * n_fox, [None] * n_fox
    w_pool, pool_b_full, pool_s_full = [None] * n_pool, [None] * n_pool, [None] * n_pool
    b_pad =[jnp.pad(fox_b_f[j], (0, LANES - n_heads))[None] for j in range(n_fox)]

    saved = []
    cur = xs
    for i in range(depth):
        j = i // 2
        gm = mix_norm_g[i][None]
        if i == 0:
            gm = gm + started
        if i % 2 == 0:
            w_in_g, w_out_g = (w_in0, w_out0) if i == 0 else _exchange_wait(mix_gather[i], cur)
            w_in[j] = jnp.pad(jnp.transpose(w_in_g, (1, 0, 2)).reshape(d, n_in), ((0, 0), (0, n_pad - n_in)))
            w_out[j] = w_out_g.reshape(d, d)
            h = _rms_fwd(f"norm_mix{i}", cur, gm)
            proj =_mm_nn(f"proj_in{i}", h, w_in[j], F32, tn=896)
            gq, gk = fox_q_norm_g[j][None], fox_k_norm_g[j][None]
            qn, kn, vb = _qkv_fwd(f"qk_norm{i}", proj, gq, gk, d)
            flog = proj[:, 3 * d:]
            c = _gate_fwd(f"gate{i}", flog, b_pad[j])
            c_t = c[:, :n_heads].T
            c_col, c_row = c_t[:, :, None], c_t[:, None, :]
            o, lse = _attn_fwd(f"attn{i}", qn, kn, vb.T, c_row, c_col)
            mid = _mm_nn(f"proj_out{i}", o, w_out[j], F32, add=cur)
            mix_saved = (cur, h, proj, flog, qn, kn, vb, c_col, c_row, o, lse)
        else:
            pw_g, pbs_g = _exchange_wait(mix_gather[i], cur)
            w_pool[j] = jnp.transpose(pw_g, (1, 0, 2, 3)).reshape(n_groups, d // n_groups, d // n_groups)
            pbs_full = jnp.transpose(pbs_g, (1, 0, 2)).reshape(2, 1, d)
            pool_b_full[j], pool_s_full[j] = pbs_full[0], pbs_full[1]
            mid, y, zb = _pool_fwd(f"pool{i}", cur, gm, w_pool[j], pool_b_full[j], pool_s_full[j])
            mix_saved = (cur, y, zb)
        h2 = _rms_fwd(f"norm_ffn{i}", mid, ffn_norm_g[i][None])
        w_gu_g[i], dn_g = _exchange_wait(ffn_gather[i], h2)
        w_dn[i] = dn_g.reshape(-1, d)
        gate, up, act =_ffn_up(f"ffn_up{i}", h2, w_gu_g[i])
        nxt = _mm_nn(f"ffn_down{i}", act, w_dn[i], F32, add=mid, tk=1408)
        saved.append((mix_saved, mid, h2, gate, up, act))
        cur = nxt

    dcur, lpart = _loss_head("loss_head", cur, tgt)
    loss = lax.psum(0.5 * lpart[0, 0], axes)

    d_mix, d_ffn = [None] * depth, [None] * depth
    d_bf, d_gq, d_gk = [None] * n_fox, [None] * n_fox, [None] * n_fox
    mix_scatter, ffn_scatter = [None] * depth, [None] * depth
    pending = jnp.zeros((1, 1), F32)
    for i in reversed(range(depth)):
        j = i // 2
        mix_saved, mid, h2, gate, up, act = saved[i]
        dgu = _ffn_dact(f"ffn_dact{i}", dcur, w_dn[i].reshape(half, -1, d), gate, up)
        g_dn = _mm_tn(f"ffn_dw_down{i}", act, dcur, BF16, tm=1408).reshape(N_DEV, -1, d)
        sc_dn = _exchange_start(f"scatter_down{i}", *_scatter_plan([g_dn]))
        g_gu = _ffn_dw_gu(f"ffn_dw_up{i}", h2, dgu, dep=sc_dn["token"])
        sc_gu = _exchange_start(f"scatter_up{i}", *_scatter_plan([g_gu]))
        ffn_scatter[i] = (sc_gu, sc_dn)
        dh2 = _ffn_dh(f"ffn_dh{i}", dgu, w_gu_g[i], dep=sc_gu["token"])
        g_ffn = ffn_norm_g[i][None] + pending
        dmid, d_ffn[i] = _rms_bwd(f"norm_ffn_bwd{i}", dh2, mid, g_ffn, dcur)
        gm = mix_norm_g[i][None]
        if i % 2 == 0:
            xin, h, proj, flog, qn, kn, vb, c_col, c_row, o, lse = mix_saved
            g_out = _mm_tn(f"proj_out_dw{i}", o, dmid, BF16).reshape(N_DEV, dsh, d)
            sc_out = _exchange_start(f"scatter_out{i}", *_scatter_plan([g_out]))
            do = _mm_nt(f"proj_out_dx{i}", dmid, w_out[j], BF16, dep=sc_out["token"])
            delta = _attn_delta(f"attn_delta{i}", o, do, n_heads)
            delta_row = delta[:, :n_heads].T[:, None, :]
            dqn, dkn, dv, dck, dcq = _attn_bwd(f"attn_bwd{i}", qn, kn, vb, do, c_row,
                                               lse, delta_row, c_col)
            lane_pad = ((0, 0), (0, LANES - n_heads))
            dflog, d_bf[j] = _gate_bwd(f"gate_bwd{i}", jnp.pad(dck[:, :, 0].T, lane_pad),
                                       jnp.pad(dcq[:, 0, :].T, lane_pad), flog, b_pad[j], n_heads)
            gq, gk = fox_q_norm_g[j][None], fox_k_norm_g[j][None]
            dproj, d_gq[j], d_gk[j] = _qkv_bwd(f"qk_norm_bwd{i}", proj, dqn, dkn, dv, dflog, gq, gk, d, n_pad)
            dw_in = _mm_tn(f"proj_in_dw{i}", h, dproj, BF16, tn=896)
            g_in = jnp.transpose(dw_in[:, :n_in].reshape(d, N_DEV, n_in // N_DEV), (1, 0, 2))
            sc_in = _exchange_start(f"scatter_in{i}", *_scatter_plan([g_in]))
            mix_scatter[i] = (sc_in, sc_out)
            dh = _mm_nt(f"proj_in_dx{i}", dproj, w_in[j], F32, tk=896, dep=sc_in["token"])
            dcur, d_mix[i] = _rms_bwd(f"norm_mix_bwd{i}", dh, xin, gm, dmid)
        else:
            xin, y, zb = mix_saved
            dcur, dz, d_mix[i], dsc, db = _pool_bwd(f"pool_bwd{i}", dmid, xin, zb, gm, w_pool[j], pool_s_full[j])
            dwp = _pool_dw(f"pool_dw{i}", y, dz, n_groups)
            dg = d // n_groups
            g_pw = jnp.transpose(dwp.reshape(n_groups, N_DEV, dg // N_DEV, dg), (1, 0, 2, 3)).astype(BF16)
            g_pbs = jnp.stack([db.reshape(N_DEV, dsh), dsc.reshape(N_DEV, dsh)], axis=1)
            sc_pool = _exchange_start(f"scatter_pool{i}", *_scatter_plan([g_pw, g_pbs]))
            mix_scatter[i] = (sc_pool,)
            pending = sc_pool["token"][:1, :1]
    grad_x = dcur[None]

    mix_landed = [sum((_exchange_wait(hd, dcur) for hd in mix_scatter[l]), []) for l in range(depth)]
    landed = [sum((_exchange_wait(hd, dcur) for hd in ffn_scatter[l]), []) for l in range(depth)]
    r_in, r_out = [t[0] for t in mix_landed[0::2]], [t[1] for t in mix_landed[0::2]]
    r_pw = [t[0].reshape(N_DEV, -1, t[0].shape[-1]) for t in mix_landed[1::2]]
    r_pbs = [t[1] for t in mix_landed[1::2]]
    r_gu, r_dn = [t[0] for t in landed], [t[1] for t in landed]
    upd = {}
    upd["fox_w_in"] = _adamw_layers("adamw_w_in", fox_w_in, m_fox_w_in, v_fox_w_in, r_in)
    upd["fox_w_out"] = _adamw_layers("adamw_w_out", fox_w_out, m_fox_w_out, v_fox_w_out, r_out)
    fold = lambda a: a.reshape(n_pool, -1, a.shape[-1])
    upd["pool_w"] = [o.reshape(pool_w.shape) for o in
                     _adamw_layers("adamw_pool_w", fold(pool_w), fold(m_pool_w), fold(v_pool_w), r_pw)]
    pbs = _adamw_layers("adamw_pool_bs", pool_bs, jnp.stack([m_pool_b, m_pool_scale], axis=1),
                        jnp.stack([v_pool_b, v_pool_scale], axis=1), r_pbs)
    upd["pool_b"] = [o[:, 0] for o in pbs]
    upd["pool_scale"] = [o[:, 1] for o in pbs]
    upd["ffn_w_gate_up"] = _adamw_layers("adamw_gate_up", ffn_w_gate_up, m_ffn_w_gate_up, v_ffn_w_gate_up, r_gu)
    upd["ffn_w_down"] = _adamw_layers("adamw_down", ffn_w_down, m_ffn_w_down, v_ffn_w_down, r_dn)

    small_w = (mix_norm_g, ffn_norm_g, fox_b_f, fox_q_norm_g, fox_k_norm_g)
    small_g = _pack_small(jnp.concatenate(d_mix), jnp.concatenate(d_ffn),
                          jnp.concatenate(d_bf)[:, :n_heads], jnp.concatenate(d_gq), jnp.concatenate(d_gk))
    (small_pieces,), = _all_gather_layers("gather_small", [small_g[None]])
    small = _adamw("adamw_small", _pack_small(*small_w),
                   _pack_small(m_mix_norm_g, m_ffn_norm_g, m_fox_b_f, m_fox_q_norm_g, m_fox_k_norm_g),
                   _pack_small(v_mix_norm_g, v_ffn_norm_g, v_fox_b_f, v_fox_q_norm_g, v_fox_k_norm_g),
                   small_pieces)
    small = [_unpack_small(o, *small_w) for o in small]
    for n, name in enumerate(("mix_norm_g", "ffn_norm_g", "fox_b_f", "fox_q_norm_g", "fox_k_norm_g")):
        upd[name] = [o[n] for o in small]

    order = ("mix_norm_g", "ffn_norm_g", "fox_w_in", "fox_b_f", "fox_q_norm_g", "fox_k_norm_g", "fox_w_out",
             "pool_w", "pool_b", "pool_scale", "ffn_w_gate_up", "ffn_w_down")
    return (loss, grad_x) + tuple(upd[name][q] for q in range(4) for name in order)
```

```python
import functools

import jax
import jax.numpy as jnp
from jax import lax
from jax.experimental import pallas as pl
from jax.experimental.pallas import tpu as pltpu

F32 = jnp.float32
BF16 = jnp.bfloat16
MESH = pl.DeviceIdType.MESH

N_DEV = 8
HEAD_DIM = 128
LANES = 128
POOL_WINDOWS = (2, 4, 8, 16)
POOL_HALO = 16
RMS_EPS = 1e-6
NEG_INF = -1e30
ADAM_LR = 0.001
ADAM_B1 = 0.9
ADAM_B2 = 0.999
ADAM_EPS = 1e-08
ADAM_WD = 0.01
ADAM_STEP = 10
VMEM_LIMIT = 52 * 1024 * 1024

NN = (((1,), (0,)), ((), ()))
NT = (((1,), (1,)), ((), ()))
TN = (((0,), (0,)), ((), ()))


def _tile(n, pref, align):
    best = None
    d = align
    while d <= min(n, pref):
        if n % d == 0:
            best = d
        d += align
    return n if best is None else best


def _params(*sem):
    return pltpu.CompilerParams(dimension_semantics=sem, vmem_limit_bytes=VMEM_LIMIT)


def _position():
    x, y, c = lax.axis_index("x"), lax.axis_index("y"), lax.axis_index("c")
    return x, y, c, 4 * x + 2 * y + c


def _peer(x, y, c, k):
    px = 1 - x if k & 4 else x
    py = 1 - y if k & 2 else y
    pc = 1 - c if k & 1 else c
    return (px, py, pc), 4 * px + 2 * py + pc


def _exchange(name, ins, out_shapes, copies):
    n_in, n_cp = len(ins), len(copies)

    def body(*refs):
        in_refs = refs[:n_in]
        out_refs = refs[n_in:n_in + len(out_shapes)]
        send_sems, recv_sems, loc_sems = refs[n_in + len(out_shapes):]
        x, y, c, me = _position()
        local = []
        for ci, (ii, src_of, oi, dst_of) in enumerate(copies):
            cp = pltpu.make_async_copy(src_of(in_refs[ii], me), dst_of(out_refs[oi], me), loc_sems.at[ci])
            cp.start()
            local.append(cp)
        sends, recvs = [], []
        for k in range(1, N_DEV):
            pid, p = _peer(x, y, c, k)
            for ci, (ii, src_of, oi, dst_of) in enumerate(copies):
                sem = ci * (N_DEV - 1) + k - 1
                send = pltpu.make_async_remote_copy(
                    src_ref=src_of(in_refs[ii], p), dst_ref=dst_of(out_refs[oi], me),
                    send_sem=send_sems.at[sem], recv_sem=recv_sems.at[sem],
                    device_id=pid, device_id_type=MESH)
                send.start()
                sends.append(send)
                recvs.append(pltpu.make_async_remote_copy(
                    src_ref=src_of(in_refs[ii], p), dst_ref=dst_of(out_refs[oi], p),
                    send_sem=send_sems.at[sem], recv_sem=recv_sems.at[sem],
                    device_id=pid, device_id_type=MESH))
        for r in recvs:
            r.wait_recv()
        for s in sends:
            s.wait_send()
        for cp in local:
            cp.wait()

    any_spec = pl.BlockSpec(memory_space=pl.ANY)
    return pl.pallas_call(
        body, name=name,
        out_shape=tuple(out_shapes),
        in_specs=[any_spec] * n_in,
        out_specs=tuple([any_spec] * len(out_shapes)),
        scratch_shapes=[pltpu.SemaphoreType.DMA((n_cp * (N_DEV - 1),)),
                        pltpu.SemaphoreType.DMA((n_cp * (N_DEV - 1),)),
                        pltpu.SemaphoreType.DMA((n_cp,))],
    )(*ins)


def _exchange_start(name, ins, out_shapes, copies, dep=None):
    n_in, n_out, n_cp = len(ins), len(out_shapes), len(copies)

    def body(*refs):
        in_refs = refs[:n_in]
        land_refs = refs[n_in:n_in + n_out]
        outs = refs[n_in + n_out + (dep is not None):]
        send_sems, recv_sems = outs[:2]
        token_ref, loc_sems = outs[n_in + n_out + 2], outs[n_in + n_out + 3]
        x, y, c, me = _position()
        local = []
        for ci, (ii, src_of, oi, dst_of) in enumerate(copies):
            cp = pltpu.make_async_copy(src_of(in_refs[ii], me), dst_of(land_refs[oi], me), loc_sems.at[ci])
            cp.start()
            local.append(cp)
        for cp in local:
            cp.wait()
        for k in range(1, N_DEV):
            pid, p = _peer(x, y, c, k)
            for ci, (ii, src_of, oi, dst_of) in enumerate(copies):
                sem = ci * (N_DEV - 1) + k - 1
                pltpu.make_async_remote_copy(
                    src_ref=src_of(in_refs[ii], p), dst_ref=dst_of(land_refs[oi], me),
                    send_sem=send_sems.at[sem], recv_sem=recv_sems.at[sem],
                    device_id=pid, device_id_type=MESH).start()
        token_ref[...] = jnp.zeros_like(token_ref)

    hbm = pl.BlockSpec(memory_space=pltpu.HBM)
    sem = pl.BlockSpec(memory_space=pltpu.SEMAPHORE)
    n_sem = n_cp * (N_DEV - 1)
    lands = [pltpu.with_memory_space_constraint(lax.empty(o.shape, o.dtype), pltpu.HBM) for o in out_shapes]
    srcs = [pltpu.with_memory_space_constraint(a, pltpu.HBM) for a in ins]
    res = pl.pallas_call(
        body, name=name,
        out_shape=(pltpu.SemaphoreType.DMA((n_sem,)), pltpu.SemaphoreType.DMA((n_sem,)),
                   *[pltpu.HBM(a.shape, a.dtype) for a in ins],
                   *[pltpu.HBM(o.shape, o.dtype) for o in out_shapes],
                   jax.ShapeDtypeStruct((8, LANES), F32)),
        in_specs=[hbm] * (n_in + n_out) + ([pl.BlockSpec(memory_space=pl.ANY)] if dep is not None else []),
        out_specs=(sem, sem, *([hbm] * (n_in + n_out)), pl.BlockSpec(memory_space=pltpu.VMEM)),
        input_output_aliases={i: 2 + i for i in range(n_in + n_out)},
        scratch_shapes=[pltpu.SemaphoreType.DMA((n_cp,))],
        compiler_params=pltpu.CompilerParams(has_side_effects=pltpu.SideEffectType.DATAFLOW_SIDE_EFFECTING),
    )(*srcs, *lands, *([dep] if dep is not None else []))
    return dict(name=name, copies=copies, send=res[0], recv=res[1], srcs=list(res[2:2 + n_in]),
                lands=list(res[2 + n_in:2 + n_in + n_out]), token=res[-1])


def _exchange_wait(handle, after):
    copies, srcs, lands = handle["copies"], handle["srcs"], handle["lands"]
    n_in, n_out = len(srcs), len(lands)

    def body(*refs):
        in_refs = refs[:n_in]
        land_refs = refs[n_in:n_in + n_out]
        send_sems, recv_sems = refs[n_in + n_out:n_in + n_out + 2]
        x, y, c, me = _position()
        waits = []
        for k in range(1, N_DEV):
            pid, p = _peer(x, y, c, k)
            for ci, (ii, src_of, oi, dst_of) in enumerate(copies):
                sem = ci * (N_DEV - 1) + k - 1
                waits.append(pltpu.make_async_remote_copy(
                    src_ref=src_of(in_refs[ii], p), dst_ref=dst_of(land_refs[oi], p),
                    send_sem=send_sems.at[sem], recv_sem=recv_sems.at[sem],
                    device_id=pid, device_id_type=MESH))
        for w in waits:
            w.wait_send()
        for w in waits:
            w.wait_recv()

    hbm = pl.BlockSpec(memory_space=pltpu.HBM)
    sem = pl.BlockSpec(memory_space=pltpu.SEMAPHORE)
    res = pl.pallas_call(
        body, name=handle["name"] + "_wait",
        out_shape=tuple(pltpu.HBM(a.shape, a.dtype) for a in srcs + lands),
        in_specs=[hbm] * (n_in + n_out) + [sem, sem, pl.BlockSpec(memory_space=pl.ANY)],
        out_specs=tuple([hbm] * (n_in + n_out)),
        input_output_aliases={i: i for i in range(n_in + n_out)},
        compiler_params=pltpu.CompilerParams(has_side_effects=pltpu.SideEffectType.DATAFLOW_SIDE_EFFECTING),
    )(*srcs, *lands, handle["send"], handle["recv"], after)
    return list(res[n_in:])


def _gather_plan(stacked):
    ins, outs, copies = [], [], []
    for t in stacked:
        ii = len(ins)
        ins.append(t)
        for l in range(t.shape[0]):
            oi = len(outs)
            outs.append(jax.ShapeDtypeStruct((N_DEV,) + t.shape[1:], t.dtype))
            copies.append((ii, (lambda ref, p, l=l: ref.at[l]), oi, (lambda ref, s: ref.at[s])))
    return ins, outs, copies


def _scatter_plan(blocked):
    outs = [jax.ShapeDtypeStruct(t.shape, t.dtype) for t in blocked]
    copies = [(n, (lambda ref, p: ref.at[p]), n, (lambda ref, s: ref.at[s])) for n in range(len(blocked))]
    return list(blocked), outs, copies


def _all_gather_layers(name, stacked):
    ins, outs, copies = [], [], []
    for t in stacked:
        ii = len(ins)
        ins.append(t)
        for l in range(t.shape[0]):
            oi = len(outs)
            outs.append(jax.ShapeDtypeStruct((N_DEV,) + t.shape[1:], t.dtype))
            copies.append((ii, (lambda ref, p, l=l: ref.at[l]), oi, (lambda ref, s: ref.at[s])))
    res = _exchange(name, ins, outs, copies)
    out, pos = [], 0
    for t in stacked:
        out.append(list(res[pos:pos + t.shape[0]]))
        pos += t.shape[0]
    return out


def _mm(name, mode, a, b, out_shape, *, grid, a_spec, b_spec, o_spec, acc_shape, add=None, add_spec=None, dep=None):
    nk = grid[2]
    dn = {"nn": NN, "nt": NT, "tn": TN}[mode]
    has_add, has_dep = add is not None, dep is not None
    own_acc = nk > 1 and out_shape.dtype != F32

    def body(*refs):
        a_ref, b_ref = refs[:2]
        add_ref = refs[2] if has_add else None
        o_ref = refs[2 + has_add + has_dep]

        def product():
            return lax.dot_general(a_ref[...].astype(BF16), b_ref[...].astype(BF16), dn,
                                   preferred_element_type=F32)

        if nk == 1:
            r = product() + add_ref[...] if has_add else product()
            o_ref[...] = r.astype(o_ref.dtype)
        else:
            acc_ref = refs[-1] if own_acc else o_ref
            k = pl.program_id(2)

            @pl.when(k == 0)
            def _():
                acc_ref[...] = add_ref[...] if has_add else jnp.zeros_like(acc_ref)

            acc_ref[...] += product()
            if own_acc:
                @pl.when(k == nk - 1)
                def _():
                    o_ref[...] = acc_ref[...].astype(o_ref.dtype)

    ins = [a, b] + ([add] if has_add else []) + ([dep] if has_dep else [])
    in_specs = ([a_spec, b_spec] + ([add_spec] if has_add else [])
                + ([pl.BlockSpec(memory_space=pl.ANY)] if has_dep else []))
    scratch = [pltpu.VMEM(acc_shape, F32)] if own_acc else []
    return pl.pallas_call(
        body, name=name, grid=grid, out_shape=out_shape,
        in_specs=in_specs, out_specs=o_spec, scratch_shapes=scratch,
        compiler_params=_params("parallel", "parallel", "arbitrary"),
    )(*ins)


def _mm_nn(name, a, b, out_dtype, add=None, tm=1024, tn=1024, tk=2048):
    m, kd = a.shape
    n = b.shape[1]
    tm, tn, tk = _tile(m, tm, 16), _tile(n, tn, LANES), _tile(kd, tk, LANES)
    return _mm(name, "nn", a, b, jax.ShapeDtypeStruct((m, n), out_dtype),
               grid=(m // tm, n // tn, kd // tk),
               a_spec=pl.BlockSpec((tm, tk), lambda i, j, k: (i, k)),
               b_spec=pl.BlockSpec((tk, tn), lambda i, j, k: (k, j)),
               o_spec=pl.BlockSpec((tm, tn), lambda i, j, k: (i, j)),
               acc_shape=(tm, tn), add=add,
               add_spec=pl.BlockSpec((tm, tn), lambda i, j, k: (i, j)))


def _mm_nt(name, a, b, out_dtype, tm=1024, tn=1024, tk=2048, dep=None):
    m, kd = a.shape
    n = b.shape[0]
    tm, tn, tk = _tile(m, tm, 16), _tile(n, tn, LANES), _tile(kd, tk, LANES)
    return _mm(name, "nt", a, b, jax.ShapeDtypeStruct((m, n), out_dtype),
               grid=(m // tm, n // tn, kd // tk),
               a_spec=pl.BlockSpec((tm, tk), lambda i, j, k: (i, k)),
               b_spec=pl.BlockSpec((tn, tk), lambda i, j, k: (j, k)),
               o_spec=pl.BlockSpec((tm, tn), lambda i, j, k: (i, j)),
               acc_shape=(tm, tn), dep=dep)


def _mm_tn(name, a, b, out_dtype, tm=1024, tn=1024, ts=1024, dep=None):
    s, m = a.shape
    n = b.shape[1]
    tm, tn, ts = _tile(m, tm, LANES), _tile(n, tn, LANES), _tile(s, ts, 16)
    return _mm(name, "tn", a, b, jax.ShapeDtypeStruct((m, n), out_dtype),
               grid=(m // tm, n // tn, s // ts),
               a_spec=pl.BlockSpec((ts, tm), lambda i, j, k: (k, i)),
               b_spec=pl.BlockSpec((ts, tn), lambda i, j, k: (k, j)),
               o_spec=pl.BlockSpec((tm, tn), lambda i, j, k: (i, j)),
               acc_shape=(tm, tn), dep=dep)


def _rms_fwd(name, x, g):
    s, d = x.shape
    tm = _tile(s, 512, 16)

    def body(x_ref, g_ref, h_ref):
        xv = x_ref[...]
        r = lax.rsqrt(jnp.mean(xv * xv, axis=-1, keepdims=True) + RMS_EPS)
        h_ref[...] = ((xv * r) * g_ref[...]).astype(BF16)

    return pl.pallas_call(
        body, name=name, grid=(s // tm,), out_shape=jax.ShapeDtypeStruct((s, d), BF16),
        in_specs=[pl.BlockSpec((tm, d), lambda i: (i, 0)), pl.BlockSpec((1, d), lambda i: (0, 0))],
        out_specs=pl.BlockSpec((tm, d), lambda i: (i, 0)),
        compiler_params=_params("parallel"),
    )(x, g)


def _mm_rms_bwd(name, a, b, x, g, dres, *, tm, nk, a_spec, b_spec, dep=None):
    s, d = x.shape
    has_dep = dep is not None
    ch = _tile(tm, 128, 8)

    def body(*refs):
        a_ref, b_ref, x_ref, g_ref, dres_ref = refs[:5]
        dx_ref, dg_ref = refs[5 + has_dep], refs[6 + has_dep]
        i, k = pl.program_id(0), pl.program_id(1)

        @pl.when(k == 0)
        def _():
            dx_ref[...] = jnp.zeros_like(dx_ref)

        dx_ref[...] += lax.dot_general(a_ref[...].astype(BF16), b_ref[...].astype(BF16), NT,
                                       preferred_element_type=F32)

        @pl.when(k == nk - 1)
        def _():
            def rows_bwd(c, part):
                rows = pl.ds(pl.multiple_of(c * ch, ch), ch)
                dhv = dx_ref[rows, :]
                xv = x_ref[rows, :]
                r = lax.rsqrt(jnp.mean(xv * xv, axis=-1, keepdims=True) + RMS_EPS)
                xhat = xv * r
                gdh = dhv * g_ref[...]
                dx_ref[rows, :] = dres_ref[rows, :] + r * (gdh - xhat * jnp.mean(gdh * xhat, axis=-1, keepdims=True))
                return part + jnp.sum(dhv * xhat, axis=0, keepdims=True)

            part = lax.fori_loop(0, tm // ch, rows_bwd, jnp.zeros((1, d), F32))

            @pl.when(i == 0)
            def _():
                dg_ref[...] = part

            @pl.when(i > 0)
            def _():
                dg_ref[...] += part

    row = pl.BlockSpec((tm, d), lambda i, k: (i, 0))
    vec = pl.BlockSpec((1, d), lambda i, k: (0, 0))
    return pl.pallas_call(
        body, name=name, grid=(s // tm, nk),
        out_shape=(jax.ShapeDtypeStruct((s, d), F32), jax.ShapeDtypeStruct((1, d), F32)),
        in_specs=[a_spec, b_spec, row, vec, row] + ([pl.BlockSpec(memory_space=pl.ANY)] if has_dep else []),
        out_specs=(row, vec),
        compiler_params=_params("arbitrary", "arbitrary"),
    )(a, b, x, g, dres, *([dep] if has_dep else []))


def _split3(v):
    hi = v.astype(BF16)
    r1 = v - hi.astype(F32)
    mid = r1.astype(BF16)
    lo = (r1 - mid.astype(F32)).astype(BF16)
    return hi, mid, lo


def _tri_sum(tri, v):
    hi, mid, lo = _split3(v)
    dot = functools.partial(lax.dot_general, dimension_numbers=NN, preferred_element_type=F32)
    return dot(tri, hi) + dot(tri, mid) + dot(tri, lo)


def _gate_fwd(name, flog, b_pad):
    s = flog.shape[0]
    tb = _tile(s, 256, 16)

    def body(f_ref, b_ref, c_ref, carry_ref):
        i = pl.program_id(0)

        @pl.when(i == 0)
        def _():
            carry_ref[...] = jnp.zeros_like(carry_ref)

        z = f_ref[...] + b_ref[...]
        lf = jnp.minimum(z, 0.0) - jnp.log(1.0 + jnp.exp(-jnp.abs(z)))
        rows = lax.broadcasted_iota(jnp.int32, (tb, tb), 0)
        cols = lax.broadcasted_iota(jnp.int32, (tb, tb), 1)
        tri = (rows >= cols).astype(BF16)
        c_ref[...] = _tri_sum(tri, lf) + carry_ref[...]
        carry_ref[...] = c_ref[pl.ds(tb - 1, 1), :]

    return pl.pallas_call(
        body, name=name, grid=(s // tb,), out_shape=jax.ShapeDtypeStruct((s, LANES), F32),
        in_specs=[pl.BlockSpec((tb, LANES), lambda i: (i, 0)), pl.BlockSpec((1, LANES), lambda i: (0, 0))],
        out_specs=pl.BlockSpec((tb, LANES), lambda i: (i, 0)),
        scratch_shapes=[pltpu.VMEM((1, LANES), F32)],
        compiler_params=_params("arbitrary"),
    )(flog, b_pad)


def _gate_bwd(name, dck, dcq, flog, b_pad, n_heads):
    s = flog.shape[0]
    tb = _tile(s, 256, 16)
    nb = s // tb

    def body(dck_ref, dcq_ref, f_ref, b_ref, df_ref, db_ref, carry_ref, tmp_ref):
        i = pl.program_id(0)

        @pl.when(i == 0)
        def _():
            carry_ref[...] = jnp.zeros_like(carry_ref)

        rows = lax.broadcasted_iota(jnp.int32, (tb, tb), 0)
        cols = lax.broadcasted_iota(jnp.int32, (tb, tb), 1)
        tri = (rows <= cols).astype(BF16)
        tmp_ref[...] = _tri_sum(tri, dck_ref[...] + dcq_ref[...]) + carry_ref[...]
        carry_ref[...] = tmp_ref[pl.ds(0, 1), :]
        z = f_ref[...] + b_ref[...]
        lane = lax.broadcasted_iota(jnp.int32, (tb, LANES), 1)
        df = jnp.where(lane < n_heads, tmp_ref[...] / (1.0 + jnp.exp(z)), 0.0)
        df_ref[...] = df.astype(BF16)
        part = jnp.sum(df, axis=0, keepdims=True)

        @pl.when(i == 0)
        def _():
            db_ref[...] = part

        @pl.when(i > 0)
        def _():
            db_ref[...] += part

    rev = pl.BlockSpec((tb, LANES), lambda i: (nb - 1 - i, 0))
    vec = pl.BlockSpec((1, LANES), lambda i: (0, 0))
    return pl.pallas_call(
        body, name=name, grid=(nb,),
        out_shape=(jax.ShapeDtypeStruct((s, LANES), BF16), jax.ShapeDtypeStruct((1, LANES), F32)),
        in_specs=[rev, rev, rev, vec], out_specs=(rev, vec),
        scratch_shapes=[pltpu.VMEM((1, LANES), F32), pltpu.VMEM((tb, LANES), F32)],
        compiler_params=_params("arbitrary"),
    )(dck, dcq, flog, b_pad)


def _head_rms(v, g):
    r = lax.rsqrt(jnp.mean(v * v, axis=-1, keepdims=True) + RMS_EPS)
    return (v * r) * g


def _qkv_fwd(name, proj, gq, gk, d):
    s = proj.shape[0]
    tm = _tile(s, 256, 16)
    n_heads = d // HEAD_DIM

    def body(q_ref, k_ref, v_ref, gq_ref, gk_ref, qn_ref, kn_ref, vb_ref):
        for h in range(n_heads):
            sl = slice(h * HEAD_DIM, (h + 1) * HEAD_DIM)
            qn_ref[:, sl] = _head_rms(q_ref[:, sl], gq_ref[...]).astype(BF16)
            kn_ref[:, sl] = _head_rms(k_ref[:, sl], gk_ref[...]).astype(BF16)
        vb_ref[...] = v_ref[...].astype(BF16)

    col = lambda c: pl.BlockSpec((tm, d), lambda i, c=c: (i, c))
    vec = pl.BlockSpec((1, HEAD_DIM), lambda i: (0, 0))
    out = jax.ShapeDtypeStruct((s, d), BF16)
    return pl.pallas_call(
        body, name=name, grid=(s // tm,), out_shape=(out, out, out),
        in_specs=[col(0), col(1), col(2), vec, vec], out_specs=(col(0), col(0), col(0)),
        compiler_params=_params("parallel"),
    )(proj, proj, proj, gq, gk)


def _qkv_bwd(name, proj, dqn, dkn, dv, dflog, gq, gk, d, n_pad):
    s = proj.shape[0]
    tm = _tile(s, 256, 16)
    n_heads = d // HEAD_DIM

    def head_bwd(raw, dy, g):
        r = lax.rsqrt(jnp.mean(raw * raw, axis=-1, keepdims=True) + RMS_EPS)
        hat = raw * r
        gdy = dy * g
        dx = r * (gdy - hat * jnp.mean(gdy * hat, axis=-1, keepdims=True))
        return dx, jnp.sum(dy * hat, axis=0, keepdims=True)

    def body(q_ref, k_ref, dqn_ref, dkn_ref, dv_ref, df_ref, gq_ref, gk_ref, dp_ref, dgq_ref, dgk_ref):
        i = pl.program_id(0)
        accq = jnp.zeros((1, HEAD_DIM), F32)
        acck = jnp.zeros((1, HEAD_DIM), F32)
        for h in range(n_heads):
            sl = slice(h * HEAD_DIM, (h + 1) * HEAD_DIM)
            dq, pq = head_bwd(q_ref[:, sl], dqn_ref[:, sl], gq_ref[...])
            dk, pk = head_bwd(k_ref[:, sl], dkn_ref[:, sl], gk_ref[...])
            dp_ref[:, sl] = dq.astype(BF16)
            dp_ref[:, d + h * HEAD_DIM:d + (h + 1) * HEAD_DIM] = dk.astype(BF16)
            accq, acck = accq + pq, acck + pk
        dp_ref[:, 2 * d:3 * d] = dv_ref[...]
        dp_ref[:, 3 * d:] = df_ref[...]

        @pl.when(i == 0)
        def _():
            dgq_ref[...] = accq
            dgk_ref[...] = acck

        @pl.when(i > 0)
        def _():
            dgq_ref[...] += accq
            dgk_ref[...] += acck

    col = lambda c: pl.BlockSpec((tm, d), lambda i, c=c: (i, c))
    vec = pl.BlockSpec((1, HEAD_DIM), lambda i: (0, 0))
    return pl.pallas_call(
        body, name=name, grid=(s // tm,),
        out_shape=(jax.ShapeDtypeStruct((s, n_pad), BF16), jax.ShapeDtypeStruct((1, HEAD_DIM), F32),
                   jax.ShapeDtypeStruct((1, HEAD_DIM), F32)),
        in_specs=[col(0), col(1), col(0), col(0), col(0), pl.BlockSpec((tm, LANES), lambda i: (i, 0)), vec, vec],
        out_specs=(pl.BlockSpec((tm, n_pad), lambda i: (i, 0)), vec, vec),
        compiler_params=_params("arbitrary"),
    )(proj, proj, dqn, dkn, dv, dflog, gq, gk)


def _attn_fwd(name, qn, kn, vt, c_row, c_col):
    s, d = qn.shape
    n_heads = d // HEAD_DIM
    t = _tile(s, 512, LANES)
    scale = HEAD_DIM ** -0.5

    hp = 2 if n_heads % 2 == 0 else 1
    log2e = 1.4426950408889634

    def body(q_ref, k_ref, vt_ref, cq_ref, ck_ref, o_ref, lse_ref, m_ref, l_ref, acc_ref):
        i = pl.program_id(1)
        m_ref[...] = jnp.full(m_ref.shape, NEG_INF, F32)
        l_ref[...] = jnp.zeros_like(l_ref)
        acc_ref[...] = jnp.zeros_like(acc_ref)

        def step(j, masked):
            start = pl.multiple_of(j * t, t)
            for hh in range(hp):
                sl = slice(hh * HEAD_DIM, (hh + 1) * HEAD_DIM)
                kj = k_ref[pl.ds(start, t), sl]
                vtj = vt_ref[sl, pl.ds(start, t)]
                bias = cq_ref[hh] * log2e - ck_ref[hh, pl.ds(start, t), :] * log2e
                st = lax.dot_general(kj, q_ref[:, sl], NT, preferred_element_type=F32) * (scale * log2e) + bias
                if masked:
                    rows = lax.broadcasted_iota(jnp.int32, (t, t), 0)
                    cols = lax.broadcasted_iota(jnp.int32, (t, t), 1)
                    st = jnp.where(cols >= rows, st, NEG_INF)
                m_prev = m_ref[hh]
                m_new = jnp.maximum(m_prev, jnp.max(st, axis=0, keepdims=True))
                pt = jnp.exp2(st - m_new)
                alpha = jnp.exp2(m_prev - m_new)
                l_ref[hh] = alpha * l_ref[hh] + jnp.sum(pt, axis=0, keepdims=True)
                acc_ref[hh] = alpha * acc_ref[hh] + lax.dot_general(
                    vtj, pt.astype(BF16), NN, preferred_element_type=F32)
                m_ref[hh] = m_new

        def loop_body(j, carry):
            step(j, False)
            return carry

        lax.fori_loop(0, i, loop_body, 0)
        step(i, True)
        for hh in range(hp):
            sl = slice(hh * HEAD_DIM, (hh + 1) * HEAD_DIM)
            o_ref[:, sl] = (acc_ref[hh] / l_ref[hh]).T.astype(BF16)
            lse_ref[hh] = (m_ref[hh] + jnp.log2(l_ref[hh])) * (1.0 / log2e)

    wide = hp * HEAD_DIM
    row_blk = pl.BlockSpec((hp, 1, t), lambda h, i: (h, 0, i))
    return pl.pallas_call(
        body, name=name, grid=(n_heads // hp, s // t),
        out_shape=(jax.ShapeDtypeStruct((s, d), BF16), jax.ShapeDtypeStruct((n_heads, 1, s), F32)),
        in_specs=[pl.BlockSpec((t, wide), lambda h, i: (i, h)),
                  pl.BlockSpec((s, wide), lambda h, i: (0, h)),
                  pl.BlockSpec((wide, s), lambda h, i: (h, 0)),
                  row_blk, pl.BlockSpec((hp, s, 1), lambda h, i: (h, 0, 0))],
        out_specs=(pl.BlockSpec((t, wide), lambda h, i: (i, h)), row_blk),
        scratch_shapes=[pltpu.VMEM((hp, 1, t), F32), pltpu.VMEM((hp, 1, t), F32),
                        pltpu.VMEM((hp, HEAD_DIM, t), F32)],
        compiler_params=_params("parallel", "arbitrary"),
    )(qn, kn, vt, c_row, c_col)


def _attn_delta(name, o, do, n_heads):
    s, d = o.shape
    tm = _tile(s, 256, 16)

    def body(o_ref, do_ref, dl_ref):
        lane = lax.broadcasted_iota(jnp.int32, (tm, LANES), 1)
        acc = jnp.zeros((tm, LANES), F32)
        for h in range(n_heads):
            sl = slice(h * HEAD_DIM, (h + 1) * HEAD_DIM)
            col = jnp.sum(o_ref[:, sl].astype(F32) * do_ref[:, sl].astype(F32), axis=-1, keepdims=True)
            acc = jnp.where(lane == h, col, acc)
        dl_ref[...] = acc

    row = pl.BlockSpec((tm, d), lambda i: (i, 0))
    return pl.pallas_call(
        body, name=name, grid=(s // tm,), out_shape=jax.ShapeDtypeStruct((s, LANES), F32),
        in_specs=[row, row], out_specs=pl.BlockSpec((tm, LANES), lambda i: (i, 0)),
        compiler_params=_params("parallel"),
    )(o, do)


def _attn_bwd(name, qn, kn, vb, do, c_row, lse_row, delta_row, c_col):
    s, d = qn.shape
    n_heads = d // HEAD_DIM
    t = _tile(s, 512, LANES)
    nq = s // t
    scale = HEAD_DIM ** -0.5

    hp = 2 if n_heads % 2 == 0 else 1

    def body(q_ref, do_ref, cr_ref, lse_ref, dl_ref, k_ref, v_ref, ck_ref,
             dq_ref, dk_ref, dv_ref, dc_ref, dcq_ref, dk_acc, dv_acc, dc_acc):
        j = pl.program_id(1)

        @pl.when(j == 0)
        def _():
            dq_ref[...] = jnp.zeros_like(dq_ref)
            dcq_ref[...] = jnp.zeros_like(dcq_ref)

        dk_acc[...] = jnp.zeros_like(dk_acc)
        dv_acc[...] = jnp.zeros_like(dv_acc)
        dc_acc[...] = jnp.zeros_like(dc_acc)

        def step(i, masked):
            start = pl.multiple_of(i * t, t)
            for hh in range(hp):
                sl = slice(hh * HEAD_DIM, (hh + 1) * HEAD_DIM)
                kj = k_ref[:, sl]
                qi = q_ref[pl.ds(start, t), sl]
                doi = do_ref[pl.ds(start, t), sl]
                bias = cr_ref[hh, :, pl.ds(start, t)] - lse_ref[hh, :, pl.ds(start, t)]
                dli = dl_ref[hh, :, pl.ds(start, t)]
                st = lax.dot_general(kj, qi, NT, preferred_element_type=F32) * scale + (bias - ck_ref[hh])
                if masked:
                    rows = lax.broadcasted_iota(jnp.int32, (t, t), 0)
                    cols = lax.broadcasted_iota(jnp.int32, (t, t), 1)
                    st = jnp.where(cols >= rows, st, NEG_INF)
                pt = jnp.exp(st)
                dpt = lax.dot_general(v_ref[:, sl], doi, NT, preferred_element_type=F32)
                dst = pt * (dpt - dli)
                dsb = dst.astype(BF16)
                dv_acc[:, sl] += lax.dot_general(pt.astype(BF16), doi, NN, preferred_element_type=F32)
                dk_acc[:, sl] += lax.dot_general(dsb, qi, NN, preferred_element_type=F32)
                dq_ref[pl.ds(start, t), sl] += lax.dot_general(dsb, kj, TN, preferred_element_type=F32) * scale
                dc_acc[hh] += jnp.sum(dst, axis=1, keepdims=True)
                dcq_ref[hh, :, pl.ds(start, t)] += jnp.sum(dst, axis=0, keepdims=True)

        step(j, True)

        def loop_body(i, carry):
            step(i, False)
            return carry

        lax.fori_loop(j + 1, nq, loop_body, 0)
        dk_ref[...] = dk_acc[...] * scale
        dv_ref[...] = dv_acc[...].astype(BF16)
        dc_ref[...] = -dc_acc[...]

    wide = hp * HEAD_DIM
    head_all = pl.BlockSpec((s, wide), lambda h, j: (0, h))
    row_all = pl.BlockSpec((hp, 1, s), lambda h, j: (h, 0, 0))
    blk = pl.BlockSpec((t, wide), lambda h, j: (j, h))
    col_blk = pl.BlockSpec((hp, t, 1), lambda h, j: (h, j, 0))
    return pl.pallas_call(
        body, name=name, grid=(n_heads // hp, nq),
        out_shape=(jax.ShapeDtypeStruct((s, d), F32), jax.ShapeDtypeStruct((s, d), F32),
                   jax.ShapeDtypeStruct((s, d), BF16), jax.ShapeDtypeStruct((n_heads, s, 1), F32),
                   jax.ShapeDtypeStruct((n_heads, 1, s), F32)),
        in_specs=[head_all, head_all, row_all, row_all, row_all, blk, blk, col_blk],
        out_specs=(head_all, blk, blk, col_blk, row_all),
        scratch_shapes=[pltpu.VMEM((t, wide), F32), pltpu.VMEM((t, wide), F32), pltpu.VMEM((hp, t, 1), F32)],
        compiler_params=_params("parallel", "arbitrary"),
    )(qn, do, c_row, lse_row, delta_row, kn, vb, c_col)


def _ffn_up(name, h, w_gu):
    s, d = h.shape
    fs = w_gu.shape[2]
    half = N_DEV // 2
    tm = _tile(s, 512, 16)

    def body(h_ref, wg_ref, wu_ref, g_ref, u_ref, a_ref):
        hv = h_ref[...]
        g = lax.dot_general(hv, wg_ref[...], NN, preferred_element_type=F32)
        u = lax.dot_general(hv, wu_ref[...], NN, preferred_element_type=F32)
        g_ref[...] = g.astype(BF16)
        u_ref[...] = u.astype(BF16)
        a_ref[...] = (g * jax.nn.sigmoid(g) * u).astype(BF16)

    out = jax.ShapeDtypeStruct((s, half * fs), BF16)
    ospec = pl.BlockSpec((tm, fs), lambda j, i: (i, j))
    return pl.pallas_call(
        body, name=name, grid=(half, s // tm), out_shape=(out, out, out),
        in_specs=[pl.BlockSpec((tm, d), lambda j, i: (i, 0)),
                  pl.BlockSpec((None, d, fs), lambda j, i: (j, 0, 0)),
                  pl.BlockSpec((None, d, fs), lambda j, i: (j + half, 0, 0))],
        out_specs=(ospec, ospec, ospec),
        compiler_params=_params("parallel", "parallel"),
    )(h, w_gu, w_gu)


def _ffn_dact(name, dx, w_dn4, g, u):
    s, d = dx.shape
    half, fs = w_dn4.shape[0], w_dn4.shape[1]
    tm = _tile(s, 512, 16)

    cut = (fs // (2 * LANES)) * LANES

    def body(dx_ref, w_ref, g_ref, u_ref, dgu_ref):
        dxv = dx_ref[...].astype(BF16)
        for lo, hi in ((0, cut), (cut, fs)) if cut else ((0, fs),):
            da = lax.dot_general(dxv, w_ref[lo:hi, :], NT, preferred_element_type=F32)
            gv = g_ref[:, lo:hi].astype(F32)
            uv = u_ref[:, lo:hi].astype(F32)
            sig = jax.nn.sigmoid(gv)
            dgu_ref[0, :, lo:hi] = (da * uv * (sig * (1.0 + gv * (1.0 - sig)))).astype(BF16)
            dgu_ref[1, :, lo:hi] = (da * (gv * sig)).astype(BF16)

    blk = pl.BlockSpec((tm, fs), lambda j, i: (i, j))
    return pl.pallas_call(
        body, name=name, grid=(half, s // tm),
        out_shape=jax.ShapeDtypeStruct((2, s, half * fs), BF16),
        in_specs=[pl.BlockSpec((tm, d), lambda j, i: (i, 0)),
                  pl.BlockSpec((None, fs, d), lambda j, i: (j, 0, 0)), blk, blk],
        out_specs=pl.BlockSpec((2, tm, fs), lambda j, i: (0, i, j)),
        compiler_params=_params("parallel", "parallel"),
    )(dx, w_dn4, g, u)


def _ffn_dw_gu(name, h, dgu, dep=None):
    s, d = h.shape
    half, fs = N_DEV // 2, dgu.shape[2] // (N_DEV // 2)
    tm, ts = _tile(d, 1024, LANES), _tile(s, 1024, 16)
    return _mm(name, "tn", h, dgu, jax.ShapeDtypeStruct((N_DEV, d, fs), BF16),
               grid=(d // tm, N_DEV, s // ts),
               a_spec=pl.BlockSpec((ts, tm), lambda i, j, k: (k, i)),
               b_spec=pl.BlockSpec((None, ts, fs), lambda i, j, k: (j // half, k, j % half)),
               o_spec=pl.BlockSpec((None, tm, fs), lambda i, j, k: (j, i, 0)),
               acc_shape=(tm, fs), dep=dep)


def _ffn_dh(name, dgu, w_gu, x, g, dres, dep=None):
    s = dgu.shape[1]
    d, fs = w_gu.shape[1], w_gu.shape[2]
    half = N_DEV // 2
    tm = _tile(s, 512, 16)
    return _mm_rms_bwd(name, dgu, w_gu, x, g, dres, tm=tm, nk=N_DEV,
                       a_spec=pl.BlockSpec((None, tm, fs), lambda i, k: (k // half, i, k % half)),
                       b_spec=pl.BlockSpec((None, d, fs), lambda i, k: (k, 0, 0)), dep=dep)


def _proj_in_dx(name, dproj, w_in, x, g, dres, dep=None):
    s, n = dproj.shape
    d = w_in.shape[0]
    tm, tk = _tile(s, 512, 16), _tile(n, 896, LANES)
    return _mm_rms_bwd(name, dproj, w_in, x, g, dres, tm=tm, nk=n // tk,
                       a_spec=pl.BlockSpec((tm, tk), lambda i, k: (i, k)),
                       b_spec=pl.BlockSpec((d, tk), lambda i, k: (0, k)), dep=dep)


def _pool_fwd(name, x, g, w, b, sc):
    s, d = x.shape
    dg = d // len(POOL_WINDOWS)
    tm = _tile(s, 256, POOL_HALO)
    per = tm // POOL_HALO

    def body(x_ref, xh_ref, g_ref, w_ref, b_ref, sc_ref, xo_ref, y_ref, zb_ref):
        i = pl.program_id(0)
        gv = g_ref[...]

        def norm(v):
            return (v * lax.rsqrt(jnp.mean(v * v, axis=-1, keepdims=True) + RMS_EPS)) * gv

        h = norm(x_ref[...])
        halo = norm(xh_ref[...]) * (i > 0).astype(F32)
        ext = jnp.concatenate([halo, h], axis=0)
        t = i * tm + lax.broadcasted_iota(jnp.int32, (tm, 1), 0)
        for gi, win in enumerate(POOL_WINDOWS):
            sl = slice(gi * dg, (gi + 1) * dg)
            acc = ext[:, sl]
            step = 1
            while step < win:
                acc = acc + pltpu.roll(acc, step, 0)
                step *= 2
            inv = 1.0 / jnp.minimum(t + 1, win).astype(F32)
            yg = (acc[POOL_HALO:, :] * inv - h[:, sl]).astype(BF16)
            y_ref[:, sl] = yg
            zb = lax.dot_general(yg, w_ref[gi], NN, preferred_element_type=F32) + b_ref[:, sl]
            zb_ref[:, sl] = zb
            xo_ref[:, sl] = x_ref[:, sl] + zb * sc_ref[:, sl]

    row = pl.BlockSpec((tm, d), lambda i: (i, 0))
    vec = pl.BlockSpec((1, d), lambda i: (0, 0))
    return pl.pallas_call(
        body, name=name, grid=(s // tm,),
        out_shape=(jax.ShapeDtypeStruct((s, d), F32), jax.ShapeDtypeStruct((s, d), BF16),
                   jax.ShapeDtypeStruct((s, d), F32)),
        in_specs=[row, pl.BlockSpec((POOL_HALO, d), lambda i: (jnp.maximum(i * per - 1, 0), 0)),
                  vec, pl.BlockSpec(w.shape, lambda i: (0, 0, 0)), vec, vec],
        out_specs=(row, row, row),
        compiler_params=_params("parallel"),
    )(x, x, g, w, b, sc)


def _pool_bwd(name, dout, x, zb, g, w, sc):
    s, d = x.shape
    dg = d // len(POOL_WINDOWS)
    tm = _tile(s, 256, POOL_HALO)
    per = tm // POOL_HALO
    nb = s // tm
    ext_rows = tm + POOL_HALO

    def body(do_ref, doh_ref, x_ref, zb_ref, g_ref, w_ref, sc_ref, dx_ref, dz_ref, dgn_ref, dsc_ref, db_ref):
        i = pl.program_id(0)
        scv = sc_ref[...]
        dov = do_ref[...]
        dz = dov * scv
        dz_ref[...] = dz.astype(BF16)
        halo = doh_ref[...] * scv * (i < nb - 1).astype(F32)
        ext = jnp.concatenate([dz, halo], axis=0).astype(BF16)
        t = i * tm + lax.broadcasted_iota(jnp.int32, (ext_rows, 1), 0)
        parts = []
        for gi, win in enumerate(POOL_WINDOWS):
            sl = slice(gi * dg, (gi + 1) * dg)
            dy = lax.dot_general(ext[:, sl], w_ref[gi], NT, preferred_element_type=F32)
            acc = dy * (1.0 / jnp.minimum(t + 1, win).astype(F32))
            step = 1
            while step < win:
                acc = acc + pltpu.roll(acc, ext_rows - step, 0)
                step *= 2
            parts.append(acc[:tm, :] - dy[:tm, :])
        dh = jnp.concatenate(parts, axis=1)
        xv = x_ref[...]
        r = lax.rsqrt(jnp.mean(xv * xv, axis=-1, keepdims=True) + RMS_EPS)
        xhat = xv * r
        gdh = dh * g_ref[...]
        dx_ref[...] = dov + r * (gdh - xhat * jnp.mean(gdh * xhat, axis=-1, keepdims=True))
        pgn = jnp.sum(dh * xhat, axis=0, keepdims=True)
        psc = jnp.sum(dov * zb_ref[...], axis=0, keepdims=True)
        pb = jnp.sum(dz, axis=0, keepdims=True)

        @pl.when(i == 0)
        def _():
            dgn_ref[...] = pgn
            dsc_ref[...] = psc
            db_ref[...] = pb

        @pl.when(i > 0)
        def _():
            dgn_ref[...] += pgn
            dsc_ref[...] += psc
            db_ref[...] += pb

    row = pl.BlockSpec((tm, d), lambda i: (i, 0))
    vec = pl.BlockSpec((1, d), lambda i: (0, 0))
    vshape = jax.ShapeDtypeStruct((1, d), F32)
    return pl.pallas_call(
        body, name=name, grid=(nb,),
        out_shape=(jax.ShapeDtypeStruct((s, d), F32), jax.ShapeDtypeStruct((s, d), BF16), vshape, vshape, vshape),
        in_specs=[row, pl.BlockSpec((POOL_HALO, d), lambda i: (jnp.minimum((i + 1) * per, s // POOL_HALO - 1), 0)),
                  row, row, vec, pl.BlockSpec(w.shape, lambda i: (0, 0, 0)), vec],
        out_specs=(row, row, vec, vec, vec),
        compiler_params=_params("arbitrary"),
    )(dout, dout, x, zb, g, w, sc)


def _pool_dw(name, y, dz, n_groups):
    s, d = y.shape
    dg = d // n_groups
    ts = _tile(s, 1024, 16)
    return _mm(name, "tn", y, dz, jax.ShapeDtypeStruct((n_groups, dg, dg), F32),
               grid=(n_groups, 1, s // ts),
               a_spec=pl.BlockSpec((ts, dg), lambda i, j, k: (k, i)),
               b_spec=pl.BlockSpec((ts, dg), lambda i, j, k: (k, i)),
               o_spec=pl.BlockSpec((None, dg, dg), lambda i, j, k: (i, 0, 0)),
               acc_shape=(dg, dg))


def _loss_head(name, y, tgt):
    s, d = y.shape
    tm = _tile(s, 512, 16)

    def body(y_ref, t_ref, dy_ref, l_ref):
        i = pl.program_id(0)
        e = y_ref[...] - t_ref[...]
        dy_ref[...] = e * (1.0 / d)
        part = jnp.sum(jnp.mean(e * e, axis=-1, keepdims=True), axis=0, keepdims=True)
        part = jnp.broadcast_to(part, l_ref.shape)

        @pl.when(i == 0)
        def _():
            l_ref[...] = part

        @pl.when(i > 0)
        def _():
            l_ref[...] += part

    row = pl.BlockSpec((tm, d), lambda i: (i, 0))
    return pl.pallas_call(
        body, name=name, grid=(s // tm,),
        out_shape=(jax.ShapeDtypeStruct((s, d), F32), jax.ShapeDtypeStruct((8, LANES), F32)),
        in_specs=[row, row], out_specs=(row, pl.BlockSpec((8, LANES), lambda i: (0, 0))),
        compiler_params=_params("arbitrary"),
    )(y, tgt)


def _adam_update(w_ref, m_ref, v_ref, p_ref, g_ref, d_ref, nm_ref, nv_ref):
    g = p_ref[0].astype(F32)
    for k in range(1, N_DEV):
        g = g + p_ref[k].astype(F32)
    mn = ADAM_B1 * m_ref[...] + (1.0 - ADAM_B1) * g
    vn = ADAM_B2 * v_ref[...] + (1.0 - ADAM_B2) * (g * g)
    m_hat = mn / (1.0 - ADAM_B1 ** ADAM_STEP)
    v_hat = vn / (1.0 - ADAM_B2 ** ADAM_STEP)
    g_ref[...] = g
    d_ref[...] = -ADAM_LR * (m_hat / (jnp.sqrt(v_hat) + ADAM_EPS) + ADAM_WD * w_ref[...])
    nm_ref[...] = mn
    nv_ref[...] = vn


def _adamw_layers(name, w, m, v, pieces):
    n_layers, r, c = w.shape
    tr = _tile(r, 128, 16)

    def body(w_ref, m_ref, v_ref, *rest):
        p_refs, outs = rest[:n_layers], rest[n_layers:]
        layer = pl.program_id(0)
        for l in range(n_layers):
            @pl.when(layer == l)
            def _(l=l):
                _adam_update(w_ref, m_ref, v_ref, p_refs[l], *outs)

    blk = pl.BlockSpec((None, tr, c), lambda l, i: (l, i, 0))
    terms = [pl.BlockSpec((N_DEV, tr, c), lambda l, i, n=n: (0, jnp.where(l == n, i, 0), 0))
             for n in range(n_layers)]
    out = jax.ShapeDtypeStruct(w.shape, F32)
    return list(pl.pallas_call(
        body, name=name, grid=(n_layers, r // tr), out_shape=(out, out, out, out),
        in_specs=[blk, blk, blk] + terms, out_specs=(blk, blk, blk, blk),
        compiler_params=_params("parallel", "parallel"),
    )(w, m, v, *pieces))


def _adamw(name, w, m, v, pieces):
    r, c = w.shape
    tr = _tile(r, 128, 16)

    def body(w_ref, m_ref, v_ref, p_ref, g_ref, d_ref, nm_ref, nv_ref):
        _adam_update(w_ref, m_ref, v_ref, p_ref, g_ref, d_ref, nm_ref, nv_ref)

    blk = pl.BlockSpec((tr, c), lambda i: (i, 0))
    out = jax.ShapeDtypeStruct((r, c), F32)
    return pl.pallas_call(
        body, name=name, grid=(r // tr,), out_shape=(out, out, out, out),
        in_specs=[blk, blk, blk, pl.BlockSpec((N_DEV, tr, c), lambda i: (0, i, 0))],
        out_specs=(blk, blk, blk, blk),
        compiler_params=_params("parallel"),
    )(w, m, v, pieces)


def _pack_small(mix, ffn, b_f, gq, gk):
    def rows(a):
        a = a.reshape(-1, LANES) if a.shape[-1] >= LANES else jnp.pad(a, ((0, 0), (0, LANES - a.shape[-1])))
        return jnp.pad(a, ((0, -a.shape[0] % 8), (0, 0)))
    return jnp.concatenate([rows(mix), rows(ffn), rows(b_f), rows(gq), rows(gk)], axis=0)


def _unpack_small(p, mix, ffn, b_f, gq, gk):
    out, pos = [], 0
    for a in (mix, ffn, b_f, gq, gk):
        n = a.size // LANES if a.shape[-1] >= LANES else a.shape[0]
        blk = p[pos:pos + n]
        out.append(blk.reshape(a.shape) if a.shape[-1] >= LANES else blk[:, :a.shape[-1]])
        pos += n + (-n % 8)
    return out


def kernel(x, mix_norm_g, ffn_norm_g, fox_w_in, fox_b_f, fox_q_norm_g, fox_k_norm_g, fox_w_out, pool_w, pool_b, pool_scale, ffn_w_gate_up, ffn_w_down, loss_target, m_mix_norm_g, m_ffn_norm_g, m_fox_w_in, m_fox_b_f, m_fox_q_norm_g, m_fox_k_norm_g, m_fox_w_out, m_pool_w, m_pool_b, m_pool_scale, m_ffn_w_gate_up, m_ffn_w_down, v_mix_norm_g, v_ffn_norm_g, v_fox_w_in, v_fox_b_f, v_fox_q_norm_g, v_fox_k_norm_g, v_fox_w_out, v_pool_w, v_pool_b, v_pool_scale, v_ffn_w_gate_up, v_ffn_w_down):
    xs, tgt = x[0], loss_target[0]
    s, d = xs.shape
    depth = mix_norm_g.shape[0]
    n_fox, n_pool = fox_w_in.shape[0], pool_w.shape[0]
    n_heads = d // HEAD_DIM
    n_in = fox_w_in.shape[2] * N_DEV
    n_pad = 3 * d + LANES
    n_groups = pool_w.shape[1]
    dsh = d // N_DEV
    half = N_DEV // 2
    axes = ("x", "y", "c")

    w_in_bf, w_out_bf, pool_w_bf = fox_w_in.astype(BF16), fox_w_out.astype(BF16), pool_w.astype(BF16)
    gu_bf, dn_bf = ffn_w_gate_up.astype(BF16), ffn_w_down.astype(BF16)
    pool_bs = jnp.stack([pool_b, pool_scale], axis=1)
    mix_gather, ffn_gather = [None] * depth, [None] * depth
    last = None
    for l in range(depth):
        j = l // 2
        shards = [w_in_bf[j:j + 1], w_out_bf[j:j + 1]] if l % 2 == 0 else [pool_w_bf[j:j + 1], pool_bs[j:j + 1]]
        mix_gather[l] = _exchange_start(f"gather_mixer{l}", *_gather_plan(shards), dep=last)
        ffn_gather[l] = _exchange_start(f"gather_ffn{l}", *_gather_plan([gu_bf[l:l + 1], dn_bf[l:l + 1]]),
                                        dep=mix_gather[l]["token"])
        last = ffn_gather[l]["token"]
    started = last[:1, :1]
    w_gu_g, w_dn = [None] * depth, [None] * depth
    w_in, w_out = [None] * n_fox, [None] * n_fox
    w_pool, pool_b_full, pool_s_full = [None] * n_pool, [None] * n_pool, [None] * n_pool
    b_pad =[jnp.pad(fox_b_f[j], (0, LANES - n_heads))[None] for j in range(n_fox)]

    saved = []
    cur = xs
    for i in range(depth):
        j = i // 2
        gm = mix_norm_g[i][None]
        if i == 0:
            gm = gm + started
        if i % 2 == 0:
            w_in_g, w_out_g = _exchange_wait(mix_gather[i], last if i == 0 else cur)
            w_in[j] = jnp.pad(jnp.transpose(w_in_g, (1, 0, 2)).reshape(d, n_in), ((0, 0), (0, n_pad - n_in)))
            w_out[j] = w_out_g.reshape(d, d)
            h = _rms_fwd(f"norm_mix{i}", cur, gm)
            proj =_mm_nn(f"proj_in{i}", h, w_in[j], F32, tn=896)
            gq, gk = fox_q_norm_g[j][None], fox_k_norm_g[j][None]
            qn, kn, vb = _qkv_fwd(f"qk_norm{i}", proj, gq, gk, d)
            flog = proj[:, 3 * d:]
            c = _gate_fwd(f"gate{i}", flog, b_pad[j])
            c_t = c[:, :n_heads].T
            c_col, c_row = c_t[:, :, None], c_t[:, None, :]
            o, lse = _attn_fwd(f"attn{i}", qn, kn, vb.T, c_row, c_col)
            mid = _mm_nn(f"proj_out{i}", o, w_out[j], F32, add=cur)
            mix_saved = (cur, h, proj, flog, qn, kn, vb, c_col, c_row, o, lse)
        else:
            pw_g, pbs_g = _exchange_wait(mix_gather[i], cur)
            w_pool[j] = jnp.transpose(pw_g, (1, 0, 2, 3)).reshape(n_groups, d // n_groups, d // n_groups)
            pbs_full = jnp.transpose(pbs_g, (1, 0, 2)).reshape(2, 1, d)
            pool_b_full[j], pool_s_full[j] = pbs_full[0], pbs_full[1]
            mid, y, zb = _pool_fwd(f"pool{i}", cur, gm, w_pool[j], pool_b_full[j], pool_s_full[j])
            mix_saved = (cur, y, zb)
        h2 = _rms_fwd(f"norm_ffn{i}", mid, ffn_norm_g[i][None])
        w_gu_g[i], dn_g = _exchange_wait(ffn_gather[i], h2)
        w_dn[i] = dn_g.reshape(-1, d)
        gate, up, act =_ffn_up(f"ffn_up{i}", h2, w_gu_g[i])
        nxt = _mm_nn(f"ffn_down{i}", act, w_dn[i], F32, add=mid, tk=1408)
        saved.append((mix_saved, mid, h2, gate, up, act))
        cur = nxt

    dcur, lpart = _loss_head("loss_head", cur, tgt)
    loss = lax.psum(0.5 * lpart[0, 0], axes)

    d_mix, d_ffn = [None] * depth, [None] * depth
    d_bf, d_gq, d_gk = [None] * n_fox, [None] * n_fox, [None] * n_fox
    mix_scatter, ffn_scatter = [None] * depth, [None] * depth
    pending = jnp.zeros((1, 1), F32)
    for i in reversed(range(depth)):
        j = i // 2
        mix_saved, mid, h2, gate, up, act = saved[i]
        dgu = _ffn_dact(f"ffn_dact{i}", dcur, w_dn[i].reshape(half, -1, d), gate, up)
        g_dn = _mm_tn(f"ffn_dw_down{i}", act, dcur, BF16, tm=1408).reshape(N_DEV, -1, d)
        sc_dn = _exchange_start(f"scatter_down{i}", *_scatter_plan([g_dn]))
        g_gu = _ffn_dw_gu(f"ffn_dw_up{i}", h2, dgu, dep=sc_dn["token"])
        sc_gu = _exchange_start(f"scatter_up{i}", *_scatter_plan([g_gu]))
        ffn_scatter[i] = (sc_gu, sc_dn)
        g_ffn = ffn_norm_g[i][None] + pending
        dmid, d_ffn[i] = _ffn_dh(f"ffn_dh{i}", dgu, w_gu_g[i], mid, g_ffn, dcur, dep=sc_gu["token"])
        gm = mix_norm_g[i][None]
        if i % 2 == 0:
            xin, h, proj, flog, qn, kn, vb, c_col, c_row, o, lse = mix_saved
            g_out = _mm_tn(f"proj_out_dw{i}", o, dmid, BF16).reshape(N_DEV, dsh, d)
            sc_out = _exchange_start(f"scatter_out{i}", *_scatter_plan([g_out]))
            do = _mm_nt(f"proj_out_dx{i}", dmid, w_out[j], BF16, dep=sc_out["token"])
            delta = _attn_delta(f"attn_delta{i}", o, do, n_heads)
            delta_row = delta[:, :n_heads].T[:, None, :]
            dqn, dkn, dv, dck, dcq = _attn_bwd(f"attn_bwd{i}", qn, kn, vb, do, c_row,
                                               lse, delta_row, c_col)
            lane_pad = ((0, 0), (0, LANES - n_heads))
            dflog, d_bf[j] = _gate_bwd(f"gate_bwd{i}", jnp.pad(dck[:, :, 0].T, lane_pad),
                                       jnp.pad(dcq[:, 0, :].T, lane_pad), flog, b_pad[j], n_heads)
            gq, gk = fox_q_norm_g[j][None], fox_k_norm_g[j][None]
            dproj, d_gq[j], d_gk[j] = _qkv_bwd(f"qk_norm_bwd{i}", proj, dqn, dkn, dv, dflog, gq, gk, d, n_pad)
            dw_in = _mm_tn(f"proj_in_dw{i}", h, dproj, BF16, tn=896)
            g_in = jnp.transpose(dw_in[:, :n_in].reshape(d, N_DEV, n_in // N_DEV), (1, 0, 2))
            sc_in = _exchange_start(f"scatter_in{i}", *_scatter_plan([g_in]))
            mix_scatter[i] = (sc_in, sc_out)
            dcur, d_mix[i] = _proj_in_dx(f"proj_in_dx{i}", dproj, w_in[j], xin, gm, dmid, dep=sc_in["token"])
        else:
            xin, y, zb = mix_saved
            dcur, dz, d_mix[i], dsc, db = _pool_bwd(f"pool_bwd{i}", dmid, xin, zb, gm, w_pool[j], pool_s_full[j])
            dwp = _pool_dw(f"pool_dw{i}", y, dz, n_groups)
            dg = d // n_groups
            g_pw = jnp.transpose(dwp.reshape(n_groups, N_DEV, dg // N_DEV, dg), (1, 0, 2, 3)).astype(BF16)
            g_pbs = jnp.stack([db.reshape(N_DEV, dsh), dsc.reshape(N_DEV, dsh)], axis=1)
            sc_pool = _exchange_start(f"scatter_pool{i}", *_scatter_plan([g_pw, g_pbs]))
            mix_scatter[i] = (sc_pool,)
            pending = sc_pool["token"][:1, :1]
    grad_x = dcur[None]

    mix_landed = [sum((_exchange_wait(hd, dcur) for hd in mix_scatter[l]), []) for l in range(depth)]
    landed = [sum((_exchange_wait(hd, dcur) for hd in ffn_scatter[l]), []) for l in range(depth)]
    r_in, r_out = [t[0] for t in mix_landed[0::2]], [t[1] for t in mix_landed[0::2]]
    r_pw = [t[0].reshape(N_DEV, -1, t[0].shape[-1]) for t in mix_landed[1::2]]
    r_pbs = [t[1] for t in mix_landed[1::2]]
    r_gu, r_dn = [t[0] for t in landed], [t[1] for t in landed]
    upd = {}
    upd["fox_w_in"] = _adamw_layers("adamw_w_in", fox_w_in, m_fox_w_in, v_fox_w_in, r_in)
    upd["fox_w_out"] = _adamw_layers("adamw_w_out", fox_w_out, m_fox_w_out, v_fox_w_out, r_out)
    fold = lambda a: a.reshape(n_pool, -1, a.shape[-1])
    upd["pool_w"] = [o.reshape(pool_w.shape) for o in
                     _adamw_layers("adamw_pool_w", fold(pool_w), fold(m_pool_w), fold(v_pool_w), r_pw)]
    pbs = _adamw_layers("adamw_pool_bs", pool_bs, jnp.stack([m_pool_b, m_pool_scale], axis=1),
                        jnp.stack([v_pool_b, v_pool_scale], axis=1), r_pbs)
    upd["pool_b"] = [o[:, 0] for o in pbs]
    upd["pool_scale"] = [o[:, 1] for o in pbs]
    upd["ffn_w_gate_up"] = _adamw_layers("adamw_gate_up", ffn_w_gate_up, m_ffn_w_gate_up, v_ffn_w_gate_up, r_gu)
    upd["ffn_w_down"] = _adamw_layers("adamw_down", ffn_w_down, m_ffn_w_down, v_ffn_w_down, r_dn)

    small_w = (mix_norm_g, ffn_norm_g, fox_b_f, fox_q_norm_g, fox_k_norm_g)
    small_g = _pack_small(jnp.concatenate(d_mix), jnp.concatenate(d_ffn),
                          jnp.concatenate(d_bf)[:, :n_heads], jnp.concatenate(d_gq), jnp.concatenate(d_gk))
    (small_pieces,), = _all_gather_layers("gather_small", [small_g[None]])
    small = _adamw("adamw_small", _pack_small(*small_w),
                   _pack_small(m_mix_norm_g, m_ffn_norm_g, m_fox_b_f, m_fox_q_norm_g, m_fox_k_norm_g),
                   _pack_small(v_mix_norm_g, v_ffn_norm_g, v_fox_b_f, v_fox_q_norm_g, v_fox_k_norm_g),
                   small_pieces)
    small = [_unpack_small(o, *small_w) for o in small]
    for n, name in enumerate(("mix_norm_g", "ffn_norm_g", "fox_b_f", "fox_q_norm_g", "fox_k_norm_g")):
        upd[name] = [o[n] for o in small]

    order = ("mix_norm_g", "ffn_norm_g", "fox_w_in", "fox_b_f", "fox_q_norm_g", "fox_k_norm_g", "fox_w_out",
             "pool_w", "pool_b", "pool_scale", "ffn_w_gate_up", "ffn_w_down")
    return (loss, grad_x) + tuple(upd[name][q] for q in range(4) for name in order)
```

```python
import functools

import jax
import jax.numpy as jnp
from jax import lax
from jax.experimental import pallas as pl
from jax.experimental.pallas import tpu as pltpu

F32 = jnp.float32
BF16 = jnp.bfloat16
MESH = pl.DeviceIdType.MESH

N_DEV = 8
HEAD_DIM = 128
LANES = 128
POOL_WINDOWS = (2, 4, 8, 16)
POOL_HALO = 16
RMS_EPS = 1e-6
NEG_INF = -1e30
ADAM_LR = 0.001
ADAM_B1 = 0.9
ADAM_B2 = 0.999
ADAM_EPS = 1e-08
ADAM_WD = 0.01
ADAM_STEP = 10
VMEM_LIMIT = 52 * 1024 * 1024

NN = (((1,), (0,)), ((), ()))
NT = (((1,), (1,)), ((), ()))
TN = (((0,), (0,)), ((), ()))


def _tile(n, pref, align):
    best = None
    d = align
    while d <= min(n, pref):
        if n % d == 0:
            best = d
        d += align
    return n if best is None else best


def _params(*sem):
    return pltpu.CompilerParams(dimension_semantics=sem, vmem_limit_bytes=VMEM_LIMIT)


def _position():
    x, y, c = lax.axis_index("x"), lax.axis_index("y"), lax.axis_index("c")
    return x, y, c, 4 * x + 2 * y + c


def _peer(x, y, c, k):
    px = 1 - x if k & 4 else x
    py = 1 - y if k & 2 else y
    pc = 1 - c if k & 1 else c
    return (px, py, pc), 4 * px + 2 * py + pc


def _exchange(name, ins, out_shapes, copies):
    n_in, n_cp = len(ins), len(copies)

    def body(*refs):
        in_refs = refs[:n_in]
        out_refs = refs[n_in:n_in + len(out_shapes)]
        send_sems, recv_sems, loc_sems = refs[n_in + len(out_shapes):]
        x, y, c, me = _position()
        local = []
        for ci, (ii, src_of, oi, dst_of) in enumerate(copies):
            cp = pltpu.make_async_copy(src_of(in_refs[ii], me), dst_of(out_refs[oi], me), loc_sems.at[ci])
            cp.start()
            local.append(cp)
        sends, recvs = [], []
        for k in range(1, N_DEV):
            pid, p = _peer(x, y, c, k)
            for ci, (ii, src_of, oi, dst_of) in enumerate(copies):
                sem = ci * (N_DEV - 1) + k - 1
                send = pltpu.make_async_remote_copy(
                    src_ref=src_of(in_refs[ii], p), dst_ref=dst_of(out_refs[oi], me),
                    send_sem=send_sems.at[sem], recv_sem=recv_sems.at[sem],
                    device_id=pid, device_id_type=MESH)
                send.start()
                sends.append(send)
                recvs.append(pltpu.make_async_remote_copy(
                    src_ref=src_of(in_refs[ii], p), dst_ref=dst_of(out_refs[oi], p),
                    send_sem=send_sems.at[sem], recv_sem=recv_sems.at[sem],
                    device_id=pid, device_id_type=MESH))
        for r in recvs:
            r.wait_recv()
        for s in sends:
            s.wait_send()
        for cp in local:
            cp.wait()

    any_spec = pl.BlockSpec(memory_space=pl.ANY)
    return pl.pallas_call(
        body, name=name,
        out_shape=tuple(out_shapes),
        in_specs=[any_spec] * n_in,
        out_specs=tuple([any_spec] * len(out_shapes)),
        scratch_shapes=[pltpu.SemaphoreType.DMA((n_cp * (N_DEV - 1),)),
                        pltpu.SemaphoreType.DMA((n_cp * (N_DEV - 1),)),
                        pltpu.SemaphoreType.DMA((n_cp,))],
    )(*ins)


def _exchange_start(name, ins, out_shapes, copies, dep=None):
    n_in, n_out, n_cp = len(ins), len(out_shapes), len(copies)

    def body(*refs):
        in_refs = refs[:n_in]
        land_refs = refs[n_in:n_in + n_out]
        outs = refs[n_in + n_out + (dep is not None):]
        send_sems, recv_sems = outs[:2]
        token_ref, loc_sems = outs[n_in + n_out + 2], outs[n_in + n_out + 3]
        x, y, c, me = _position()
        local = []
        for ci, (ii, src_of, oi, dst_of) in enumerate(copies):
            cp = pltpu.make_async_copy(src_of(in_refs[ii], me), dst_of(land_refs[oi], me), loc_sems.at[ci])
            cp.start()
            local.append(cp)
        for cp in local:
            cp.wait()
        for k in range(1, N_DEV):
            pid, p = _peer(x, y, c, k)
            for ci, (ii, src_of, oi, dst_of) in enumerate(copies):
                sem = ci * (N_DEV - 1) + k - 1
                pltpu.make_async_remote_copy(
                    src_ref=src_of(in_refs[ii], p), dst_ref=dst_of(land_refs[oi], me),
                    send_sem=send_sems.at[sem], recv_sem=recv_sems.at[sem],
                    device_id=pid, device_id_type=MESH).start()
        token_ref[...] = jnp.zeros_like(token_ref)

    hbm = pl.BlockSpec(memory_space=pltpu.HBM)
    sem = pl.BlockSpec(memory_space=pltpu.SEMAPHORE)
    n_sem = n_cp * (N_DEV - 1)
    lands = [pltpu.with_memory_space_constraint(lax.empty(o.shape, o.dtype), pltpu.HBM) for o in out_shapes]
    srcs = [pltpu.with_memory_space_constraint(a, pltpu.HBM) for a in ins]
    res = pl.pallas_call(
        body, name=name,
        out_shape=(pltpu.SemaphoreType.DMA((n_sem,)), pltpu.SemaphoreType.DMA((n_sem,)),
                   *[pltpu.HBM(a.shape, a.dtype) for a in ins],
                   *[pltpu.HBM(o.shape, o.dtype) for o in out_shapes],
                   jax.ShapeDtypeStruct((8, LANES), F32)),
        in_specs=[hbm] * (n_in + n_out) + ([pl.BlockSpec(memory_space=pl.ANY)] if dep is not None else []),
        out_specs=(sem, sem, *([hbm] * (n_in + n_out)), pl.BlockSpec(memory_space=pltpu.VMEM)),
        input_output_aliases={i: 2 + i for i in range(n_in + n_out)},
        scratch_shapes=[pltpu.SemaphoreType.DMA((n_cp,))],
        compiler_params=pltpu.CompilerParams(has_side_effects=pltpu.SideEffectType.DATAFLOW_SIDE_EFFECTING),
    )(*srcs, *lands, *([dep] if dep is not None else []))
    return dict(name=name, copies=copies, send=res[0], recv=res[1], srcs=list(res[2:2 + n_in]),
                lands=list(res[2 + n_in:2 + n_in + n_out]), token=res[-1])


def _exchange_wait(handle, after):
    copies, srcs, lands = handle["copies"], handle["srcs"], handle["lands"]
    n_in, n_out = len(srcs), len(lands)

    def body(*refs):
        in_refs = refs[:n_in]
        land_refs = refs[n_in:n_in + n_out]
        send_sems, recv_sems = refs[n_in + n_out:n_in + n_out + 2]
        x, y, c, me = _position()
        waits = []
        for k in range(1, N_DEV):
            pid, p = _peer(x, y, c, k)
            for ci, (ii, src_of, oi, dst_of) in enumerate(copies):
                sem = ci * (N_DEV - 1) + k - 1
                waits.append(pltpu.make_async_remote_copy(
                    src_ref=src_of(in_refs[ii], p), dst_ref=dst_of(land_refs[oi], p),
                    send_sem=send_sems.at[sem], recv_sem=recv_sems.at[sem],
                    device_id=pid, device_id_type=MESH))
        for w in waits:
            w.wait_send()
        for w in waits:
            w.wait_recv()

    hbm = pl.BlockSpec(memory_space=pltpu.HBM)
    sem = pl.BlockSpec(memory_space=pltpu.SEMAPHORE)
    res = pl.pallas_call(
        body, name=handle["name"] + "_wait",
        out_shape=tuple(pltpu.HBM(a.shape, a.dtype) for a in srcs + lands),
        in_specs=[hbm] * (n_in + n_out) + [sem, sem, pl.BlockSpec(memory_space=pl.ANY)],
        out_specs=tuple([hbm] * (n_in + n_out)),
        input_output_aliases={i: i for i in range(n_in + n_out)},
        compiler_params=pltpu.CompilerParams(has_side_effects=pltpu.SideEffectType.DATAFLOW_SIDE_EFFECTING),
    )(*srcs, *lands, handle["send"], handle["recv"], after)
    return list(res[n_in:])


def _gather_plan(stacked):
    ins, outs, copies = [], [], []
    for t in stacked:
        ii = len(ins)
        ins.append(t)
        for l in range(t.shape[0]):
            oi = len(outs)
            outs.append(jax.ShapeDtypeStruct((N_DEV,) + t.shape[1:], t.dtype))
            copies.append((ii, (lambda ref, p, l=l: ref.at[l]), oi, (lambda ref, s: ref.at[s])))
    return ins, outs, copies


def _scatter_plan(blocked):
    outs = [jax.ShapeDtypeStruct(t.shape, t.dtype) for t in blocked]
    copies = [(n, (lambda ref, p: ref.at[p]), n, (lambda ref, s: ref.at[s])) for n in range(len(blocked))]
    return list(blocked), outs, copies


def _all_gather_layers(name, stacked):
    ins, outs, copies = [], [], []
    for t in stacked:
        ii = len(ins)
        ins.append(t)
        for l in range(t.shape[0]):
            oi = len(outs)
            outs.append(jax.ShapeDtypeStruct((N_DEV,) + t.shape[1:], t.dtype))
            copies.append((ii, (lambda ref, p, l=l: ref.at[l]), oi, (lambda ref, s: ref.at[s])))
    res = _exchange(name, ins, outs, copies)
    out, pos = [], 0
    for t in stacked:
        out.append(list(res[pos:pos + t.shape[0]]))
        pos += t.shape[0]
    return out


def _mm(name, mode, a, b, out_shape, *, grid, a_spec, b_spec, o_spec, acc_shape, add=None, add_spec=None, dep=None):
    nk = grid[2]
    dn = {"nn": NN, "nt": NT, "tn": TN}[mode]
    has_add, has_dep = add is not None, dep is not None
    own_acc = nk > 1 and out_shape.dtype != F32

    def body(*refs):
        a_ref, b_ref = refs[:2]
        add_ref = refs[2] if has_add else None
        o_ref = refs[2 + has_add + has_dep]

        def product():
            return lax.dot_general(a_ref[...].astype(BF16), b_ref[...].astype(BF16), dn,
                                   preferred_element_type=F32)

        if nk == 1:
            r = product() + add_ref[...] if has_add else product()
            o_ref[...] = r.astype(o_ref.dtype)
        else:
            acc_ref = refs[-1] if own_acc else o_ref
            k = pl.program_id(2)

            @pl.when(k == 0)
            def _():
                acc_ref[...] = add_ref[...] if has_add else jnp.zeros_like(acc_ref)

            acc_ref[...] += product()
            if own_acc:
                @pl.when(k == nk - 1)
                def _():
                    o_ref[...] = acc_ref[...].astype(o_ref.dtype)

    ins = [a, b] + ([add] if has_add else []) + ([dep] if has_dep else [])
    in_specs = ([a_spec, b_spec] + ([add_spec] if has_add else [])
                + ([pl.BlockSpec(memory_space=pl.ANY)] if has_dep else []))
    scratch = [pltpu.VMEM(acc_shape, F32)] if own_acc else []
    return pl.pallas_call(
        body, name=name, grid=grid, out_shape=out_shape,
        in_specs=in_specs, out_specs=o_spec, scratch_shapes=scratch,
        compiler_params=_params("parallel", "parallel", "arbitrary"),
    )(*ins)


def _mm_nn(name, a, b, out_dtype, add=None, tm=1024, tn=1024, tk=2048):
    m, kd = a.shape
    n = b.shape[1]
    tm, tn, tk = _tile(m, tm, 16), _tile(n, tn, LANES), _tile(kd, tk, LANES)
    return _mm(name, "nn", a, b, jax.ShapeDtypeStruct((m, n), out_dtype),
               grid=(m // tm, n // tn, kd // tk),
               a_spec=pl.BlockSpec((tm, tk), lambda i, j, k: (i, k)),
               b_spec=pl.BlockSpec((tk, tn), lambda i, j, k: (k, j)),
               o_spec=pl.BlockSpec((tm, tn), lambda i, j, k: (i, j)),
               acc_shape=(tm, tn), add=add,
               add_spec=pl.BlockSpec((tm, tn), lambda i, j, k: (i, j)))


def _mm_nt(name, a, b, out_dtype, tm=1024, tn=1024, tk=2048, dep=None):
    m, kd = a.shape
    n = b.shape[0]
    tm, tn, tk = _tile(m, tm, 16), _tile(n, tn, LANES), _tile(kd, tk, LANES)
    return _mm(name, "nt", a, b, jax.ShapeDtypeStruct((m, n), out_dtype),
               grid=(m // tm, n // tn, kd // tk),
               a_spec=pl.BlockSpec((tm, tk), lambda i, j, k: (i, k)),
               b_spec=pl.BlockSpec((tn, tk), lambda i, j, k: (j, k)),
               o_spec=pl.BlockSpec((tm, tn), lambda i, j, k: (i, j)),
               acc_shape=(tm, tn), dep=dep)


def _mm_tn(name, a, b, out_dtype, tm=1024, tn=1024, ts=1024, dep=None):
    s, m = a.shape
    n = b.shape[1]
    tm, tn, ts = _tile(m, tm, LANES), _tile(n, tn, LANES), _tile(s, ts, 16)
    return _mm(name, "tn", a, b, jax.ShapeDtypeStruct((m, n), out_dtype),
               grid=(m // tm, n // tn, s // ts),
               a_spec=pl.BlockSpec((ts, tm), lambda i, j, k: (k, i)),
               b_spec=pl.BlockSpec((ts, tn), lambda i, j, k: (k, j)),
               o_spec=pl.BlockSpec((tm, tn), lambda i, j, k: (i, j)),
               acc_shape=(tm, tn), dep=dep)


def _rms_fwd(name, x, g):
    s, d = x.shape
    tm = _tile(s, 512, 16)

    def body(x_ref, g_ref, h_ref):
        xv = x_ref[...]
        r = lax.rsqrt(jnp.mean(xv * xv, axis=-1, keepdims=True) + RMS_EPS)
        h_ref[...] = ((xv * r) * g_ref[...]).astype(BF16)

    return pl.pallas_call(
        body, name=name, grid=(s // tm,), out_shape=jax.ShapeDtypeStruct((s, d), BF16),
        in_specs=[pl.BlockSpec((tm, d), lambda i: (i, 0)), pl.BlockSpec((1, d), lambda i: (0, 0))],
        out_specs=pl.BlockSpec((tm, d), lambda i: (i, 0)),
        compiler_params=_params("parallel"),
    )(x, g)


def _mm_rms_bwd(name, a, b, x, g, dres, *, tm, nk, a_spec, b_spec, dep=None):
    s, d = x.shape
    has_dep = dep is not None
    ch = _tile(tm, 128, 8)

    def body(*refs):
        a_ref, b_ref, x_ref, g_ref, dres_ref = refs[:5]
        dx_ref, dg_ref = refs[5 + has_dep], refs[6 + has_dep]
        i, k = pl.program_id(0), pl.program_id(1)

        @pl.when(k == 0)
        def _():
            dx_ref[...] = jnp.zeros_like(dx_ref)

        dx_ref[...] += lax.dot_general(a_ref[...].astype(BF16), b_ref[...].astype(BF16), NT,
                                       preferred_element_type=F32)

        @pl.when(k == nk - 1)
        def _():
            def rows_bwd(c, part):
                rows = pl.ds(pl.multiple_of(c * ch, ch), ch)
                dhv = dx_ref[rows, :]
                xv = x_ref[rows, :]
                r = lax.rsqrt(jnp.mean(xv * xv, axis=-1, keepdims=True) + RMS_EPS)
                xhat = xv * r
                gdh = dhv * g_ref[...]
                dx_ref[rows, :] = dres_ref[rows, :] + r * (gdh - xhat * jnp.mean(gdh * xhat, axis=-1, keepdims=True))
                return part + jnp.sum(dhv * xhat, axis=0, keepdims=True)

            part = lax.fori_loop(0, tm // ch, rows_bwd, jnp.zeros((1, d), F32))

            @pl.when(i == 0)
            def _():
                dg_ref[...] = part

            @pl.when(i > 0)
            def _():
                dg_ref[...] += part

    row = pl.BlockSpec((tm, d), lambda i, k: (i, 0))
    vec = pl.BlockSpec((1, d), lambda i, k: (0, 0))
    return pl.pallas_call(
        body, name=name, grid=(s // tm, nk),
        out_shape=(jax.ShapeDtypeStruct((s, d), F32), jax.ShapeDtypeStruct((1, d), F32)),
        in_specs=[a_spec, b_spec, row, vec, row] + ([pl.BlockSpec(memory_space=pl.ANY)] if has_dep else []),
        out_specs=(row, vec),
        compiler_params=_params("arbitrary", "arbitrary"),
    )(a, b, x, g, dres, *([dep] if has_dep else []))


def _split3(v):
    hi = v.astype(BF16)
    r1 = v - hi.astype(F32)
    mid = r1.astype(BF16)
    lo = (r1 - mid.astype(F32)).astype(BF16)
    return hi, mid, lo


def _tri_sum(tri, v):
    hi, mid, lo = _split3(v)
    dot = functools.partial(lax.dot_general, dimension_numbers=NN, preferred_element_type=F32)
    return dot(tri, hi) + dot(tri, mid) + dot(tri, lo)


def _gate_fwd(name, flog, b_pad):
    s = flog.shape[0]
    tb = _tile(s, 256, 16)

    def body(f_ref, b_ref, c_ref, carry_ref):
        i = pl.program_id(0)

        @pl.when(i == 0)
        def _():
            carry_ref[...] = jnp.zeros_like(carry_ref)

        z = f_ref[...] + b_ref[...]
        lf = jnp.minimum(z, 0.0) - jnp.log(1.0 + jnp.exp(-jnp.abs(z)))
        rows = lax.broadcasted_iota(jnp.int32, (tb, tb), 0)
        cols = lax.broadcasted_iota(jnp.int32, (tb, tb), 1)
        tri = (rows >= cols).astype(BF16)
        c_ref[...] = _tri_sum(tri, lf) + carry_ref[...]
        carry_ref[...] = c_ref[pl.ds(tb - 1, 1), :]

    return pl.pallas_call(
        body, name=name, grid=(s // tb,), out_shape=jax.ShapeDtypeStruct((s, LANES), F32),
        in_specs=[pl.BlockSpec((tb, LANES), lambda i: (i, 0)), pl.BlockSpec((1, LANES), lambda i: (0, 0))],
        out_specs=pl.BlockSpec((tb, LANES), lambda i: (i, 0)),
        scratch_shapes=[pltpu.VMEM((1, LANES), F32)],
        compiler_params=_params("arbitrary"),
    )(flog, b_pad)


def _gate_bwd(name, dck, dcq, flog, b_pad, n_heads):
    s = flog.shape[0]
    tb = _tile(s, 256, 16)
    nb = s // tb

    def body(dck_ref, dcq_ref, f_ref, b_ref, df_ref, db_ref, carry_ref, tmp_ref):
        i = pl.program_id(0)

        @pl.when(i == 0)
        def _():
            carry_ref[...] = jnp.zeros_like(carry_ref)

        rows = lax.broadcasted_iota(jnp.int32, (tb, tb), 0)
        cols = lax.broadcasted_iota(jnp.int32, (tb, tb), 1)
        tri = (rows <= cols).astype(BF16)
        tmp_ref[...] = _tri_sum(tri, dck_ref[...] + dcq_ref[...]) + carry_ref[...]
        carry_ref[...] = tmp_ref[pl.ds(0, 1), :]
        z = f_ref[...] + b_ref[...]
        lane = lax.broadcasted_iota(jnp.int32, (tb, LANES), 1)
        df = jnp.where(lane < n_heads, tmp_ref[...] / (1.0 + jnp.exp(z)), 0.0)
        df_ref[...] = df.astype(BF16)
        part = jnp.sum(df, axis=0, keepdims=True)

        @pl.when(i == 0)
        def _():
            db_ref[...] = part

        @pl.when(i > 0)
        def _():
            db_ref[...] += part

    rev = pl.BlockSpec((tb, LANES), lambda i: (nb - 1 - i, 0))
    vec = pl.BlockSpec((1, LANES), lambda i: (0, 0))
    return pl.pallas_call(
        body, name=name, grid=(nb,),
        out_shape=(jax.ShapeDtypeStruct((s, LANES), BF16), jax.ShapeDtypeStruct((1, LANES), F32)),
        in_specs=[rev, rev, rev, vec], out_specs=(rev, vec),
        scratch_shapes=[pltpu.VMEM((1, LANES), F32), pltpu.VMEM((tb, LANES), F32)],
        compiler_params=_params("arbitrary"),
    )(dck, dcq, flog, b_pad)


def _head_rms(v, g):
    r = lax.rsqrt(jnp.mean(v * v, axis=-1, keepdims=True) + RMS_EPS)
    return (v * r) * g


def _qkv_fwd(name, proj, gq, gk, d):
    s = proj.shape[0]
    tm = _tile(s, 256, 16)
    n_heads = d // HEAD_DIM

    def body(q_ref, k_ref, v_ref, gq_ref, gk_ref, qn_ref, kn_ref, vb_ref):
        for h in range(n_heads):
            sl = slice(h * HEAD_DIM, (h + 1) * HEAD_DIM)
            qn_ref[:, sl] = _head_rms(q_ref[:, sl], gq_ref[...]).astype(BF16)
            kn_ref[:, sl] = _head_rms(k_ref[:, sl], gk_ref[...]).astype(BF16)
        vb_ref[...] = v_ref[...].astype(BF16)

    col = lambda c: pl.BlockSpec((tm, d), lambda i, c=c: (i, c))
    vec = pl.BlockSpec((1, HEAD_DIM), lambda i: (0, 0))
    out = jax.ShapeDtypeStruct((s, d), BF16)
    return pl.pallas_call(
        body, name=name, grid=(s // tm,), out_shape=(out, out, out),
        in_specs=[col(0), col(1), col(2), vec, vec], out_specs=(col(0), col(0), col(0)),
        compiler_params=_params("parallel"),
    )(proj, proj, proj, gq, gk)


def _qkv_bwd(name, proj, dqn, dkn, dv, dflog, gq, gk, d, n_pad):
    s = proj.shape[0]
    tm = _tile(s, 256, 16)
    n_heads = d // HEAD_DIM

    def head_bwd(raw, dy, g):
        r = lax.rsqrt(jnp.mean(raw * raw, axis=-1, keepdims=True) + RMS_EPS)
        hat = raw * r
        gdy = dy * g
        dx = r * (gdy - hat * jnp.mean(gdy * hat, axis=-1, keepdims=True))
        return dx, jnp.sum(dy * hat, axis=0, keepdims=True)

    def body(q_ref, k_ref, dqn_ref, dkn_ref, dv_ref, df_ref, gq_ref, gk_ref, dp_ref, dgq_ref, dgk_ref):
        i = pl.program_id(0)
        accq = jnp.zeros((1, HEAD_DIM), F32)
        acck = jnp.zeros((1, HEAD_DIM), F32)
        for h in range(n_heads):
            sl = slice(h * HEAD_DIM, (h + 1) * HEAD_DIM)
            dq, pq = head_bwd(q_ref[:, sl], dqn_ref[:, sl], gq_ref[...])
            dk, pk = head_bwd(k_ref[:, sl], dkn_ref[:, sl], gk_ref[...])
            dp_ref[:, sl] = dq.astype(BF16)
            dp_ref[:, d + h * HEAD_DIM:d + (h + 1) * HEAD_DIM] = dk.astype(BF16)
            accq, acck = accq + pq, acck + pk
        dp_ref[:, 2 * d:3 * d] = dv_ref[...]
        dp_ref[:, 3 * d:] = df_ref[...]

        @pl.when(i == 0)
        def _():
            dgq_ref[...] = accq
            dgk_ref[...] = acck

        @pl.when(i > 0)
        def _():
            dgq_ref[...] += accq
            dgk_ref[...] += acck

    col = lambda c: pl.BlockSpec((tm, d), lambda i, c=c: (i, c))
    vec = pl.BlockSpec((1, HEAD_DIM), lambda i: (0, 0))
    return pl.pallas_call(
        body, name=name, grid=(s // tm,),
        out_shape=(jax.ShapeDtypeStruct((s, n_pad), BF16), jax.ShapeDtypeStruct((1, HEAD_DIM), F32),
                   jax.ShapeDtypeStruct((1, HEAD_DIM), F32)),
        in_specs=[col(0), col(1), col(0), col(0), col(0), pl.BlockSpec((tm, LANES), lambda i: (i, 0)), vec, vec],
        out_specs=(pl.BlockSpec((tm, n_pad), lambda i: (i, 0)), vec, vec),
        compiler_params=_params("arbitrary"),
    )(proj, proj, dqn, dkn, dv, dflog, gq, gk)


def _attn_fwd(name, qn, kn, vt, c_row, c_col):
    s, d = qn.shape
    n_heads = d // HEAD_DIM
    t = _tile(s, 512, LANES)
    scale = HEAD_DIM ** -0.5

    hp = 2 if n_heads % 2 == 0 else 1
    log2e = 1.4426950408889634

    def body(q_ref, k_ref, vt_ref, cq_ref, ck_ref, o_ref, lse_ref, m_ref, l_ref, acc_ref):
        i = pl.program_id(1)
        m_ref[...] = jnp.full(m_ref.shape, NEG_INF, F32)
        l_ref[...] = jnp.zeros_like(l_ref)
        acc_ref[...] = jnp.zeros_like(acc_ref)

        def step(j, masked):
            start = pl.multiple_of(j * t, t)
            for hh in range(hp):
                sl = slice(hh * HEAD_DIM, (hh + 1) * HEAD_DIM)
                kj = k_ref[pl.ds(start, t), sl]
                vtj = vt_ref[sl, pl.ds(start, t)]
                st = (lax.dot_general(kj, q_ref[:, sl], NT, preferred_element_type=F32) * (scale * log2e)
                      - ck_ref[hh, pl.ds(start, t), :] * log2e)
                if masked:
                    rows = lax.broadcasted_iota(jnp.int32, (t, t), 0)
                    cols = lax.broadcasted_iota(jnp.int32, (t, t), 1)
                    st = jnp.where(cols >= rows, st, NEG_INF)
                m_prev = m_ref[hh]
                m_new = jnp.maximum(m_prev, jnp.max(st, axis=0, keepdims=True))
                pt = jnp.exp2(st - m_new)
                alpha = jnp.exp2(m_prev - m_new)
                l_ref[hh] = alpha * l_ref[hh] + jnp.sum(pt, axis=0, keepdims=True)
                acc_ref[hh] = alpha * acc_ref[hh] + lax.dot_general(
                    vtj, pt.astype(BF16), NN, preferred_element_type=F32)
                m_ref[hh] = m_new

        def loop_body(j, carry):
            step(j, False)
            return carry

        lax.fori_loop(0, i, loop_body, 0)
        step(i, True)
        for hh in range(hp):
            sl = slice(hh * HEAD_DIM, (hh + 1) * HEAD_DIM)
            o_ref[:, sl] = (acc_ref[hh] / l_ref[hh]).T.astype(BF16)
            lse_ref[hh] = (m_ref[hh] + jnp.log2(l_ref[hh])) * (1.0 / log2e) + cq_ref[hh]

    wide = hp * HEAD_DIM
    row_blk = pl.BlockSpec((hp, 1, t), lambda h, i: (h, 0, i))
    return pl.pallas_call(
        body, name=name, grid=(n_heads // hp, s // t),
        out_shape=(jax.ShapeDtypeStruct((s, d), BF16), jax.ShapeDtypeStruct((n_heads, 1, s), F32)),
        in_specs=[pl.BlockSpec((t, wide), lambda h, i: (i, h)),
                  pl.BlockSpec((s, wide), lambda h, i: (0, h)),
                  pl.BlockSpec((wide, s), lambda h, i: (h, 0)),
                  row_blk, pl.BlockSpec((hp, s, 1), lambda h, i: (h, 0, 0))],
        out_specs=(pl.BlockSpec((t, wide), lambda h, i: (i, h)), row_blk),
        scratch_shapes=[pltpu.VMEM((hp, 1, t), F32), pltpu.VMEM((hp, 1, t), F32),
                        pltpu.VMEM((hp, HEAD_DIM, t), F32)],
        compiler_params=_params("parallel", "arbitrary"),
    )(qn, kn, vt, c_row, c_col)


def _attn_delta(name, o, do, n_heads):
    s, d = o.shape
    tm = _tile(s, 256, 16)

    def body(o_ref, do_ref, dl_ref):
        lane = lax.broadcasted_iota(jnp.int32, (tm, LANES), 1)
        acc = jnp.zeros((tm, LANES), F32)
        for h in range(n_heads):
            sl = slice(h * HEAD_DIM, (h + 1) * HEAD_DIM)
            col = jnp.sum(o_ref[:, sl].astype(F32) * do_ref[:, sl].astype(F32), axis=-1, keepdims=True)
            acc = jnp.where(lane == h, col, acc)
        dl_ref[...] = acc

    row = pl.BlockSpec((tm, d), lambda i: (i, 0))
    return pl.pallas_call(
        body, name=name, grid=(s // tm,), out_shape=jax.ShapeDtypeStruct((s, LANES), F32),
        in_specs=[row, row], out_specs=pl.BlockSpec((tm, LANES), lambda i: (i, 0)),
        compiler_params=_params("parallel"),
    )(o, do)


def _attn_bwd(name, qn, kn, vb, do, c_row, lse_row, delta_row, c_col):
    s, d = qn.shape
    n_heads = d // HEAD_DIM
    t = _tile(s, 512, LANES)
    nq = s // t
    scale = HEAD_DIM ** -0.5

    hp = 2 if n_heads % 2 == 0 else 1

    def body(q_ref, do_ref, cr_ref, lse_ref, dl_ref, k_ref, v_ref, ck_ref,
             dq_ref, dk_ref, dv_ref, dc_ref, dcq_ref, dk_acc, dv_acc, dc_acc):
        j = pl.program_id(1)

        @pl.when(j == 0)
        def _():
            dq_ref[...] = jnp.zeros_like(dq_ref)
            dcq_ref[...] = jnp.zeros_like(dcq_ref)

        dk_acc[...] = jnp.zeros_like(dk_acc)
        dv_acc[...] = jnp.zeros_like(dv_acc)
        dc_acc[...] = jnp.zeros_like(dc_acc)

        def step(i, masked):
            start = pl.multiple_of(i * t, t)
            for hh in range(hp):
                sl = slice(hh * HEAD_DIM, (hh + 1) * HEAD_DIM)
                kj = k_ref[:, sl]
                qi = q_ref[pl.ds(start, t), sl]
                doi = do_ref[pl.ds(start, t), sl]
                bias = cr_ref[hh, :, pl.ds(start, t)] - lse_ref[hh, :, pl.ds(start, t)]
                dli = dl_ref[hh, :, pl.ds(start, t)]
                st = lax.dot_general(kj, qi, NT, preferred_element_type=F32) * scale + (bias - ck_ref[hh])
                if masked:
                    rows = lax.broadcasted_iota(jnp.int32, (t, t), 0)
                    cols = lax.broadcasted_iota(jnp.int32, (t, t), 1)
                    st = jnp.where(cols >= rows, st, NEG_INF)
                pt = jnp.exp(st)
                dpt = lax.dot_general(v_ref[:, sl], doi, NT, preferred_element_type=F32)
                dst = pt * (dpt - dli)
                dsb = dst.astype(BF16)
                dv_acc[:, sl] += lax.dot_general(pt.astype(BF16), doi, NN, preferred_element_type=F32)
                dk_acc[:, sl] += lax.dot_general(dsb, qi, NN, preferred_element_type=F32)
                dq_ref[pl.ds(start, t), sl] += lax.dot_general(dsb, kj, TN, preferred_element_type=F32) * scale
                dc_acc[hh] += jnp.sum(dst, axis=1, keepdims=True)
                dcq_ref[hh, :, pl.ds(start, t)] += jnp.sum(dst, axis=0, keepdims=True)

        step(j, True)

        def loop_body(i, carry):
            step(i, False)
            return carry

        lax.fori_loop(j + 1, nq, loop_body, 0)
        dk_ref[...] = dk_acc[...] * scale
        dv_ref[...] = dv_acc[...].astype(BF16)
        dc_ref[...] = -dc_acc[...]

    wide = hp * HEAD_DIM
    head_all = pl.BlockSpec((s, wide), lambda h, j: (0, h))
    row_all = pl.BlockSpec((hp, 1, s), lambda h, j: (h, 0, 0))
    blk = pl.BlockSpec((t, wide), lambda h, j: (j, h))
    col_blk = pl.BlockSpec((hp, t, 1), lambda h, j: (h, j, 0))
    return pl.pallas_call(
        body, name=name, grid=(n_heads // hp, nq),
        out_shape=(jax.ShapeDtypeStruct((s, d), F32), jax.ShapeDtypeStruct((s, d), F32),
                   jax.ShapeDtypeStruct((s, d), BF16), jax.ShapeDtypeStruct((n_heads, s, 1), F32),
                   jax.ShapeDtypeStruct((n_heads, 1, s), F32)),
        in_specs=[head_all, head_all, row_all, row_all, row_all, blk, blk, col_blk],
        out_specs=(head_all, blk, blk, col_blk, row_all),
        scratch_shapes=[pltpu.VMEM((t, wide), F32), pltpu.VMEM((t, wide), F32), pltpu.VMEM((hp, t, 1), F32)],
        compiler_params=_params("parallel", "arbitrary"),
    )(qn, do, c_row, lse_row, delta_row, kn, vb, c_col)


def _ffn_up(name, h, w_gu):
    s, d = h.shape
    fs = w_gu.shape[2]
    half = N_DEV // 2
    tm = _tile(s, 512, 16)

    def body(h_ref, wg_ref, wu_ref, s_ref, us_ref, a_ref):
        hv = h_ref[...]
        g = lax.dot_general(hv, wg_ref[...], NN, preferred_element_type=F32)
        u = lax.dot_general(hv, wu_ref[...], NN, preferred_element_type=F32)
        sig = jax.nn.sigmoid(g)
        silu = g * sig
        s_ref[...] = silu.astype(BF16)
        us_ref[...] = (u * (sig * (1.0 + g * (1.0 - sig)))).astype(BF16)
        a_ref[...] = (silu * u).astype(BF16)

    out = jax.ShapeDtypeStruct((s, half * fs), BF16)
    ospec = pl.BlockSpec((tm, fs), lambda j, i: (i, j))
    return pl.pallas_call(
        body, name=name, grid=(half, s // tm), out_shape=(out, out, out),
        in_specs=[pl.BlockSpec((tm, d), lambda j, i: (i, 0)),
                  pl.BlockSpec((None, d, fs), lambda j, i: (j, 0, 0)),
                  pl.BlockSpec((None, d, fs), lambda j, i: (j + half, 0, 0))],
        out_specs=(ospec, ospec, ospec),
        compiler_params=_params("parallel", "parallel"),
    )(h, w_gu, w_gu)


def _ffn_dact(name, dx, w_dn4, silu, usilu):
    s, d = dx.shape
    half, fs = w_dn4.shape[0], w_dn4.shape[1]
    tm = _tile(s, 512, 16)

    cut = (fs // (2 * LANES)) * LANES

    def body(dx_ref, w_ref, s_ref, us_ref, dgu_ref):
        dxv = dx_ref[...].astype(BF16)
        for lo, hi in ((0, cut), (cut, fs)) if cut else ((0, fs),):
            da = lax.dot_general(dxv, w_ref[lo:hi, :], NT, preferred_element_type=F32)
            dgu_ref[0, :, lo:hi] = (da * us_ref[:, lo:hi].astype(F32)).astype(BF16)
            dgu_ref[1, :, lo:hi] = (da * s_ref[:, lo:hi].astype(F32)).astype(BF16)

    blk = pl.BlockSpec((tm, fs), lambda j, i: (i, j))
    return pl.pallas_call(
        body, name=name, grid=(half, s // tm),
        out_shape=jax.ShapeDtypeStruct((2, s, half * fs), BF16),
        in_specs=[pl.BlockSpec((tm, d), lambda j, i: (i, 0)),
                  pl.BlockSpec((None, fs, d), lambda j, i: (j, 0, 0)), blk, blk],
        out_specs=pl.BlockSpec((2, tm, fs), lambda j, i: (0, i, j)),
        compiler_params=_params("parallel", "parallel"),
    )(dx, w_dn4, silu, usilu)


def _ffn_dw_gu(name, h, dgu, dep=None):
    s, d = h.shape
    half, fs = N_DEV // 2, dgu.shape[2] // (N_DEV // 2)
    tm, ts = _tile(d, 1024, LANES), _tile(s, 1024, 16)
    return _mm(name, "tn", h, dgu, jax.ShapeDtypeStruct((N_DEV, d, fs), BF16),
               grid=(d // tm, N_DEV, s // ts),
               a_spec=pl.BlockSpec((ts, tm), lambda i, j, k: (k, i)),
               b_spec=pl.BlockSpec((None, ts, fs), lambda i, j, k: (j // half, k, j % half)),
               o_spec=pl.BlockSpec((None, tm, fs), lambda i, j, k: (j, i, 0)),
               acc_shape=(tm, fs), dep=dep)


def _ffn_dh(name, dgu, w_gu, x, g, dres, dep=None):
    s = dgu.shape[1]
    d, fs = w_gu.shape[1], w_gu.shape[2]
    half = N_DEV // 2
    tm = _tile(s, 512, 16)
    return _mm_rms_bwd(name, dgu, w_gu, x, g, dres, tm=tm, nk=N_DEV,
                       a_spec=pl.BlockSpec((None, tm, fs), lambda i, k: (k // half, i, k % half)),
                       b_spec=pl.BlockSpec((None, d, fs), lambda i, k: (k, 0, 0)), dep=dep)


def _proj_in_dx(name, dproj, w_in, x, g, dres, dep=None):
    s, n = dproj.shape
    d = w_in.shape[0]
    tm, tk = _tile(s, 512, 16), _tile(n, 896, LANES)
    return _mm_rms_bwd(name, dproj, w_in, x, g, dres, tm=tm, nk=n // tk,
                       a_spec=pl.BlockSpec((tm, tk), lambda i, k: (i, k)),
                       b_spec=pl.BlockSpec((d, tk), lambda i, k: (0, k)), dep=dep)


def _pool_fwd(name, x, g, w, b, sc):
    s, d = x.shape
    dg = d // len(POOL_WINDOWS)
    tm = _tile(s, 256, POOL_HALO)
    per = tm // POOL_HALO

    def body(x_ref, xh_ref, g_ref, w_ref, b_ref, sc_ref, xo_ref, y_ref, zb_ref):
        i = pl.program_id(0)
        gv = g_ref[...]

        def norm(v):
            return (v * lax.rsqrt(jnp.mean(v * v, axis=-1, keepdims=True) + RMS_EPS)) * gv

        h = norm(x_ref[...])
        halo = norm(xh_ref[...]) * (i > 0).astype(F32)
        ext = jnp.concatenate([halo, h], axis=0)
        t = i * tm + lax.broadcasted_iota(jnp.int32, (tm, 1), 0)
        for gi, win in enumerate(POOL_WINDOWS):
            sl = slice(gi * dg, (gi + 1) * dg)
            acc = ext[:, sl]
            step = 1
            while step < win:
                acc = acc + pltpu.roll(acc, step, 0)
                step *= 2
            inv = 1.0 / jnp.minimum(t + 1, win).astype(F32)
            yg = (acc[POOL_HALO:, :] * inv - h[:, sl]).astype(BF16)
            y_ref[:, sl] = yg
            zb = lax.dot_general(yg, w_ref[gi], NN, preferred_element_type=F32) + b_ref[:, sl]
            zb_ref[:, sl] = zb
            xo_ref[:, sl] = x_ref[:, sl] + zb * sc_ref[:, sl]

    row = pl.BlockSpec((tm, d), lambda i: (i, 0))
    vec = pl.BlockSpec((1, d), lambda i: (0, 0))
    return pl.pallas_call(
        body, name=name, grid=(s // tm,),
        out_shape=(jax.ShapeDtypeStruct((s, d), F32), jax.ShapeDtypeStruct((s, d), BF16),
                   jax.ShapeDtypeStruct((s, d), F32)),
        in_specs=[row, pl.BlockSpec((POOL_HALO, d), lambda i: (jnp.maximum(i * per - 1, 0), 0)),
                  vec, pl.BlockSpec(w.shape, lambda i: (0, 0, 0)), vec, vec],
        out_specs=(row, row, row),
        compiler_params=_params("parallel"),
    )(x, x, g, w, b, sc)


def _pool_bwd(name, dout, x, zb, g, w, sc):
    s, d = x.shape
    dg = d // len(POOL_WINDOWS)
    tm = _tile(s, 256, POOL_HALO)
    per = tm // POOL_HALO
    nb = s // tm
    ext_rows = tm + POOL_HALO

    def body(do_ref, doh_ref, x_ref, zb_ref, g_ref, w_ref, sc_ref, dx_ref, dz_ref, dgn_ref, dsc_ref, db_ref):
        i = pl.program_id(0)
        scv = sc_ref[...]
        dov = do_ref[...]
        dz = dov * scv
        dz_ref[...] = dz.astype(BF16)
        halo = doh_ref[...] * scv * (i < nb - 1).astype(F32)
        ext = jnp.concatenate([dz, halo], axis=0).astype(BF16)
        t = i * tm + lax.broadcasted_iota(jnp.int32, (ext_rows, 1), 0)
        parts = []
        for gi, win in enumerate(POOL_WINDOWS):
            sl = slice(gi * dg, (gi + 1) * dg)
            dy = lax.dot_general(ext[:, sl], w_ref[gi], NT, preferred_element_type=F32)
            acc = dy * (1.0 / jnp.minimum(t + 1, win).astype(F32))
            step = 1
            while step < win:
                acc = acc + pltpu.roll(acc, ext_rows - step, 0)
                step *= 2
            parts.append(acc[:tm, :] - dy[:tm, :])
        dh = jnp.concatenate(parts, axis=1)
        xv = x_ref[...]
        r = lax.rsqrt(jnp.mean(xv * xv, axis=-1, keepdims=True) + RMS_EPS)
        xhat = xv * r
        gdh = dh * g_ref[...]
        dx_ref[...] = dov + r * (gdh - xhat * jnp.mean(gdh * xhat, axis=-1, keepdims=True))
        pgn = jnp.sum(dh * xhat, axis=0, keepdims=True)
        psc = jnp.sum(dov * zb_ref[...], axis=0, keepdims=True)
        pb = jnp.sum(dz, axis=0, keepdims=True)

        @pl.when(i == 0)
        def _():
            dgn_ref[...] = pgn
            dsc_ref[...] = psc
            db_ref[...] = pb

        @pl.when(i > 0)
        def _():
            dgn_ref[...] += pgn
            dsc_ref[...] += psc
            db_ref[...] += pb

    row = pl.BlockSpec((tm, d), lambda i: (i, 0))
    vec = pl.BlockSpec((1, d), lambda i: (0, 0))
    vshape = jax.ShapeDtypeStruct((1, d), F32)
    return pl.pallas_call(
        body, name=name, grid=(nb,),
        out_shape=(jax.ShapeDtypeStruct((s, d), F32), jax.ShapeDtypeStruct((s, d), BF16), vshape, vshape, vshape),
        in_specs=[row, pl.BlockSpec((POOL_HALO, d), lambda i: (jnp.minimum((i + 1) * per, s // POOL_HALO - 1), 0)),
                  row, row, vec, pl.BlockSpec(w.shape, lambda i: (0, 0, 0)), vec],
        out_specs=(row, row, vec, vec, vec),
        compiler_params=_params("arbitrary"),
    )(dout, dout, x, zb, g, w, sc)


def _pool_dw(name, y, dz, n_groups):
    s, d = y.shape
    dg = d // n_groups
    ts = _tile(s, 1024, 16)
    return _mm(name, "tn", y, dz, jax.ShapeDtypeStruct((n_groups, dg, dg), F32),
               grid=(n_groups, 1, s // ts),
               a_spec=pl.BlockSpec((ts, dg), lambda i, j, k: (k, i)),
               b_spec=pl.BlockSpec((ts, dg), lambda i, j, k: (k, i)),
               o_spec=pl.BlockSpec((None, dg, dg), lambda i, j, k: (i, 0, 0)),
               acc_shape=(dg, dg))


def _loss_head(name, y, tgt):
    s, d = y.shape
    tm = _tile(s, 512, 16)

    def body(y_ref, t_ref, dy_ref, l_ref):
        i = pl.program_id(0)
        e = y_ref[...] - t_ref[...]
        dy_ref[...] = e * (1.0 / d)
        part = jnp.sum(jnp.mean(e * e, axis=-1, keepdims=True), axis=0, keepdims=True)
        part = jnp.broadcast_to(part, l_ref.shape)

        @pl.when(i == 0)
        def _():
            l_ref[...] = part

        @pl.when(i > 0)
        def _():
            l_ref[...] += part

    row = pl.BlockSpec((tm, d), lambda i: (i, 0))
    return pl.pallas_call(
        body, name=name, grid=(s // tm,),
        out_shape=(jax.ShapeDtypeStruct((s, d), F32), jax.ShapeDtypeStruct((8, LANES), F32)),
        in_specs=[row, row], out_specs=(row, pl.BlockSpec((8, LANES), lambda i: (0, 0))),
        compiler_params=_params("arbitrary"),
    )(y, tgt)


def _adam_update(w_ref, m_ref, v_ref, p_ref, g_ref, d_ref, nm_ref, nv_ref):
    g = p_ref[0].astype(F32)
    for k in range(1, N_DEV):
        g = g + p_ref[k].astype(F32)
    mn = ADAM_B1 * m_ref[...] + (1.0 - ADAM_B1) * g
    vn = ADAM_B2 * v_ref[...] + (1.0 - ADAM_B2) * (g * g)
    m_hat = mn / (1.0 - ADAM_B1 ** ADAM_STEP)
    v_hat = vn / (1.0 - ADAM_B2 ** ADAM_STEP)
    g_ref[...] = g
    d_ref[...] = -ADAM_LR * (m_hat / (jnp.sqrt(v_hat) + ADAM_EPS) + ADAM_WD * w_ref[...])
    nm_ref[...] = mn
    nv_ref[...] = vn


def _adamw_layers(name, w, m, v, pieces):
    n_layers, r, c = w.shape
    tr = _tile(r, 128, 16)

    def body(w_ref, m_ref, v_ref, *rest):
        p_refs, outs = rest[:n_layers], rest[n_layers:]
        layer = pl.program_id(0)
        for l in range(n_layers):
            @pl.when(layer == l)
            def _(l=l):
                _adam_update(w_ref, m_ref, v_ref, p_refs[l], *outs)

    blk = pl.BlockSpec((None, tr, c), lambda l, i: (l, i, 0))
    terms = [pl.BlockSpec((N_DEV, tr, c), lambda l, i, n=n: (0, jnp.where(l == n, i, 0), 0))
             for n in range(n_layers)]
    out = jax.ShapeDtypeStruct(w.shape, F32)
    return list(pl.pallas_call(
        body, name=name, grid=(n_layers, r // tr), out_shape=(out, out, out, out),
        in_specs=[blk, blk, blk] + terms, out_specs=(blk, blk, blk, blk),
        compiler_params=_params("parallel", "parallel"),
    )(w, m, v, *pieces))


def _adamw(name, w, m, v, pieces):
    r, c = w.shape
    tr = _tile(r, 128, 16)

    def body(w_ref, m_ref, v_ref, p_ref, g_ref, d_ref, nm_ref, nv_ref):
        _adam_update(w_ref, m_ref, v_ref, p_ref, g_ref, d_ref, nm_ref, nv_ref)

    blk = pl.BlockSpec((tr, c), lambda i: (i, 0))
    out = jax.ShapeDtypeStruct((r, c), F32)
    return pl.pallas_call(
        body, name=name, grid=(r // tr,), out_shape=(out, out, out, out),
        in_specs=[blk, blk, blk, pl.BlockSpec((N_DEV, tr, c), lambda i: (0, i, 0))],
        out_specs=(blk, blk, blk, blk),
        compiler_params=_params("parallel"),
    )(w, m, v, pieces)


def _pack_small(mix, ffn, b_f, gq, gk):
    def rows(a):
        a = a.reshape(-1, LANES) if a.shape[-1] >= LANES else jnp.pad(a, ((0, 0), (0, LANES - a.shape[-1])))
        return jnp.pad(a, ((0, -a.shape[0] % 8), (0, 0)))
    return jnp.concatenate([rows(mix), rows(ffn), rows(b_f), rows(gq), rows(gk)], axis=0)


def _unpack_small(p, mix, ffn, b_f, gq, gk):
    out, pos = [], 0
    for a in (mix, ffn, b_f, gq, gk):
        n = a.size // LANES if a.shape[-1] >= LANES else a.shape[0]
        blk = p[pos:pos + n]
        out.append(blk.reshape(a.shape) if a.shape[-1] >= LANES else blk[:, :a.shape[-1]])
        pos += n + (-n % 8)
    return out


def kernel(x, mix_norm_g, ffn_norm_g, fox_w_in, fox_b_f, fox_q_norm_g, fox_k_norm_g, fox_w_out, pool_w, pool_b, pool_scale, ffn_w_gate_up, ffn_w_down, loss_target, m_mix_norm_g, m_ffn_norm_g, m_fox_w_in, m_fox_b_f, m_fox_q_norm_g, m_fox_k_norm_g, m_fox_w_out, m_pool_w, m_pool_b, m_pool_scale, m_ffn_w_gate_up, m_ffn_w_down, v_mix_norm_g, v_ffn_norm_g, v_fox_w_in, v_fox_b_f, v_fox_q_norm_g, v_fox_k_norm_g, v_fox_w_out, v_pool_w, v_pool_b, v_pool_scale, v_ffn_w_gate_up, v_ffn_w_down):
    xs, tgt = x[0], loss_target[0]
    s, d = xs.shape
    depth = mix_norm_g.shape[0]
    n_fox, n_pool = fox_w_in.shape[0], pool_w.shape[0]
    n_heads = d // HEAD_DIM
    n_in = fox_w_in.shape[2] * N_DEV
    n_pad = 3 * d + LANES
    n_groups = pool_w.shape[1]
    dsh = d // N_DEV
    half = N_DEV // 2
    axes = ("x", "y", "c")

    w_in_bf, w_out_bf, pool_w_bf = fox_w_in.astype(BF16), fox_w_out.astype(BF16), pool_w.astype(BF16)
    gu_bf, dn_bf = ffn_w_gate_up.astype(BF16), ffn_w_down.astype(BF16)
    pool_bs = jnp.stack([pool_b, pool_scale], axis=1)
    mix_gather, ffn_gather = [None] * depth, [None] * depth
    last = None
    for l in range(depth):
        j = l // 2
        shards = [w_in_bf[j:j + 1], w_out_bf[j:j + 1]] if l % 2 == 0 else [pool_w_bf[j:j + 1], pool_bs[j:j + 1]]
        mix_gather[l] = _exchange_start(f"gather_mixer{l}", *_gather_plan(shards), dep=last)
        ffn_gather[l] = _exchange_start(f"gather_ffn{l}", *_gather_plan([gu_bf[l:l + 1], dn_bf[l:l + 1]]),
                                        dep=mix_gather[l]["token"])
        last = ffn_gather[l]["token"]
    started = last[:1, :1]
    w_gu_g, w_dn = [None] * depth, [None] * depth
    w_in, w_out = [None] * n_fox, [None] * n_fox
    w_pool, pool_b_full, pool_s_full = [None] * n_pool, [None] * n_pool, [None] * n_pool
    b_pad =[jnp.pad(fox_b_f[j], (0, LANES - n_heads))[None] for j in range(n_fox)]

    saved = []
    cur = xs
    for i in range(depth):
        j = i // 2
        gm = mix_norm_g[i][None]
        if i == 0:
            gm = gm + started
        if i % 2 == 0:
            w_in_g, w_out_g = _exchange_wait(mix_gather[i], last if i == 0 else cur)
            w_in[j] = jnp.pad(jnp.transpose(w_in_g, (1, 0, 2)).reshape(d, n_in), ((0, 0), (0, n_pad - n_in)))
            w_out[j] = w_out_g.reshape(d, d)
            h = _rms_fwd(f"norm_mix{i}", cur, gm)
            proj =_mm_nn(f"proj_in{i}", h, w_in[j], F32, tn=896)
            gq, gk = fox_q_norm_g[j][None], fox_k_norm_g[j][None]
            qn, kn, vb = _qkv_fwd(f"qk_norm{i}", proj, gq, gk, d)
            flog = proj[:, 3 * d:]
            c = _gate_fwd(f"gate{i}", flog, b_pad[j])
            c_t = c[:, :n_heads].T
            c_col, c_row = c_t[:, :, None], c_t[:, None, :]
            o, lse = _attn_fwd(f"attn{i}", qn, kn, vb.T, c_row, c_col)
            mid = _mm_nn(f"proj_out{i}", o, w_out[j], F32, add=cur)
            mix_saved = (cur, h, proj, flog, qn, kn, vb, c_col, c_row, o, lse)
        else:
            pw_g, pbs_g = _exchange_wait(mix_gather[i], cur)
            w_pool[j] = jnp.transpose(pw_g, (1, 0, 2, 3)).reshape(n_groups, d // n_groups, d // n_groups)
            pbs_full = jnp.transpose(pbs_g, (1, 0, 2)).reshape(2, 1, d)
            pool_b_full[j], pool_s_full[j] = pbs_full[0], pbs_full[1]
            mid, y, zb = _pool_fwd(f"pool{i}", cur, gm, w_pool[j], pool_b_full[j], pool_s_full[j])
            mix_saved = (cur, y, zb)
        h2 = _rms_fwd(f"norm_ffn{i}", mid, ffn_norm_g[i][None])
        w_gu_g[i], dn_g = _exchange_wait(ffn_gather[i], h2)
        w_dn[i] = dn_g.reshape(-1, d)
        silu, usilu, act = _ffn_up(f"ffn_up{i}", h2, w_gu_g[i])
        nxt = _mm_nn(f"ffn_down{i}", act, w_dn[i], F32, add=mid, tk=1408)
        saved.append((mix_saved, mid, h2, silu, usilu, act))
        cur = nxt

    dcur, lpart = _loss_head("loss_head", cur, tgt)
    loss = lax.psum(0.5 * lpart[0, 0], axes)

    d_mix, d_ffn = [None] * depth, [None] * depth
    d_bf, d_gq, d_gk = [None] * n_fox, [None] * n_fox, [None] * n_fox
    mix_scatter, ffn_scatter = [None] * depth, [None] * depth
    pending = jnp.zeros((1, 1), F32)
    for i in reversed(range(depth)):
        j = i // 2
        mix_saved, mid, h2, silu, usilu, act = saved[i]
        dgu = _ffn_dact(f"ffn_dact{i}", dcur, w_dn[i].reshape(half, -1, d), silu, usilu)
        g_dn = _mm_tn(f"ffn_dw_down{i}", act, dcur, BF16, tm=1408).reshape(N_DEV, -1, d)
        sc_dn = _exchange_start(f"scatter_down{i}", *_scatter_plan([g_dn]))
        g_gu = _ffn_dw_gu(f"ffn_dw_up{i}", h2, dgu, dep=sc_dn["token"])
        sc_gu = _exchange_start(f"scatter_up{i}", *_scatter_plan([g_gu]))
        ffn_scatter[i] = (sc_gu, sc_dn)
        g_ffn = ffn_norm_g[i][None] + pending
        dmid, d_ffn[i] = _ffn_dh(f"ffn_dh{i}", dgu, w_gu_g[i], mid, g_ffn, dcur, dep=sc_gu["token"])
        gm = mix_norm_g[i][None]
        if i % 2 == 0:
            xin, h, proj, flog, qn, kn, vb, c_col, c_row, o, lse = mix_saved
            g_out = _mm_tn(f"proj_out_dw{i}", o, dmid, BF16).reshape(N_DEV, dsh, d)
            sc_out = _exchange_start(f"scatter_out{i}", *_scatter_plan([g_out]))
            do = _mm_nt(f"proj_out_dx{i}", dmid, w_out[j], BF16, dep=sc_out["token"])
            delta = _attn_delta(f"attn_delta{i}", o, do, n_heads)
            delta_row = delta[:, :n_heads].T[:, None, :]
            dqn, dkn, dv, dck, dcq = _attn_bwd(f"attn_bwd{i}", qn, kn, vb, do, c_row,
                                               lse, delta_row, c_col)
            lane_pad = ((0, 0), (0, LANES - n_heads))
            dflog, d_bf[j] = _gate_bwd(f"gate_bwd{i}", jnp.pad(dck[:, :, 0].T, lane_pad),
                                       jnp.pad(dcq[:, 0, :].T, lane_pad), flog, b_pad[j], n_heads)
            gq, gk = fox_q_norm_g[j][None], fox_k_norm_g[j][None]
            dproj, d_gq[j], d_gk[j] = _qkv_bwd(f"qk_norm_bwd{i}", proj, dqn, dkn, dv, dflog, gq, gk, d, n_pad)
            dw_in = _mm_tn(f"proj_in_dw{i}", h, dproj, BF16, tn=896)
            g_in = jnp.transpose(dw_in[:, :n_in].reshape(d, N_DEV, n_in // N_DEV), (1, 0, 2))
            sc_in = _exchange_start(f"scatter_in{i}", *_scatter_plan([g_in]))
            mix_scatter[i] = (sc_in, sc_out)
            dcur, d_mix[i] = _proj_in_dx(f"proj_in_dx{i}", dproj, w_in[j], xin, gm, dmid, dep=sc_in["token"])
        else:
            xin, y, zb = mix_saved
            dcur, dz, d_mix[i], dsc, db = _pool_bwd(f"pool_bwd{i}", dmid, xin, zb, gm, w_pool[j], pool_s_full[j])
            dwp = _pool_dw(f"pool_dw{i}", y, dz, n_groups)
            dg = d // n_groups
            g_pw = jnp.transpose(dwp.reshape(n_groups, N_DEV, dg // N_DEV, dg), (1, 0, 2, 3)).astype(BF16)
            g_pbs = jnp.stack([db.reshape(N_DEV, dsh), dsc.reshape(N_DEV, dsh)], axis=1)
            sc_pool = _exchange_start(f"scatter_pool{i}", *_scatter_plan([g_pw, g_pbs]))
            mix_scatter[i] = (sc_pool,)
            pending = sc_pool["token"][:1, :1]
    grad_x = dcur[None]

    mix_landed = [sum((_exchange_wait(hd, dcur) for hd in mix_scatter[l]), []) for l in range(depth)]
    landed = [sum((_exchange_wait(hd, dcur) for hd in ffn_scatter[l]), []) for l in range(depth)]
    r_in, r_out = [t[0] for t in mix_landed[0::2]], [t[1] for t in mix_landed[0::2]]
    r_pw = [t[0].reshape(N_DEV, -1, t[0].shape[-1]) for t in mix_landed[1::2]]
    r_pbs = [t[1] for t in mix_landed[1::2]]
    r_gu, r_dn = [t[0] for t in landed], [t[1] for t in landed]
    upd = {}
    upd["fox_w_in"] = _adamw_layers("adamw_w_in", fox_w_in, m_fox_w_in, v_fox_w_in, r_in)
    upd["fox_w_out"] = _adamw_layers("adamw_w_out", fox_w_out, m_fox_w_out, v_fox_w_out, r_out)
    fold = lambda a: a.reshape(n_pool, -1, a.shape[-1])
    upd["pool_w"] = [o.reshape(pool_w.shape) for o in
                     _adamw_layers("adamw_pool_w", fold(pool_w), fold(m_pool_w), fold(v_pool_w), r_pw)]
    pbs = _adamw_layers("adamw_pool_bs", pool_bs, jnp.stack([m_pool_b, m_pool_scale], axis=1),
                        jnp.stack([v_pool_b, v_pool_scale], axis=1), r_pbs)
    upd["pool_b"] = [o[:, 0] for o in pbs]
    upd["pool_scale"] = [o[:, 1] for o in pbs]
    upd["ffn_w_gate_up"] = _adamw_layers("adamw_gate_up", ffn_w_gate_up, m_ffn_w_gate_up, v_ffn_w_gate_up, r_gu)
    upd["ffn_w_down"] = _adamw_layers("adamw_down", ffn_w_down, m_ffn_w_down, v_ffn_w_down, r_dn)

    small_w = (mix_norm_g, ffn_norm_g, fox_b_f, fox_q_norm_g, fox_k_norm_g)
    small_g = _pack_small(jnp.concatenate(d_mix), jnp.concatenate(d_ffn),
                          jnp.concatenate(d_bf)[:, :n_heads], jnp.concatenate(d_gq), jnp.concatenate(d_gk))
    (small_pieces,), = _all_gather_layers("gather_small", [small_g[None]])
    small = _adamw("adamw_small", _pack_small(*small_w),
                   _pack_small(m_mix_norm_g, m_ffn_norm_g, m_fox_b_f, m_fox_q_norm_g, m_fox_k_norm_g),
                   _pack_small(v_mix_norm_g, v_ffn_norm_g, v_fox_b_f, v_fox_q_norm_g, v_fox_k_norm_g),
                   small_pieces)
    small = [_unpack_small(o, *small_w) for o in small]
    for n, name in enumerate(("mix_norm_g", "ffn_norm_g", "fox_b_f", "fox_q_norm_g", "fox_k_norm_g")):
        upd[name] = [o[n] for o in small]

    order = ("mix_norm_g", "ffn_norm_g", "fox_w_in", "fox_b_f", "fox_q_norm_g", "fox_k_norm_g", "fox_w_out",
             "pool_w", "pool_b", "pool_scale", "ffn_w_gate_up", "ffn_w_down")
    return (loss, grad_x) + tuple(upd[name][q] for q in range(4) for name in order)
```

```python
import functools

import jax
import jax.numpy as jnp
from jax import lax
from jax.experimental import pallas as pl
from jax.experimental.pallas import tpu as pltpu

F32 = jnp.float32
BF16 = jnp.bfloat16
MESH = pl.DeviceIdType.MESH

N_DEV = 8
HEAD_DIM = 128
LANES = 128
POOL_WINDOWS = (2, 4, 8, 16)
POOL_HALO = 16
RMS_EPS = 1e-6
NEG_INF = -1e30
ADAM_LR = 0.001
ADAM_B1 = 0.9
ADAM_B2 = 0.999
ADAM_EPS = 1e-08
ADAM_WD = 0.01
ADAM_STEP = 10
VMEM_LIMIT = 52 * 1024 * 1024

NN = (((1,), (0,)), ((), ()))
NT = (((1,), (1,)), ((), ()))
TN = (((0,), (0,)), ((), ()))


def _tile(n, pref, align):
    best = None
    d = align
    while d <= min(n, pref):
        if n % d == 0:
            best = d
        d += align
    return n if best is None else best


def _params(*sem):
    return pltpu.CompilerParams(dimension_semantics=sem, vmem_limit_bytes=VMEM_LIMIT)


def _position():
    x, y, c = lax.axis_index("x"), lax.axis_index("y"), lax.axis_index("c")
    return x, y, c, 4 * x + 2 * y + c


def _peer(x, y, c, k):
    px = 1 - x if k & 4 else x
    py = 1 - y if k & 2 else y
    pc = 1 - c if k & 1 else c
    return (px, py, pc), 4 * px + 2 * py + pc


def _exchange(name, ins, out_shapes, copies):
    n_in, n_cp = len(ins), len(copies)

    def body(*refs):
        in_refs = refs[:n_in]
        out_refs = refs[n_in:n_in + len(out_shapes)]
        send_sems, recv_sems, loc_sems = refs[n_in + len(out_shapes):]
        x, y, c, me = _position()
        local = []
        for ci, (ii, src_of, oi, dst_of) in enumerate(copies):
            cp = pltpu.make_async_copy(src_of(in_refs[ii], me), dst_of(out_refs[oi], me), loc_sems.at[ci])
            cp.start()
            local.append(cp)
        sends, recvs = [], []
        for k in range(1, N_DEV):
            pid, p = _peer(x, y, c, k)
            for ci, (ii, src_of, oi, dst_of) in enumerate(copies):
                sem = ci * (N_DEV - 1) + k - 1
                send = pltpu.make_async_remote_copy(
                    src_ref=src_of(in_refs[ii], p), dst_ref=dst_of(out_refs[oi], me),
                    send_sem=send_sems.at[sem], recv_sem=recv_sems.at[sem],
                    device_id=pid, device_id_type=MESH)
                send.start()
                sends.append(send)
                recvs.append(pltpu.make_async_remote_copy(
                    src_ref=src_of(in_refs[ii], p), dst_ref=dst_of(out_refs[oi], p),
                    send_sem=send_sems.at[sem], recv_sem=recv_sems.at[sem],
                    device_id=pid, device_id_type=MESH))
        for r in recvs:
            r.wait_recv()
        for s in sends:
            s.wait_send()
        for cp in local:
            cp.wait()

    any_spec = pl.BlockSpec(memory_space=pl.ANY)
    return pl.pallas_call(
        body, name=name,
        out_shape=tuple(out_shapes),
        in_specs=[any_spec] * n_in,
        out_specs=tuple([any_spec] * len(out_shapes)),
        scratch_shapes=[pltpu.SemaphoreType.DMA((n_cp * (N_DEV - 1),)),
                        pltpu.SemaphoreType.DMA((n_cp * (N_DEV - 1),)),
                        pltpu.SemaphoreType.DMA((n_cp,))],
    )(*ins)


def _exchange_start(name, ins, out_shapes, copies, dep=None):
    n_in, n_out, n_cp = len(ins), len(out_shapes), len(copies)

    def body(*refs):
        in_refs = refs[:n_in]
        land_refs = refs[n_in:n_in + n_out]
        outs = refs[n_in + n_out + (dep is not None):]
        send_sems, recv_sems = outs[:2]
        token_ref, loc_sems = outs[n_in + n_out + 2], outs[n_in + n_out + 3]
        x, y, c, me = _position()
        local = []
        for ci, (ii, src_of, oi, dst_of) in enumerate(copies):
            cp = pltpu.make_async_copy(src_of(in_refs[ii], me), dst_of(land_refs[oi], me), loc_sems.at[ci])
            cp.start()
            local.append(cp)
        for cp in local:
            cp.wait()
        for k in range(1, N_DEV):
            pid, p = _peer(x, y, c, k)
            for ci, (ii, src_of, oi, dst_of) in enumerate(copies):
                sem = ci * (N_DEV - 1) + k - 1
                pltpu.make_async_remote_copy(
                    src_ref=src_of(in_refs[ii], p), dst_ref=dst_of(land_refs[oi], me),
                    send_sem=send_sems.at[sem], recv_sem=recv_sems.at[sem],
                    device_id=pid, device_id_type=MESH).start()
        token_ref[...] = jnp.zeros_like(token_ref)

    hbm = pl.BlockSpec(memory_space=pltpu.HBM)
    sem = pl.BlockSpec(memory_space=pltpu.SEMAPHORE)
    n_sem = n_cp * (N_DEV - 1)
    lands = [pltpu.with_memory_space_constraint(lax.empty(o.shape, o.dtype), pltpu.HBM) for o in out_shapes]
    srcs = [pltpu.with_memory_space_constraint(a, pltpu.HBM) for a in ins]
    res = pl.pallas_call(
        body, name=name,
        out_shape=(pltpu.SemaphoreType.DMA((n_sem,)), pltpu.SemaphoreType.DMA((n_sem,)),
                   *[pltpu.HBM(a.shape, a.dtype) for a in ins],
                   *[pltpu.HBM(o.shape, o.dtype) for o in out_shapes],
                   jax.ShapeDtypeStruct((8, LANES), F32)),
        in_specs=[hbm] * (n_in + n_out) + ([pl.BlockSpec(memory_space=pl.ANY)] if dep is not None else []),
        out_specs=(sem, sem, *([hbm] * (n_in + n_out)), pl.BlockSpec(memory_space=pltpu.VMEM)),
        input_output_aliases={i: 2 + i for i in range(n_in + n_out)},
        scratch_shapes=[pltpu.SemaphoreType.DMA((n_cp,))],
        compiler_params=pltpu.CompilerParams(has_side_effects=pltpu.SideEffectType.DATAFLOW_SIDE_EFFECTING),
    )(*srcs, *lands, *([dep] if dep is not None else []))
    return dict(name=name, copies=copies, send=res[0], recv=res[1], srcs=list(res[2:2 + n_in]),
                lands=list(res[2 + n_in:2 + n_in + n_out]), token=res[-1])


def _exchange_wait(handle, after):
    copies, srcs, lands = handle["copies"], handle["srcs"], handle["lands"]
    n_in, n_out = len(srcs), len(lands)

    def body(*refs):
        in_refs = refs[:n_in]
        land_refs = refs[n_in:n_in + n_out]
        send_sems, recv_sems = refs[n_in + n_out:n_in + n_out + 2]
        x, y, c, me = _position()
        waits = []
        for k in range(1, N_DEV):
            pid, p = _peer(x, y, c, k)
            for ci, (ii, src_of, oi, dst_of) in enumerate(copies):
                sem = ci * (N_DEV - 1) + k - 1
                waits.append(pltpu.make_async_remote_copy(
                    src_ref=src_of(in_refs[ii], p), dst_ref=dst_of(land_refs[oi], p),
                    send_sem=send_sems.at[sem], recv_sem=recv_sems.at[sem],
                    device_id=pid, device_id_type=MESH))
        for w in waits:
            w.wait_send()
        for w in waits:
            w.wait_recv()

    hbm = pl.BlockSpec(memory_space=pltpu.HBM)
    sem = pl.BlockSpec(memory_space=pltpu.SEMAPHORE)
    res = pl.pallas_call(
        body, name=handle["name"] + "_wait",
        out_shape=tuple(pltpu.HBM(a.shape, a.dtype) for a in srcs + lands),
        in_specs=[hbm] * (n_in + n_out) + [sem, sem, pl.BlockSpec(memory_space=pl.ANY)],
        out_specs=tuple([hbm] * (n_in + n_out)),
        input_output_aliases={i: i for i in range(n_in + n_out)},
        compiler_params=pltpu.CompilerParams(has_side_effects=pltpu.SideEffectType.DATAFLOW_SIDE_EFFECTING),
    )(*srcs, *lands, handle["send"], handle["recv"], after)
    return list(res[n_in:])


def _gather_plan(stacked):
    ins, outs, copies = [], [], []
    for t in stacked:
        ii = len(ins)
        ins.append(t)
        for l in range(t.shape[0]):
            oi = len(outs)
            outs.append(jax.ShapeDtypeStruct((N_DEV,) + t.shape[1:], t.dtype))
            copies.append((ii, (lambda ref, p, l=l: ref.at[l]), oi, (lambda ref, s: ref.at[s])))
    return ins, outs, copies


def _scatter_plan(blocked):
    outs = [jax.ShapeDtypeStruct(t.shape, t.dtype) for t in blocked]
    copies = [(n, (lambda ref, p: ref.at[p]), n, (lambda ref, s: ref.at[s])) for n in range(len(blocked))]
    return list(blocked), outs, copies


def _all_gather_layers(name, stacked):
    ins, outs, copies = [], [], []
    for t in stacked:
        ii = len(ins)
        ins.append(t)
        for l in range(t.shape[0]):
            oi = len(outs)
            outs.append(jax.ShapeDtypeStruct((N_DEV,) + t.shape[1:], t.dtype))
            copies.append((ii, (lambda ref, p, l=l: ref.at[l]), oi, (lambda ref, s: ref.at[s])))
    res = _exchange(name, ins, outs, copies)
    out, pos = [], 0
    for t in stacked:
        out.append(list(res[pos:pos + t.shape[0]]))
        pos += t.shape[0]
    return out


def _mm(name, mode, a, b, out_shape, *, grid, a_spec, b_spec, o_spec, acc_shape, add=None, add_spec=None, dep=None):
    nk = grid[2]
    dn = {"nn": NN, "nt": NT, "tn": TN}[mode]
    has_add, has_dep = add is not None, dep is not None
    own_acc = nk > 1 and out_shape.dtype != F32

    def body(*refs):
        a_ref, b_ref = refs[:2]
        add_ref = refs[2] if has_add else None
        o_ref = refs[2 + has_add + has_dep]

        def product():
            return lax.dot_general(a_ref[...].astype(BF16), b_ref[...].astype(BF16), dn,
                                   preferred_element_type=F32)

        if nk == 1:
            r = product() + add_ref[...] if has_add else product()
            o_ref[...] = r.astype(o_ref.dtype)
        else:
            acc_ref = refs[-1] if own_acc else o_ref
            k = pl.program_id(2)

            @pl.when(k == 0)
            def _():
                acc_ref[...] = add_ref[...] if has_add else jnp.zeros_like(acc_ref)

            acc_ref[...] += product()
            if own_acc:
                @pl.when(k == nk - 1)
                def _():
                    o_ref[...] = acc_ref[...].astype(o_ref.dtype)

    ins = [a, b] + ([add] if has_add else []) + ([dep] if has_dep else [])
    in_specs = ([a_spec, b_spec] + ([add_spec] if has_add else [])
                + ([pl.BlockSpec(memory_space=pl.ANY)] if has_dep else []))
    scratch = [pltpu.VMEM(acc_shape, F32)] if own_acc else []
    return pl.pallas_call(
        body, name=name, grid=grid, out_shape=out_shape,
        in_specs=in_specs, out_specs=o_spec, scratch_shapes=scratch,
        compiler_params=_params("parallel", "parallel", "arbitrary"),
    )(*ins)


def _mm_nn(name, a, b, out_dtype, add=None, tm=1024, tn=1024, tk=2048):
    m, kd = a.shape
    n = b.shape[1]
    tm, tn, tk = _tile(m, tm, 16), _tile(n, tn, LANES), _tile(kd, tk, LANES)
    return _mm(name, "nn", a, b, jax.ShapeDtypeStruct((m, n), out_dtype),
               grid=(m // tm, n // tn, kd // tk),
               a_spec=pl.BlockSpec((tm, tk), lambda i, j, k: (i, k)),
               b_spec=pl.BlockSpec((tk, tn), lambda i, j, k: (k, j)),
               o_spec=pl.BlockSpec((tm, tn), lambda i, j, k: (i, j)),
               acc_shape=(tm, tn), add=add,
               add_spec=pl.BlockSpec((tm, tn), lambda i, j, k: (i, j)))


def _mm_nt(name, a, b, out_dtype, tm=1024, tn=1024, tk=2048, dep=None):
    m, kd = a.shape
    n = b.shape[0]
    tm, tn, tk = _tile(m, tm, 16), _tile(n, tn, LANES), _tile(kd, tk, LANES)
    return _mm(name, "nt", a, b, jax.ShapeDtypeStruct((m, n), out_dtype),
               grid=(m // tm, n // tn, kd // tk),
               a_spec=pl.BlockSpec((tm, tk), lambda i, j, k: (i, k)),
               b_spec=pl.BlockSpec((tn, tk), lambda i, j, k: (j, k)),
               o_spec=pl.BlockSpec((tm, tn), lambda i, j, k: (i, j)),
               acc_shape=(tm, tn), dep=dep)


def _mm_tn(name, a, b, out_dtype, tm=1024, tn=1024, ts=2048, dep=None):
    s, m = a.shape
    n = b.shape[1]
    tm, tn, ts = _tile(m, tm, LANES), _tile(n, tn, LANES), _tile(s, ts, 16)
    return _mm(name, "tn", a, b, jax.ShapeDtypeStruct((m, n), out_dtype),
               grid=(m // tm, n // tn, s // ts),
               a_spec=pl.BlockSpec((ts, tm), lambda i, j, k: (k, i)),
               b_spec=pl.BlockSpec((ts, tn), lambda i, j, k: (k, j)),
               o_spec=pl.BlockSpec((tm, tn), lambda i, j, k: (i, j)),
               acc_shape=(tm, tn), dep=dep)


def _rms_fwd(name, x, g):
    s, d = x.shape
    tm = _tile(s, 512, 16)

    def body(x_ref, g_ref, h_ref):
        xv = x_ref[...]
        r = lax.rsqrt(jnp.mean(xv * xv, axis=-1, keepdims=True) + RMS_EPS)
        h_ref[...] = ((xv * r) * g_ref[...]).astype(BF16)

    return pl.pallas_call(
        body, name=name, grid=(s // tm,), out_shape=jax.ShapeDtypeStruct((s, d), BF16),
        in_specs=[pl.BlockSpec((tm, d), lambda i: (i, 0)), pl.BlockSpec((1, d), lambda i: (0, 0))],
        out_specs=pl.BlockSpec((tm, d), lambda i: (i, 0)),
        compiler_params=_params("parallel"),
    )(x, g)


def _mm_rms_bwd(name, a, b, x, g, dres, *, tm, nk, a_spec, b_spec, dep=None):
    s, d = x.shape
    has_dep = dep is not None
    ch = _tile(tm, 128, 8)

    def body(*refs):
        a_ref, b_ref, x_ref, g_ref, dres_ref = refs[:5]
        dx_ref, dg_ref = refs[5 + has_dep], refs[6 + has_dep]
        i, k = pl.program_id(0), pl.program_id(1)

        @pl.when(k == 0)
        def _():
            dx_ref[...] = jnp.zeros_like(dx_ref)

        dx_ref[...] += lax.dot_general(a_ref[...].astype(BF16), b_ref[...].astype(BF16), NT,
                                       preferred_element_type=F32)

        @pl.when(k == nk - 1)
        def _():
            def rows_bwd(c, part):
                rows = pl.ds(pl.multiple_of(c * ch, ch), ch)
                dhv = dx_ref[rows, :]
                xv = x_ref[rows, :]
                r = lax.rsqrt(jnp.mean(xv * xv, axis=-1, keepdims=True) + RMS_EPS)
                xhat = xv * r
                gdh = dhv * g_ref[...]
                dx_ref[rows, :] = dres_ref[rows, :] + r * (gdh - xhat * jnp.mean(gdh * xhat, axis=-1, keepdims=True))
                return part + jnp.sum(dhv * xhat, axis=0, keepdims=True)

            part = lax.fori_loop(0, tm // ch, rows_bwd, jnp.zeros((1, d), F32))

            @pl.when(i == 0)
            def _():
                dg_ref[...] = part

            @pl.when(i > 0)
            def _():
                dg_ref[...] += part

    row = pl.BlockSpec((tm, d), lambda i, k: (i, 0))
    vec = pl.BlockSpec((1, d), lambda i, k: (0, 0))
    return pl.pallas_call(
        body, name=name, grid=(s // tm, nk),
        out_shape=(jax.ShapeDtypeStruct((s, d), F32), jax.ShapeDtypeStruct((1, d), F32)),
        in_specs=[a_spec, b_spec, row, vec, row] + ([pl.BlockSpec(memory_space=pl.ANY)] if has_dep else []),
        out_specs=(row, vec),
        compiler_params=_params("arbitrary", "arbitrary"),
    )(a, b, x, g, dres, *([dep] if has_dep else []))


def _split3(v):
    hi = v.astype(BF16)
    r1 = v - hi.astype(F32)
    mid = r1.astype(BF16)
    lo = (r1 - mid.astype(F32)).astype(BF16)
    return hi, mid, lo


def _tri_sum(tri, v):
    hi, mid, lo = _split3(v)
    dot = functools.partial(lax.dot_general, dimension_numbers=NN, preferred_element_type=F32)
    return dot(tri, hi) + dot(tri, mid) + dot(tri, lo)


def _gate_fwd(name, flog, b_pad):
    s = flog.shape[0]
    tb = _tile(s, 256, 16)

    def body(f_ref, b_ref, c_ref, carry_ref):
        i = pl.program_id(0)

        @pl.when(i == 0)
        def _():
            carry_ref[...] = jnp.zeros_like(carry_ref)

        z = f_ref[...] + b_ref[...]
        lf = jnp.minimum(z, 0.0) - jnp.log(1.0 + jnp.exp(-jnp.abs(z)))
        rows = lax.broadcasted_iota(jnp.int32, (tb, tb), 0)
        cols = lax.broadcasted_iota(jnp.int32, (tb, tb), 1)
        tri = (rows >= cols).astype(BF16)
        c_ref[...] = _tri_sum(tri, lf) + carry_ref[...]
        carry_ref[...] = c_ref[pl.ds(tb - 1, 1), :]

    return pl.pallas_call(
        body, name=name, grid=(s // tb,), out_shape=jax.ShapeDtypeStruct((s, LANES), F32),
        in_specs=[pl.BlockSpec((tb, LANES), lambda i: (i, 0)), pl.BlockSpec((1, LANES), lambda i: (0, 0))],
        out_specs=pl.BlockSpec((tb, LANES), lambda i: (i, 0)),
        scratch_shapes=[pltpu.VMEM((1, LANES), F32)],
        compiler_params=_params("arbitrary"),
    )(flog, b_pad)


def _gate_bwd(name, dck, dcq, flog, b_pad, n_heads):
    s = flog.shape[0]
    tb = _tile(s, 256, 16)
    nb = s // tb

    def body(dck_ref, dcq_ref, f_ref, b_ref, df_ref, db_ref, carry_ref, tmp_ref):
        i = pl.program_id(0)

        @pl.when(i == 0)
        def _():
            carry_ref[...] = jnp.zeros_like(carry_ref)

        rows = lax.broadcasted_iota(jnp.int32, (tb, tb), 0)
        cols = lax.broadcasted_iota(jnp.int32, (tb, tb), 1)
        tri = (rows <= cols).astype(BF16)
        tmp_ref[...] = _tri_sum(tri, dck_ref[...] + dcq_ref[...]) + carry_ref[...]
        carry_ref[...] = tmp_ref[pl.ds(0, 1), :]
        z = f_ref[...] + b_ref[...]
        lane = lax.broadcasted_iota(jnp.int32, (tb, LANES), 1)
        df = jnp.where(lane < n_heads, tmp_ref[...] / (1.0 + jnp.exp(z)), 0.0)
        df_ref[...] = df.astype(BF16)
        part = jnp.sum(df, axis=0, keepdims=True)

        @pl.when(i == 0)
        def _():
            db_ref[...] = part

        @pl.when(i > 0)
        def _():
            db_ref[...] += part

    rev = pl.BlockSpec((tb, LANES), lambda i: (nb - 1 - i, 0))
    vec = pl.BlockSpec((1, LANES), lambda i: (0, 0))
    return pl.pallas_call(
        body, name=name, grid=(nb,),
        out_shape=(jax.ShapeDtypeStruct((s, LANES), BF16), jax.ShapeDtypeStruct((1, LANES), F32)),
        in_specs=[rev, rev, rev, vec], out_specs=(rev, vec),
        scratch_shapes=[pltpu.VMEM((1, LANES), F32), pltpu.VMEM((tb, LANES), F32)],
        compiler_params=_params("arbitrary"),
    )(dck, dcq, flog, b_pad)


def _head_rms(v, g):
    r = lax.rsqrt(jnp.mean(v * v, axis=-1, keepdims=True) + RMS_EPS)
    return (v * r) * g


def _qkv_fwd(name, proj, gq, gk, d):
    s = proj.shape[0]
    tm = _tile(s, 256, 16)
    n_heads = d // HEAD_DIM

    def body(q_ref, k_ref, v_ref, gq_ref, gk_ref, qn_ref, kn_ref, vb_ref):
        for h in range(n_heads):
            sl = slice(h * HEAD_DIM, (h + 1) * HEAD_DIM)
            qn_ref[:, sl] = _head_rms(q_ref[:, sl], gq_ref[...]).astype(BF16)
            kn_ref[:, sl] = _head_rms(k_ref[:, sl], gk_ref[...]).astype(BF16)
        vb_ref[...] = v_ref[...].astype(BF16)

    col = lambda c: pl.BlockSpec((tm, d), lambda i, c=c: (i, c))
    vec = pl.BlockSpec((1, HEAD_DIM), lambda i: (0, 0))
    out = jax.ShapeDtypeStruct((s, d), BF16)
    return pl.pallas_call(
        body, name=name, grid=(s // tm,), out_shape=(out, out, out),
        in_specs=[col(0), col(1), col(2), vec, vec], out_specs=(col(0), col(0), col(0)),
        compiler_params=_params("parallel"),
    )(proj, proj, proj, gq, gk)


def _qkv_bwd(name, proj, dqn, dkn, dv, dflog, gq, gk, d, n_pad):
    s = proj.shape[0]
    tm = _tile(s, 256, 16)
    n_heads = d // HEAD_DIM

    def head_bwd(raw, dy, g):
        r = lax.rsqrt(jnp.mean(raw * raw, axis=-1, keepdims=True) + RMS_EPS)
        hat = raw * r
        gdy = dy * g
        dx = r * (gdy - hat * jnp.mean(gdy * hat, axis=-1, keepdims=True))
        return dx, jnp.sum(dy * hat, axis=0, keepdims=True)

    def body(q_ref, k_ref, dqn_ref, dkn_ref, dv_ref, df_ref, gq_ref, gk_ref, dp_ref, dgq_ref, dgk_ref):
        i = pl.program_id(0)
        accq = jnp.zeros((1, HEAD_DIM), F32)
        acck = jnp.zeros((1, HEAD_DIM), F32)
        for h in range(n_heads):
            sl = slice(h * HEAD_DIM, (h + 1) * HEAD_DIM)
            dq, pq = head_bwd(q_ref[:, sl], dqn_ref[:, sl], gq_ref[...])
            dk, pk = head_bwd(k_ref[:, sl], dkn_ref[:, sl], gk_ref[...])
            dp_ref[:, sl] = dq.astype(BF16)
            dp_ref[:, d + h * HEAD_DIM:d + (h + 1) * HEAD_DIM] = dk.astype(BF16)
            accq, acck = accq + pq, acck + pk
        dp_ref[:, 2 * d:3 * d] = dv_ref[...]
        dp_ref[:, 3 * d:] = df_ref[...]

        @pl.when(i == 0)
        def _():
            dgq_ref[...] = accq
            dgk_ref[...] = acck

        @pl.when(i > 0)
        def _():
            dgq_ref[...] += accq
            dgk_ref[...] += acck

    col = lambda c: pl.BlockSpec((tm, d), lambda i, c=c: (i, c))
    vec = pl.BlockSpec((1, HEAD_DIM), lambda i: (0, 0))
    return pl.pallas_call(
        body, name=name, grid=(s // tm,),
        out_shape=(jax.ShapeDtypeStruct((s, n_pad), BF16), jax.ShapeDtypeStruct((1, HEAD_DIM), F32),
                   jax.ShapeDtypeStruct((1, HEAD_DIM), F32)),
        in_specs=[col(0), col(1), col(0), col(0), col(0), pl.BlockSpec((tm, LANES), lambda i: (i, 0)), vec, vec],
        out_specs=(pl.BlockSpec((tm, n_pad), lambda i: (i, 0)), vec, vec),
        compiler_params=_params("arbitrary"),
    )(proj, proj, dqn, dkn, dv, dflog, gq, gk)


def _attn_fwd(name, qn, kn, vt, c_row, c_col):
    s, d = qn.shape
    n_heads = d // HEAD_DIM
    t = _tile(s, 512, LANES)
    scale = HEAD_DIM ** -0.5

    hp = 2 if n_heads % 2 == 0 else 1
    log2e = 1.4426950408889634

    def body(q_ref, k_ref, vt_ref, cq_ref, ck_ref, o_ref, lse_ref, m_ref, l_ref, acc_ref):
        i = pl.program_id(1)
        m_ref[...] = jnp.full(m_ref.shape, NEG_INF, F32)
        l_ref[...] = jnp.zeros_like(l_ref)
        acc_ref[...] = jnp.zeros_like(acc_ref)

        def step(j, masked):
            start = pl.multiple_of(j * t, t)
            for hh in range(hp):
                sl = slice(hh * HEAD_DIM, (hh + 1) * HEAD_DIM)
                kj = k_ref[pl.ds(start, t), sl]
                vtj = vt_ref[sl, pl.ds(start, t)]
                st = (lax.dot_general(kj, q_ref[:, sl], NT, preferred_element_type=F32) * (scale * log2e)
                      - ck_ref[hh, pl.ds(start, t), :] * log2e)
                if masked:
                    rows = lax.broadcasted_iota(jnp.int32, (t, t), 0)
                    cols = lax.broadcasted_iota(jnp.int32, (t, t), 1)
                    st = jnp.where(cols >= rows, st, NEG_INF)
                m_prev = m_ref[hh]
                m_new = jnp.maximum(m_prev, jnp.max(st, axis=0, keepdims=True))
                pt = jnp.exp2(st - m_new)
                alpha = jnp.exp2(m_prev - m_new)
                l_ref[hh] = alpha * l_ref[hh] + jnp.sum(pt, axis=0, keepdims=True)
                acc_ref[hh] = alpha * acc_ref[hh] + lax.dot_general(
                    vtj, pt.astype(BF16), NN, preferred_element_type=F32)
                m_ref[hh] = m_new

        def loop_body(j, carry):
            step(j, False)
            return carry

        lax.fori_loop(0, i, loop_body, 0)
        step(i, True)
        for hh in range(hp):
            sl = slice(hh * HEAD_DIM, (hh + 1) * HEAD_DIM)
            o_ref[:, sl] = (acc_ref[hh] / l_ref[hh]).T.astype(BF16)
            lse_ref[hh] = (m_ref[hh] + jnp.log2(l_ref[hh])) * (1.0 / log2e) + cq_ref[hh]

    wide = hp * HEAD_DIM
    row_blk = pl.BlockSpec((hp, 1, t), lambda h, i: (h, 0, i))
    return pl.pallas_call(
        body, name=name, grid=(n_heads // hp, s // t),
        out_shape=(jax.ShapeDtypeStruct((s, d), BF16), jax.ShapeDtypeStruct((n_heads, 1, s), F32)),
        in_specs=[pl.BlockSpec((t, wide), lambda h, i: (i, h)),
                  pl.BlockSpec((s, wide), lambda h, i: (0, h)),
                  pl.BlockSpec((wide, s), lambda h, i: (h, 0)),
                  row_blk, pl.BlockSpec((hp, s, 1), lambda h, i: (h, 0, 0))],
        out_specs=(pl.BlockSpec((t, wide), lambda h, i: (i, h)), row_blk),
        scratch_shapes=[pltpu.VMEM((hp, 1, t), F32), pltpu.VMEM((hp, 1, t), F32),
                        pltpu.VMEM((hp, HEAD_DIM, t), F32)],
        compiler_params=_params("parallel", "arbitrary"),
    )(qn, kn, vt, c_row, c_col)


def _attn_delta(name, o, do, n_heads):
    s, d = o.shape
    tm = _tile(s, 256, 16)

    def body(o_ref, do_ref, dl_ref):
        lane = lax.broadcasted_iota(jnp.int32, (tm, LANES), 1)
        acc = jnp.zeros((tm, LANES), F32)
        for h in range(n_heads):
            sl = slice(h * HEAD_DIM, (h + 1) * HEAD_DIM)
            col = jnp.sum(o_ref[:, sl].astype(F32) * do_ref[:, sl].astype(F32), axis=-1, keepdims=True)
            acc = jnp.where(lane == h, col, acc)
        dl_ref[...] = acc

    row = pl.BlockSpec((tm, d), lambda i: (i, 0))
    return pl.pallas_call(
        body, name=name, grid=(s // tm,), out_shape=jax.ShapeDtypeStruct((s, LANES), F32),
        in_specs=[row, row], out_specs=pl.BlockSpec((tm, LANES), lambda i: (i, 0)),
        compiler_params=_params("parallel"),
    )(o, do)


def _attn_bwd(name, qn, kn, vb, do, c_row, lse_row, delta_row, c_col):
    s, d = qn.shape
    n_heads = d // HEAD_DIM
    t = _tile(s, 512, LANES)
    nq = s // t
    scale = HEAD_DIM ** -0.5

    hp = 2 if n_heads % 2 == 0 else 1

    def body(q_ref, do_ref, cr_ref, lse_ref, dl_ref, k_ref, v_ref, ck_ref,
             dq_ref, dk_ref, dv_ref, dc_ref, dcq_ref, dk_acc, dv_acc, dc_acc):
        j = pl.program_id(1)

        @pl.when(j == 0)
        def _():
            dq_ref[...] = jnp.zeros_like(dq_ref)
            dcq_ref[...] = jnp.zeros_like(dcq_ref)

        dk_acc[...] = jnp.zeros_like(dk_acc)
        dv_acc[...] = jnp.zeros_like(dv_acc)
        dc_acc[...] = jnp.zeros_like(dc_acc)

        def step(i, masked):
            start = pl.multiple_of(i * t, t)
            for hh in range(hp):
                sl = slice(hh * HEAD_DIM, (hh + 1) * HEAD_DIM)
                kj = k_ref[:, sl]
                qi = q_ref[pl.ds(start, t), sl]
                doi = do_ref[pl.ds(start, t), sl]
                bias = cr_ref[hh, :, pl.ds(start, t)] - lse_ref[hh, :, pl.ds(start, t)]
                dli = dl_ref[hh, :, pl.ds(start, t)]
                st = lax.dot_general(kj, qi, NT, preferred_element_type=F32) * scale + (bias - ck_ref[hh])
                if masked:
                    rows = lax.broadcasted_iota(jnp.int32, (t, t), 0)
                    cols = lax.broadcasted_iota(jnp.int32, (t, t), 1)
                    st = jnp.where(cols >= rows, st, NEG_INF)
                pt = jnp.exp(st)
                dpt = lax.dot_general(v_ref[:, sl], doi, NT, preferred_element_type=F32)
                dst = pt * (dpt - dli)
                dsb = dst.astype(BF16)
                dv_acc[:, sl] += lax.dot_general(pt.astype(BF16), doi, NN, preferred_element_type=F32)
                dk_acc[:, sl] += lax.dot_general(dsb, qi, NN, preferred_element_type=F32)
                dq_ref[pl.ds(start, t), sl] += lax.dot_general(dsb, kj, TN, preferred_element_type=F32) * scale
                dc_acc[hh] += jnp.sum(dst, axis=1, keepdims=True)
                dcq_ref[hh, :, pl.ds(start, t)] += jnp.sum(dst, axis=0, keepdims=True)

        step(j, True)

        def loop_body(i, carry):
            step(i, False)
            return carry

        lax.fori_loop(j + 1, nq, loop_body, 0)
        dk_ref[...] = dk_acc[...] * scale
        dv_ref[...] = dv_acc[...].astype(BF16)
        dc_ref[...] = -dc_acc[...]

    wide = hp * HEAD_DIM
    head_all = pl.BlockSpec((s, wide), lambda h, j: (0, h))
    row_all = pl.BlockSpec((hp, 1, s), lambda h, j: (h, 0, 0))
    blk = pl.BlockSpec((t, wide), lambda h, j: (j, h))
    col_blk = pl.BlockSpec((hp, t, 1), lambda h, j: (h, j, 0))
    return pl.pallas_call(
        body, name=name, grid=(n_heads // hp, nq),
        out_shape=(jax.ShapeDtypeStruct((s, d), F32), jax.ShapeDtypeStruct((s, d), F32),
                   jax.ShapeDtypeStruct((s, d), BF16), jax.ShapeDtypeStruct((n_heads, s, 1), F32),
                   jax.ShapeDtypeStruct((n_heads, 1, s), F32)),
        in_specs=[head_all, head_all, row_all, row_all, row_all, blk, blk, col_blk],
        out_specs=(head_all, blk, blk, col_blk, row_all),
        scratch_shapes=[pltpu.VMEM((t, wide), F32), pltpu.VMEM((t, wide), F32), pltpu.VMEM((hp, t, 1), F32)],
        compiler_params=_params("parallel", "arbitrary"),
    )(qn, do, c_row, lse_row, delta_row, kn, vb, c_col)


def _ffn_up(name, h, w_gu):
    s, d = h.shape
    fs = w_gu.shape[2]
    half = N_DEV // 2
    tm = _tile(s, 512, 16)

    def body(h_ref, wg_ref, wu_ref, s_ref, us_ref, a_ref):
        hv = h_ref[...]
        g = lax.dot_general(hv, wg_ref[...], NN, preferred_element_type=F32)
        u = lax.dot_general(hv, wu_ref[...], NN, preferred_element_type=F32)
        sig = jax.nn.sigmoid(g)
        silu = g * sig
        s_ref[...] = silu.astype(BF16)
        us_ref[...] = (u * (sig * (1.0 + g * (1.0 - sig)))).astype(BF16)
        a_ref[...] = (silu * u).astype(BF16)

    out = jax.ShapeDtypeStruct((s, half * fs), BF16)
    ospec = pl.BlockSpec((tm, fs), lambda j, i: (i, j))
    return pl.pallas_call(
        body, name=name, grid=(half, s // tm), out_shape=(out, out, out),
        in_specs=[pl.BlockSpec((tm, d), lambda j, i: (i, 0)),
                  pl.BlockSpec((None, d, fs), lambda j, i: (j, 0, 0)),
                  pl.BlockSpec((None, d, fs), lambda j, i: (j + half, 0, 0))],
        out_specs=(ospec, ospec, ospec),
        compiler_params=_params("parallel", "parallel"),
    )(h, w_gu, w_gu)


def _ffn_dact(name, dx, w_dn4, silu, usilu):
    s, d = dx.shape
    half, fs = w_dn4.shape[0], w_dn4.shape[1]
    tm = _tile(s, 512, 16)

    cut = (fs // (2 * LANES)) * LANES

    def body(dx_ref, w_ref, s_ref, us_ref, dgu_ref):
        dxv = dx_ref[...].astype(BF16)
        for lo, hi in ((0, cut), (cut, fs)) if cut else ((0, fs),):
            da = lax.dot_general(dxv, w_ref[lo:hi, :], NT, preferred_element_type=F32)
            dgu_ref[0, :, lo:hi] = (da * us_ref[:, lo:hi].astype(F32)).astype(BF16)
            dgu_ref[1, :, lo:hi] = (da * s_ref[:, lo:hi].astype(F32)).astype(BF16)

    blk = pl.BlockSpec((tm, fs), lambda j, i: (i, j))
    return pl.pallas_call(
        body, name=name, grid=(half, s // tm),
        out_shape=jax.ShapeDtypeStruct((2, s, half * fs), BF16),
        in_specs=[pl.BlockSpec((tm, d), lambda j, i: (i, 0)),
                  pl.BlockSpec((None, fs, d), lambda j, i: (j, 0, 0)), blk, blk],
        out_specs=pl.BlockSpec((2, tm, fs), lambda j, i: (0, i, j)),
        compiler_params=_params("parallel", "parallel"),
    )(dx, w_dn4, silu, usilu)


def _ffn_dw_gu(name, h, dgu, dep=None):
    s, d = h.shape
    half, fs = N_DEV // 2, dgu.shape[2] // (N_DEV // 2)
    tm, ts = _tile(d, 1024, LANES), _tile(s, 2048, 16)
    return _mm(name, "tn", h, dgu, jax.ShapeDtypeStruct((N_DEV, d, fs), BF16),
               grid=(d // tm, N_DEV, s // ts),
               a_spec=pl.BlockSpec((ts, tm), lambda i, j, k: (k, i)),
               b_spec=pl.BlockSpec((None, ts, fs), lambda i, j, k: (j // half, k, j % half)),
               o_spec=pl.BlockSpec((None, tm, fs), lambda i, j, k: (j, i, 0)),
               acc_shape=(tm, fs), dep=dep)


def _ffn_dh(name, dgu, w_gu, x, g, dres, dep=None):
    s = dgu.shape[1]
    d, fs = w_gu.shape[1], w_gu.shape[2]
    half = N_DEV // 2
    tm = _tile(s, 512, 16)
    return _mm_rms_bwd(name, dgu, w_gu, x, g, dres, tm=tm, nk=N_DEV,
                       a_spec=pl.BlockSpec((None, tm, fs), lambda i, k: (k // half, i, k % half)),
                       b_spec=pl.BlockSpec((None, d, fs), lambda i, k: (k, 0, 0)), dep=dep)


def _proj_in_dx(name, dproj, w_in, x, g, dres, dep=None):
    s, n = dproj.shape
    d = w_in.shape[0]
    tm, tk = _tile(s, 512, 16), _tile(n, 896, LANES)
    return _mm_rms_bwd(name, dproj, w_in, x, g, dres, tm=tm, nk=n // tk,
                       a_spec=pl.BlockSpec((tm, tk), lambda i, k: (i, k)),
                       b_spec=pl.BlockSpec((d, tk), lambda i, k: (0, k)), dep=dep)


def _pool_fwd(name, x, g, w, b, sc):
    s, d = x.shape
    dg = d // len(POOL_WINDOWS)
    tm = _tile(s, 256, POOL_HALO)
    per = tm // POOL_HALO

    def body(x_ref, xh_ref, g_ref, w_ref, b_ref, sc_ref, xo_ref, y_ref, zb_ref):
        i = pl.program_id(0)
        gv = g_ref[...]

        def norm(v):
            return (v * lax.rsqrt(jnp.mean(v * v, axis=-1, keepdims=True) + RMS_EPS)) * gv

        h = norm(x_ref[...])
        halo = norm(xh_ref[...]) * (i > 0).astype(F32)
        ext = jnp.concatenate([halo, h], axis=0)
        t = i * tm + lax.broadcasted_iota(jnp.int32, (tm, 1), 0)
        for gi, win in enumerate(POOL_WINDOWS):
            sl = slice(gi * dg, (gi + 1) * dg)
            acc = ext[:, sl]
            step = 1
            while step < win:
                acc = acc + pltpu.roll(acc, step, 0)
                step *= 2
            inv = 1.0 / jnp.minimum(t + 1, win).astype(F32)
            yg = (acc[POOL_HALO:, :] * inv - h[:, sl]).astype(BF16)
            y_ref[:, sl] = yg
            zb = lax.dot_general(yg, w_ref[gi], NN, preferred_element_type=F32) + b_ref[:, sl]
            zb_ref[:, sl] = zb
            xo_ref[:, sl] = x_ref[:, sl] + zb * sc_ref[:, sl]

    row = pl.BlockSpec((tm, d), lambda i: (i, 0))
    vec = pl.BlockSpec((1, d), lambda i: (0, 0))
    return pl.pallas_call(
        body, name=name, grid=(s // tm,),
        out_shape=(jax.ShapeDtypeStruct((s, d), F32), jax.ShapeDtypeStruct((s, d), BF16),
                   jax.ShapeDtypeStruct((s, d), F32)),
        in_specs=[row, pl.BlockSpec((POOL_HALO, d), lambda i: (jnp.maximum(i * per - 1, 0), 0)),
                  vec, pl.BlockSpec(w.shape, lambda i: (0, 0, 0)), vec, vec],
        out_specs=(row, row, row),
        compiler_params=_params("parallel"),
    )(x, x, g, w, b, sc)


def _pool_bwd(name, dout, x, zb, g, w, sc):
    s, d = x.shape
    dg = d // len(POOL_WINDOWS)
    tm = _tile(s, 256, POOL_HALO)
    per = tm // POOL_HALO
    nb = s // tm
    ext_rows = tm + POOL_HALO

    def body(do_ref, doh_ref, x_ref, zb_ref, g_ref, w_ref, sc_ref, dx_ref, dz_ref, dgn_ref, dsc_ref, db_ref):
        i = pl.program_id(0)
        scv = sc_ref[...]
        dov = do_ref[...]
        dz = dov * scv
        dz_ref[...] = dz.astype(BF16)
        halo = doh_ref[...] * scv * (i < nb - 1).astype(F32)
        ext = jnp.concatenate([dz, halo], axis=0).astype(BF16)
        t = i * tm + lax.broadcasted_iota(jnp.int32, (ext_rows, 1), 0)
        parts = []
        for gi, win in enumerate(POOL_WINDOWS):
            sl = slice(gi * dg, (gi + 1) * dg)
            dy = lax.dot_general(ext[:, sl], w_ref[gi], NT, preferred_element_type=F32)
            acc = dy * (1.0 / jnp.minimum(t + 1, win).astype(F32))
            step = 1
            while step < win:
                acc = acc + pltpu.roll(acc, ext_rows - step, 0)
                step *= 2
            parts.append(acc[:tm, :] - dy[:tm, :])
        dh = jnp.concatenate(parts, axis=1)
        xv = x_ref[...]
        r = lax.rsqrt(jnp.mean(xv * xv, axis=-1, keepdims=True) + RMS_EPS)
        xhat = xv * r
        gdh = dh * g_ref[...]
        dx_ref[...] = dov + r * (gdh - xhat * jnp.mean(gdh * xhat, axis=-1, keepdims=True))
        pgn = jnp.sum(dh * xhat, axis=0, keepdims=True)
        psc = jnp.sum(dov * zb_ref[...], axis=0, keepdims=True)
        pb = jnp.sum(dz, axis=0, keepdims=True)

        @pl.when(i == 0)
        def _():
            dgn_ref[...] = pgn
            dsc_ref[...] = psc
            db_ref[...] = pb

        @pl.when(i > 0)
        def _():
            dgn_ref[...] += pgn
            dsc_ref[...] += psc
            db_ref[...] += pb

    row = pl.BlockSpec((tm, d), lambda i: (i, 0))
    vec = pl.BlockSpec((1, d), lambda i: (0, 0))
    vshape = jax.ShapeDtypeStruct((1, d), F32)
    return pl.pallas_call(
        body, name=name, grid=(nb,),
        out_shape=(jax.ShapeDtypeStruct((s, d), F32), jax.ShapeDtypeStruct((s, d), BF16), vshape, vshape, vshape),
        in_specs=[row, pl.BlockSpec((POOL_HALO, d), lambda i: (jnp.minimum((i + 1) * per, s // POOL_HALO - 1), 0)),
                  row, row, vec, pl.BlockSpec(w.shape, lambda i: (0, 0, 0)), vec],
        out_specs=(row, row, vec, vec, vec),
        compiler_params=_params("arbitrary"),
    )(dout, dout, x, zb, g, w, sc)


def _pool_dw(name, y, dz, n_groups):
    s, d = y.shape
    dg = d // n_groups
    ts = _tile(s, 1024, 16)
    return _mm(name, "tn", y, dz, jax.ShapeDtypeStruct((n_groups, dg, dg), F32),
               grid=(n_groups, 1, s // ts),
               a_spec=pl.BlockSpec((ts, dg), lambda i, j, k: (k, i)),
               b_spec=pl.BlockSpec((ts, dg), lambda i, j, k: (k, i)),
               o_spec=pl.BlockSpec((None, dg, dg), lambda i, j, k: (i, 0, 0)),
               acc_shape=(dg, dg))


def _loss_head(name, y, tgt):
    s, d = y.shape
    tm = _tile(s, 512, 16)

    def body(y_ref, t_ref, dy_ref, l_ref):
        i = pl.program_id(0)
        e = y_ref[...] - t_ref[...]
        dy_ref[...] = e * (1.0 / d)
        part = jnp.sum(jnp.mean(e * e, axis=-1, keepdims=True), axis=0, keepdims=True)
        part = jnp.broadcast_to(part, l_ref.shape)

        @pl.when(i == 0)
        def _():
            l_ref[...] = part

        @pl.when(i > 0)
        def _():
            l_ref[...] += part

    row = pl.BlockSpec((tm, d), lambda i: (i, 0))
    return pl.pallas_call(
        body, name=name, grid=(s // tm,),
        out_shape=(jax.ShapeDtypeStruct((s, d), F32), jax.ShapeDtypeStruct((8, LANES), F32)),
        in_specs=[row, row], out_specs=(row, pl.BlockSpec((8, LANES), lambda i: (0, 0))),
        compiler_params=_params("arbitrary"),
    )(y, tgt)


def _adam_update(w_ref, m_ref, v_ref, p_ref, g_ref, d_ref, nm_ref, nv_ref):
    g = p_ref[0].astype(F32)
    for k in range(1, N_DEV):
        g = g + p_ref[k].astype(F32)
    mn = ADAM_B1 * m_ref[...] + (1.0 - ADAM_B1) * g
    vn = ADAM_B2 * v_ref[...] + (1.0 - ADAM_B2) * (g * g)
    m_hat = mn / (1.0 - ADAM_B1 ** ADAM_STEP)
    v_hat = vn / (1.0 - ADAM_B2 ** ADAM_STEP)
    g_ref[...] = g
    d_ref[...] = -ADAM_LR * (m_hat / (jnp.sqrt(v_hat) + ADAM_EPS) + ADAM_WD * w_ref[...])
    nm_ref[...] = mn
    nv_ref[...] = vn


def _adamw_layers(name, w, m, v, pieces):
    n_layers, r, c = w.shape
    tr = _tile(r, 128, 16)

    def body(w_ref, m_ref, v_ref, *rest):
        p_refs, outs = rest[:n_layers], rest[n_layers:]
        layer = pl.program_id(0)
        for l in range(n_layers):
            @pl.when(layer == l)
            def _(l=l):
                _adam_update(w_ref, m_ref, v_ref, p_refs[l], *outs)

    blk = pl.BlockSpec((None, tr, c), lambda l, i: (l, i, 0))
    terms = [pl.BlockSpec((N_DEV, tr, c), lambda l, i, n=n: (0, jnp.where(l == n, i, 0), 0))
             for n in range(n_layers)]
    out = jax.ShapeDtypeStruct(w.shape, F32)
    return list(pl.pallas_call(
        body, name=name, grid=(n_layers, r // tr), out_shape=(out, out, out, out),
        in_specs=[blk, blk, blk] + terms, out_specs=(blk, blk, blk, blk),
        compiler_params=_params("parallel", "parallel"),
    )(w, m, v, *pieces))


def _adamw(name, w, m, v, pieces):
    r, c = w.shape
    tr = _tile(r, 128, 16)

    def body(w_ref, m_ref, v_ref, p_ref, g_ref, d_ref, nm_ref, nv_ref):
        _adam_update(w_ref, m_ref, v_ref, p_ref, g_ref, d_ref, nm_ref, nv_ref)

    blk = pl.BlockSpec((tr, c), lambda i: (i, 0))
    out = jax.ShapeDtypeStruct((r, c), F32)
    return pl.pallas_call(
        body, name=name, grid=(r // tr,), out_shape=(out, out, out, out),
        in_specs=[blk, blk, blk, pl.BlockSpec((N_DEV, tr, c), lambda i: (0, i, 0))],
        out_specs=(blk, blk, blk, blk),
        compiler_params=_params("parallel"),
    )(w, m, v, pieces)


def _pack_small(mix, ffn, b_f, gq, gk):
    def rows(a):
        a = a.reshape(-1, LANES) if a.shape[-1] >= LANES else jnp.pad(a, ((0, 0), (0, LANES - a.shape[-1])))
        return jnp.pad(a, ((0, -a.shape[0] % 8), (0, 0)))
    return jnp.concatenate([rows(mix), rows(ffn), rows(b_f), rows(gq), rows(gk)], axis=0)


def _unpack_small(p, mix, ffn, b_f, gq, gk):
    out, pos = [], 0
    for a in (mix, ffn, b_f, gq, gk):
        n = a.size // LANES if a.shape[-1] >= LANES else a.shape[0]
        blk = p[pos:pos + n]
        out.append(blk.reshape(a.shape) if a.shape[-1] >= LANES else blk[:, :a.shape[-1]])
        pos += n + (-n % 8)
    return out


def kernel(x, mix_norm_g, ffn_norm_g, fox_w_in, fox_b_f, fox_q_norm_g, fox_k_norm_g, fox_w_out, pool_w, pool_b, pool_scale, ffn_w_gate_up, ffn_w_down, loss_target, m_mix_norm_g, m_ffn_norm_g, m_fox_w_in, m_fox_b_f, m_fox_q_norm_g, m_fox_k_norm_g, m_fox_w_out, m_pool_w, m_pool_b, m_pool_scale, m_ffn_w_gate_up, m_ffn_w_down, v_mix_norm_g, v_ffn_norm_g, v_fox_w_in, v_fox_b_f, v_fox_q_norm_g, v_fox_k_norm_g, v_fox_w_out, v_pool_w, v_pool_b, v_pool_scale, v_ffn_w_gate_up, v_ffn_w_down):
    xs, tgt = x[0], loss_target[0]
    s, d = xs.shape
    depth = mix_norm_g.shape[0]
    n_fox, n_pool = fox_w_in.shape[0], pool_w.shape[0]
    n_heads = d // HEAD_DIM
    n_in = fox_w_in.shape[2] * N_DEV
    n_pad = 3 * d + LANES
    n_groups = pool_w.shape[1]
    dsh = d // N_DEV
    half = N_DEV // 2
    axes = ("x", "y", "c")

    w_in_bf, w_out_bf, pool_w_bf = fox_w_in.astype(BF16), fox_w_out.astype(BF16), pool_w.astype(BF16)
    gu_bf, dn_bf = ffn_w_gate_up.astype(BF16), ffn_w_down.astype(BF16)
    pool_bs = jnp.stack([pool_b, pool_scale], axis=1)
    mix_gather, ffn_gather = [None] * depth, [None] * depth
    last = None
    for l in range(depth):
        j = l // 2
        shards = [w_in_bf[j:j + 1], w_out_bf[j:j + 1]] if l % 2 == 0 else [pool_w_bf[j:j + 1], pool_bs[j:j + 1]]
        mix_gather[l] = _exchange_start(f"gather_mixer{l}", *_gather_plan(shards), dep=last)
        ffn_gather[l] = _exchange_start(f"gather_ffn{l}", *_gather_plan([gu_bf[l:l + 1], dn_bf[l:l + 1]]),
                                        dep=mix_gather[l]["token"])
        last = ffn_gather[l]["token"]
    started = last[:1, :1]
    w_gu_g, w_dn = [None] * depth, [None] * depth
    w_in, w_out = [None] * n_fox, [None] * n_fox
    w_pool, pool_b_full, pool_s_full = [None] * n_pool, [None] * n_pool, [None] * n_pool
    b_pad =[jnp.pad(fox_b_f[j], (0, LANES - n_heads))[None] for j in range(n_fox)]

    saved = []
    cur = xs
    for i in range(depth):
        j = i // 2
        gm = mix_norm_g[i][None]
        if i == 0:
            gm = gm + started
        if i % 2 == 0:
            w_in_g, w_out_g = _exchange_wait(mix_gather[i], last if i == 0 else cur)
            w_in[j] = jnp.pad(jnp.transpose(w_in_g, (1, 0, 2)).reshape(d, n_in), ((0, 0), (0, n_pad - n_in)))
            w_out[j] = w_out_g.reshape(d, d)
            h = _rms_fwd(f"norm_mix{i}", cur, gm)
            proj =_mm_nn(f"proj_in{i}", h, w_in[j], F32, tn=896)
            gq, gk = fox_q_norm_g[j][None], fox_k_norm_g[j][None]
            qn, kn, vb = _qkv_fwd(f"qk_norm{i}", proj, gq, gk, d)
            flog = proj[:, 3 * d:]
            c = _gate_fwd(f"gate{i}", flog, b_pad[j])
            c_t = c[:, :n_heads].T
            c_col, c_row = c_t[:, :, None], c_t[:, None, :]
            o, lse = _attn_fwd(f"attn{i}", qn, kn, vb.T, c_row, c_col)
            mid = _mm_nn(f"proj_out{i}", o, w_out[j], F32, add=cur)
            mix_saved = (cur, h, proj, flog, qn, kn, vb, c_col, c_row, o, lse)
        else:
            pw_g, pbs_g = _exchange_wait(mix_gather[i], cur)
            w_pool[j] = jnp.transpose(pw_g, (1, 0, 2, 3)).reshape(n_groups, d // n_groups, d // n_groups)
            pbs_full = jnp.transpose(pbs_g, (1, 0, 2)).reshape(2, 1, d)
            pool_b_full[j], pool_s_full[j] = pbs_full[0], pbs_full[1]
            mid, y, zb = _pool_fwd(f"pool{i}", cur, gm, w_pool[j], pool_b_full[j], pool_s_full[j])
            mix_saved = (cur, y, zb)
        h2 = _rms_fwd(f"norm_ffn{i}", mid, ffn_norm_g[i][None])
        w_gu_g[i], dn_g = _exchange_wait(ffn_gather[i], h2)
        w_dn[i] = dn_g.reshape(-1, d)
        silu, usilu, act = _ffn_up(f"ffn_up{i}", h2, w_gu_g[i])
        nxt = _mm_nn(f"ffn_down{i}", act, w_dn[i], F32, add=mid, tk=2816)
        saved.append((mix_saved, mid, h2, silu, usilu, act))
        cur = nxt

    dcur, lpart = _loss_head("loss_head", cur, tgt)
    loss = lax.psum(0.5 * lpart[0, 0], axes)

    d_mix, d_ffn = [None] * depth, [None] * depth
    d_bf, d_gq, d_gk = [None] * n_fox, [None] * n_fox, [None] * n_fox
    mix_scatter, ffn_scatter = [None] * depth, [None] * depth
    pending = jnp.zeros((1, 1), F32)
    for i in reversed(range(depth)):
        j = i // 2
        mix_saved, mid, h2, silu, usilu, act = saved[i]
        dgu = _ffn_dact(f"ffn_dact{i}", dcur, w_dn[i].reshape(half, -1, d), silu, usilu)
        g_dn = _mm_tn(f"ffn_dw_down{i}", act, dcur, BF16, tm=1408).reshape(N_DEV, -1, d)
        sc_dn = _exchange_start(f"scatter_down{i}", *_scatter_plan([g_dn]))
        g_gu = _ffn_dw_gu(f"ffn_dw_up{i}", h2, dgu, dep=sc_dn["token"])
        sc_gu = _exchange_start(f"scatter_up{i}", *_scatter_plan([g_gu]))
        ffn_scatter[i] = (sc_gu, sc_dn)
        g_ffn = ffn_norm_g[i][None] + pending
        dmid, d_ffn[i] = _ffn_dh(f"ffn_dh{i}", dgu, w_gu_g[i], mid, g_ffn, dcur, dep=sc_gu["token"])
        gm = mix_norm_g[i][None]
        if i % 2 == 0:
            xin, h, proj, flog, qn, kn, vb, c_col, c_row, o, lse = mix_saved
            g_out = _mm_tn(f"proj_out_dw{i}", o, dmid, BF16).reshape(N_DEV, dsh, d)
            sc_out = _exchange_start(f"scatter_out{i}", *_scatter_plan([g_out]))
            do = _mm_nt(f"proj_out_dx{i}", dmid, w_out[j], BF16, dep=sc_out["token"])
            delta = _attn_delta(f"attn_delta{i}", o, do, n_heads)
            delta_row = delta[:, :n_heads].T[:, None, :]
            dqn, dkn, dv, dck, dcq = _attn_bwd(f"attn_bwd{i}", qn, kn, vb, do, c_row,
                                               lse, delta_row, c_col)
            lane_pad = ((0, 0), (0, LANES - n_heads))
            dflog, d_bf[j] = _gate_bwd(f"gate_bwd{i}", jnp.pad(dck[:, :, 0].T, lane_pad),
                                       jnp.pad(dcq[:, 0, :].T, lane_pad), flog, b_pad[j], n_heads)
            gq, gk = fox_q_norm_g[j][None], fox_k_norm_g[j][None]
            dproj, d_gq[j], d_gk[j] = _qkv_bwd(f"qk_norm_bwd{i}", proj, dqn, dkn, dv, dflog, gq, gk, d, n_pad)
            dw_in = _mm_tn(f"proj_in_dw{i}", h, dproj, BF16, tn=896)
            g_in = jnp.transpose(dw_in[:, :n_in].reshape(d, N_DEV, n_in // N_DEV), (1, 0, 2))
            sc_in = _exchange_start(f"scatter_in{i}", *_scatter_plan([g_in]))
            mix_scatter[i] = (sc_in, sc_out)
            dcur, d_mix[i] = _proj_in_dx(f"proj_in_dx{i}", dproj, w_in[j], xin, gm, dmid, dep=sc_in["token"])
        else:
            xin, y, zb = mix_saved
            dcur, dz, d_mix[i], dsc, db = _pool_bwd(f"pool_bwd{i}", dmid, xin, zb, gm, w_pool[j], pool_s_full[j])
            dwp = _pool_dw(f"pool_dw{i}", y, dz, n_groups)
            dg = d // n_groups
            g_pw = jnp.transpose(dwp.reshape(n_groups, N_DEV, dg // N_DEV, dg), (1, 0, 2, 3)).astype(BF16)
            g_pbs = jnp.stack([db.reshape(N_DEV, dsh), dsc.reshape(N_DEV, dsh)], axis=1)
            sc_pool = _exchange_start(f"scatter_pool{i}", *_scatter_plan([g_pw, g_pbs]))
            mix_scatter[i] = (sc_pool,)
            pending = sc_pool["token"][:1, :1]
    grad_x = dcur[None]

    mix_landed = [sum((_exchange_wait(hd, dcur) for hd in mix_scatter[l]), []) for l in range(depth)]
    landed = [sum((_exchange_wait(hd, dcur) for hd in ffn_scatter[l]), []) for l in range(depth)]
    r_in, r_out = [t[0] for t in mix_landed[0::2]], [t[1] for t in mix_landed[0::2]]
    r_pw = [t[0].reshape(N_DEV, -1, t[0].shape[-1]) for t in mix_landed[1::2]]
    r_pbs = [t[1] for t in mix_landed[1::2]]
    r_gu, r_dn = [t[0] for t in landed], [t[1] for t in landed]
    upd = {}
    upd["fox_w_in"] = _adamw_layers("adamw_w_in", fox_w_in, m_fox_w_in, v_fox_w_in, r_in)
    upd["fox_w_out"] = _adamw_layers("adamw_w_out", fox_w_out, m_fox_w_out, v_fox_w_out, r_out)
    fold = lambda a: a.reshape(n_pool, -1, a.shape[-1])
    upd["pool_w"] = [o.reshape(pool_w.shape) for o in
                     _adamw_layers("adamw_pool_w", fold(pool_w), fold(m_pool_w), fold(v_pool_w), r_pw)]
    pbs = _adamw_layers("adamw_pool_bs", pool_bs, jnp.stack([m_pool_b, m_pool_scale], axis=1),
                        jnp.stack([v_pool_b, v_pool_scale], axis=1), r_pbs)
    upd["pool_b"] = [o[:, 0] for o in pbs]
    upd["pool_scale"] = [o[:, 1] for o in pbs]
    upd["ffn_w_gate_up"] = _adamw_layers("adamw_gate_up", ffn_w_gate_up, m_ffn_w_gate_up, v_ffn_w_gate_up, r_gu)
    upd["ffn_w_down"] = _adamw_layers("adamw_down", ffn_w_down, m_ffn_w_down, v_ffn_w_down, r_dn)

    small_w = (mix_norm_g, ffn_norm_g, fox_b_f, fox_q_norm_g, fox_k_norm_g)
    small_g = _pack_small(jnp.concatenate(d_mix), jnp.concatenate(d_ffn),
                          jnp.concatenate(d_bf)[:, :n_heads], jnp.concatenate(d_gq), jnp.concatenate(d_gk))
    (small_pieces,), = _all_gather_layers("gather_small", [small_g[None]])
    small = _adamw("adamw_small", _pack_small(*small_w),
                   _pack_small(m_mix_norm_g, m_ffn_norm_g, m_fox_b_f, m_fox_q_norm_g, m_fox_k_norm_g),
                   _pack_small(v_mix_norm_g, v_ffn_norm_g, v_fox_b_f, v_fox_q_norm_g, v_fox_k_norm_g),
                   small_pieces)
    small = [_unpack_small(o, *small_w) for o in small]
    for n, name in enumerate(("mix_norm_g", "ffn_norm_g", "fox_b_f", "fox_q_norm_g", "fox_k_norm_g")):
        upd[name] = [o[n] for o in small]

    order = ("mix_norm_g", "ffn_norm_g", "fox_w_in", "fox_b_f", "fox_q_norm_g", "fox_k_norm_g", "fox_w_out",
             "pool_w", "pool_b", "pool_scale", "ffn_w_gate_up", "ffn_w_down")
    return (loss, grad_x) + tuple(upd[name][q] for q in range(4) for name in order)
```

```python
import functools

import jax
import jax.numpy as jnp
from jax import lax
from jax.experimental import pallas as pl
from jax.experimental.pallas import tpu as pltpu

F32 = jnp.float32
BF16 = jnp.bfloat16
MESH = pl.DeviceIdType.MESH

N_DEV = 8
HEAD_DIM = 128
LANES = 128
POOL_WINDOWS = (2, 4, 8, 16)
POOL_HALO = 16
RMS_EPS = 1e-6
NEG_INF = -1e30
ADAM_LR = 0.001
ADAM_B1 = 0.9
ADAM_B2 = 0.999
ADAM_EPS = 1e-08
ADAM_WD = 0.01
ADAM_STEP = 10
VMEM_LIMIT = 52 * 1024 * 1024

NN = (((1,), (0,)), ((), ()))
NT = (((1,), (1,)), ((), ()))
TN = (((0,), (0,)), ((), ()))


def _tile(n, pref, align):
    best = None
    d = align
    while d <= min(n, pref):
        if n % d == 0:
            best = d
        d += align
    return n if best is None else best


def _params(*sem):
    return pltpu.CompilerParams(dimension_semantics=sem, vmem_limit_bytes=VMEM_LIMIT)


def _position():
    x, y, c = lax.axis_index("x"), lax.axis_index("y"), lax.axis_index("c")
    return x, y, c, 4 * x + 2 * y + c


def _peer(x, y, c, k):
    px = 1 - x if k & 4 else x
    py = 1 - y if k & 2 else y
    pc = 1 - c if k & 1 else c
    return (px, py, pc), 4 * px + 2 * py + pc


def _exchange(name, ins, out_shapes, copies):
    n_in, n_cp = len(ins), len(copies)

    def body(*refs):
        in_refs = refs[:n_in]
        out_refs = refs[n_in:n_in + len(out_shapes)]
        send_sems, recv_sems, loc_sems = refs[n_in + len(out_shapes):]
        x, y, c, me = _position()
        local = []
        for ci, (ii, src_of, oi, dst_of) in enumerate(copies):
            cp = pltpu.make_async_copy(src_of(in_refs[ii], me), dst_of(out_refs[oi], me), loc_sems.at[ci])
            cp.start()
            local.append(cp)
        sends, recvs = [], []
        for k in range(1, N_DEV):
            pid, p = _peer(x, y, c, k)
            for ci, (ii, src_of, oi, dst_of) in enumerate(copies):
                sem = ci * (N_DEV - 1) + k - 1
                send = pltpu.make_async_remote_copy(
                    src_ref=src_of(in_refs[ii], p), dst_ref=dst_of(out_refs[oi], me),
                    send_sem=send_sems.at[sem], recv_sem=recv_sems.at[sem],
                    device_id=pid, device_id_type=MESH)
                send.start()
                sends.append(send)
                recvs.append(pltpu.make_async_remote_copy(
                    src_ref=src_of(in_refs[ii], p), dst_ref=dst_of(out_refs[oi], p),
                    send_sem=send_sems.at[sem], recv_sem=recv_sems.at[sem],
                    device_id=pid, device_id_type=MESH))
        for r in recvs:
            r.wait_recv()
        for s in sends:
            s.wait_send()
        for cp in local:
            cp.wait()

    any_spec = pl.BlockSpec(memory_space=pl.ANY)
    return pl.pallas_call(
        body, name=name,
        out_shape=tuple(out_shapes),
        in_specs=[any_spec] * n_in,
        out_specs=tuple([any_spec] * len(out_shapes)),
        scratch_shapes=[pltpu.SemaphoreType.DMA((n_cp * (N_DEV - 1),)),
                        pltpu.SemaphoreType.DMA((n_cp * (N_DEV - 1),)),
                        pltpu.SemaphoreType.DMA((n_cp,))],
    )(*ins)


def _exchange_start(name, ins, out_shapes, copies, dep=None):
    n_in, n_out, n_cp = len(ins), len(out_shapes), len(copies)

    def body(*refs):
        in_refs = refs[:n_in]
        land_refs = refs[n_in:n_in + n_out]
        outs = refs[n_in + n_out + 1 + (dep is not None):]
        send_sems, recv_sems = outs[:2]
        loc_sems = outs[n_in + n_out + 3]
        x, y, c, me = _position()
        local = []
        for ci, (ii, src_of, oi, dst_of) in enumerate(copies):
            cp = pltpu.make_async_copy(src_of(in_refs[ii], me), dst_of(land_refs[oi], me), loc_sems.at[ci])
            cp.start()
            local.append(cp)
        for cp in local:
            cp.wait()
        for k in range(1, N_DEV):
            pid, p = _peer(x, y, c, k)
            for ci, (ii, src_of, oi, dst_of) in enumerate(copies):
                sem = ci * (N_DEV - 1) + k - 1
                pltpu.make_async_remote_copy(
                    src_ref=src_of(in_refs[ii], p), dst_ref=dst_of(land_refs[oi], me),
                    send_sem=send_sems.at[sem], recv_sem=recv_sems.at[sem],
                    device_id=pid, device_id_type=MESH).start()

    hbm = pl.BlockSpec(memory_space=pltpu.HBM)
    sem = pl.BlockSpec(memory_space=pltpu.SEMAPHORE)
    n_sem = n_cp * (N_DEV - 1)
    lands = [pltpu.with_memory_space_constraint(lax.empty(o.shape, o.dtype), pltpu.HBM) for o in out_shapes]
    srcs = [pltpu.with_memory_space_constraint(a, pltpu.HBM) for a in ins]
    token = pltpu.with_memory_space_constraint(jnp.zeros((8, LANES), F32), pltpu.HBM)
    res = pl.pallas_call(
        body, name=name,
        out_shape=(pltpu.SemaphoreType.DMA((n_sem,)), pltpu.SemaphoreType.DMA((n_sem,)),
                   *[pltpu.HBM(a.shape, a.dtype) for a in ins],
                   *[pltpu.HBM(o.shape, o.dtype) for o in out_shapes],
                   pltpu.HBM((8, LANES), F32)),
        in_specs=[hbm] * (n_in + n_out + 1) + ([pl.BlockSpec(memory_space=pl.ANY)] if dep is not None else []),
        out_specs=(sem, sem, *([hbm] * (n_in + n_out + 1))),
        input_output_aliases={i: 2 + i for i in range(n_in + n_out + 1)},
        scratch_shapes=[pltpu.SemaphoreType.DMA((n_cp,))],
        compiler_params=pltpu.CompilerParams(has_side_effects=pltpu.SideEffectType.DATAFLOW_SIDE_EFFECTING),
    )(*srcs, *lands, token, *([dep] if dep is not None else []))
    return dict(name=name, copies=copies, send=res[0], recv=res[1], srcs=list(res[2:2 + n_in]),
                lands=list(res[2 + n_in:2 + n_in + n_out]), token=res[-1])


def _exchange_wait(handle, after):
    copies, srcs, lands = handle["copies"], handle["srcs"], handle["lands"]
    n_in, n_out = len(srcs), len(lands)

    def body(*refs):
        in_refs = refs[:n_in]
        land_refs = refs[n_in:n_in + n_out]
        send_sems, recv_sems = refs[n_in + n_out:n_in + n_out + 2]
        x, y, c, me = _position()
        waits = []
        for k in range(1, N_DEV):
            pid, p = _peer(x, y, c, k)
            for ci, (ii, src_of, oi, dst_of) in enumerate(copies):
                sem = ci * (N_DEV - 1) + k - 1
                waits.append(pltpu.make_async_remote_copy(
                    src_ref=src_of(in_refs[ii], p), dst_ref=dst_of(land_refs[oi], p),
                    send_sem=send_sems.at[sem], recv_sem=recv_sems.at[sem],
                    device_id=pid, device_id_type=MESH))
        for w in waits:
            w.wait_send()
        for w in waits:
            w.wait_recv()

    hbm = pl.BlockSpec(memory_space=pltpu.HBM)
    sem = pl.BlockSpec(memory_space=pltpu.SEMAPHORE)
    res = pl.pallas_call(
        body, name=handle["name"] + "_wait",
        out_shape=tuple(pltpu.HBM(a.shape, a.dtype) for a in srcs + lands),
        in_specs=[hbm] * (n_in + n_out) + [sem, sem, pl.BlockSpec(memory_space=pl.ANY)],
        out_specs=tuple([hbm] * (n_in + n_out)),
        input_output_aliases={i: i for i in range(n_in + n_out)},
        compiler_params=pltpu.CompilerParams(has_side_effects=pltpu.SideEffectType.DATAFLOW_SIDE_EFFECTING),
    )(*srcs, *lands, handle["send"], handle["recv"], after)
    return list(res[n_in:])


def _gather_plan(stacked):
    ins, outs, copies = [], [], []
    for t in stacked:
        ii = len(ins)
        ins.append(t)
        for l in range(t.shape[0]):
            oi = len(outs)
            outs.append(jax.ShapeDtypeStruct((N_DEV,) + t.shape[1:], t.dtype))
            copies.append((ii, (lambda ref, p, l=l: ref.at[l]), oi, (lambda ref, s: ref.at[s])))
    return ins, outs, copies


def _scatter_plan(blocked):
    outs = [jax.ShapeDtypeStruct(t.shape, t.dtype) for t in blocked]
    copies = [(n, (lambda ref, p: ref.at[p]), n, (lambda ref, s: ref.at[s])) for n in range(len(blocked))]
    return list(blocked), outs, copies


def _all_gather_layers(name, stacked):
    ins, outs, copies = [], [], []
    for t in stacked:
        ii = len(ins)
        ins.append(t)
        for l in range(t.shape[0]):
            oi = len(outs)
            outs.append(jax.ShapeDtypeStruct((N_DEV,) + t.shape[1:], t.dtype))
            copies.append((ii, (lambda ref, p, l=l: ref.at[l]), oi, (lambda ref, s: ref.at[s])))
    res = _exchange(name, ins, outs, copies)
    out, pos = [], 0
    for t in stacked:
        out.append(list(res[pos:pos + t.shape[0]]))
        pos += t.shape[0]
    return out


def _mm(name, mode, a, b, out_shape, *, grid, a_spec, b_spec, o_spec, acc_shape, add=None, add_spec=None, dep=None):
    nk = grid[2]
    dn = {"nn": NN, "nt": NT, "tn": TN}[mode]
    has_add, has_dep = add is not None, dep is not None
    own_acc = nk > 1 and out_shape.dtype != F32

    def body(*refs):
        a_ref, b_ref = refs[:2]
        add_ref = refs[2] if has_add else None
        o_ref = refs[2 + has_add + has_dep]

        def product():
            return lax.dot_general(a_ref[...].astype(BF16), b_ref[...].astype(BF16), dn,
                                   preferred_element_type=F32)

        if nk == 1:
            r = product() + add_ref[...] if has_add else product()
            o_ref[...] = r.astype(o_ref.dtype)
        else:
            acc_ref = refs[-1] if own_acc else o_ref
            k = pl.program_id(2)

            @pl.when(k == 0)
            def _():
                acc_ref[...] = add_ref[...] if has_add else jnp.zeros_like(acc_ref)

            acc_ref[...] += product()
            if own_acc:
                @pl.when(k == nk - 1)
                def _():
                    o_ref[...] = acc_ref[...].astype(o_ref.dtype)

    ins = [a, b] + ([add] if has_add else []) + ([dep] if has_dep else [])
    in_specs = ([a_spec, b_spec] + ([add_spec] if has_add else [])
                + ([pl.BlockSpec(memory_space=pl.ANY)] if has_dep else []))
    scratch = [pltpu.VMEM(acc_shape, F32)] if own_acc else []
    return pl.pallas_call(
        body, name=name, grid=grid, out_shape=out_shape,
        in_specs=in_specs, out_specs=o_spec, scratch_shapes=scratch,
        compiler_params=_params("parallel", "parallel", "arbitrary"),
    )(*ins)


def _mm_nn(name, a, b, out_dtype, add=None, tm=1024, tn=1024, tk=2048):
    m, kd = a.shape
    n = b.shape[1]
    tm, tn, tk = _tile(m, tm, 16), _tile(n, tn, LANES), _tile(kd, tk, LANES)
    return _mm(name, "nn", a, b, jax.ShapeDtypeStruct((m, n), out_dtype),
               grid=(m // tm, n // tn, kd // tk),
               a_spec=pl.BlockSpec((tm, tk), lambda i, j, k: (i, k)),
               b_spec=pl.BlockSpec((tk, tn), lambda i, j, k: (k, j)),
               o_spec=pl.BlockSpec((tm, tn), lambda i, j, k: (i, j)),
               acc_shape=(tm, tn), add=add,
               add_spec=pl.BlockSpec((tm, tn), lambda i, j, k: (i, j)))


def _mm_nt(name, a, b, out_dtype, tm=1024, tn=1024, tk=2048, dep=None):
    m, kd = a.shape
    n = b.shape[0]
    tm, tn, tk = _tile(m, tm, 16), _tile(n, tn, LANES), _tile(kd, tk, LANES)
    return _mm(name, "nt", a, b, jax.ShapeDtypeStruct((m, n), out_dtype),
               grid=(m // tm, n // tn, kd // tk),
               a_spec=pl.BlockSpec((tm, tk), lambda i, j, k: (i, k)),
               b_spec=pl.BlockSpec((tn, tk), lambda i, j, k: (j, k)),
               o_spec=pl.BlockSpec((tm, tn), lambda i, j, k: (i, j)),
               acc_shape=(tm, tn), dep=dep)


def _mm_tn(name, a, b, out_dtype, tm=1024, tn=1024, ts=2048, dep=None):
    s, m = a.shape
    n = b.shape[1]
    tm, tn, ts = _tile(m, tm, LANES), _tile(n, tn, LANES), _tile(s, ts, 16)
    return _mm(name, "tn", a, b, jax.ShapeDtypeStruct((m, n), out_dtype),
               grid=(m // tm, n // tn, s // ts),
               a_spec=pl.BlockSpec((ts, tm), lambda i, j, k: (k, i)),
               b_spec=pl.BlockSpec((ts, tn), lambda i, j, k: (k, j)),
               o_spec=pl.BlockSpec((tm, tn), lambda i, j, k: (i, j)),
               acc_shape=(tm, tn), dep=dep)


def _rms_fwd(name, x, g):
    s, d = x.shape
    tm = _tile(s, 512, 16)

    def body(x_ref, g_ref, h_ref):
        xv = x_ref[...]
        r = lax.rsqrt(jnp.mean(xv * xv, axis=-1, keepdims=True) + RMS_EPS)
        h_ref[...] = ((xv * r) * g_ref[...]).astype(BF16)

    return pl.pallas_call(
        body, name=name, grid=(s // tm,), out_shape=jax.ShapeDtypeStruct((s, d), BF16),
        in_specs=[pl.BlockSpec((tm, d), lambda i: (i, 0)), pl.BlockSpec((1, d), lambda i: (0, 0))],
        out_specs=pl.BlockSpec((tm, d), lambda i: (i, 0)),
        compiler_params=_params("parallel"),
    )(x, g)


def _mm_rms_bwd(name, a, b, x, g, dres, *, tm, nk, a_spec, b_spec, dep=None):
    s, d = x.shape
    has_dep = dep is not None
    ch = _tile(tm, 128, 8)

    def body(*refs):
        a_ref, b_ref, x_ref, g_ref, dres_ref = refs[:5]
        dx_ref, dg_ref = refs[5 + has_dep], refs[6 + has_dep]
        i, k = pl.program_id(0), pl.program_id(1)

        @pl.when(k == 0)
        def _():
            dx_ref[...] = jnp.zeros_like(dx_ref)

        dx_ref[...] += lax.dot_general(a_ref[...].astype(BF16), b_ref[...].astype(BF16), NT,
                                       preferred_element_type=F32)

        @pl.when(k == nk - 1)
        def _():
            def rows_bwd(c, part):
                rows = pl.ds(pl.multiple_of(c * ch, ch), ch)
                dhv = dx_ref[rows, :]
                xv = x_ref[rows, :]
                r = lax.rsqrt(jnp.mean(xv * xv, axis=-1, keepdims=True) + RMS_EPS)
                xhat = xv * r
                gdh = dhv * g_ref[...]
                dx_ref[rows, :] = dres_ref[rows, :] + r * (gdh - xhat * jnp.mean(gdh * xhat, axis=-1, keepdims=True))
                return part + jnp.sum(dhv * xhat, axis=0, keepdims=True)

            part = lax.fori_loop(0, tm // ch, rows_bwd, jnp.zeros((1, d), F32))

            @pl.when(i == 0)
            def _():
                dg_ref[...] = part

            @pl.when(i > 0)
            def _():
                dg_ref[...] += part

    row = pl.BlockSpec((tm, d), lambda i, k: (i, 0))
    vec = pl.BlockSpec((1, d), lambda i, k: (0, 0))
    return pl.pallas_call(
        body, name=name, grid=(s // tm, nk),
        out_shape=(jax.ShapeDtypeStruct((s, d), F32), jax.ShapeDtypeStruct((1, d), F32)),
        in_specs=[a_spec, b_spec, row, vec, row] + ([pl.BlockSpec(memory_space=pl.ANY)] if has_dep else []),
        out_specs=(row, vec),
        compiler_params=_params("arbitrary", "arbitrary"),
    )(a, b, x, g, dres, *([dep] if has_dep else []))


def _split3(v):
    hi = v.astype(BF16)
    r1 = v - hi.astype(F32)
    mid = r1.astype(BF16)
    lo = (r1 - mid.astype(F32)).astype(BF16)
    return hi, mid, lo


def _tri_sum(tri, v):
    hi, mid, lo = _split3(v)
    dot = functools.partial(lax.dot_general, dimension_numbers=NN, preferred_element_type=F32)
    return dot(tri, hi) + dot(tri, mid) + dot(tri, lo)


def _gate_fwd(name, flog, b_pad):
    s = flog.shape[0]
    tb = _tile(s, 256, 16)

    def body(f_ref, b_ref, c_ref, carry_ref):
        i = pl.program_id(0)

        @pl.when(i == 0)
        def _():
            carry_ref[...] = jnp.zeros_like(carry_ref)

        z = f_ref[...] + b_ref[...]
        lf = jnp.minimum(z, 0.0) - jnp.log(1.0 + jnp.exp(-jnp.abs(z)))
        rows = lax.broadcasted_iota(jnp.int32, (tb, tb), 0)
        cols = lax.broadcasted_iota(jnp.int32, (tb, tb), 1)
        tri = (rows >= cols).astype(BF16)
        c_ref[...] = _tri_sum(tri, lf) + carry_ref[...]
        carry_ref[...] = c_ref[pl.ds(tb - 1, 1), :]

    return pl.pallas_call(
        body, name=name, grid=(s // tb,), out_shape=jax.ShapeDtypeStruct((s, LANES), F32),
        in_specs=[pl.BlockSpec((tb, LANES), lambda i: (i, 0)), pl.BlockSpec((1, LANES), lambda i: (0, 0))],
        out_specs=pl.BlockSpec((tb, LANES), lambda i: (i, 0)),
        scratch_shapes=[pltpu.VMEM((1, LANES), F32)],
        compiler_params=_params("arbitrary"),
    )(flog, b_pad)


def _gate_bwd(name, dck, dcq, flog, b_pad, n_heads):
    s = flog.shape[0]
    tb = _tile(s, 256, 16)
    nb = s // tb

    def body(dck_ref, dcq_ref, f_ref, b_ref, df_ref, db_ref, carry_ref, tmp_ref):
        i = pl.program_id(0)

        @pl.when(i == 0)
        def _():
            carry_ref[...] = jnp.zeros_like(carry_ref)

        rows = lax.broadcasted_iota(jnp.int32, (tb, tb), 0)
        cols = lax.broadcasted_iota(jnp.int32, (tb, tb), 1)
        tri = (rows <= cols).astype(BF16)
        tmp_ref[...] = _tri_sum(tri, dck_ref[...] + dcq_ref[...]) + carry_ref[...]
        carry_ref[...] = tmp_ref[pl.ds(0, 1), :]
        z = f_ref[...] + b_ref[...]
        lane = lax.broadcasted_iota(jnp.int32, (tb, LANES), 1)
        df = jnp.where(lane < n_heads, tmp_ref[...] / (1.0 + jnp.exp(z)), 0.0)
        df_ref[...] = df.astype(BF16)
        part = jnp.sum(df, axis=0, keepdims=True)

        @pl.when(i == 0)
        def _():
            db_ref[...] = part

        @pl.when(i > 0)
        def _():
            db_ref[...] += part

    rev = pl.BlockSpec((tb, LANES), lambda i: (nb - 1 - i, 0))
    vec = pl.BlockSpec((1, LANES), lambda i: (0, 0))
    return pl.pallas_call(
        body, name=name, grid=(nb,),
        out_shape=(jax.ShapeDtypeStruct((s, LANES), BF16), jax.ShapeDtypeStruct((1, LANES), F32)),
        in_specs=[rev, rev, rev, vec], out_specs=(rev, vec),
        scratch_shapes=[pltpu.VMEM((1, LANES), F32), pltpu.VMEM((tb, LANES), F32)],
        compiler_params=_params("arbitrary"),
    )(dck, dcq, flog, b_pad)


def _head_rms(v, g):
    r = lax.rsqrt(jnp.mean(v * v, axis=-1, keepdims=True) + RMS_EPS)
    return (v * r) * g


def _qkv_fwd(name, proj, gq, gk, d):
    s = proj.shape[0]
    tm = _tile(s, 256, 16)
    n_heads = d // HEAD_DIM

    def body(q_ref, k_ref, v_ref, gq_ref, gk_ref, qn_ref, kn_ref, vb_ref):
        for h in range(n_heads):
            sl = slice(h * HEAD_DIM, (h + 1) * HEAD_DIM)
            qn_ref[:, sl] = _head_rms(q_ref[:, sl], gq_ref[...]).astype(BF16)
            kn_ref[:, sl] = _head_rms(k_ref[:, sl], gk_ref[...]).astype(BF16)
        vb_ref[...] = v_ref[...].astype(BF16)

    col = lambda c: pl.BlockSpec((tm, d), lambda i, c=c: (i, c))
    vec = pl.BlockSpec((1, HEAD_DIM), lambda i: (0, 0))
    out = jax.ShapeDtypeStruct((s, d), BF16)
    return pl.pallas_call(
        body, name=name, grid=(s // tm,), out_shape=(out, out, out),
        in_specs=[col(0), col(1), col(2), vec, vec], out_specs=(col(0), col(0), col(0)),
        compiler_params=_params("parallel"),
    )(proj, proj, proj, gq, gk)


def _qkv_bwd(name, proj, dqn, dkn, dv, dflog, gq, gk, d, n_pad):
    s = proj.shape[0]
    tm = _tile(s, 256, 16)
    n_heads = d // HEAD_DIM

    def head_bwd(raw, dy, g):
        r = lax.rsqrt(jnp.mean(raw * raw, axis=-1, keepdims=True) + RMS_EPS)
        hat = raw * r
        gdy = dy * g
        dx = r * (gdy - hat * jnp.mean(gdy * hat, axis=-1, keepdims=True))
        return dx, jnp.sum(dy * hat, axis=0, keepdims=True)

    def body(q_ref, k_ref, dqn_ref, dkn_ref, dv_ref, df_ref, gq_ref, gk_ref, dp_ref, dgq_ref, dgk_ref):
        i = pl.program_id(0)
        accq = jnp.zeros((1, HEAD_DIM), F32)
        acck = jnp.zeros((1, HEAD_DIM), F32)
        for h in range(n_heads):
            sl = slice(h * HEAD_DIM, (h + 1) * HEAD_DIM)
            dq, pq = head_bwd(q_ref[:, sl], dqn_ref[:, sl], gq_ref[...])
            dk, pk = head_bwd(k_ref[:, sl], dkn_ref[:, sl], gk_ref[...])
            dp_ref[:, sl] = dq.astype(BF16)
            dp_ref[:, d + h * HEAD_DIM:d + (h + 1) * HEAD_DIM] = dk.astype(BF16)
            accq, acck = accq + pq, acck + pk
        dp_ref[:, 2 * d:3 * d] = dv_ref[...]
        dp_ref[:, 3 * d:] = df_ref[...]

        @pl.when(i == 0)
        def _():
            dgq_ref[...] = accq
            dgk_ref[...] = acck

        @pl.when(i > 0)
        def _():
            dgq_ref[...] += accq
            dgk_ref[...] += acck

    col = lambda c: pl.BlockSpec((tm, d), lambda i, c=c: (i, c))
    vec = pl.BlockSpec((1, HEAD_DIM), lambda i: (0, 0))
    return pl.pallas_call(
        body, name=name, grid=(s // tm,),
        out_shape=(jax.ShapeDtypeStruct((s, n_pad), BF16), jax.ShapeDtypeStruct((1, HEAD_DIM), F32),
                   jax.ShapeDtypeStruct((1, HEAD_DIM), F32)),
        in_specs=[col(0), col(1), col(0), col(0), col(0), pl.BlockSpec((tm, LANES), lambda i: (i, 0)), vec, vec],
        out_specs=(pl.BlockSpec((tm, n_pad), lambda i: (i, 0)), vec, vec),
        compiler_params=_params("arbitrary"),
    )(proj, proj, dqn, dkn, dv, dflog, gq, gk)


def _attn_fwd(name, qn, kn, vt, c_row, c_col):
    s, d = qn.shape
    n_heads = d // HEAD_DIM
    t = _tile(s, 512, LANES)
    scale = HEAD_DIM ** -0.5

    hp = 2 if n_heads % 2 == 0 else 1
    log2e = 1.4426950408889634

    def body(q_ref, k_ref, vt_ref, cq_ref, ck_ref, o_ref, lse_ref, m_ref, l_ref, acc_ref):
        i = pl.program_id(1)
        m_ref[...] = jnp.full(m_ref.shape, NEG_INF, F32)
        l_ref[...] = jnp.zeros_like(l_ref)
        acc_ref[...] = jnp.zeros_like(acc_ref)

        def step(j, masked):
            start = pl.multiple_of(j * t, t)
            for hh in range(hp):
                sl = slice(hh * HEAD_DIM, (hh + 1) * HEAD_DIM)
                kj = k_ref[pl.ds(start, t), sl]
                vtj = vt_ref[sl, pl.ds(start, t)]
                st = (lax.dot_general(kj, q_ref[:, sl], NT, preferred_element_type=F32) * (scale * log2e)
                      - ck_ref[hh, pl.ds(start, t), :] * log2e)
                if masked:
                    rows = lax.broadcasted_iota(jnp.int32, (t, t), 0)
                    cols = lax.broadcasted_iota(jnp.int32, (t, t), 1)
                    st = jnp.where(cols >= rows, st, NEG_INF)
                m_prev = m_ref[hh]
                m_new = jnp.maximum(m_prev, jnp.max(st, axis=0, keepdims=True))
                pt = jnp.exp2(st - m_new)
                alpha = jnp.exp2(m_prev - m_new)
                l_ref[hh] = alpha * l_ref[hh] + jnp.sum(pt, axis=0, keepdims=True)
                acc_ref[hh] = alpha * acc_ref[hh] + lax.dot_general(
                    vtj, pt.astype(BF16), NN, preferred_element_type=F32)
                m_ref[hh] = m_new

        def loop_body(j, carry):
            step(j, False)
            return carry

        lax.fori_loop(0, i, loop_body, 0)
        step(i, True)
        for hh in range(hp):
            sl = slice(hh * HEAD_DIM, (hh + 1) * HEAD_DIM)
            o_ref[:, sl] = (acc_ref[hh] / l_ref[hh]).T.astype(BF16)
            lse_ref[hh] = (m_ref[hh] + jnp.log2(l_ref[hh])) * (1.0 / log2e) + cq_ref[hh]

    wide = hp * HEAD_DIM
    row_blk = pl.BlockSpec((hp, 1, t), lambda h, i: (h, 0, i))
    return pl.pallas_call(
        body, name=name, grid=(n_heads // hp, s // t),
        out_shape=(jax.ShapeDtypeStruct((s, d), BF16), jax.ShapeDtypeStruct((n_heads, 1, s), F32)),
        in_specs=[pl.BlockSpec((t, wide), lambda h, i: (i, h)),
                  pl.BlockSpec((s, wide), lambda h, i: (0, h)),
                  pl.BlockSpec((wide, s), lambda h, i: (h, 0)),
                  row_blk, pl.BlockSpec((hp, s, 1), lambda h, i: (h, 0, 0))],
        out_specs=(pl.BlockSpec((t, wide), lambda h, i: (i, h)), row_blk),
        scratch_shapes=[pltpu.VMEM((hp, 1, t), F32), pltpu.VMEM((hp, 1, t), F32),
                        pltpu.VMEM((hp, HEAD_DIM, t), F32)],
        compiler_params=_params("parallel", "arbitrary"),
    )(qn, kn, vt, c_row, c_col)


def _attn_delta(name, o, do, n_heads):
    s, d = o.shape
    tm = _tile(s, 256, 16)

    def body(o_ref, do_ref, dl_ref):
        lane = lax.broadcasted_iota(jnp.int32, (tm, LANES), 1)
        acc = jnp.zeros((tm, LANES), F32)
        for h in range(n_heads):
            sl = slice(h * HEAD_DIM, (h + 1) * HEAD_DIM)
            col = jnp.sum(o_ref[:, sl].astype(F32) * do_ref[:, sl].astype(F32), axis=-1, keepdims=True)
            acc = jnp.where(lane == h, col, acc)
        dl_ref[...] = acc

    row = pl.BlockSpec((tm, d), lambda i: (i, 0))
    return pl.pallas_call(
        body, name=name, grid=(s // tm,), out_shape=jax.ShapeDtypeStruct((s, LANES), F32),
        in_specs=[row, row], out_specs=pl.BlockSpec((tm, LANES), lambda i: (i, 0)),
        compiler_params=_params("parallel"),
    )(o, do)


def _attn_bwd(name, qn, kn, vb, do, c_row, lse_row, delta_row, c_col):
    s, d = qn.shape
    n_heads = d // HEAD_DIM
    t = _tile(s, 512, LANES)
    nq = s // t
    scale = HEAD_DIM ** -0.5

    hp = 2 if n_heads % 2 == 0 else 1

    def body(q_ref, do_ref, cr_ref, lse_ref, dl_ref, k_ref, v_ref, ck_ref,
             dq_ref, dk_ref, dv_ref, dc_ref, dcq_ref, dk_acc, dv_acc, dc_acc):
        j = pl.program_id(1)

        @pl.when(j == 0)
        def _():
            dq_ref[...] = jnp.zeros_like(dq_ref)
            dcq_ref[...] = jnp.zeros_like(dcq_ref)

        dk_acc[...] = jnp.zeros_like(dk_acc)
        dv_acc[...] = jnp.zeros_like(dv_acc)
        dc_acc[...] = jnp.zeros_like(dc_acc)

        def step(i, masked):
            start = pl.multiple_of(i * t, t)
            for hh in range(hp):
                sl = slice(hh * HEAD_DIM, (hh + 1) * HEAD_DIM)
                kj = k_ref[:, sl]
                qi = q_ref[pl.ds(start, t), sl]
                doi = do_ref[pl.ds(start, t), sl]
                bias = cr_ref[hh, :, pl.ds(start, t)] - lse_ref[hh, :, pl.ds(start, t)]
                dli = dl_ref[hh, :, pl.ds(start, t)]
                st = lax.dot_general(kj, qi, NT, preferred_element_type=F32) * scale + (bias - ck_ref[hh])
                if masked:
                    rows = lax.broadcasted_iota(jnp.int32, (t, t), 0)
                    cols = lax.broadcasted_iota(jnp.int32, (t, t), 1)
                    st = jnp.where(cols >= rows, st, NEG_INF)
                pt = jnp.exp(st)
                dpt = lax.dot_general(v_ref[:, sl], doi, NT, preferred_element_type=F32)
                dst = pt * (dpt - dli)
                dsb = dst.astype(BF16)
                dv_acc[:, sl] += lax.dot_general(pt.astype(BF16), doi, NN, preferred_element_type=F32)
                dk_acc[:, sl] += lax.dot_general(dsb, qi, NN, preferred_element_type=F32)
                dq_ref[pl.ds(start, t), sl] += lax.dot_general(dsb, kj, TN, preferred_element_type=F32) * scale
                dc_acc[hh] += jnp.sum(dst, axis=1, keepdims=True)
                dcq_ref[hh, :, pl.ds(start, t)] += jnp.sum(dst, axis=0, keepdims=True)

        step(j, True)

        def loop_body(i, carry):
            step(i, False)
            return carry

        lax.fori_loop(j + 1, nq, loop_body, 0)
        dk_ref[...] = dk_acc[...] * scale
        dv_ref[...] = dv_acc[...].astype(BF16)
        dc_ref[...] = -dc_acc[...]

    wide = hp * HEAD_DIM
    head_all = pl.BlockSpec((s, wide), lambda h, j: (0, h))
    row_all = pl.BlockSpec((hp, 1, s), lambda h, j: (h, 0, 0))
    blk = pl.BlockSpec((t, wide), lambda h, j: (j, h))
    col_blk = pl.BlockSpec((hp, t, 1), lambda h, j: (h, j, 0))
    return pl.pallas_call(
        body, name=name, grid=(n_heads // hp, nq),
        out_shape=(jax.ShapeDtypeStruct((s, d), F32), jax.ShapeDtypeStruct((s, d), F32),
                   jax.ShapeDtypeStruct((s, d), BF16), jax.ShapeDtypeStruct((n_heads, s, 1), F32),
                   jax.ShapeDtypeStruct((n_heads, 1, s), F32)),
        in_specs=[head_all, head_all, row_all, row_all, row_all, blk, blk, col_blk],
        out_specs=(head_all, blk, blk, col_blk, row_all),
        scratch_shapes=[pltpu.VMEM((t, wide), F32), pltpu.VMEM((t, wide), F32), pltpu.VMEM((hp, t, 1), F32)],
        compiler_params=_params("parallel", "arbitrary"),
    )(qn, do, c_row, lse_row, delta_row, kn, vb, c_col)


def _ffn_up(name, h, w_gu):
    s, d = h.shape
    fs = w_gu.shape[2]
    half = N_DEV // 2
    tm = _tile(s, 512, 16)

    def body(h_ref, wg_ref, wu_ref, s_ref, us_ref, a_ref):
        hv = h_ref[...]
        g = lax.dot_general(hv, wg_ref[...], NN, preferred_element_type=F32)
        u = lax.dot_general(hv, wu_ref[...], NN, preferred_element_type=F32)
        sig = jax.nn.sigmoid(g)
        silu = g * sig
        s_ref[...] = silu.astype(BF16)
        us_ref[...] = (u * (sig * (1.0 + g * (1.0 - sig)))).astype(BF16)
        a_ref[...] = (silu * u).astype(BF16)

    out = jax.ShapeDtypeStruct((s, half * fs), BF16)
    ospec = pl.BlockSpec((tm, fs), lambda j, i: (i, j))
    return pl.pallas_call(
        body, name=name, grid=(half, s // tm), out_shape=(out, out, out),
        in_specs=[pl.BlockSpec((tm, d), lambda j, i: (i, 0)),
                  pl.BlockSpec((None, d, fs), lambda j, i: (j, 0, 0)),
                  pl.BlockSpec((None, d, fs), lambda j, i: (j + half, 0, 0))],
        out_specs=(ospec, ospec, ospec),
        compiler_params=_params("parallel", "parallel"),
    )(h, w_gu, w_gu)


def _ffn_dact(name, dx, w_dn4, silu, usilu):
    s, d = dx.shape
    half, fs = w_dn4.shape[0], w_dn4.shape[1]
    tm = _tile(s, 512, 16)

    cut = (fs // (2 * LANES)) * LANES

    def body(dx_ref, w_ref, s_ref, us_ref, dgu_ref):
        dxv = dx_ref[...].astype(BF16)
        for lo, hi in ((0, cut), (cut, fs)) if cut else ((0, fs),):
            da = lax.dot_general(dxv, w_ref[lo:hi, :], NT, preferred_element_type=F32)
            dgu_ref[0, :, lo:hi] = (da * us_ref[:, lo:hi].astype(F32)).astype(BF16)
            dgu_ref[1, :, lo:hi] = (da * s_ref[:, lo:hi].astype(F32)).astype(BF16)

    blk = pl.BlockSpec((tm, fs), lambda j, i: (i, j))
    return pl.pallas_call(
        body, name=name, grid=(half, s // tm),
        out_shape=jax.ShapeDtypeStruct((2, s, half * fs), BF16),
        in_specs=[pl.BlockSpec((tm, d), lambda j, i: (i, 0)),
                  pl.BlockSpec((None, fs, d), lambda j, i: (j, 0, 0)), blk, blk],
        out_specs=pl.BlockSpec((2, tm, fs), lambda j, i: (0, i, j)),
        compiler_params=_params("parallel", "parallel"),
    )(dx, w_dn4, silu, usilu)


def _ffn_dw_gu(name, h, dgu, dep=None):
    s, d = h.shape
    half, fs = N_DEV // 2, dgu.shape[2] // (N_DEV // 2)
    tm, ts = _tile(d, 1024, LANES), _tile(s, 2048, 16)
    return _mm(name, "tn", h, dgu, jax.ShapeDtypeStruct((N_DEV, d, fs), BF16),
               grid=(d // tm, N_DEV, s // ts),
               a_spec=pl.BlockSpec((ts, tm), lambda i, j, k: (k, i)),
               b_spec=pl.BlockSpec((None, ts, fs), lambda i, j, k: (j // half, k, j % half)),
               o_spec=pl.BlockSpec((None, tm, fs), lambda i, j, k: (j, i, 0)),
               acc_shape=(tm, fs), dep=dep)


def _ffn_dh(name, dgu, w_gu, x, g, dres, dep=None):
    s = dgu.shape[1]
    d, fs = w_gu.shape[1], w_gu.shape[2]
    half = N_DEV // 2
    tm = _tile(s, 512, 16)
    return _mm_rms_bwd(name, dgu, w_gu, x, g, dres, tm=tm, nk=N_DEV,
                       a_spec=pl.BlockSpec((None, tm, fs), lambda i, k: (k // half, i, k % half)),
                       b_spec=pl.BlockSpec((None, d, fs), lambda i, k: (k, 0, 0)), dep=dep)


def _proj_in_dx(name, dproj, w_in, x, g, dres, dep=None):
    s, n = dproj.shape
    d = w_in.shape[0]
    tm, tk = _tile(s, 512, 16), _tile(n, 896, LANES)
    return _mm_rms_bwd(name, dproj, w_in, x, g, dres, tm=tm, nk=n // tk,
                       a_spec=pl.BlockSpec((tm, tk), lambda i, k: (i, k)),
                       b_spec=pl.BlockSpec((d, tk), lambda i, k: (0, k)), dep=dep)


def _pool_fwd(name, x, g, w, b, sc):
    s, d = x.shape
    dg = d // len(POOL_WINDOWS)
    tm = _tile(s, 256, POOL_HALO)
    per = tm // POOL_HALO

    def body(x_ref, xh_ref, g_ref, w_ref, b_ref, sc_ref, xo_ref, y_ref, zb_ref):
        i = pl.program_id(0)
        gv = g_ref[...]

        def norm(v):
            return (v * lax.rsqrt(jnp.mean(v * v, axis=-1, keepdims=True) + RMS_EPS)) * gv

        h = norm(x_ref[...])
        halo = norm(xh_ref[...]) * (i > 0).astype(F32)
        ext = jnp.concatenate([halo, h], axis=0)
        t = i * tm + lax.broadcasted_iota(jnp.int32, (tm, 1), 0)
        for gi, win in enumerate(POOL_WINDOWS):
            sl = slice(gi * dg, (gi + 1) * dg)
            acc = ext[:, sl]
            step = 1
            while step < win:
                acc = acc + pltpu.roll(acc, step, 0)
                step *= 2
            inv = 1.0 / jnp.minimum(t + 1, win).astype(F32)
            yg = (acc[POOL_HALO:, :] * inv - h[:, sl]).astype(BF16)
            y_ref[:, sl] = yg
            zb = lax.dot_general(yg, w_ref[gi], NN, preferred_element_type=F32) + b_ref[:, sl]
            zb_ref[:, sl] = zb
            xo_ref[:, sl] = x_ref[:, sl] + zb * sc_ref[:, sl]

    row = pl.BlockSpec((tm, d), lambda i: (i, 0))
    vec = pl.BlockSpec((1, d), lambda i: (0, 0))
    return pl.pallas_call(
        body, name=name, grid=(s // tm,),
        out_shape=(jax.ShapeDtypeStruct((s, d), F32), jax.ShapeDtypeStruct((s, d), BF16),
                   jax.ShapeDtypeStruct((s, d), F32)),
        in_specs=[row, pl.BlockSpec((POOL_HALO, d), lambda i: (jnp.maximum(i * per - 1, 0), 0)),
                  vec, pl.BlockSpec(w.shape, lambda i: (0, 0, 0)), vec, vec],
        out_specs=(row, row, row),
        compiler_params=_params("parallel"),
    )(x, x, g, w, b, sc)


def _pool_bwd(name, dout, x, zb, g, w, sc):
    s, d = x.shape
    dg = d // len(POOL_WINDOWS)
    tm = _tile(s, 256, POOL_HALO)
    per = tm // POOL_HALO
    nb = s // tm
    ext_rows = tm + POOL_HALO

    def body(do_ref, doh_ref, x_ref, zb_ref, g_ref, w_ref, sc_ref, dx_ref, dz_ref, dgn_ref, dsc_ref, db_ref):
        i = pl.program_id(0)
        scv = sc_ref[...]
        dov = do_ref[...]
        dz = dov * scv
        dz_ref[...] = dz.astype(BF16)
        halo = doh_ref[...] * scv * (i < nb - 1).astype(F32)
        ext = jnp.concatenate([dz, halo], axis=0).astype(BF16)
        t = i * tm + lax.broadcasted_iota(jnp.int32, (ext_rows, 1), 0)
        parts = []
        for gi, win in enumerate(POOL_WINDOWS):
            sl = slice(gi * dg, (gi + 1) * dg)
            dy = lax.dot_general(ext[:, sl], w_ref[gi], NT, preferred_element_type=F32)
            acc = dy * (1.0 / jnp.minimum(t + 1, win).astype(F32))
            step = 1
            while step < win:
                acc = acc + pltpu.roll(acc, ext_rows - step, 0)
                step *= 2
            parts.append(acc[:tm, :] - dy[:tm, :])
        dh = jnp.concatenate(parts, axis=1)
        xv = x_ref[...]
        r = lax.rsqrt(jnp.mean(xv * xv, axis=-1, keepdims=True) + RMS_EPS)
        xhat = xv * r
        gdh = dh * g_ref[...]
        dx_ref[...] = dov + r * (gdh - xhat * jnp.mean(gdh * xhat, axis=-1, keepdims=True))
        pgn = jnp.sum(dh * xhat, axis=0, keepdims=True)
        psc = jnp.sum(dov * zb_ref[...], axis=0, keepdims=True)
        pb = jnp.sum(dz, axis=0, keepdims=True)

        @pl.when(i == 0)
        def _():
            dgn_ref[...] = pgn
            dsc_ref[...] = psc
            db_ref[...] = pb

        @pl.when(i > 0)
        def _():
            dgn_ref[...] += pgn
            dsc_ref[...] += psc
            db_ref[...] += pb

    row = pl.BlockSpec((tm, d), lambda i: (i, 0))
    vec = pl.BlockSpec((1, d), lambda i: (0, 0))
    vshape = jax.ShapeDtypeStruct((1, d), F32)
    return pl.pallas_call(
        body, name=name, grid=(nb,),
        out_shape=(jax.ShapeDtypeStruct((s, d), F32), jax.ShapeDtypeStruct((s, d), BF16), vshape, vshape, vshape),
        in_specs=[row, pl.BlockSpec((POOL_HALO, d), lambda i: (jnp.minimum((i + 1) * per, s // POOL_HALO - 1), 0)),
                  row, row, vec, pl.BlockSpec(w.shape, lambda i: (0, 0, 0)), vec],
        out_specs=(row, row, vec, vec, vec),
        compiler_params=_params("arbitrary"),
    )(dout, dout, x, zb, g, w, sc)


def _pool_dw(name, y, dz, n_groups):
    s, d = y.shape
    dg = d // n_groups
    ts = _tile(s, 1024, 16)
    return _mm(name, "tn", y, dz, jax.ShapeDtypeStruct((n_groups, dg, dg), F32),
               grid=(n_groups, 1, s // ts),
               a_spec=pl.BlockSpec((ts, dg), lambda i, j, k: (k, i)),
               b_spec=pl.BlockSpec((ts, dg), lambda i, j, k: (k, i)),
               o_spec=pl.BlockSpec((None, dg, dg), lambda i, j, k: (i, 0, 0)),
               acc_shape=(dg, dg))


def _loss_head(name, y, tgt):
    s, d = y.shape
    tm = _tile(s, 512, 16)

    def body(y_ref, t_ref, dy_ref, l_ref):
        i = pl.program_id(0)
        e = y_ref[...] - t_ref[...]
        dy_ref[...] = e * (1.0 / d)
        part = jnp.sum(jnp.mean(e * e, axis=-1, keepdims=True), axis=0, keepdims=True)
        part = jnp.broadcast_to(part, l_ref.shape)

        @pl.when(i == 0)
        def _():
            l_ref[...] = part

        @pl.when(i > 0)
        def _():
            l_ref[...] += part

    row = pl.BlockSpec((tm, d), lambda i: (i, 0))
    return pl.pallas_call(
        body, name=name, grid=(s // tm,),
        out_shape=(jax.ShapeDtypeStruct((s, d), F32), jax.ShapeDtypeStruct((8, LANES), F32)),
        in_specs=[row, row], out_specs=(row, pl.BlockSpec((8, LANES), lambda i: (0, 0))),
        compiler_params=_params("arbitrary"),
    )(y, tgt)


def _adam_update(w_ref, m_ref, v_ref, p_ref, g_ref, d_ref, nm_ref, nv_ref):
    g = p_ref[0].astype(F32)
    for k in range(1, N_DEV):
        g = g + p_ref[k].astype(F32)
    mn = ADAM_B1 * m_ref[...] + (1.0 - ADAM_B1) * g
    vn = ADAM_B2 * v_ref[...] + (1.0 - ADAM_B2) * (g * g)
    m_hat = mn / (1.0 - ADAM_B1 ** ADAM_STEP)
    v_hat = vn / (1.0 - ADAM_B2 ** ADAM_STEP)
    g_ref[...] = g
    d_ref[...] = -ADAM_LR * (m_hat / (jnp.sqrt(v_hat) + ADAM_EPS) + ADAM_WD * w_ref[...])
    nm_ref[...] = mn
    nv_ref[...] = vn


def _adamw_layers(name, w, m, v, pieces):
    n_layers, r, c = w.shape
    tr = _tile(r, 128, 16)

    def body(w_ref, m_ref, v_ref, *rest):
        p_refs, outs = rest[:n_layers], rest[n_layers:]
        layer = pl.program_id(0)
        for l in range(n_layers):
            @pl.when(layer == l)
            def _(l=l):
                _adam_update(w_ref, m_ref, v_ref, p_refs[l], *outs)

    blk = pl.BlockSpec((None, tr, c), lambda l, i: (l, i, 0))
    terms = [pl.BlockSpec((N_DEV, tr, c), lambda l, i, n=n: (0, jnp.where(l == n, i, 0), 0))
             for n in range(n_layers)]
    out = jax.ShapeDtypeStruct(w.shape, F32)
    return list(pl.pallas_call(
        body, name=name, grid=(n_layers, r // tr), out_shape=(out, out, out, out),
        in_specs=[blk, blk, blk] + terms, out_specs=(blk, blk, blk, blk),
        compiler_params=_params("parallel", "parallel"),
    )(w, m, v, *pieces))


def _adamw(name, w, m, v, pieces):
    r, c = w.shape
    tr = _tile(r, 128, 16)

    def body(w_ref, m_ref, v_ref, p_ref, g_ref, d_ref, nm_ref, nv_ref):
        _adam_update(w_ref, m_ref, v_ref, p_ref, g_ref, d_ref, nm_ref, nv_ref)

    blk = pl.BlockSpec((tr, c), lambda i: (i, 0))
    out = jax.ShapeDtypeStruct((r, c), F32)
    return pl.pallas_call(
        body, name=name, grid=(r // tr,), out_shape=(out, out, out, out),
        in_specs=[blk, blk, blk, pl.BlockSpec((N_DEV, tr, c), lambda i: (0, i, 0))],
        out_specs=(blk, blk, blk, blk),
        compiler_params=_params("parallel"),
    )(w, m, v, pieces)


def _pack_small(mix, ffn, b_f, gq, gk):
    def rows(a):
        a = a.reshape(-1, LANES) if a.shape[-1] >= LANES else jnp.pad(a, ((0, 0), (0, LANES - a.shape[-1])))
        return jnp.pad(a, ((0, -a.shape[0] % 8), (0, 0)))
    return jnp.concatenate([rows(mix), rows(ffn), rows(b_f), rows(gq), rows(gk)], axis=0)


def _unpack_small(p, mix, ffn, b_f, gq, gk):
    out, pos = [], 0
    for a in (mix, ffn, b_f, gq, gk):
        n = a.size // LANES if a.shape[-1] >= LANES else a.shape[0]
        blk = p[pos:pos + n]
        out.append(blk.reshape(a.shape) if a.shape[-1] >= LANES else blk[:, :a.shape[-1]])
        pos += n + (-n % 8)
    return out


def kernel(x, mix_norm_g, ffn_norm_g, fox_w_in, fox_b_f, fox_q_norm_g, fox_k_norm_g, fox_w_out, pool_w, pool_b, pool_scale, ffn_w_gate_up, ffn_w_down, loss_target, m_mix_norm_g, m_ffn_norm_g, m_fox_w_in, m_fox_b_f, m_fox_q_norm_g, m_fox_k_norm_g, m_fox_w_out, m_pool_w, m_pool_b, m_pool_scale, m_ffn_w_gate_up, m_ffn_w_down, v_mix_norm_g, v_ffn_norm_g, v_fox_w_in, v_fox_b_f, v_fox_q_norm_g, v_fox_k_norm_g, v_fox_w_out, v_pool_w, v_pool_b, v_pool_scale, v_ffn_w_gate_up, v_ffn_w_down):
    xs, tgt = x[0], loss_target[0]
    s, d = xs.shape
    depth = mix_norm_g.shape[0]
    n_fox, n_pool = fox_w_in.shape[0], pool_w.shape[0]
    n_heads = d // HEAD_DIM
    n_in = fox_w_in.shape[2] * N_DEV
    n_pad = 3 * d + LANES
    n_groups = pool_w.shape[1]
    dsh = d // N_DEV
    half = N_DEV // 2
    axes = ("x", "y", "c")

    w_in_bf, w_out_bf, pool_w_bf = fox_w_in.astype(BF16), fox_w_out.astype(BF16), pool_w.astype(BF16)
    gu_bf, dn_bf = ffn_w_gate_up.astype(BF16), ffn_w_down.astype(BF16)
    pool_bs = jnp.stack([pool_b, pool_scale], axis=1)
    mix_gather, ffn_gather = [None] * depth, [None] * depth
    last = None
    for l in range(depth):
        j = l // 2
        shards = [w_in_bf[j:j + 1], w_out_bf[j:j + 1]] if l % 2 == 0 else [pool_w_bf[j:j + 1], pool_bs[j:j + 1]]
        mix_gather[l] = _exchange_start(f"gather_mixer{l}", *_gather_plan(shards), dep=last)
        ffn_gather[l] = _exchange_start(f"gather_ffn{l}", *_gather_plan([gu_bf[l:l + 1], dn_bf[l:l + 1]]),
                                        dep=mix_gather[l]["token"])
        last = ffn_gather[l]["token"]
    started = last[:1, :1]
    w_gu_g, w_dn = [None] * depth, [None] * depth
    w_in, w_out = [None] * n_fox, [None] * n_fox
    w_pool, pool_b_full, pool_s_full = [None] * n_pool, [None] * n_pool, [None] * n_pool
    b_pad =[jnp.pad(fox_b_f[j], (0, LANES - n_heads))[None] for j in range(n_fox)]

    saved = []
    cur = xs
    for i in range(depth):
        j = i // 2
        gm = mix_norm_g[i][None]
        if i == 0:
            gm = gm + started
        if i % 2 == 0:
            w_in_g, w_out_g = _exchange_wait(mix_gather[i], last if i == 0 else cur)
            w_in[j] = jnp.pad(jnp.transpose(w_in_g, (1, 0, 2)).reshape(d, n_in), ((0, 0), (0, n_pad - n_in)))
            w_out[j] = w_out_g.reshape(d, d)
            h = _rms_fwd(f"norm_mix{i}", cur, gm)
            proj =_mm_nn(f"proj_in{i}", h, w_in[j], F32, tn=896)
            gq, gk = fox_q_norm_g[j][None], fox_k_norm_g[j][None]
            qn, kn, vb = _qkv_fwd(f"qk_norm{i}", proj, gq, gk, d)
            flog = proj[:, 3 * d:]
            c = _gate_fwd(f"gate{i}", flog, b_pad[j])
            c_t = c[:, :n_heads].T
            c_col, c_row = c_t[:, :, None], c_t[:, None, :]
            o, lse = _attn_fwd(f"attn{i}", qn, kn, vb.T, c_row, c_col)
            mid = _mm_nn(f"proj_out{i}", o, w_out[j], F32, add=cur)
            mix_saved = (cur, h, proj, flog, qn, kn, vb, c_col, c_row, o, lse)
        else:
            pw_g, pbs_g = _exchange_wait(mix_gather[i], cur)
            w_pool[j] = jnp.transpose(pw_g, (1, 0, 2, 3)).reshape(n_groups, d // n_groups, d // n_groups)
            pbs_full = jnp.transpose(pbs_g, (1, 0, 2)).reshape(2, 1, d)
            pool_b_full[j], pool_s_full[j] = pbs_full[0], pbs_full[1]
            mid, y, zb = _pool_fwd(f"pool{i}", cur, gm, w_pool[j], pool_b_full[j], pool_s_full[j])
            mix_saved = (cur, y, zb)
        h2 = _rms_fwd(f"norm_ffn{i}", mid, ffn_norm_g[i][None])
        w_gu_g[i], dn_g = _exchange_wait(ffn_gather[i], h2)
        w_dn[i] = dn_g.reshape(-1, d)
        silu, usilu, act = _ffn_up(f"ffn_up{i}", h2, w_gu_g[i])
        nxt = _mm_nn(f"ffn_down{i}", act, w_dn[i], F32, add=mid, tk=2816)
        saved.append((mix_saved, mid, h2, silu, usilu, act))
        cur = nxt

    dcur, lpart = _loss_head("loss_head", cur, tgt)
    loss = lax.psum(0.5 * lpart[0, 0], axes)

    d_mix, d_ffn = [None] * depth, [None] * depth
    d_bf, d_gq, d_gk = [None] * n_fox, [None] * n_fox, [None] * n_fox
    mix_scatter, ffn_scatter = [None] * depth, [None] * depth
    pending = jnp.zeros((1, 1), F32)
    for i in reversed(range(depth)):
        j = i // 2
        mix_saved, mid, h2, silu, usilu, act = saved[i]
        dgu = _ffn_dact(f"ffn_dact{i}", dcur, w_dn[i].reshape(half, -1, d), silu, usilu)
        g_dn = _mm_tn(f"ffn_dw_down{i}", act, dcur, BF16, tm=1408).reshape(N_DEV, -1, d)
        sc_dn = _exchange_start(f"scatter_down{i}", *_scatter_plan([g_dn]))
        g_gu = _ffn_dw_gu(f"ffn_dw_up{i}", h2, dgu, dep=sc_dn["token"])
        sc_gu = _exchange_start(f"scatter_up{i}", *_scatter_plan([g_gu]))
        ffn_scatter[i] = (sc_gu, sc_dn)
        g_ffn = ffn_norm_g[i][None] + pending
        dmid, d_ffn[i] = _ffn_dh(f"ffn_dh{i}", dgu, w_gu_g[i], mid, g_ffn, dcur, dep=sc_gu["token"])
        gm = mix_norm_g[i][None]
        if i % 2 == 0:
            xin, h, proj, flog, qn, kn, vb, c_col, c_row, o, lse = mix_saved
            g_out = _mm_tn(f"proj_out_dw{i}", o, dmid, BF16).reshape(N_DEV, dsh, d)
            sc_out = _exchange_start(f"scatter_out{i}", *_scatter_plan([g_out]))
            do = _mm_nt(f"proj_out_dx{i}", dmid, w_out[j], BF16, dep=sc_out["token"])
            delta = _attn_delta(f"attn_delta{i}", o, do, n_heads)
            delta_row = delta[:, :n_heads].T[:, None, :]
            dqn, dkn, dv, dck, dcq = _attn_bwd(f"attn_bwd{i}", qn, kn, vb, do, c_row,
                                               lse, delta_row, c_col)
            lane_pad = ((0, 0), (0, LANES - n_heads))
            dflog, d_bf[j] = _gate_bwd(f"gate_bwd{i}", jnp.pad(dck[:, :, 0].T, lane_pad),
                                       jnp.pad(dcq[:, 0, :].T, lane_pad), flog, b_pad[j], n_heads)
            gq, gk = fox_q_norm_g[j][None], fox_k_norm_g[j][None]
            dproj, d_gq[j], d_gk[j] = _qkv_bwd(f"qk_norm_bwd{i}", proj, dqn, dkn, dv, dflog, gq, gk, d, n_pad)
            dw_in = _mm_tn(f"proj_in_dw{i}", h, dproj, BF16, tn=896)
            g_in = jnp.transpose(dw_in[:, :n_in].reshape(d, N_DEV, n_in // N_DEV), (1, 0, 2))
            sc_in = _exchange_start(f"scatter_in{i}", *_scatter_plan([g_in]))
            mix_scatter[i] = (sc_in, sc_out)
            dcur, d_mix[i] = _proj_in_dx(f"proj_in_dx{i}", dproj, w_in[j], xin, gm, dmid, dep=sc_in["token"])
        else:
            xin, y, zb = mix_saved
            dcur, dz, d_mix[i], dsc, db = _pool_bwd(f"pool_bwd{i}", dmid, xin, zb, gm, w_pool[j], pool_s_full[j])
            dwp = _pool_dw(f"pool_dw{i}", y, dz, n_groups)
            dg = d // n_groups
            g_pw = jnp.transpose(dwp.reshape(n_groups, N_DEV, dg // N_DEV, dg), (1, 0, 2, 3)).astype(BF16)
            g_pbs = jnp.stack([db.reshape(N_DEV, dsh), dsc.reshape(N_DEV, dsh)], axis=1)
            sc_pool = _exchange_start(f"scatter_pool{i}", *_scatter_plan([g_pw, g_pbs]))
            mix_scatter[i] = (sc_pool,)
            pending = sc_pool["token"][:1, :1]
    grad_x = dcur[None]

    mix_landed = [sum((_exchange_wait(hd, dcur) for hd in mix_scatter[l]), []) for l in range(depth)]
    landed = [sum((_exchange_wait(hd, dcur) for hd in ffn_scatter[l]), []) for l in range(depth)]
    r_in, r_out = [t[0] for t in mix_landed[0::2]], [t[1] for t in mix_landed[0::2]]
    r_pw = [t[0].reshape(N_DEV, -1, t[0].shape[-1]) for t in mix_landed[1::2]]
    r_pbs = [t[1] for t in mix_landed[1::2]]
    r_gu, r_dn = [t[0] for t in landed], [t[1] for t in landed]
    upd = {}
    upd["fox_w_in"] = _adamw_layers("adamw_w_in", fox_w_in, m_fox_w_in, v_fox_w_in, r_in)
    upd["fox_w_out"] = _adamw_layers("adamw_w_out", fox_w_out, m_fox_w_out, v_fox_w_out, r_out)
    fold = lambda a: a.reshape(n_pool, -1, a.shape[-1])
    upd["pool_w"] = [o.reshape(pool_w.shape) for o in
                     _adamw_layers("adamw_pool_w", fold(pool_w), fold(m_pool_w), fold(v_pool_w), r_pw)]
    pbs = _adamw_layers("adamw_pool_bs", pool_bs, jnp.stack([m_pool_b, m_pool_scale], axis=1),
                        jnp.stack([v_pool_b, v_pool_scale], axis=1), r_pbs)
    upd["pool_b"] = [o[:, 0] for o in pbs]
    upd["pool_scale"] = [o[:, 1] for o in pbs]
    upd["ffn_w_gate_up"] = _adamw_layers("adamw_gate_up", ffn_w_gate_up, m_ffn_w_gate_up, v_ffn_w_gate_up, r_gu)
    upd["ffn_w_down"] = _adamw_layers("adamw_down", ffn_w_down, m_ffn_w_down, v_ffn_w_down, r_dn)

    small_w = (mix_norm_g, ffn_norm_g, fox_b_f, fox_q_norm_g, fox_k_norm_g)
    small_g = _pack_small(jnp.concatenate(d_mix), jnp.concatenate(d_ffn),
                          jnp.concatenate(d_bf)[:, :n_heads], jnp.concatenate(d_gq), jnp.concatenate(d_gk))
    (small_pieces,), = _all_gather_layers("gather_small", [small_g[None]])
    small = _adamw("adamw_small", _pack_small(*small_w),
                   _pack_small(m_mix_norm_g, m_ffn_norm_g, m_fox_b_f, m_fox_q_norm_g, m_fox_k_norm_g),
                   _pack_small(v_mix_norm_g, v_ffn_norm_g, v_fox_b_f, v_fox_q_norm_g, v_fox_k_norm_g),
                   small_pieces)
    small = [_unpack_small(o, *small_w) for o in small]
    for n, name in enumerate(("mix_norm_g", "ffn_norm_g", "fox_b_f", "fox_q_norm_g", "fox_k_norm_g")):
        upd[name] = [o[n] for o in small]

    order = ("mix_norm_g", "ffn_norm_g", "fox_w_in", "fox_b_f", "fox_q_norm_g", "fox_k_norm_g", "fox_w_out",
             "pool_w", "pool_b", "pool_scale", "ffn_w_gate_up", "ffn_w_down")
    return (loss, grad_x) + tuple(upd[name][q] for q in range(4) for name in order)
```

```python
import functools

import jax
import jax.numpy as jnp
from jax import lax
from jax.experimental import pallas as pl
from jax.experimental.pallas import tpu as pltpu

F32 = jnp.float32
BF16 = jnp.bfloat16
MESH = pl.DeviceIdType.MESH

N_DEV = 8
HEAD_DIM = 128
LANES = 128
POOL_WINDOWS = (2, 4, 8, 16)
POOL_HALO = 16
RMS_EPS = 1e-6
NEG_INF = -1e30
ADAM_LR = 0.001
ADAM_B1 = 0.9
ADAM_B2 = 0.999
ADAM_EPS = 1e-08
ADAM_WD = 0.01
ADAM_STEP = 10
VMEM_LIMIT = 52 * 1024 * 1024

NN = (((1,), (0,)), ((), ()))
NT = (((1,), (1,)), ((), ()))
TN = (((0,), (0,)), ((), ()))


def _tile(n, pref, align):
    best = None
    d = align
    while d <= min(n, pref):
        if n % d == 0:
            best = d
        d += align
    return n if best is None else best


def _params(*sem):
    return pltpu.CompilerParams(dimension_semantics=sem, vmem_limit_bytes=VMEM_LIMIT)


def _position():
    x, y, c = lax.axis_index("x"), lax.axis_index("y"), lax.axis_index("c")
    return x, y, c, 4 * x + 2 * y + c


def _peer(x, y, c, k):
    px = 1 - x if k & 4 else x
    py = 1 - y if k & 2 else y
    pc = 1 - c if k & 1 else c
    return (px, py, pc), 4 * px + 2 * py + pc


def _exchange(name, ins, out_shapes, copies):
    n_in, n_cp = len(ins), len(copies)

    def body(*refs):
        in_refs = refs[:n_in]
        out_refs = refs[n_in:n_in + len(out_shapes)]
        send_sems, recv_sems, loc_sems = refs[n_in + len(out_shapes):]
        x, y, c, me = _position()
        local = []
        for ci, (ii, src_of, oi, dst_of) in enumerate(copies):
            cp = pltpu.make_async_copy(src_of(in_refs[ii], me), dst_of(out_refs[oi], me), loc_sems.at[ci])
            cp.start()
            local.append(cp)
        sends, recvs = [], []
        for k in range(1, N_DEV):
            pid, p = _peer(x, y, c, k)
            for ci, (ii, src_of, oi, dst_of) in enumerate(copies):
                sem = ci * (N_DEV - 1) + k - 1
                send = pltpu.make_async_remote_copy(
                    src_ref=src_of(in_refs[ii], p), dst_ref=dst_of(out_refs[oi], me),
                    send_sem=send_sems.at[sem], recv_sem=recv_sems.at[sem],
                    device_id=pid, device_id_type=MESH)
                send.start()
                sends.append(send)
                recvs.append(pltpu.make_async_remote_copy(
                    src_ref=src_of(in_refs[ii], p), dst_ref=dst_of(out_refs[oi], p),
                    send_sem=send_sems.at[sem], recv_sem=recv_sems.at[sem],
                    device_id=pid, device_id_type=MESH))
        for r in recvs:
            r.wait_recv()
        for s in sends:
            s.wait_send()
        for cp in local:
            cp.wait()

    any_spec = pl.BlockSpec(memory_space=pl.ANY)
    return pl.pallas_call(
        body, name=name,
        out_shape=tuple(out_shapes),
        in_specs=[any_spec] * n_in,
        out_specs=tuple([any_spec] * len(out_shapes)),
        scratch_shapes=[pltpu.SemaphoreType.DMA((n_cp * (N_DEV - 1),)),
                        pltpu.SemaphoreType.DMA((n_cp * (N_DEV - 1),)),
                        pltpu.SemaphoreType.DMA((n_cp,))],
    )(*ins)


def _exchange_start(name, ins, out_shapes, copies, dep=None):
    n_in, n_out, n_cp = len(ins), len(out_shapes), len(copies)

    def body(*refs):
        in_refs = refs[:n_in]
        land_refs = refs[n_in:n_in + n_out]
        outs = refs[n_in + n_out + (dep is not None):]
        send_sems, recv_sems = outs[:2]
        token_ref, loc_sems = outs[n_in + n_out + 2], outs[n_in + n_out + 3]
        x, y, c, me = _position()
        local = []
        for ci, (ii, src_of, oi, dst_of) in enumerate(copies):
            cp = pltpu.make_async_copy(src_of(in_refs[ii], me), dst_of(land_refs[oi], me), loc_sems.at[ci])
            cp.start()
            local.append(cp)
        for cp in local:
            cp.wait()
        for k in range(1, N_DEV):
            pid, p = _peer(x, y, c, k)
            for ci, (ii, src_of, oi, dst_of) in enumerate(copies):
                sem = ci * (N_DEV - 1) + k - 1
                pltpu.make_async_remote_copy(
                    src_ref=src_of(in_refs[ii], p), dst_ref=dst_of(land_refs[oi], me),
                    send_sem=send_sems.at[sem], recv_sem=recv_sems.at[sem],
                    device_id=pid, device_id_type=MESH).start()
        token_ref[...] = jnp.zeros_like(token_ref)

    hbm = pl.BlockSpec(memory_space=pltpu.HBM)
    sem = pl.BlockSpec(memory_space=pltpu.SEMAPHORE)
    n_sem = n_cp * (N_DEV - 1)
    lands = [pltpu.with_memory_space_constraint(lax.empty(o.shape, o.dtype), pltpu.HBM) for o in out_shapes]
    srcs = [pltpu.with_memory_space_constraint(a, pltpu.HBM) for a in ins]
    res = pl.pallas_call(
        body, name=name,
        out_shape=(pltpu.SemaphoreType.DMA((n_sem,)), pltpu.SemaphoreType.DMA((n_sem,)),
                   *[pltpu.HBM(a.shape, a.dtype) for a in ins],
                   *[pltpu.HBM(o.shape, o.dtype) for o in out_shapes],
                   jax.ShapeDtypeStruct((8, LANES), F32)),
        in_specs=[hbm] * (n_in + n_out) + ([pl.BlockSpec(memory_space=pl.ANY)] if dep is not None else []),
        out_specs=(sem, sem, *([hbm] * (n_in + n_out)), pl.BlockSpec(memory_space=pltpu.VMEM)),
        input_output_aliases={i: 2 + i for i in range(n_in + n_out)},
        scratch_shapes=[pltpu.SemaphoreType.DMA((n_cp,))],
        compiler_params=pltpu.CompilerParams(has_side_effects=pltpu.SideEffectType.DATAFLOW_SIDE_EFFECTING),
    )(*srcs, *lands, *([dep] if dep is not None else []))
    return dict(name=name, copies=copies, send=res[0], recv=res[1], srcs=list(res[2:2 + n_in]),
                lands=list(res[2 + n_in:2 + n_in + n_out]), token=res[-1])


def _exchange_wait(handle, after):
    copies, srcs, lands = handle["copies"], handle["srcs"], handle["lands"]
    n_in, n_out = len(srcs), len(lands)

    def body(*refs):
        in_refs = refs[:n_in]
        land_refs = refs[n_in:n_in + n_out]
        send_sems, recv_sems = refs[n_in + n_out:n_in + n_out + 2]
        x, y, c, me = _position()
        waits = []
        for k in range(1, N_DEV):
            pid, p = _peer(x, y, c, k)
            for ci, (ii, src_of, oi, dst_of) in enumerate(copies):
                sem = ci * (N_DEV - 1) + k - 1
                waits.append(pltpu.make_async_remote_copy(
                    src_ref=src_of(in_refs[ii], p), dst_ref=dst_of(land_refs[oi], p),
                    send_sem=send_sems.at[sem], recv_sem=recv_sems.at[sem],
                    device_id=pid, device_id_type=MESH))
        for w in waits:
            w.wait_send()
        for w in waits:
            w.wait_recv()

    hbm = pl.BlockSpec(memory_space=pltpu.HBM)
    sem = pl.BlockSpec(memory_space=pltpu.SEMAPHORE)
    res = pl.pallas_call(
        body, name=handle["name"] + "_wait",
        out_shape=tuple(pltpu.HBM(a.shape, a.dtype) for a in srcs + lands),
        in_specs=[hbm] * (n_in + n_out) + [sem, sem, pl.BlockSpec(memory_space=pl.ANY)],
        out_specs=tuple([hbm] * (n_in + n_out)),
        input_output_aliases={i: i for i in range(n_in + n_out)},
        compiler_params=pltpu.CompilerParams(has_side_effects=pltpu.SideEffectType.DATAFLOW_SIDE_EFFECTING),
    )(*srcs, *lands, handle["send"], handle["recv"], after)
    return list(res[n_in:])


def _gather_plan(stacked):
    ins, outs, copies = [], [], []
    for t in stacked:
        ii = len(ins)
        ins.append(t)
        for l in range(t.shape[0]):
            oi = len(outs)
            outs.append(jax.ShapeDtypeStruct((N_DEV,) + t.shape[1:], t.dtype))
            copies.append((ii, (lambda ref, p, l=l: ref.at[l]), oi, (lambda ref, s: ref.at[s])))
    return ins, outs, copies


def _scatter_plan(blocked):
    outs = [jax.ShapeDtypeStruct(t.shape, t.dtype) for t in blocked]
    copies = [(n, (lambda ref, p: ref.at[p]), n, (lambda ref, s: ref.at[s])) for n in range(len(blocked))]
    return list(blocked), outs, copies


def _all_gather_layers(name, stacked):
    ins, outs, copies = [], [], []
    for t in stacked:
        ii = len(ins)
        ins.append(t)
        for l in range(t.shape[0]):
            oi = len(outs)
            outs.append(jax.ShapeDtypeStruct((N_DEV,) + t.shape[1:], t.dtype))
            copies.append((ii, (lambda ref, p, l=l: ref.at[l]), oi, (lambda ref, s: ref.at[s])))
    res = _exchange(name, ins, outs, copies)
    out, pos = [], 0
    for t in stacked:
        out.append(list(res[pos:pos + t.shape[0]]))
        pos += t.shape[0]
    return out


def _mm(name, mode, a, b, out_shape, *, grid, a_spec, b_spec, o_spec, acc_shape, add=None, add_spec=None, dep=None):
    nk = grid[2]
    dn = {"nn": NN, "nt": NT, "tn": TN}[mode]
    has_add, has_dep = add is not None, dep is not None
    own_acc = nk > 1 and out_shape.dtype != F32

    def body(*refs):
        a_ref, b_ref = refs[:2]
        add_ref = refs[2] if has_add else None
        o_ref = refs[2 + has_add + has_dep]

        def product():
            return lax.dot_general(a_ref[...].astype(BF16), b_ref[...].astype(BF16), dn,
                                   preferred_element_type=F32)

        if nk == 1:
            r = product() + add_ref[...] if has_add else product()
            o_ref[...] = r.astype(o_ref.dtype)
        else:
            acc_ref = refs[-1] if own_acc else o_ref
            k = pl.program_id(2)

            @pl.when(k == 0)
            def _():
                acc_ref[...] = add_ref[...] if has_add else jnp.zeros_like(acc_ref)

            acc_ref[...] += product()
            if own_acc:
                @pl.when(k == nk - 1)
                def _():
                    o_ref[...] = acc_ref[...].astype(o_ref.dtype)

    ins = [a, b] + ([add] if has_add else []) + ([dep] if has_dep else [])
    in_specs = ([a_spec, b_spec] + ([add_spec] if has_add else [])
                + ([pl.BlockSpec(memory_space=pl.ANY)] if has_dep else []))
    scratch = [pltpu.VMEM(acc_shape, F32)] if own_acc else []
    return pl.pallas_call(
        body, name=name, grid=grid, out_shape=out_shape,
        in_specs=in_specs, out_specs=o_spec, scratch_shapes=scratch,
        compiler_params=_params("parallel", "parallel", "arbitrary"),
    )(*ins)


def _mm_nn(name, a, b, out_dtype, add=None, tm=1024, tn=1024, tk=2048):
    m, kd = a.shape
    n = b.shape[1]
    tm, tn, tk = _tile(m, tm, 16), _tile(n, tn, LANES), _tile(kd, tk, LANES)
    return _mm(name, "nn", a, b, jax.ShapeDtypeStruct((m, n), out_dtype),
               grid=(m // tm, n // tn, kd // tk),
               a_spec=pl.BlockSpec((tm, tk), lambda i, j, k: (i, k)),
               b_spec=pl.BlockSpec((tk, tn), lambda i, j, k: (k, j)),
               o_spec=pl.BlockSpec((tm, tn), lambda i, j, k: (i, j)),
               acc_shape=(tm, tn), add=add,
               add_spec=pl.BlockSpec((tm, tn), lambda i, j, k: (i, j)))


def _mm_nt(name, a, b, out_dtype, tm=1024, tn=1024, tk=2048, dep=None):
    m, kd = a.shape
    n = b.shape[0]
    tm, tn, tk = _tile(m, tm, 16), _tile(n, tn, LANES), _tile(kd, tk, LANES)
    return _mm(name, "nt", a, b, jax.ShapeDtypeStruct((m, n), out_dtype),
               grid=(m // tm, n // tn, kd // tk),
               a_spec=pl.BlockSpec((tm, tk), lambda i, j, k: (i, k)),
               b_spec=pl.BlockSpec((tn, tk), lambda i, j, k: (j, k)),
               o_spec=pl.BlockSpec((tm, tn), lambda i, j, k: (i, j)),
               acc_shape=(tm, tn), dep=dep)


def _mm_tn(name, a, b, out_dtype, tm=1024, tn=1024, ts=2048, dep=None):
    s, m = a.shape
    n = b.shape[1]
    tm, tn, ts = _tile(m, tm, LANES), _tile(n, tn, LANES), _tile(s, ts, 16)
    return _mm(name, "tn", a, b, jax.ShapeDtypeStruct((m, n), out_dtype),
               grid=(m // tm, n // tn, s // ts),
               a_spec=pl.BlockSpec((ts, tm), lambda i, j, k: (k, i)),
               b_spec=pl.BlockSpec((ts, tn), lambda i, j, k: (k, j)),
               o_spec=pl.BlockSpec((tm, tn), lambda i, j, k: (i, j)),
               acc_shape=(tm, tn), dep=dep)


def _rms_fwd(name, x, g):
    s, d = x.shape
    tm = _tile(s, 512, 16)

    def body(x_ref, g_ref, h_ref):
        xv = x_ref[...]
        r = lax.rsqrt(jnp.mean(xv * xv, axis=-1, keepdims=True) + RMS_EPS)
        h_ref[...] = ((xv * r) * g_ref[...]).astype(BF16)

    return pl.pallas_call(
        body, name=name, grid=(s // tm,), out_shape=jax.ShapeDtypeStruct((s, d), BF16),
        in_specs=[pl.BlockSpec((tm, d), lambda i: (i, 0)), pl.BlockSpec((1, d), lambda i: (0, 0))],
        out_specs=pl.BlockSpec((tm, d), lambda i: (i, 0)),
        compiler_params=_params("parallel"),
    )(x, g)


def _mm_rms_bwd(name, a, b, x, g, dres, *, tm, nk, a_spec, b_spec, dep=None):
    s, d = x.shape
    has_dep = dep is not None
    ch = _tile(tm, 128, 8)

    def body(*refs):
        a_ref, b_ref, x_ref, g_ref, dres_ref = refs[:5]
        dx_ref, dxb_ref, dg_ref = refs[5 + has_dep:8 + has_dep]
        i, k = pl.program_id(0), pl.program_id(1)

        @pl.when(k == 0)
        def _():
            dx_ref[...] = jnp.zeros_like(dx_ref)

        dx_ref[...] += lax.dot_general(a_ref[...].astype(BF16), b_ref[...].astype(BF16), NT,
                                       preferred_element_type=F32)

        @pl.when(k == nk - 1)
        def _():
            def rows_bwd(c, part):
                rows = pl.ds(pl.multiple_of(c * ch, ch), ch)
                dhv = dx_ref[rows, :]
                xv = x_ref[rows, :]
                r = lax.rsqrt(jnp.mean(xv * xv, axis=-1, keepdims=True) + RMS_EPS)
                xhat = xv * r
                gdh = dhv * g_ref[...]
                dxv = dres_ref[rows, :] + r * (gdh - xhat * jnp.mean(gdh * xhat, axis=-1, keepdims=True))
                dx_ref[rows, :] = dxv
                dxb_ref[rows, :] = dxv.astype(BF16)
                return part + jnp.sum(dhv * xhat, axis=0, keepdims=True)

            part = lax.fori_loop(0, tm // ch, rows_bwd, jnp.zeros((1, d), F32))

            @pl.when(i == 0)
            def _():
                dg_ref[...] = part

            @pl.when(i > 0)
            def _():
                dg_ref[...] += part

    row = pl.BlockSpec((tm, d), lambda i, k: (i, 0))
    vec = pl.BlockSpec((1, d), lambda i, k: (0, 0))
    return pl.pallas_call(
        body, name=name, grid=(s // tm, nk),
        out_shape=(jax.ShapeDtypeStruct((s, d), F32), jax.ShapeDtypeStruct((s, d), BF16),
                   jax.ShapeDtypeStruct((1, d), F32)),
        in_specs=[a_spec, b_spec, row, vec, row] + ([pl.BlockSpec(memory_space=pl.ANY)] if has_dep else []),
        out_specs=(row, row, vec),
        compiler_params=_params("arbitrary", "arbitrary"),
    )(a, b, x, g, dres, *([dep] if has_dep else []))


def _split3(v):
    hi = v.astype(BF16)
    r1 = v - hi.astype(F32)
    mid = r1.astype(BF16)
    lo = (r1 - mid.astype(F32)).astype(BF16)
    return hi, mid, lo


def _tri_sum(tri, v):
    hi, mid, lo = _split3(v)
    dot = functools.partial(lax.dot_general, dimension_numbers=NN, preferred_element_type=F32)
    return dot(tri, hi) + dot(tri, mid) + dot(tri, lo)


def _gate_fwd(name, flog, b_pad):
    s = flog.shape[0]
    tb = _tile(s, 256, 16)

    def body(f_ref, b_ref, c_ref, carry_ref):
        i = pl.program_id(0)

        @pl.when(i == 0)
        def _():
            carry_ref[...] = jnp.zeros_like(carry_ref)

        z = f_ref[...] + b_ref[...]
        lf = jnp.minimum(z, 0.0) - jnp.log(1.0 + jnp.exp(-jnp.abs(z)))
        rows = lax.broadcasted_iota(jnp.int32, (tb, tb), 0)
        cols = lax.broadcasted_iota(jnp.int32, (tb, tb), 1)
        tri = (rows >= cols).astype(BF16)
        c_ref[...] = _tri_sum(tri, lf) + carry_ref[...]
        carry_ref[...] = c_ref[pl.ds(tb - 1, 1), :]

    return pl.pallas_call(
        body, name=name, grid=(s // tb,), out_shape=jax.ShapeDtypeStruct((s, LANES), F32),
        in_specs=[pl.BlockSpec((tb, LANES), lambda i: (i, 0)), pl.BlockSpec((1, LANES), lambda i: (0, 0))],
        out_specs=pl.BlockSpec((tb, LANES), lambda i: (i, 0)),
        scratch_shapes=[pltpu.VMEM((1, LANES), F32)],
        compiler_params=_params("arbitrary"),
    )(flog, b_pad)


def _gate_bwd(name, dck, dcq, flog, b_pad, n_heads):
    s = flog.shape[0]
    tb = _tile(s, 256, 16)
    nb = s // tb

    def body(dck_ref, dcq_ref, f_ref, b_ref, df_ref, db_ref, carry_ref, tmp_ref):
        i = pl.program_id(0)

        @pl.when(i == 0)
        def _():
            carry_ref[...] = jnp.zeros_like(carry_ref)

        rows = lax.broadcasted_iota(jnp.int32, (tb, tb), 0)
        cols = lax.broadcasted_iota(jnp.int32, (tb, tb), 1)
        tri = (rows <= cols).astype(BF16)
        tmp_ref[...] = _tri_sum(tri, dck_ref[...] + dcq_ref[...]) + carry_ref[...]
        carry_ref[...] = tmp_ref[pl.ds(0, 1), :]
        z = f_ref[...] + b_ref[...]
        lane = lax.broadcasted_iota(jnp.int32, (tb, LANES), 1)
        df = jnp.where(lane < n_heads, tmp_ref[...] / (1.0 + jnp.exp(z)), 0.0)
        df_ref[...] = df.astype(BF16)
        part = jnp.sum(df, axis=0, keepdims=True)

        @pl.when(i == 0)
        def _():
            db_ref[...] = part

        @pl.when(i > 0)
        def _():
            db_ref[...] += part

    rev = pl.BlockSpec((tb, LANES), lambda i: (nb - 1 - i, 0))
    vec = pl.BlockSpec((1, LANES), lambda i: (0, 0))
    return pl.pallas_call(
        body, name=name, grid=(nb,),
        out_shape=(jax.ShapeDtypeStruct((s, LANES), BF16), jax.ShapeDtypeStruct((1, LANES), F32)),
        in_specs=[rev, rev, rev, vec], out_specs=(rev, vec),
        scratch_shapes=[pltpu.VMEM((1, LANES), F32), pltpu.VMEM((tb, LANES), F32)],
        compiler_params=_params("arbitrary"),
    )(dck, dcq, flog, b_pad)


def _head_rms(v, g):
    r = lax.rsqrt(jnp.mean(v * v, axis=-1, keepdims=True) + RMS_EPS)
    return (v * r) * g


def _qkv_fwd(name, proj, gq, gk, d):
    s = proj.shape[0]
    tm = _tile(s, 256, 16)
    n_heads = d // HEAD_DIM

    def body(q_ref, k_ref, v_ref, gq_ref, gk_ref, qn_ref, kn_ref, vb_ref):
        for h in range(n_heads):
            sl = slice(h * HEAD_DIM, (h + 1) * HEAD_DIM)
            qn_ref[:, sl] = _head_rms(q_ref[:, sl], gq_ref[...]).astype(BF16)
            kn_ref[:, sl] = _head_rms(k_ref[:, sl], gk_ref[...]).astype(BF16)
        vb_ref[...] = v_ref[...].astype(BF16)

    col = lambda c: pl.BlockSpec((tm, d), lambda i, c=c: (i, c))
    vec = pl.BlockSpec((1, HEAD_DIM), lambda i: (0, 0))
    out = jax.ShapeDtypeStruct((s, d), BF16)
    return pl.pallas_call(
        body, name=name, grid=(s // tm,), out_shape=(out, out, out),
        in_specs=[col(0), col(1), col(2), vec, vec], out_specs=(col(0), col(0), col(0)),
        compiler_params=_params("parallel"),
    )(proj, proj, proj, gq, gk)


def _qkv_bwd(name, proj, dqn, dkn, dv, dflog, gq, gk, d, n_pad):
    s = proj.shape[0]
    tm = _tile(s, 256, 16)
    n_heads = d // HEAD_DIM

    def head_bwd(raw, dy, g):
        r = lax.rsqrt(jnp.mean(raw * raw, axis=-1, keepdims=True) + RMS_EPS)
        hat = raw * r
        gdy = dy * g
        dx = r * (gdy - hat * jnp.mean(gdy * hat, axis=-1, keepdims=True))
        return dx, jnp.sum(dy * hat, axis=0, keepdims=True)

    def body(q_ref, k_ref, dqn_ref, dkn_ref, dv_ref, df_ref, gq_ref, gk_ref, dp_ref, dgq_ref, dgk_ref):
        i = pl.program_id(0)
        accq = jnp.zeros((1, HEAD_DIM), F32)
        acck = jnp.zeros((1, HEAD_DIM), F32)
        for h in range(n_heads):
            sl = slice(h * HEAD_DIM, (h + 1) * HEAD_DIM)
            dq, pq = head_bwd(q_ref[:, sl], dqn_ref[:, sl], gq_ref[...])
            dk, pk = head_bwd(k_ref[:, sl], dkn_ref[:, sl], gk_ref[...])
            dp_ref[:, sl] = dq.astype(BF16)
            dp_ref[:, d + h * HEAD_DIM:d + (h + 1) * HEAD_DIM] = dk.astype(BF16)
            accq, acck = accq + pq, acck + pk
        dp_ref[:, 2 * d:3 * d] = dv_ref[...]
        dp_ref[:, 3 * d:] = df_ref[...]

        @pl.when(i == 0)
        def _():
            dgq_ref[...] = accq
            dgk_ref[...] = acck

        @pl.when(i > 0)
        def _():
            dgq_ref[...] += accq
            dgk_ref[...] += acck

    col = lambda c: pl.BlockSpec((tm, d), lambda i, c=c: (i, c))
    vec = pl.BlockSpec((1, HEAD_DIM), lambda i: (0, 0))
    return pl.pallas_call(
        body, name=name, grid=(s // tm,),
        out_shape=(jax.ShapeDtypeStruct((s, n_pad), BF16), jax.ShapeDtypeStruct((1, HEAD_DIM), F32),
                   jax.ShapeDtypeStruct((1, HEAD_DIM), F32)),
        in_specs=[col(0), col(1), col(0), col(0), col(0), pl.BlockSpec((tm, LANES), lambda i: (i, 0)), vec, vec],
        out_specs=(pl.BlockSpec((tm, n_pad), lambda i: (i, 0)), vec, vec),
        compiler_params=_params("arbitrary"),
    )(proj, proj, dqn, dkn, dv, dflog, gq, gk)


def _attn_fwd(name, qn, kn, vt, c_row, c_col):
    s, d = qn.shape
    n_heads = d // HEAD_DIM
    t = _tile(s, 512, LANES)
    scale = HEAD_DIM ** -0.5

    hp = 2 if n_heads % 2 == 0 else 1
    log2e = 1.4426950408889634

    def body(q_ref, k_ref, vt_ref, cq_ref, ck_ref, o_ref, lse_ref, m_ref, l_ref, acc_ref):
        i = pl.program_id(1)
        m_ref[...] = jnp.full(m_ref.shape, NEG_INF, F32)
        l_ref[...] = jnp.zeros_like(l_ref)
        acc_ref[...] = jnp.zeros_like(acc_ref)

        def step(j, masked):
            start = pl.multiple_of(j * t, t)
            for hh in range(hp):
                sl = slice(hh * HEAD_DIM, (hh + 1) * HEAD_DIM)
                kj = k_ref[pl.ds(start, t), sl]
                vtj = vt_ref[sl, pl.ds(start, t)]
                st = (lax.dot_general(kj, q_ref[:, sl], NT, preferred_element_type=F32) * (scale * log2e)
                      - ck_ref[hh, pl.ds(start, t), :] * log2e)
                if masked:
                    rows = lax.broadcasted_iota(jnp.int32, (t, t), 0)
                    cols = lax.broadcasted_iota(jnp.int32, (t, t), 1)
                    st = jnp.where(cols >= rows, st, NEG_INF)
                m_prev = m_ref[hh]
                m_new = jnp.maximum(m_prev, jnp.max(st, axis=0, keepdims=True))
                pt = jnp.exp2(st - m_new)
                alpha = jnp.exp2(m_prev - m_new)
                l_ref[hh] = alpha * l_ref[hh] + jnp.sum(pt, axis=0, keepdims=True)
                acc_ref[hh] = alpha * acc_ref[hh] + lax.dot_general(
                    vtj, pt.astype(BF16), NN, preferred_element_type=F32)
                m_ref[hh] = m_new

        def loop_body(j, carry):
            step(j, False)
            return carry

        lax.fori_loop(0, i, loop_body, 0)
        step(i, True)
        for hh in range(hp):
            sl = slice(hh * HEAD_DIM, (hh + 1) * HEAD_DIM)
            o_ref[:, sl] = (acc_ref[hh] / l_ref[hh]).T.astype(BF16)
            lse_ref[hh] = (m_ref[hh] + jnp.log2(l_ref[hh])) * (1.0 / log2e) + cq_ref[hh]

    wide = hp * HEAD_DIM
    row_blk = pl.BlockSpec((hp, 1, t), lambda h, i: (h, 0, i))
    return pl.pallas_call(
        body, name=name, grid=(n_heads // hp, s // t),
        out_shape=(jax.ShapeDtypeStruct((s, d), BF16), jax.ShapeDtypeStruct((n_heads, 1, s), F32)),
        in_specs=[pl.BlockSpec((t, wide), lambda h, i: (i, h)),
                  pl.BlockSpec((s, wide), lambda h, i: (0, h)),
                  pl.BlockSpec((wide, s), lambda h, i: (h, 0)),
                  row_blk, pl.BlockSpec((hp, s, 1), lambda h, i: (h, 0, 0))],
        out_specs=(pl.BlockSpec((t, wide), lambda h, i: (i, h)), row_blk),
        scratch_shapes=[pltpu.VMEM((hp, 1, t), F32), pltpu.VMEM((hp, 1, t), F32),
                        pltpu.VMEM((hp, HEAD_DIM, t), F32)],
        compiler_params=_params("parallel", "arbitrary"),
    )(qn, kn, vt, c_row, c_col)


def _attn_delta(name, o, do, n_heads):
    s, d = o.shape
    tm = _tile(s, 256, 16)

    def body(o_ref, do_ref, dl_ref):
        lane = lax.broadcasted_iota(jnp.int32, (tm, LANES), 1)
        acc = jnp.zeros((tm, LANES), F32)
        for h in range(n_heads):
            sl = slice(h * HEAD_DIM, (h + 1) * HEAD_DIM)
            col = jnp.sum(o_ref[:, sl].astype(F32) * do_ref[:, sl].astype(F32), axis=-1, keepdims=True)
            acc = jnp.where(lane == h, col, acc)
        dl_ref[...] = acc

    row = pl.BlockSpec((tm, d), lambda i: (i, 0))
    return pl.pallas_call(
        body, name=name, grid=(s // tm,), out_shape=jax.ShapeDtypeStruct((s, LANES), F32),
        in_specs=[row, row], out_specs=pl.BlockSpec((tm, LANES), lambda i: (i, 0)),
        compiler_params=_params("parallel"),
    )(o, do)


def _attn_bwd(name, qn, kn, vb, do, c_row, lse_row, delta_row, c_col):
    s, d = qn.shape
    n_heads = d // HEAD_DIM
    t = _tile(s, 512, LANES)
    nq = s // t
    scale = HEAD_DIM ** -0.5

    hp = 2 if n_heads % 2 == 0 else 1

    def body(q_ref, do_ref, cr_ref, lse_ref, dl_ref, k_ref, v_ref, ck_ref,
             dq_ref, dk_ref, dv_ref, dc_ref, dcq_ref, dk_acc, dv_acc, dc_acc):
        j = pl.program_id(1)

        @pl.when(j == 0)
        def _():
            dq_ref[...] = jnp.zeros_like(dq_ref)
            dcq_ref[...] = jnp.zeros_like(dcq_ref)

        dk_acc[...] = jnp.zeros_like(dk_acc)
        dv_acc[...] = jnp.zeros_like(dv_acc)
        dc_acc[...] = jnp.zeros_like(dc_acc)

        def step(i, masked):
            start = pl.multiple_of(i * t, t)
            for hh in range(hp):
                sl = slice(hh * HEAD_DIM, (hh + 1) * HEAD_DIM)
                kj = k_ref[:, sl]
                qi = q_ref[pl.ds(start, t), sl]
                doi = do_ref[pl.ds(start, t), sl]
                bias = cr_ref[hh, :, pl.ds(start, t)] - lse_ref[hh, :, pl.ds(start, t)]
                dli = dl_ref[hh, :, pl.ds(start, t)]
                st = lax.dot_general(kj, qi, NT, preferred_element_type=F32) * scale + (bias - ck_ref[hh])
                if masked:
                    rows = lax.broadcasted_iota(jnp.int32, (t, t), 0)
                    cols = lax.broadcasted_iota(jnp.int32, (t, t), 1)
                    st = jnp.where(cols >= rows, st, NEG_INF)
                pt = jnp.exp(st)
                dpt = lax.dot_general(v_ref[:, sl], doi, NT, preferred_element_type=F32)
                dst = pt * (dpt - dli)
                dsb = dst.astype(BF16)
                dv_acc[:, sl] += lax.dot_general(pt.astype(BF16), doi, NN, preferred_element_type=F32)
                dk_acc[:, sl] += lax.dot_general(dsb, qi, NN, preferred_element_type=F32)
                dq_ref[pl.ds(start, t), sl] += lax.dot_general(dsb, kj, TN, preferred_element_type=F32) * scale
                dc_acc[hh] += jnp.sum(dst, axis=1, keepdims=True)
                dcq_ref[hh, :, pl.ds(start, t)] += jnp.sum(dst, axis=0, keepdims=True)

        step(j, True)

        def loop_body(i, carry):
            step(i, False)
            return carry

        lax.fori_loop(j + 1, nq, loop_body, 0)
        dk_ref[...] = dk_acc[...] * scale
        dv_ref[...] = dv_acc[...].astype(BF16)
        dc_ref[...] = -dc_acc[...]

    wide = hp * HEAD_DIM
    head_all = pl.BlockSpec((s, wide), lambda h, j: (0, h))
    row_all = pl.BlockSpec((hp, 1, s), lambda h, j: (h, 0, 0))
    blk = pl.BlockSpec((t, wide), lambda h, j: (j, h))
    col_blk = pl.BlockSpec((hp, t, 1), lambda h, j: (h, j, 0))
    return pl.pallas_call(
        body, name=name, grid=(n_heads // hp, nq),
        out_shape=(jax.ShapeDtypeStruct((s, d), F32), jax.ShapeDtypeStruct((s, d), F32),
                   jax.ShapeDtypeStruct((s, d), BF16), jax.ShapeDtypeStruct((n_heads, s, 1), F32),
                   jax.ShapeDtypeStruct((n_heads, 1, s), F32)),
        in_specs=[head_all, head_all, row_all, row_all, row_all, blk, blk, col_blk],
        out_specs=(head_all, blk, blk, col_blk, row_all),
        scratch_shapes=[pltpu.VMEM((t, wide), F32), pltpu.VMEM((t, wide), F32), pltpu.VMEM((hp, t, 1), F32)],
        compiler_params=_params("parallel", "arbitrary"),
    )(qn, do, c_row, lse_row, delta_row, kn, vb, c_col)


def _ffn_up(name, h, w_gu):
    s, d = h.shape
    fs = w_gu.shape[2]
    half = N_DEV // 2
    tm = _tile(s, 512, 16)

    def body(h_ref, wg_ref, wu_ref, s_ref, us_ref, a_ref):
        hv = h_ref[...]
        g = lax.dot_general(hv, wg_ref[...], NN, preferred_element_type=F32)
        u = lax.dot_general(hv, wu_ref[...], NN, preferred_element_type=F32)
        sig = jax.nn.sigmoid(g)
        silu = g * sig
        s_ref[...] = silu.astype(BF16)
        us_ref[...] = (u * (sig * (1.0 + g * (1.0 - sig)))).astype(BF16)
        a_ref[...] = (silu * u).astype(BF16)

    out = jax.ShapeDtypeStruct((s, half * fs), BF16)
    ospec = pl.BlockSpec((tm, fs), lambda j, i: (i, j))
    return pl.pallas_call(
        body, name=name, grid=(half, s // tm), out_shape=(out, out, out),
        in_specs=[pl.BlockSpec((tm, d), lambda j, i: (i, 0)),
                  pl.BlockSpec((None, d, fs), lambda j, i: (j, 0, 0)),
                  pl.BlockSpec((None, d, fs), lambda j, i: (j + half, 0, 0))],
        out_specs=(ospec, ospec, ospec),
        compiler_params=_params("parallel", "parallel"),
    )(h, w_gu, w_gu)


def _ffn_dact(name, dx, w_dn4, silu, usilu):
    s, d = dx.shape
    half, fs = w_dn4.shape[0], w_dn4.shape[1]
    tm = _tile(s, 512, 16)

    cut = (fs // (2 * LANES)) * LANES

    def body(dx_ref, w_ref, s_ref, us_ref, dgu_ref):
        dxv = dx_ref[...].astype(BF16)
        for lo, hi in ((0, cut), (cut, fs)) if cut else ((0, fs),):
            da = lax.dot_general(dxv, w_ref[lo:hi, :], NT, preferred_element_type=F32)
            dgu_ref[0, :, lo:hi] = (da * us_ref[:, lo:hi].astype(F32)).astype(BF16)
            dgu_ref[1, :, lo:hi] = (da * s_ref[:, lo:hi].astype(F32)).astype(BF16)

    blk = pl.BlockSpec((tm, fs), lambda j, i: (i, j))
    return pl.pallas_call(
        body, name=name, grid=(half, s // tm),
        out_shape=jax.ShapeDtypeStruct((2, s, half * fs), BF16),
        in_specs=[pl.BlockSpec((tm, d), lambda j, i: (i, 0)),
                  pl.BlockSpec((None, fs, d), lambda j, i: (j, 0, 0)), blk, blk],
        out_specs=pl.BlockSpec((2, tm, fs), lambda j, i: (0, i, j)),
        compiler_params=_params("parallel", "parallel"),
    )(dx, w_dn4, silu, usilu)


def _ffn_dw_gu(name, h, dgu, dep=None):
    s, d = h.shape
    half, fs = N_DEV // 2, dgu.shape[2] // (N_DEV // 2)
    tm, ts = _tile(d, 1024, LANES), _tile(s, 2048, 16)
    return _mm(name, "tn", h, dgu, jax.ShapeDtypeStruct((N_DEV, d, fs), BF16),
               grid=(d // tm, N_DEV, s // ts),
               a_spec=pl.BlockSpec((ts, tm), lambda i, j, k: (k, i)),
               b_spec=pl.BlockSpec((None, ts, fs), lambda i, j, k: (j // half, k, j % half)),
               o_spec=pl.BlockSpec((None, tm, fs), lambda i, j, k: (j, i, 0)),
               acc_shape=(tm, fs), dep=dep)


def _ffn_dh(name, dgu, w_gu, x, g, dres, dep=None):
    s = dgu.shape[1]
    d, fs = w_gu.shape[1], w_gu.shape[2]
    half = N_DEV // 2
    tm = _tile(s, 512, 16)
    return _mm_rms_bwd(name, dgu, w_gu, x, g, dres, tm=tm, nk=N_DEV,
                       a_spec=pl.BlockSpec((None, tm, fs), lambda i, k: (k // half, i, k % half)),
                       b_spec=pl.BlockSpec((None, d, fs), lambda i, k: (k, 0, 0)), dep=dep)


def _proj_in_dx(name, dproj, w_in, x, g, dres, dep=None):
    s, n = dproj.shape
    d = w_in.shape[0]
    tm, tk = _tile(s, 512, 16), _tile(n, 896, LANES)
    return _mm_rms_bwd(name, dproj, w_in, x, g, dres, tm=tm, nk=n // tk,
                       a_spec=pl.BlockSpec((tm, tk), lambda i, k: (i, k)),
                       b_spec=pl.BlockSpec((d, tk), lambda i, k: (0, k)), dep=dep)


def _pool_fwd(name, x, g, w, b, sc):
    s, d = x.shape
    dg = d // len(POOL_WINDOWS)
    tm = _tile(s, 256, POOL_HALO)
    per = tm // POOL_HALO

    def body(x_ref, xh_ref, g_ref, w_ref, b_ref, sc_ref, xo_ref, y_ref, zb_ref):
        i = pl.program_id(0)
        gv = g_ref[...]

        def norm(v):
            return (v * lax.rsqrt(jnp.mean(v * v, axis=-1, keepdims=True) + RMS_EPS)) * gv

        h = norm(x_ref[...])
        halo = norm(xh_ref[...]) * (i > 0).astype(F32)
        ext = jnp.concatenate([halo, h], axis=0)
        t = i * tm + lax.broadcasted_iota(jnp.int32, (tm, 1), 0)
        for gi, win in enumerate(POOL_WINDOWS):
            sl = slice(gi * dg, (gi + 1) * dg)
            acc = ext[:, sl]
            step = 1
            while step < win:
                acc = acc + pltpu.roll(acc, step, 0)
                step *= 2
            inv = 1.0 / jnp.minimum(t + 1, win).astype(F32)
            yg = (acc[POOL_HALO:, :] * inv - h[:, sl]).astype(BF16)
            y_ref[:, sl] = yg
            zb = lax.dot_general(yg, w_ref[gi], NN, preferred_element_type=F32) + b_ref[:, sl]
            zb_ref[:, sl] = zb
            xo_ref[:, sl] = x_ref[:, sl] + zb * sc_ref[:, sl]

    row = pl.BlockSpec((tm, d), lambda i: (i, 0))
    vec = pl.BlockSpec((1, d), lambda i: (0, 0))
    return pl.pallas_call(
        body, name=name, grid=(s // tm,),
        out_shape=(jax.ShapeDtypeStruct((s, d), F32), jax.ShapeDtypeStruct((s, d), BF16),
                   jax.ShapeDtypeStruct((s, d), F32)),
        in_specs=[row, pl.BlockSpec((POOL_HALO, d), lambda i: (jnp.maximum(i * per - 1, 0), 0)),
                  vec, pl.BlockSpec(w.shape, lambda i: (0, 0, 0)), vec, vec],
        out_specs=(row, row, row),
        compiler_params=_params("parallel"),
    )(x, x, g, w, b, sc)


def _pool_bwd(name, dout, x, zb, g, w, sc):
    s, d = x.shape
    dg = d // len(POOL_WINDOWS)
    tm = _tile(s, 256, POOL_HALO)
    per = tm // POOL_HALO
    nb = s // tm
    ext_rows = tm + POOL_HALO

    def body(do_ref, doh_ref, x_ref, zb_ref, g_ref, w_ref, sc_ref, dx_ref, dxb_ref, dz_ref, dgn_ref, dsc_ref, db_ref):
        i = pl.program_id(0)
        scv = sc_ref[...]
        dov = do_ref[...]
        dz = dov * scv
        dz_ref[...] = dz.astype(BF16)
        halo = doh_ref[...] * scv * (i < nb - 1).astype(F32)
        ext = jnp.concatenate([dz, halo], axis=0).astype(BF16)
        t = i * tm + lax.broadcasted_iota(jnp.int32, (ext_rows, 1), 0)
        parts = []
        for gi, win in enumerate(POOL_WINDOWS):
            sl = slice(gi * dg, (gi + 1) * dg)
            dy = lax.dot_general(ext[:, sl], w_ref[gi], NT, preferred_element_type=F32)
            acc = dy * (1.0 / jnp.minimum(t + 1, win).astype(F32))
            step = 1
            while step < win:
                acc = acc + pltpu.roll(acc, ext_rows - step, 0)
                step *= 2
            parts.append(acc[:tm, :] - dy[:tm, :])
        dh = jnp.concatenate(parts, axis=1)
        xv = x_ref[...]
        r = lax.rsqrt(jnp.mean(xv * xv, axis=-1, keepdims=True) + RMS_EPS)
        xhat = xv * r
        gdh = dh * g_ref[...]
        dxv = dov + r * (gdh - xhat * jnp.mean(gdh * xhat, axis=-1, keepdims=True))
        dx_ref[...] = dxv
        dxb_ref[...] = dxv.astype(BF16)
        pgn = jnp.sum(dh * xhat, axis=0, keepdims=True)
        psc = jnp.sum(dov * zb_ref[...], axis=0, keepdims=True)
        pb = jnp.sum(dz, axis=0, keepdims=True)

        @pl.when(i == 0)
        def _():
            dgn_ref[...] = pgn
            dsc_ref[...] = psc
            db_ref[...] = pb

        @pl.when(i > 0)
        def _():
            dgn_ref[...] += pgn
            dsc_ref[...] += psc
            db_ref[...] += pb

    row = pl.BlockSpec((tm, d), lambda i: (i, 0))
    vec = pl.BlockSpec((1, d), lambda i: (0, 0))
    vshape = jax.ShapeDtypeStruct((1, d), F32)
    return pl.pallas_call(
        body, name=name, grid=(nb,),
        out_shape=(jax.ShapeDtypeStruct((s, d), F32), jax.ShapeDtypeStruct((s, d), BF16),
                   jax.ShapeDtypeStruct((s, d), BF16), vshape, vshape, vshape),
        in_specs=[row, pl.BlockSpec((POOL_HALO, d), lambda i: (jnp.minimum((i + 1) * per, s // POOL_HALO - 1), 0)),
                  row, row, vec, pl.BlockSpec(w.shape, lambda i: (0, 0, 0)), vec],
        out_specs=(row, row, row, vec, vec, vec),
        compiler_params=_params("arbitrary"),
    )(dout, dout, x, zb, g, w, sc)


def _pool_dw(name, y, dz, n_groups):
    s, d = y.shape
    dg = d // n_groups
    ts = _tile(s, 1024, 16)
    return _mm(name, "tn", y, dz, jax.ShapeDtypeStruct((n_groups, dg, dg), F32),
               grid=(n_groups, 1, s // ts),
               a_spec=pl.BlockSpec((ts, dg), lambda i, j, k: (k, i)),
               b_spec=pl.BlockSpec((ts, dg), lambda i, j, k: (k, i)),
               o_spec=pl.BlockSpec((None, dg, dg), lambda i, j, k: (i, 0, 0)),
               acc_shape=(dg, dg))


def _loss_head(name, y, tgt):
    s, d = y.shape
    tm = _tile(s, 512, 16)

    def body(y_ref, t_ref, dy_ref, dyb_ref, l_ref):
        i = pl.program_id(0)
        e = y_ref[...] - t_ref[...]
        dy_ref[...] = e * (1.0 / d)
        dyb_ref[...] = (e * (1.0 / d)).astype(BF16)
        part = jnp.sum(jnp.mean(e * e, axis=-1, keepdims=True), axis=0, keepdims=True)
        part = jnp.broadcast_to(part, l_ref.shape)

        @pl.when(i == 0)
        def _():
            l_ref[...] = part

        @pl.when(i > 0)
        def _():
            l_ref[...] += part

    row = pl.BlockSpec((tm, d), lambda i: (i, 0))
    return pl.pallas_call(
        body, name=name, grid=(s // tm,),
        out_shape=(jax.ShapeDtypeStruct((s, d), F32), jax.ShapeDtypeStruct((s, d), BF16),
                   jax.ShapeDtypeStruct((8, LANES), F32)),
        in_specs=[row, row], out_specs=(row, row, pl.BlockSpec((8, LANES), lambda i: (0, 0))),
        compiler_params=_params("arbitrary"),
    )(y, tgt)


def _adam_update(w_ref, m_ref, v_ref, p_ref, g_ref, d_ref, nm_ref, nv_ref):
    g = p_ref[0].astype(F32)
    for k in range(1, N_DEV):
        g = g + p_ref[k].astype(F32)
    mn = ADAM_B1 * m_ref[...] + (1.0 - ADAM_B1) * g
    vn = ADAM_B2 * v_ref[...] + (1.0 - ADAM_B2) * (g * g)
    m_hat = mn / (1.0 - ADAM_B1 ** ADAM_STEP)
    v_hat = vn / (1.0 - ADAM_B2 ** ADAM_STEP)
    g_ref[...] = g
    d_ref[...] = -ADAM_LR * (m_hat / (jnp.sqrt(v_hat) + ADAM_EPS) + ADAM_WD * w_ref[...])
    nm_ref[...] = mn
    nv_ref[...] = vn


def _adamw_layers(name, w, m, v, pieces):
    n_layers, r, c = w.shape
    tr = _tile(r, 128, 16)

    def body(w_ref, m_ref, v_ref, *rest):
        p_refs, outs = rest[:n_layers], rest[n_layers:]
        layer = pl.program_id(0)
        for l in range(n_layers):
            @pl.when(layer == l)
            def _(l=l):
                _adam_update(w_ref, m_ref, v_ref, p_refs[l], *outs)

    blk = pl.BlockSpec((None, tr, c), lambda l, i: (l, i, 0))
    terms = [pl.BlockSpec((N_DEV, tr, c), lambda l, i, n=n: (0, jnp.where(l == n, i, 0), 0))
             for n in range(n_layers)]
    out = jax.ShapeDtypeStruct(w.shape, F32)
    return list(pl.pallas_call(
        body, name=name, grid=(n_layers, r // tr), out_shape=(out, out, out, out),
        in_specs=[blk, blk, blk] + terms, out_specs=(blk, blk, blk, blk),
        compiler_params=_params("parallel", "parallel"),
    )(w, m, v, *pieces))


def _adamw(name, w, m, v, pieces):
    r, c = w.shape
    tr = _tile(r, 128, 16)

    def body(w_ref, m_ref, v_ref, p_ref, g_ref, d_ref, nm_ref, nv_ref):
        _adam_update(w_ref, m_ref, v_ref, p_ref, g_ref, d_ref, nm_ref, nv_ref)

    blk = pl.BlockSpec((tr, c), lambda i: (i, 0))
    out = jax.ShapeDtypeStruct((r, c), F32)
    return pl.pallas_call(
        body, name=name, grid=(r // tr,), out_shape=(out, out, out, out),
        in_specs=[blk, blk, blk, pl.BlockSpec((N_DEV, tr, c), lambda i: (0, i, 0))],
        out_specs=(blk, blk, blk, blk),
        compiler_params=_params("parallel"),
    )(w, m, v, pieces)


def _pack_small(mix, ffn, b_f, gq, gk):
    def rows(a):
        a = a.reshape(-1, LANES) if a.shape[-1] >= LANES else jnp.pad(a, ((0, 0), (0, LANES - a.shape[-1])))
        return jnp.pad(a, ((0, -a.shape[0] % 8), (0, 0)))
    return jnp.concatenate([rows(mix), rows(ffn), rows(b_f), rows(gq), rows(gk)], axis=0)


def _unpack_small(p, mix, ffn, b_f, gq, gk):
    out, pos = [], 0
    for a in (mix, ffn, b_f, gq, gk):
        n = a.size // LANES if a.shape[-1] >= LANES else a.shape[0]
        blk = p[pos:pos + n]
        out.append(blk.reshape(a.shape) if a.shape[-1] >= LANES else blk[:, :a.shape[-1]])
        pos += n + (-n % 8)
    return out


def kernel(x, mix_norm_g, ffn_norm_g, fox_w_in, fox_b_f, fox_q_norm_g, fox_k_norm_g, fox_w_out, pool_w, pool_b, pool_scale, ffn_w_gate_up, ffn_w_down, loss_target, m_mix_norm_g, m_ffn_norm_g, m_fox_w_in, m_fox_b_f, m_fox_q_norm_g, m_fox_k_norm_g, m_fox_w_out, m_pool_w, m_pool_b, m_pool_scale, m_ffn_w_gate_up, m_ffn_w_down, v_mix_norm_g, v_ffn_norm_g, v_fox_w_in, v_fox_b_f, v_fox_q_norm_g, v_fox_k_norm_g, v_fox_w_out, v_pool_w, v_pool_b, v_pool_scale, v_ffn_w_gate_up, v_ffn_w_down):
    xs, tgt = x[0], loss_target[0]
    s, d = xs.shape
    depth = mix_norm_g.shape[0]
    n_fox, n_pool = fox_w_in.shape[0], pool_w.shape[0]
    n_heads = d // HEAD_DIM
    n_in = fox_w_in.shape[2] * N_DEV
    n_pad = 3 * d + LANES
    n_groups = pool_w.shape[1]
    dsh = d // N_DEV
    half = N_DEV // 2
    axes = ("x", "y", "c")

    w_in_bf, w_out_bf, pool_w_bf = fox_w_in.astype(BF16), fox_w_out.astype(BF16), pool_w.astype(BF16)
    gu_bf, dn_bf = ffn_w_gate_up.astype(BF16), ffn_w_down.astype(BF16)
    pool_bs = jnp.stack([pool_b, pool_scale], axis=1)
    mix_gather, ffn_gather = [None] * depth, [None] * depth
    last = None
    for l in range(depth):
        j = l // 2
        shards = [w_in_bf[j:j + 1], w_out_bf[j:j + 1]] if l % 2 == 0 else [pool_w_bf[j:j + 1], pool_bs[j:j + 1]]
        mix_gather[l] = _exchange_start(f"gather_mixer{l}", *_gather_plan(shards), dep=last)
        ffn_gather[l] = _exchange_start(f"gather_ffn{l}", *_gather_plan([gu_bf[l:l + 1], dn_bf[l:l + 1]]),
                                        dep=mix_gather[l]["token"])
        last = ffn_gather[l]["token"]
    started = last[:1, :1]
    w_gu_g, w_dn = [None] * depth, [None] * depth
    w_in, w_out = [None] * n_fox, [None] * n_fox
    w_pool, pool_b_full, pool_s_full = [None] * n_pool, [None] * n_pool, [None] * n_pool
    b_pad =[jnp.pad(fox_b_f[j], (0, LANES - n_heads))[None] for j in range(n_fox)]

    saved = []
    cur = xs
    for i in range(depth):
        j = i // 2
        gm = mix_norm_g[i][None]
        if i == 0:
            gm = gm + started
        if i % 2 == 0:
            w_in_g, w_out_g = _exchange_wait(mix_gather[i], last if i == 0 else cur)
            w_in[j] = jnp.pad(jnp.transpose(w_in_g, (1, 0, 2)).reshape(d, n_in), ((0, 0), (0, n_pad - n_in)))
            w_out[j] = w_out_g.reshape(d, d)
            h = _rms_fwd(f"norm_mix{i}", cur, gm)
            proj =_mm_nn(f"proj_in{i}", h, w_in[j], F32, tn=896)
            gq, gk = fox_q_norm_g[j][None], fox_k_norm_g[j][None]
            qn, kn, vb = _qkv_fwd(f"qk_norm{i}", proj, gq, gk, d)
            flog = proj[:, 3 * d:]
            c = _gate_fwd(f"gate{i}", flog, b_pad[j])
            c_t = c[:, :n_heads].T
            c_col, c_row = c_t[:, :, None], c_t[:, None, :]
            o, lse = _attn_fwd(f"attn{i}", qn, kn, vb.T, c_row, c_col)
            mid = _mm_nn(f"proj_out{i}", o, w_out[j], F32, add=cur)
            mix_saved = (cur, h, proj, flog, qn, kn, vb, c_col, c_row, o, lse)
        else:
            pw_g, pbs_g = _exchange_wait(mix_gather[i], cur)
            w_pool[j] = jnp.transpose(pw_g, (1, 0, 2, 3)).reshape(n_groups, d // n_groups, d // n_groups)
            pbs_full = jnp.transpose(pbs_g, (1, 0, 2)).reshape(2, 1, d)
            pool_b_full[j], pool_s_full[j] = pbs_full[0], pbs_full[1]
            mid, y, zb = _pool_fwd(f"pool{i}", cur, gm, w_pool[j], pool_b_full[j], pool_s_full[j])
            mix_saved = (cur, y, zb)
        h2 = _rms_fwd(f"norm_ffn{i}", mid, ffn_norm_g[i][None])
        w_gu_g[i], dn_g = _exchange_wait(ffn_gather[i], h2)
        w_dn[i] = dn_g.reshape(-1, d)
        silu, usilu, act = _ffn_up(f"ffn_up{i}", h2, w_gu_g[i])
        nxt = _mm_nn(f"ffn_down{i}", act, w_dn[i], F32, add=mid, tk=2816)
        saved.append((mix_saved, mid, h2, silu, usilu, act))
        cur = nxt

    dcur, dcur_b, lpart = _loss_head("loss_head", cur, tgt)
    loss = lax.psum(0.5 * lpart[0, 0], axes)

    d_mix, d_ffn = [None] * depth, [None] * depth
    d_bf, d_gq, d_gk = [None] * n_fox, [None] * n_fox, [None] * n_fox
    mix_scatter, ffn_scatter = [None] * depth, [None] * depth
    pending = jnp.zeros((1, 1), F32)
    for i in reversed(range(depth)):
        j = i // 2
        mix_saved, mid, h2, silu, usilu, act = saved[i]
        dgu = _ffn_dact(f"ffn_dact{i}", dcur_b, w_dn[i].reshape(half, -1, d), silu, usilu)
        g_dn = _mm_tn(f"ffn_dw_down{i}", act, dcur_b, BF16, tm=1408).reshape(N_DEV, -1, d)
        sc_dn = _exchange_start(f"scatter_down{i}", *_scatter_plan([g_dn]))
        g_gu = _ffn_dw_gu(f"ffn_dw_up{i}", h2, dgu, dep=sc_dn["token"])
        sc_gu = _exchange_start(f"scatter_up{i}", *_scatter_plan([g_gu]))
        ffn_scatter[i] = (sc_gu, sc_dn)
        g_ffn = ffn_norm_g[i][None] + pending
        dmid, dmid_b, d_ffn[i] = _ffn_dh(f"ffn_dh{i}", dgu, w_gu_g[i], mid, g_ffn, dcur, dep=sc_gu["token"])
        gm = mix_norm_g[i][None]
        if i % 2 == 0:
            xin, h, proj, flog, qn, kn, vb, c_col, c_row, o, lse = mix_saved
            g_out = _mm_tn(f"proj_out_dw{i}", o, dmid_b, BF16).reshape(N_DEV, dsh, d)
            sc_out = _exchange_start(f"scatter_out{i}", *_scatter_plan([g_out]))
            do = _mm_nt(f"proj_out_dx{i}", dmid_b, w_out[j], BF16, dep=sc_out["token"])
            delta = _attn_delta(f"attn_delta{i}", o, do, n_heads)
            delta_row = delta[:, :n_heads].T[:, None, :]
            dqn, dkn, dv, dck, dcq = _attn_bwd(f"attn_bwd{i}", qn, kn, vb, do, c_row,
                                               lse, delta_row, c_col)
            lane_pad = ((0, 0), (0, LANES - n_heads))
            dflog, d_bf[j] = _gate_bwd(f"gate_bwd{i}", jnp.pad(dck[:, :, 0].T, lane_pad),
                                       jnp.pad(dcq[:, 0, :].T, lane_pad), flog, b_pad[j], n_heads)
            gq, gk = fox_q_norm_g[j][None], fox_k_norm_g[j][None]
            dproj, d_gq[j], d_gk[j] = _qkv_bwd(f"qk_norm_bwd{i}", proj, dqn, dkn, dv, dflog, gq, gk, d, n_pad)
            dw_in = _mm_tn(f"proj_in_dw{i}", h, dproj, BF16, tn=896)
            g_in = jnp.transpose(dw_in[:, :n_in].reshape(d, N_DEV, n_in // N_DEV), (1, 0, 2))
            sc_in = _exchange_start(f"scatter_in{i}", *_scatter_plan([g_in]))
            mix_scatter[i] = (sc_in, sc_out)
            dcur, dcur_b, d_mix[i] = _proj_in_dx(f"proj_in_dx{i}", dproj, w_in[j], xin, gm, dmid, dep=sc_in["token"])
        else:
            xin, y, zb = mix_saved
            dcur, dcur_b, dz, d_mix[i], dsc, db = _pool_bwd(f"pool_bwd{i}", dmid, xin, zb, gm, w_pool[j], pool_s_full[j])
            dwp = _pool_dw(f"pool_dw{i}", y, dz, n_groups)
            dg = d // n_groups
            g_pw = jnp.transpose(dwp.reshape(n_groups, N_DEV, dg // N_DEV, dg), (1, 0, 2, 3)).astype(BF16)
            g_pbs = jnp.stack([db.reshape(N_DEV, dsh), dsc.reshape(N_DEV, dsh)], axis=1)
            sc_pool = _exchange_start(f"scatter_pool{i}", *_scatter_plan([g_pw, g_pbs]))
            mix_scatter[i] = (sc_pool,)
            pending = sc_pool["token"][:1, :1]
    grad_x = dcur[None]

    mix_landed = [sum((_exchange_wait(hd, dcur) for hd in mix_scatter[l]), []) for l in range(depth)]
    landed = [sum((_exchange_wait(hd, dcur) for hd in ffn_scatter[l]), []) for l in range(depth)]
    r_in, r_out = [t[0] for t in mix_landed[0::2]], [t[1] for t in mix_landed[0::2]]
    r_pw = [t[0].reshape(N_DEV, -1, t[0].shape[-1]) for t in mix_landed[1::2]]
    r_pbs = [t[1] for t in mix_landed[1::2]]
    r_gu, r_dn = [t[0] for t in landed], [t[1] for t in landed]
    upd = {}
    upd["fox_w_in"] = _adamw_layers("adamw_w_in", fox_w_in, m_fox_w_in, v_fox_w_in, r_in)
    upd["fox_w_out"] = _adamw_layers("adamw_w_out", fox_w_out, m_fox_w_out, v_fox_w_out, r_out)
    fold = lambda a: a.reshape(n_pool, -1, a.shape[-1])
    upd["pool_w"] = [o.reshape(pool_w.shape) for o in
                     _adamw_layers("adamw_pool_w", fold(pool_w), fold(m_pool_w), fold(v_pool_w), r_pw)]
    pbs = _adamw_layers("adamw_pool_bs", pool_bs, jnp.stack([m_pool_b, m_pool_scale], axis=1),
                        jnp.stack([v_pool_b, v_pool_scale], axis=1), r_pbs)
    upd["pool_b"] = [o[:, 0] for o in pbs]
    upd["pool_scale"] = [o[:, 1] for o in pbs]
    upd["ffn_w_gate_up"] = _adamw_layers("adamw_gate_up", ffn_w_gate_up, m_ffn_w_gate_up, v_ffn_w_gate_up, r_gu)
    upd["ffn_w_down"] = _adamw_layers("adamw_down", ffn_w_down, m_ffn_w_down, v_ffn_w_down, r_dn)

    small_w = (mix_norm_g, ffn_norm_g, fox_b_f, fox_q_norm_g, fox_k_norm_g)
    small_g = _pack_small(jnp.concatenate(d_mix), jnp.concatenate(d_ffn),
                          jnp.concatenate(d_bf)[:, :n_heads], jnp.concatenate(d_gq), jnp.concatenate(d_gk))
    (small_pieces,), = _all_gather_layers("gather_small", [small_g[None]])
    small = _adamw("adamw_small", _pack_small(*small_w),
                   _pack_small(m_mix_norm_g, m_ffn_norm_g, m_fox_b_f, m_fox_q_norm_g, m_fox_k_norm_g),
                   _pack_small(v_mix_norm_g, v_ffn_norm_g, v_fox_b_f, v_fox_q_norm_g, v_fox_k_norm_g),
                   small_pieces)
    small = [_unpack_small(o, *small_w) for o in small]
    for n, name in enumerate(("mix_norm_g", "ffn_norm_g", "fox_b_f", "fox_q_norm_g", "fox_k_norm_g")):
        upd[name] = [o[n] for o in small]

    order = ("mix_norm_g", "ffn_norm_g", "fox_w_in", "fox_b_f", "fox_q_norm_g", "fox_k_norm_g", "fox_w_out",
             "pool_w", "pool_b", "pool_scale", "ffn_w_gate_up", "ffn_w_down")
    return (loss, grad_x) + tuple(upd[name][q] for q in range(4) for name in order)
```

```python
import functools

import jax
import jax.numpy as jnp
from jax import lax
from jax.experimental import pallas as pl
from jax.experimental.pallas import tpu as pltpu

F32 = jnp.float32
BF16 = jnp.bfloat16
MESH = pl.DeviceIdType.MESH

N_DEV = 8
HEAD_DIM = 128
LANES = 128
POOL_WINDOWS = (2, 4, 8, 16)
POOL_HALO = 16
RMS_EPS = 1e-6
NEG_INF = -1e30
ADAM_LR = 0.001
ADAM_B1 = 0.9
ADAM_B2 = 0.999
ADAM_EPS = 1e-08
ADAM_WD = 0.01
ADAM_STEP = 10
VMEM_LIMIT = 52 * 1024 * 1024
LN2 = 0.6931471805599453
QK_FOLD = (HEAD_DIM ** -0.5 / LN2) ** 0.5
QK_UNFOLD = (HEAD_DIM ** -0.5 * LN2) ** 0.5

NN = (((1,), (0,)), ((), ()))
NT = (((1,), (1,)), ((), ()))
TN = (((0,), (0,)), ((), ()))


def _tile(n, pref, align):
    best = None
    d = align
    while d <= min(n, pref):
        if n % d == 0:
            best = d
        d += align
    return n if best is None else best


def _params(*sem):
    return pltpu.CompilerParams(dimension_semantics=sem, vmem_limit_bytes=VMEM_LIMIT)


def _position():
    x, y, c = lax.axis_index("x"), lax.axis_index("y"), lax.axis_index("c")
    return x, y, c, 4 * x + 2 * y + c


def _peer(x, y, c, k):
    px = 1 - x if k & 4 else x
    py = 1 - y if k & 2 else y
    pc = 1 - c if k & 1 else c
    return (px, py, pc), 4 * px + 2 * py + pc


def _exchange(name, ins, out_shapes, copies):
    n_in, n_cp = len(ins), len(copies)

    def body(*refs):
        in_refs = refs[:n_in]
        out_refs = refs[n_in:n_in + len(out_shapes)]
        send_sems, recv_sems, loc_sems = refs[n_in + len(out_shapes):]
        x, y, c, me = _position()
        local = []
        for ci, (ii, src_of, oi, dst_of) in enumerate(copies):
            cp = pltpu.make_async_copy(src_of(in_refs[ii], me), dst_of(out_refs[oi], me), loc_sems.at[ci])
            cp.start()
            local.append(cp)
        sends, recvs = [], []
        for k in range(1, N_DEV):
            pid, p = _peer(x, y, c, k)
            for ci, (ii, src_of, oi, dst_of) in enumerate(copies):
                sem = ci * (N_DEV - 1) + k - 1
                send = pltpu.make_async_remote_copy(
                    src_ref=src_of(in_refs[ii], p), dst_ref=dst_of(out_refs[oi], me),
                    send_sem=send_sems.at[sem], recv_sem=recv_sems.at[sem],
                    device_id=pid, device_id_type=MESH)
                send.start()
                sends.append(send)
                recvs.append(pltpu.make_async_remote_copy(
                    src_ref=src_of(in_refs[ii], p), dst_ref=dst_of(out_refs[oi], p),
                    send_sem=send_sems.at[sem], recv_sem=recv_sems.at[sem],
                    device_id=pid, device_id_type=MESH))
        for r in recvs:
            r.wait_recv()
        for s in sends:
            s.wait_send()
        for cp in local:
            cp.wait()

    any_spec = pl.BlockSpec(memory_space=pl.ANY)
    return pl.pallas_call(
        body, name=name,
        out_shape=tuple(out_shapes),
        in_specs=[any_spec] * n_in,
        out_specs=tuple([any_spec] * len(out_shapes)),
        scratch_shapes=[pltpu.SemaphoreType.DMA((n_cp * (N_DEV - 1),)),
                        pltpu.SemaphoreType.DMA((n_cp * (N_DEV - 1),)),
                        pltpu.SemaphoreType.DMA((n_cp,))],
    )(*ins)


def _exchange_start(name, ins, out_shapes, copies, dep=None):
    n_in, n_out, n_cp = len(ins), len(out_shapes), len(copies)

    def body(*refs):
        in_refs = refs[:n_in]
        land_refs = refs[n_in:n_in + n_out]
        outs = refs[n_in + n_out + (dep is not None):]
        send_sems, recv_sems = outs[:2]
        token_ref, loc_sems = outs[n_in + n_out + 2], outs[n_in + n_out + 3]
        x, y, c, me = _position()
        local = []
        for ci, (ii, src_of, oi, dst_of) in enumerate(copies):
            cp = pltpu.make_async_copy(src_of(in_refs[ii], me), dst_of(land_refs[oi], me), loc_sems.at[ci])
            cp.start()
            local.append(cp)
        for cp in local:
            cp.wait()
        for k in range(1, N_DEV):
            pid, p = _peer(x, y, c, k)
            for ci, (ii, src_of, oi, dst_of) in enumerate(copies):
                sem = ci * (N_DEV - 1) + k - 1
                pltpu.make_async_remote_copy(
                    src_ref=src_of(in_refs[ii], p), dst_ref=dst_of(land_refs[oi], me),
                    send_sem=send_sems.at[sem], recv_sem=recv_sems.at[sem],
                    device_id=pid, device_id_type=MESH).start()
        token_ref[...] = jnp.zeros_like(token_ref)

    hbm = pl.BlockSpec(memory_space=pltpu.HBM)
    sem = pl.BlockSpec(memory_space=pltpu.SEMAPHORE)
    n_sem = n_cp * (N_DEV - 1)
    lands = [pltpu.with_memory_space_constraint(lax.empty(o.shape, o.dtype), pltpu.HBM) for o in out_shapes]
    srcs = [pltpu.with_memory_space_constraint(a, pltpu.HBM) for a in ins]
    res = pl.pallas_call(
        body, name=name,
        out_shape=(pltpu.SemaphoreType.DMA((n_sem,)), pltpu.SemaphoreType.DMA((n_sem,)),
                   *[pltpu.HBM(a.shape, a.dtype) for a in ins],
                   *[pltpu.HBM(o.shape, o.dtype) for o in out_shapes],
                   jax.ShapeDtypeStruct((8, LANES), F32)),
        in_specs=[hbm] * (n_in + n_out) + ([pl.BlockSpec(memory_space=pl.ANY)] if dep is not None else []),
        out_specs=(sem, sem, *([hbm] * (n_in + n_out)), pl.BlockSpec(memory_space=pltpu.VMEM)),
        input_output_aliases={i: 2 + i for i in range(n_in + n_out)},
        scratch_shapes=[pltpu.SemaphoreType.DMA((n_cp,))],
        compiler_params=pltpu.CompilerParams(has_side_effects=pltpu.SideEffectType.DATAFLOW_SIDE_EFFECTING),
    )(*srcs, *lands, *([dep] if dep is not None else []))
    return dict(name=name, copies=copies, send=res[0], recv=res[1], srcs=list(res[2:2 + n_in]),
                lands=list(res[2 + n_in:2 + n_in + n_out]), token=res[-1])


def _exchange_wait(handle, after):
    copies, srcs, lands = handle["copies"], handle["srcs"], handle["lands"]
    n_in, n_out = len(srcs), len(lands)

    def body(*refs):
        in_refs = refs[:n_in]
        land_refs = refs[n_in:n_in + n_out]
        send_sems, recv_sems = refs[n_in + n_out:n_in + n_out + 2]
        x, y, c, me = _position()
        waits = []
        for k in range(1, N_DEV):
            pid, p = _peer(x, y, c, k)
            for ci, (ii, src_of, oi, dst_of) in enumerate(copies):
                sem = ci * (N_DEV - 1) + k - 1
                waits.append(pltpu.make_async_remote_copy(
                    src_ref=src_of(in_refs[ii], p), dst_ref=dst_of(land_refs[oi], p),
                    send_sem=send_sems.at[sem], recv_sem=recv_sems.at[sem],
                    device_id=pid, device_id_type=MESH))
        for w in waits:
            w.wait_send()
        for w in waits:
            w.wait_recv()

    hbm = pl.BlockSpec(memory_space=pltpu.HBM)
    sem = pl.BlockSpec(memory_space=pltpu.SEMAPHORE)
    res = pl.pallas_call(
        body, name=handle["name"] + "_wait",
        out_shape=tuple(pltpu.HBM(a.shape, a.dtype) for a in srcs + lands),
        in_specs=[hbm] * (n_in + n_out) + [sem, sem, pl.BlockSpec(memory_space=pl.ANY)],
        out_specs=tuple([hbm] * (n_in + n_out)),
        input_output_aliases={i: i for i in range(n_in + n_out)},
        compiler_params=pltpu.CompilerParams(has_side_effects=pltpu.SideEffectType.DATAFLOW_SIDE_EFFECTING),
    )(*srcs, *lands, handle["send"], handle["recv"], after)
    return list(res[n_in:])


def _gather_plan(stacked):
    ins, outs, copies = [], [], []
    for t in stacked:
        ii = len(ins)
        ins.append(t)
        for l in range(t.shape[0]):
            oi = len(outs)
            outs.append(jax.ShapeDtypeStruct((N_DEV,) + t.shape[1:], t.dtype))
            copies.append((ii, (lambda ref, p, l=l: ref.at[l]), oi, (lambda ref, s: ref.at[s])))
    return ins, outs, copies


def _scatter_plan(blocked):
    outs = [jax.ShapeDtypeStruct(t.shape, t.dtype) for t in blocked]
    copies = [(n, (lambda ref, p: ref.at[p]), n, (lambda ref, s: ref.at[s])) for n in range(len(blocked))]
    return list(blocked), outs, copies


def _all_gather_layers(name, stacked):
    ins, outs, copies = [], [], []
    for t in stacked:
        ii = len(ins)
        ins.append(t)
        for l in range(t.shape[0]):
            oi = len(outs)
            outs.append(jax.ShapeDtypeStruct((N_DEV,) + t.shape[1:], t.dtype))
            copies.append((ii, (lambda ref, p, l=l: ref.at[l]), oi, (lambda ref, s: ref.at[s])))
    res = _exchange(name, ins, outs, copies)
    out, pos = [], 0
    for t in stacked:
        out.append(list(res[pos:pos + t.shape[0]]))
        pos += t.shape[0]
    return out


def _mm(name, mode, a, b, out_shape, *, grid, a_spec, b_spec, o_spec, acc_shape, add=None, add_spec=None, dep=None):
    nk = grid[2]
    dn = {"nn": NN, "nt": NT, "tn": TN}[mode]
    has_add, has_dep = add is not None, dep is not None
    own_acc = nk > 1 and out_shape.dtype != F32

    def body(*refs):
        a_ref, b_ref = refs[:2]
        add_ref = refs[2] if has_add else None
        o_ref = refs[2 + has_add + has_dep]

        def product():
            return lax.dot_general(a_ref[...].astype(BF16), b_ref[...].astype(BF16), dn,
                                   preferred_element_type=F32)

        if nk == 1:
            r = product() + add_ref[...] if has_add else product()
            o_ref[...] = r.astype(o_ref.dtype)
        else:
            acc_ref = refs[-1] if own_acc else o_ref
            k = pl.program_id(2)

            @pl.when(k == 0)
            def _():
                acc_ref[...] = add_ref[...] if has_add else jnp.zeros_like(acc_ref)

            acc_ref[...] += product()
            if own_acc:
                @pl.when(k == nk - 1)
                def _():
                    o_ref[...] = acc_ref[...].astype(o_ref.dtype)

    ins = [a, b] + ([add] if has_add else []) + ([dep] if has_dep else [])
    in_specs = ([a_spec, b_spec] + ([add_spec] if has_add else [])
                + ([pl.BlockSpec(memory_space=pl.ANY)] if has_dep else []))
    scratch = [pltpu.VMEM(acc_shape, F32)] if own_acc else []
    return pl.pallas_call(
        body, name=name, grid=grid, out_shape=out_shape,
        in_specs=in_specs, out_specs=o_spec, scratch_shapes=scratch,
        compiler_params=_params("parallel", "parallel", "arbitrary"),
    )(*ins)


def _mm_nn(name, a, b, out_dtype, add=None, tm=1024, tn=1024, tk=2048):
    m, kd = a.shape
    n = b.shape[1]
    tm, tn, tk = _tile(m, tm, 16), _tile(n, tn, LANES), _tile(kd, tk, LANES)
    return _mm(name, "nn", a, b, jax.ShapeDtypeStruct((m, n), out_dtype),
               grid=(m // tm, n // tn, kd // tk),
               a_spec=pl.BlockSpec((tm, tk), lambda i, j, k: (i, k)),
               b_spec=pl.BlockSpec((tk, tn), lambda i, j, k: (k, j)),
               o_spec=pl.BlockSpec((tm, tn), lambda i, j, k: (i, j)),
               acc_shape=(tm, tn), add=add,
               add_spec=pl.BlockSpec((tm, tn), lambda i, j, k: (i, j)))


def _mm_nt(name, a, b, out_dtype, tm=1024, tn=1024, tk=2048, dep=None):
    m, kd = a.shape
    n = b.shape[0]
    tm, tn, tk = _tile(m, tm, 16), _tile(n, tn, LANES), _tile(kd, tk, LANES)
    return _mm(name, "nt", a, b, jax.ShapeDtypeStruct((m, n), out_dtype),
               grid=(m // tm, n // tn, kd // tk),
               a_spec=pl.BlockSpec((tm, tk), lambda i, j, k: (i, k)),
               b_spec=pl.BlockSpec((tn, tk), lambda i, j, k: (j, k)),
               o_spec=pl.BlockSpec((tm, tn), lambda i, j, k: (i, j)),
               acc_shape=(tm, tn), dep=dep)


def _mm_tn(name, a, b, out_dtype, tm=1024, tn=1024, ts=2048, dep=None):
    s, m = a.shape
    n = b.shape[1]
    tm, tn, ts = _tile(m, tm, LANES), _tile(n, tn, LANES), _tile(s, ts, 16)
    return _mm(name, "tn", a, b, jax.ShapeDtypeStruct((m, n), out_dtype),
               grid=(m // tm, n // tn, s // ts),
               a_spec=pl.BlockSpec((ts, tm), lambda i, j, k: (k, i)),
               b_spec=pl.BlockSpec((ts, tn), lambda i, j, k: (k, j)),
               o_spec=pl.BlockSpec((tm, tn), lambda i, j, k: (i, j)),
               acc_shape=(tm, tn), dep=dep)


def _rms_fwd(name, x, g):
    s, d = x.shape
    tm = _tile(s, 512, 16)

    def body(x_ref, g_ref, h_ref):
        xv = x_ref[...]
        r = lax.rsqrt(jnp.mean(xv * xv, axis=-1, keepdims=True) + RMS_EPS)
        h_ref[...] = ((xv * r) * g_ref[...]).astype(BF16)

    return pl.pallas_call(
        body, name=name, grid=(s // tm,), out_shape=jax.ShapeDtypeStruct((s, d), BF16),
        in_specs=[pl.BlockSpec((tm, d), lambda i: (i, 0)), pl.BlockSpec((1, d), lambda i: (0, 0))],
        out_specs=pl.BlockSpec((tm, d), lambda i: (i, 0)),
        compiler_params=_params("parallel"),
    )(x, g)


def _mm_rms_bwd(name, a, b, x, g, dres, *, tm, nk, a_spec, b_spec, dep=None):
    s, d = x.shape
    has_dep = dep is not None
    ch = _tile(tm, 128, 8)

    def body(*refs):
        a_ref, b_ref, x_ref, g_ref, dres_ref = refs[:5]
        dx_ref, dg_ref = refs[5 + has_dep], refs[6 + has_dep]
        i, k = pl.program_id(0), pl.program_id(1)

        @pl.when(k == 0)
        def _():
            dx_ref[...] = jnp.zeros_like(dx_ref)

        dx_ref[...] += lax.dot_general(a_ref[...].astype(BF16), b_ref[...].astype(BF16), NT,
                                       preferred_element_type=F32)

        @pl.when(k == nk - 1)
        def _():
            def rows_bwd(c, part):
                rows = pl.ds(pl.multiple_of(c * ch, ch), ch)
                dhv = dx_ref[rows, :]
                xv = x_ref[rows, :]
                r = lax.rsqrt(jnp.mean(xv * xv, axis=-1, keepdims=True) + RMS_EPS)
                xhat = xv * r
                gdh = dhv * g_ref[...]
                dx_ref[rows, :] = dres_ref[rows, :] + r * (gdh - xhat * jnp.mean(gdh * xhat, axis=-1, keepdims=True))
                return part + jnp.sum(dhv * xhat, axis=0, keepdims=True)

            part = lax.fori_loop(0, tm // ch, rows_bwd, jnp.zeros((1, d), F32))

            @pl.when(i == 0)
            def _():
                dg_ref[...] = part

            @pl.when(i > 0)
            def _():
                dg_ref[...] += part

    row = pl.BlockSpec((tm, d), lambda i, k: (i, 0))
    vec = pl.BlockSpec((1, d), lambda i, k: (0, 0))
    return pl.pallas_call(
        body, name=name, grid=(s // tm, nk),
        out_shape=(jax.ShapeDtypeStruct((s, d), F32), jax.ShapeDtypeStruct((1, d), F32)),
        in_specs=[a_spec, b_spec, row, vec, row] + ([pl.BlockSpec(memory_space=pl.ANY)] if has_dep else []),
        out_specs=(row, vec),
        compiler_params=_params("arbitrary", "arbitrary"),
    )(a, b, x, g, dres, *([dep] if has_dep else []))


def _split3(v):
    hi = v.astype(BF16)
    r1 = v - hi.astype(F32)
    mid = r1.astype(BF16)
    lo = (r1 - mid.astype(F32)).astype(BF16)
    return hi, mid, lo


def _tri_sum(tri, v):
    hi, mid, lo = _split3(v)
    dot = functools.partial(lax.dot_general, dimension_numbers=NN, preferred_element_type=F32)
    return dot(tri, hi) + dot(tri, mid) + dot(tri, lo)


def _gate_fwd(name, flog, b_pad):
    s = flog.shape[0]
    tb = _tile(s, 256, 16)

    def body(f_ref, b_ref, c_ref, carry_ref):
        i = pl.program_id(0)

        @pl.when(i == 0)
        def _():
            carry_ref[...] = jnp.zeros_like(carry_ref)

        z = f_ref[...] + b_ref[...]
        lf = jnp.minimum(z, 0.0) - jnp.log(1.0 + jnp.exp(-jnp.abs(z)))
        rows = lax.broadcasted_iota(jnp.int32, (tb, tb), 0)
        cols = lax.broadcasted_iota(jnp.int32, (tb, tb), 1)
        tri = (rows >= cols).astype(BF16)
        c_ref[...] = _tri_sum(tri, lf) + carry_ref[...]
        carry_ref[...] = c_ref[pl.ds(tb - 1, 1), :]

    return pl.pallas_call(
        body, name=name, grid=(s // tb,), out_shape=jax.ShapeDtypeStruct((s, LANES), F32),
        in_specs=[pl.BlockSpec((tb, LANES), lambda i: (i, 0)), pl.BlockSpec((1, LANES), lambda i: (0, 0))],
        out_specs=pl.BlockSpec((tb, LANES), lambda i: (i, 0)),
        scratch_shapes=[pltpu.VMEM((1, LANES), F32)],
        compiler_params=_params("arbitrary"),
    )(flog, b_pad)


def _gate_bwd(name, dck, dcq, flog, b_pad, n_heads):
    s = flog.shape[0]
    tb = _tile(s, 256, 16)
    nb = s // tb

    def body(dck_ref, dcq_ref, f_ref, b_ref, df_ref, db_ref, carry_ref, tmp_ref):
        i = pl.program_id(0)

        @pl.when(i == 0)
        def _():
            carry_ref[...] = jnp.zeros_like(carry_ref)

        rows = lax.broadcasted_iota(jnp.int32, (tb, tb), 0)
        cols = lax.broadcasted_iota(jnp.int32, (tb, tb), 1)
        tri = (rows <= cols).astype(BF16)
        tmp_ref[...] = _tri_sum(tri, dck_ref[...] + dcq_ref[...]) + carry_ref[...]
        carry_ref[...] = tmp_ref[pl.ds(0, 1), :]
        z = f_ref[...] + b_ref[...]
        lane = lax.broadcasted_iota(jnp.int32, (tb, LANES), 1)
        df = jnp.where(lane < n_heads, tmp_ref[...] / (1.0 + jnp.exp(z)), 0.0)
        df_ref[...] = df.astype(BF16)
        part = jnp.sum(df, axis=0, keepdims=True)

        @pl.when(i == 0)
        def _():
            db_ref[...] = part

        @pl.when(i > 0)
        def _():
            db_ref[...] += part

    rev = pl.BlockSpec((tb, LANES), lambda i: (nb - 1 - i, 0))
    vec = pl.BlockSpec((1, LANES), lambda i: (0, 0))
    return pl.pallas_call(
        body, name=name, grid=(nb,),
        out_shape=(jax.ShapeDtypeStruct((s, LANES), BF16), jax.ShapeDtypeStruct((1, LANES), F32)),
        in_specs=[rev, rev, rev, vec], out_specs=(rev, vec),
        scratch_shapes=[pltpu.VMEM((1, LANES), F32), pltpu.VMEM((tb, LANES), F32)],
        compiler_params=_params("arbitrary"),
    )(dck, dcq, flog, b_pad)


def _head_rms(v, g):
    r = lax.rsqrt(jnp.mean(v * v, axis=-1, keepdims=True) + RMS_EPS)
    return (v * r) * g


def _qkv_fwd(name, proj, gq, gk, d):
    s = proj.shape[0]
    tm = _tile(s, 256, 16)
    n_heads = d // HEAD_DIM

    def body(q_ref, k_ref, v_ref, gq_ref, gk_ref, qn_ref, kn_ref, vb_ref):
        for h in range(n_heads):
            sl = slice(h * HEAD_DIM, (h + 1) * HEAD_DIM)
            qn_ref[:, sl] = (_head_rms(q_ref[:, sl], gq_ref[...]) * QK_FOLD).astype(BF16)
            kn_ref[:, sl] = (_head_rms(k_ref[:, sl], gk_ref[...]) * QK_FOLD).astype(BF16)
        vb_ref[...] = v_ref[...].astype(BF16)

    col = lambda c: pl.BlockSpec((tm, d), lambda i, c=c: (i, c))
    vec = pl.BlockSpec((1, HEAD_DIM), lambda i: (0, 0))
    out = jax.ShapeDtypeStruct((s, d), BF16)
    return pl.pallas_call(
        body, name=name, grid=(s // tm,), out_shape=(out, out, out),
        in_specs=[col(0), col(1), col(2), vec, vec], out_specs=(col(0), col(0), col(0)),
        compiler_params=_params("parallel"),
    )(proj, proj, proj, gq, gk)


def _qkv_bwd(name, proj, dqn, dkn, dv, dflog, gq, gk, d, n_pad):
    s = proj.shape[0]
    tm = _tile(s, 256, 16)
    n_heads = d // HEAD_DIM

    def head_bwd(raw, dy, g):
        r = lax.rsqrt(jnp.mean(raw * raw, axis=-1, keepdims=True) + RMS_EPS)
        hat = raw * r
        gdy = dy * g
        dx = r * (gdy - hat * jnp.mean(gdy * hat, axis=-1, keepdims=True))
        return dx, jnp.sum(dy * hat, axis=0, keepdims=True)

    def body(q_ref, k_ref, dqn_ref, dkn_ref, dv_ref, df_ref, gq_ref, gk_ref, dp_ref, dgq_ref, dgk_ref):
        i = pl.program_id(0)
        accq = jnp.zeros((1, HEAD_DIM), F32)
        acck = jnp.zeros((1, HEAD_DIM), F32)
        for h in range(n_heads):
            sl = slice(h * HEAD_DIM, (h + 1) * HEAD_DIM)
            dq, pq = head_bwd(q_ref[:, sl], dqn_ref[:, sl], gq_ref[...])
            dk, pk = head_bwd(k_ref[:, sl], dkn_ref[:, sl], gk_ref[...])
            dp_ref[:, sl] = dq.astype(BF16)
            dp_ref[:, d + h * HEAD_DIM:d + (h + 1) * HEAD_DIM] = dk.astype(BF16)
            accq, acck = accq + pq, acck + pk
        dp_ref[:, 2 * d:3 * d] = dv_ref[...]
        dp_ref[:, 3 * d:] = df_ref[...]

        @pl.when(i == 0)
        def _():
            dgq_ref[...] = accq
            dgk_ref[...] = acck

        @pl.when(i > 0)
        def _():
            dgq_ref[...] += accq
            dgk_ref[...] += acck

    col = lambda c: pl.BlockSpec((tm, d), lambda i, c=c: (i, c))
    vec = pl.BlockSpec((1, HEAD_DIM), lambda i: (0, 0))
    return pl.pallas_call(
        body, name=name, grid=(s // tm,),
        out_shape=(jax.ShapeDtypeStruct((s, n_pad), BF16), jax.ShapeDtypeStruct((1, HEAD_DIM), F32),
                   jax.ShapeDtypeStruct((1, HEAD_DIM), F32)),
        in_specs=[col(0), col(1), col(0), col(0), col(0), pl.BlockSpec((tm, LANES), lambda i: (i, 0)), vec, vec],
        out_specs=(pl.BlockSpec((tm, n_pad), lambda i: (i, 0)), vec, vec),
        compiler_params=_params("arbitrary"),
    )(proj, proj, dqn, dkn, dv, dflog, gq, gk)


def _attn_fwd(name, qn, kn, vt, c_row, c_col):
    s, d = qn.shape
    n_heads = d // HEAD_DIM
    t = _tile(s, 512, LANES)
    scale = HEAD_DIM ** -0.5

    hp = 2 if n_heads % 2 == 0 else 1
    log2e = 1.4426950408889634

    def body(q_ref, k_ref, vt_ref, cq_ref, ck_ref, o_ref, lse_ref, m_ref, l_ref, acc_ref):
        i = pl.program_id(1)
        m_ref[...] = jnp.full(m_ref.shape, NEG_INF, F32)
        l_ref[...] = jnp.zeros_like(l_ref)
        acc_ref[...] = jnp.zeros_like(acc_ref)

        def step(j, masked):
            start = pl.multiple_of(j * t, t)
            for hh in range(hp):
                sl = slice(hh * HEAD_DIM, (hh + 1) * HEAD_DIM)
                kj = k_ref[pl.ds(start, t), sl]
                vtj = vt_ref[sl, pl.ds(start, t)]
                st = (lax.dot_general(kj, q_ref[:, sl], NT, preferred_element_type=F32)
                      - ck_ref[hh, pl.ds(start, t), :] * log2e)
                if masked:
                    rows = lax.broadcasted_iota(jnp.int32, (t, t), 0)
                    cols = lax.broadcasted_iota(jnp.int32, (t, t), 1)
                    st = jnp.where(cols >= rows, st, NEG_INF)
                m_prev = m_ref[hh]
                m_new = jnp.maximum(m_prev, jnp.max(st, axis=0, keepdims=True))
                pt = jnp.exp2(st - m_new)
                alpha = jnp.exp2(m_prev - m_new)
                l_ref[hh] = alpha * l_ref[hh] + jnp.sum(pt, axis=0, keepdims=True)
                acc_ref[hh] = alpha * acc_ref[hh] + lax.dot_general(
                    vtj, pt.astype(BF16), NN, preferred_element_type=F32)
                m_ref[hh] = m_new

        def loop_body(j, carry):
            step(j, False)
            return carry

        lax.fori_loop(0, i, loop_body, 0)
        step(i, True)
        for hh in range(hp):
            sl = slice(hh * HEAD_DIM, (hh + 1) * HEAD_DIM)
            o_ref[:, sl] = (acc_ref[hh] / l_ref[hh]).T.astype(BF16)
            lse_ref[hh] = (m_ref[hh] + jnp.log2(l_ref[hh])) * (1.0 / log2e) + cq_ref[hh]

    wide = hp * HEAD_DIM
    row_blk = pl.BlockSpec((hp, 1, t), lambda h, i: (h, 0, i))
    return pl.pallas_call(
        body, name=name, grid=(n_heads // hp, s // t),
        out_shape=(jax.ShapeDtypeStruct((s, d), BF16), jax.ShapeDtypeStruct((n_heads, 1, s), F32)),
        in_specs=[pl.BlockSpec((t, wide), lambda h, i: (i, h)),
                  pl.BlockSpec((s, wide), lambda h, i: (0, h)),
                  pl.BlockSpec((wide, s), lambda h, i: (h, 0)),
                  row_blk, pl.BlockSpec((hp, s, 1), lambda h, i: (h, 0, 0))],
        out_specs=(pl.BlockSpec((t, wide), lambda h, i: (i, h)), row_blk),
        scratch_shapes=[pltpu.VMEM((hp, 1, t), F32), pltpu.VMEM((hp, 1, t), F32),
                        pltpu.VMEM((hp, HEAD_DIM, t), F32)],
        compiler_params=_params("parallel", "arbitrary"),
    )(qn, kn, vt, c_row, c_col)


def _attn_delta(name, o, do, n_heads):
    s, d = o.shape
    tm = _tile(s, 256, 16)

    def body(o_ref, do_ref, dl_ref):
        lane = lax.broadcasted_iota(jnp.int32, (tm, LANES), 1)
        acc = jnp.zeros((tm, LANES), F32)
        for h in range(n_heads):
            sl = slice(h * HEAD_DIM, (h + 1) * HEAD_DIM)
            col = jnp.sum(o_ref[:, sl].astype(F32) * do_ref[:, sl].astype(F32), axis=-1, keepdims=True)
            acc = jnp.where(lane == h, col, acc)
        dl_ref[...] = acc

    row = pl.BlockSpec((tm, d), lambda i: (i, 0))
    return pl.pallas_call(
        body, name=name, grid=(s // tm,), out_shape=jax.ShapeDtypeStruct((s, LANES), F32),
        in_specs=[row, row], out_specs=pl.BlockSpec((tm, LANES), lambda i: (i, 0)),
        compiler_params=_params("parallel"),
    )(o, do)


def _attn_bwd(name, qn, kn, vb, do, c_row, lse_row, delta_row, c_col):
    s, d = qn.shape
    n_heads = d // HEAD_DIM
    t = _tile(s, 512, LANES)
    nq = s // t
    scale = HEAD_DIM ** -0.5

    hp = 2 if n_heads % 2 == 0 else 1

    def body(q_ref, do_ref, cr_ref, lse_ref, dl_ref, k_ref, v_ref, ck_ref,
             dq_ref, dk_ref, dv_ref, dc_ref, dcq_ref, dk_acc, dv_acc, dc_acc):
        j = pl.program_id(1)

        @pl.when(j == 0)
        def _():
            dq_ref[...] = jnp.zeros_like(dq_ref)
            dcq_ref[...] = jnp.zeros_like(dcq_ref)

        dk_acc[...] = jnp.zeros_like(dk_acc)
        dv_acc[...] = jnp.zeros_like(dv_acc)
        dc_acc[...] = jnp.zeros_like(dc_acc)

        def step(i, masked):
            start = pl.multiple_of(i * t, t)
            for hh in range(hp):
                sl = slice(hh * HEAD_DIM, (hh + 1) * HEAD_DIM)
                kj = k_ref[:, sl]
                qi = q_ref[pl.ds(start, t), sl]
                doi = do_ref[pl.ds(start, t), sl]
                bias = cr_ref[hh, :, pl.ds(start, t)] - lse_ref[hh, :, pl.ds(start, t)]
                dli = dl_ref[hh, :, pl.ds(start, t)]
                st = lax.dot_general(kj, qi, NT, preferred_element_type=F32) * LN2 + (bias - ck_ref[hh])
                if masked:
                    rows = lax.broadcasted_iota(jnp.int32, (t, t), 0)
                    cols = lax.broadcasted_iota(jnp.int32, (t, t), 1)
                    st = jnp.where(cols >= rows, st, NEG_INF)
                pt = jnp.exp(st)
                dpt = lax.dot_general(v_ref[:, sl], doi, NT, preferred_element_type=F32)
                dst = pt * (dpt - dli)
                dsb = dst.astype(BF16)
                dv_acc[:, sl] += lax.dot_general(pt.astype(BF16), doi, NN, preferred_element_type=F32)
                dk_acc[:, sl] += lax.dot_general(dsb, qi, NN, preferred_element_type=F32)
                dq_ref[pl.ds(start, t), sl] += lax.dot_general(dsb, kj, TN, preferred_element_type=F32) * QK_UNFOLD
                dc_acc[hh] += jnp.sum(dst, axis=1, keepdims=True)
                dcq_ref[hh, :, pl.ds(start, t)] += jnp.sum(dst, axis=0, keepdims=True)

        step(j, True)

        def loop_body(i, carry):
            step(i, False)
            return carry

        lax.fori_loop(j + 1, nq, loop_body, 0)
        dk_ref[...] = dk_acc[...] * QK_UNFOLD
        dv_ref[...] = dv_acc[...].astype(BF16)
        dc_ref[...] = -dc_acc[...]

    wide = hp * HEAD_DIM
    head_all = pl.BlockSpec((s, wide), lambda h, j: (0, h))
    row_all = pl.BlockSpec((hp, 1, s), lambda h, j: (h, 0, 0))
    blk = pl.BlockSpec((t, wide), lambda h, j: (j, h))
    col_blk = pl.BlockSpec((hp, t, 1), lambda h, j: (h, j, 0))
    return pl.pallas_call(
        body, name=name, grid=(n_heads // hp, nq),
        out_shape=(jax.ShapeDtypeStruct((s, d), F32), jax.ShapeDtypeStruct((s, d), F32),
                   jax.ShapeDtypeStruct((s, d), BF16), jax.ShapeDtypeStruct((n_heads, s, 1), F32),
                   jax.ShapeDtypeStruct((n_heads, 1, s), F32)),
        in_specs=[head_all, head_all, row_all, row_all, row_all, blk, blk, col_blk],
        out_specs=(head_all, blk, blk, col_blk, row_all),
        scratch_shapes=[pltpu.VMEM((t, wide), F32), pltpu.VMEM((t, wide), F32), pltpu.VMEM((hp, t, 1), F32)],
        compiler_params=_params("parallel", "arbitrary"),
    )(qn, do, c_row, lse_row, delta_row, kn, vb, c_col)


def _ffn_up(name, h, w_gu):
    s, d = h.shape
    fs = w_gu.shape[2]
    half = N_DEV // 2
    tm = _tile(s, 512, 16)

    def body(h_ref, wg_ref, wu_ref, s_ref, us_ref, a_ref):
        hv = h_ref[...]
        g = lax.dot_general(hv, wg_ref[...], NN, preferred_element_type=F32)
        u = lax.dot_general(hv, wu_ref[...], NN, preferred_element_type=F32)
        sig = jax.nn.sigmoid(g)
        silu = g * sig
        s_ref[...] = silu.astype(BF16)
        us_ref[...] = (u * (sig * (1.0 + g * (1.0 - sig)))).astype(BF16)
        a_ref[...] = (silu * u).astype(BF16)

    out = jax.ShapeDtypeStruct((s, half * fs), BF16)
    ospec = pl.BlockSpec((tm, fs), lambda j, i: (i, j))
    return pl.pallas_call(
        body, name=name, grid=(half, s // tm), out_shape=(out, out, out),
        in_specs=[pl.BlockSpec((tm, d), lambda j, i: (i, 0)),
                  pl.BlockSpec((None, d, fs), lambda j, i: (j, 0, 0)),
                  pl.BlockSpec((None, d, fs), lambda j, i: (j + half, 0, 0))],
        out_specs=(ospec, ospec, ospec),
        compiler_params=_params("parallel", "parallel"),
    )(h, w_gu, w_gu)


def _ffn_dact(name, dx, w_dn4, silu, usilu):
    s, d = dx.shape
    half, fs = w_dn4.shape[0], w_dn4.shape[1]
    tm = _tile(s, 512, 16)

    cut = (fs // (2 * LANES)) * LANES

    def body(dx_ref, w_ref, s_ref, us_ref, dgu_ref):
        dxv = dx_ref[...].astype(BF16)
        for lo, hi in ((0, cut), (cut, fs)) if cut else ((0, fs),):
            da = lax.dot_general(dxv, w_ref[lo:hi, :], NT, preferred_element_type=F32)
            dgu_ref[0, :, lo:hi] = (da * us_ref[:, lo:hi].astype(F32)).astype(BF16)
            dgu_ref[1, :, lo:hi] = (da * s_ref[:, lo:hi].astype(F32)).astype(BF16)

    blk = pl.BlockSpec((tm, fs), lambda j, i: (i, j))
    return pl.pallas_call(
        body, name=name, grid=(half, s // tm),
        out_shape=jax.ShapeDtypeStruct((2, s, half * fs), BF16),
        in_specs=[pl.BlockSpec((tm, d), lambda j, i: (i, 0)),
                  pl.BlockSpec((None, fs, d), lambda j, i: (j, 0, 0)), blk, blk],
        out_specs=pl.BlockSpec((2, tm, fs), lambda j, i: (0, i, j)),
        compiler_params=_params("parallel", "parallel"),
    )(dx, w_dn4, silu, usilu)


def _ffn_dw_gu(name, h, dgu, dep=None):
    s, d = h.shape
    half, fs = N_DEV // 2, dgu.shape[2] // (N_DEV // 2)
    tm, ts = _tile(d, 1024, LANES), _tile(s, 2048, 16)
    return _mm(name, "tn", h, dgu, jax.ShapeDtypeStruct((N_DEV, d, fs), BF16),
               grid=(d // tm, N_DEV, s // ts),
               a_spec=pl.BlockSpec((ts, tm), lambda i, j, k: (k, i)),
               b_spec=pl.BlockSpec((None, ts, fs), lambda i, j, k: (j // half, k, j % half)),
               o_spec=pl.BlockSpec((None, tm, fs), lambda i, j, k: (j, i, 0)),
               acc_shape=(tm, fs), dep=dep)


def _ffn_dh(name, dgu, w_gu, x, g, dres, dep=None):
    s = dgu.shape[1]
    d, fs = w_gu.shape[1], w_gu.shape[2]
    half = N_DEV // 2
    tm = _tile(s, 512, 16)
    return _mm_rms_bwd(name, dgu, w_gu, x, g, dres, tm=tm, nk=N_DEV,
                       a_spec=pl.BlockSpec((None, tm, fs), lambda i, k: (k // half, i, k % half)),
                       b_spec=pl.BlockSpec((None, d, fs), lambda i, k: (k, 0, 0)), dep=dep)


def _proj_in_dx(name, dproj, w_in, x, g, dres, dep=None):
    s, n = dproj.shape
    d = w_in.shape[0]
    tm, tk = _tile(s, 512, 16), _tile(n, 896, LANES)
    return _mm_rms_bwd(name, dproj, w_in, x, g, dres, tm=tm, nk=n // tk,
                       a_spec=pl.BlockSpec((tm, tk), lambda i, k: (i, k)),
                       b_spec=pl.BlockSpec((d, tk), lambda i, k: (0, k)), dep=dep)


def _pool_fwd(name, x, g, w, b, sc):
    s, d = x.shape
    dg = d // len(POOL_WINDOWS)
    tm = _tile(s, 256, POOL_HALO)
    per = tm // POOL_HALO

    def body(x_ref, xh_ref, g_ref, w_ref, b_ref, sc_ref, xo_ref, y_ref, zb_ref):
        i = pl.program_id(0)
        gv = g_ref[...]

        def norm(v):
            return (v * lax.rsqrt(jnp.mean(v * v, axis=-1, keepdims=True) + RMS_EPS)) * gv

        h = norm(x_ref[...])
        halo = norm(xh_ref[...]) * (i > 0).astype(F32)
        ext = jnp.concatenate([halo, h], axis=0)
        t = i * tm + lax.broadcasted_iota(jnp.int32, (tm, 1), 0)
        for gi, win in enumerate(POOL_WINDOWS):
            sl = slice(gi * dg, (gi + 1) * dg)
            acc = ext[:, sl]
            step = 1
            while step < win:
                acc = acc + pltpu.roll(acc, step, 0)
                step *= 2
            inv = 1.0 / jnp.minimum(t + 1, win).astype(F32)
            yg = (acc[POOL_HALO:, :] * inv - h[:, sl]).astype(BF16)
            y_ref[:, sl] = yg
            zb = lax.dot_general(yg, w_ref[gi], NN, preferred_element_type=F32) + b_ref[:, sl]
            zb_ref[:, sl] = zb
            xo_ref[:, sl] = x_ref[:, sl] + zb * sc_ref[:, sl]

    row = pl.BlockSpec((tm, d), lambda i: (i, 0))
    vec = pl.BlockSpec((1, d), lambda i: (0, 0))
    return pl.pallas_call(
        body, name=name, grid=(s // tm,),
        out_shape=(jax.ShapeDtypeStruct((s, d), F32), jax.ShapeDtypeStruct((s, d), BF16),
                   jax.ShapeDtypeStruct((s, d), F32)),
        in_specs=[row, pl.BlockSpec((POOL_HALO, d), lambda i: (jnp.maximum(i * per - 1, 0), 0)),
                  vec, pl.BlockSpec(w.shape, lambda i: (0, 0, 0)), vec, vec],
        out_specs=(row, row, row),
        compiler_params=_params("parallel"),
    )(x, x, g, w, b, sc)


def _pool_bwd(name, dout, x, zb, g, w, sc):
    s, d = x.shape
    dg = d // len(POOL_WINDOWS)
    tm = _tile(s, 256, POOL_HALO)
    per = tm // POOL_HALO
    nb = s // tm
    ext_rows = tm + POOL_HALO

    def body(do_ref, doh_ref, x_ref, zb_ref, g_ref, w_ref, sc_ref, dx_ref, dz_ref, dgn_ref, dsc_ref, db_ref):
        i = pl.program_id(0)
        scv = sc_ref[...]
        dov = do_ref[...]
        dz = dov * scv
        dz_ref[...] = dz.astype(BF16)
        halo = doh_ref[...] * scv * (i < nb - 1).astype(F32)
        ext = jnp.concatenate([dz, halo], axis=0).astype(BF16)
        t = i * tm + lax.broadcasted_iota(jnp.int32, (ext_rows, 1), 0)
        parts = []
        for gi, win in enumerate(POOL_WINDOWS):
            sl = slice(gi * dg, (gi + 1) * dg)
            dy = lax.dot_general(ext[:, sl], w_ref[gi], NT, preferred_element_type=F32)
            acc = dy * (1.0 / jnp.minimum(t + 1, win).astype(F32))
            step = 1
            while step < win:
                acc = acc + pltpu.roll(acc, ext_rows - step, 0)
                step *= 2
            parts.append(acc[:tm, :] - dy[:tm, :])
        dh = jnp.concatenate(parts, axis=1)
        xv = x_ref[...]
        r = lax.rsqrt(jnp.mean(xv * xv, axis=-1, keepdims=True) + RMS_EPS)
        xhat = xv * r
        gdh = dh * g_ref[...]
        dx_ref[...] = dov + r * (gdh - xhat * jnp.mean(gdh * xhat, axis=-1, keepdims=True))
        pgn = jnp.sum(dh * xhat, axis=0, keepdims=True)
        psc = jnp.sum(dov * zb_ref[...], axis=0, keepdims=True)
        pb = jnp.sum(dz, axis=0, keepdims=True)

        @pl.when(i == 0)
        def _():
            dgn_ref[...] = pgn
            dsc_ref[...] = psc
            db_ref[...] = pb

        @pl.when(i > 0)
        def _():
            dgn_ref[...] += pgn
            dsc_ref[...] += psc
            db_ref[...] += pb

    row = pl.BlockSpec((tm, d), lambda i: (i, 0))
    vec = pl.BlockSpec((1, d), lambda i: (0, 0))
    vshape = jax.ShapeDtypeStruct((1, d), F32)
    return pl.pallas_call(
        body, name=name, grid=(nb,),
        out_shape=(jax.ShapeDtypeStruct((s, d), F32), jax.ShapeDtypeStruct((s, d), BF16), vshape, vshape, vshape),
        in_specs=[row, pl.BlockSpec((POOL_HALO, d), lambda i: (jnp.minimum((i + 1) * per, s // POOL_HALO - 1), 0)),
                  row, row, vec, pl.BlockSpec(w.shape, lambda i: (0, 0, 0)), vec],
        out_specs=(row, row, vec, vec, vec),
        compiler_params=_params("arbitrary"),
    )(dout, dout, x, zb, g, w, sc)


def _pool_dw(name, y, dz, n_groups):
    s, d = y.shape
    dg = d // n_groups
    ts = _tile(s, 1024, 16)
    return _mm(name, "tn", y, dz, jax.ShapeDtypeStruct((n_groups, dg, dg), F32),
               grid=(n_groups, 1, s // ts),
               a_spec=pl.BlockSpec((ts, dg), lambda i, j, k: (k, i)),
               b_spec=pl.BlockSpec((ts, dg), lambda i, j, k: (k, i)),
               o_spec=pl.BlockSpec((None, dg, dg), lambda i, j, k: (i, 0, 0)),
               acc_shape=(dg, dg))


def _loss_head(name, y, tgt):
    s, d = y.shape
    tm = _tile(s, 512, 16)

    def body(y_ref, t_ref, dy_ref, l_ref):
        i = pl.program_id(0)
        e = y_ref[...] - t_ref[...]
        dy_ref[...] = e * (1.0 / d)
        part = jnp.sum(jnp.mean(e * e, axis=-1, keepdims=True), axis=0, keepdims=True)
        part = jnp.broadcast_to(part, l_ref.shape)

        @pl.when(i == 0)
        def _():
            l_ref[...] = part

        @pl.when(i > 0)
        def _():
            l_ref[...] += part

    row = pl.BlockSpec((tm, d), lambda i: (i, 0))
    return pl.pallas_call(
        body, name=name, grid=(s // tm,),
        out_shape=(jax.ShapeDtypeStruct((s, d), F32), jax.ShapeDtypeStruct((8, LANES), F32)),
        in_specs=[row, row], out_specs=(row, pl.BlockSpec((8, LANES), lambda i: (0, 0))),
        compiler_params=_params("arbitrary"),
    )(y, tgt)


def _adam_update(w_ref, m_ref, v_ref, p_ref, g_ref, d_ref, nm_ref, nv_ref):
    g = p_ref[0].astype(F32)
    for k in range(1, N_DEV):
        g = g + p_ref[k].astype(F32)
    mn = ADAM_B1 * m_ref[...] + (1.0 - ADAM_B1) * g
    vn = ADAM_B2 * v_ref[...] + (1.0 - ADAM_B2) * (g * g)
    m_hat = mn / (1.0 - ADAM_B1 ** ADAM_STEP)
    v_hat = vn / (1.0 - ADAM_B2 ** ADAM_STEP)
    g_ref[...] = g
    d_ref[...] = -ADAM_LR * (m_hat / (jnp.sqrt(v_hat) + ADAM_EPS) + ADAM_WD * w_ref[...])
    nm_ref[...] = mn
    nv_ref[...] = vn


def _adamw_layers(name, w, m, v, pieces):
    n_layers, r, c = w.shape
    tr = _tile(r, 128, 16)

    def body(w_ref, m_ref, v_ref, *rest):
        p_refs, outs = rest[:n_layers], rest[n_layers:]
        layer = pl.program_id(0)
        for l in range(n_layers):
            @pl.when(layer == l)
            def _(l=l):
                _adam_update(w_ref, m_ref, v_ref, p_refs[l], *outs)

    blk = pl.BlockSpec((None, tr, c), lambda l, i: (l, i, 0))
    terms = [pl.BlockSpec((N_DEV, tr, c), lambda l, i, n=n: (0, jnp.where(l == n, i, 0), 0))
             for n in range(n_layers)]
    out = jax.ShapeDtypeStruct(w.shape, F32)
    return list(pl.pallas_call(
        body, name=name, grid=(n_layers, r // tr), out_shape=(out, out, out, out),
        in_specs=[blk, blk, blk] + terms, out_specs=(blk, blk, blk, blk),
        compiler_params=_params("parallel", "parallel"),
    )(w, m, v, *pieces))


def _adamw(name, w, m, v, pieces):
    r, c = w.shape
    tr = _tile(r, 128, 16)

    def body(w_ref, m_ref, v_ref, p_ref, g_ref, d_ref, nm_ref, nv_ref):
        _adam_update(w_ref, m_ref, v_ref, p_ref, g_ref, d_ref, nm_ref, nv_ref)

    blk = pl.BlockSpec((tr, c), lambda i: (i, 0))
    out = jax.ShapeDtypeStruct((r, c), F32)
    return pl.pallas_call(
        body, name=name, grid=(r // tr,), out_shape=(out, out, out, out),
        in_specs=[blk, blk, blk, pl.BlockSpec((N_DEV, tr, c), lambda i: (0, i, 0))],
        out_specs=(blk, blk, blk, blk),
        compiler_params=_params("parallel"),
    )(w, m, v, pieces)


def _pack_small(mix, ffn, b_f, gq, gk):
    def rows(a):
        a = a.reshape(-1, LANES) if a.shape[-1] >= LANES else jnp.pad(a, ((0, 0), (0, LANES - a.shape[-1])))
        return jnp.pad(a, ((0, -a.shape[0] % 8), (0, 0)))
    return jnp.concatenate([rows(mix), rows(ffn), rows(b_f), rows(gq), rows(gk)], axis=0)


def _unpack_small(p, mix, ffn, b_f, gq, gk):
    out, pos = [], 0
    for a in (mix, ffn, b_f, gq, gk):
        n = a.size // LANES if a.shape[-1] >= LANES else a.shape[0]
        blk = p[pos:pos + n]
        out.append(blk.reshape(a.shape) if a.shape[-1] >= LANES else blk[:, :a.shape[-1]])
        pos += n + (-n % 8)
    return out


def kernel(x, mix_norm_g, ffn_norm_g, fox_w_in, fox_b_f, fox_q_norm_g, fox_k_norm_g, fox_w_out, pool_w, pool_b, pool_scale, ffn_w_gate_up, ffn_w_down, loss_target, m_mix_norm_g, m_ffn_norm_g, m_fox_w_in, m_fox_b_f, m_fox_q_norm_g, m_fox_k_norm_g, m_fox_w_out, m_pool_w, m_pool_b, m_pool_scale, m_ffn_w_gate_up, m_ffn_w_down, v_mix_norm_g, v_ffn_norm_g, v_fox_w_in, v_fox_b_f, v_fox_q_norm_g, v_fox_k_norm_g, v_fox_w_out, v_pool_w, v_pool_b, v_pool_scale, v_ffn_w_gate_up, v_ffn_w_down):
    xs, tgt = x[0], loss_target[0]
    s, d = xs.shape
    depth = mix_norm_g.shape[0]
    n_fox, n_pool = fox_w_in.shape[0], pool_w.shape[0]
    n_heads = d // HEAD_DIM
    n_in = fox_w_in.shape[2] * N_DEV
    n_pad = 3 * d + LANES
    n_groups = pool_w.shape[1]
    dsh = d // N_DEV
    half = N_DEV // 2
    axes = ("x", "y", "c")

    w_in_bf, w_out_bf, pool_w_bf = fox_w_in.astype(BF16), fox_w_out.astype(BF16), pool_w.astype(BF16)
    gu_bf, dn_bf = ffn_w_gate_up.astype(BF16), ffn_w_down.astype(BF16)
    pool_bs = jnp.stack([pool_b, pool_scale], axis=1)
    mix_gather, ffn_gather = [None] * depth, [None] * depth
    last = None
    for l in range(depth):
        j = l // 2
        shards = [w_in_bf[j:j + 1], w_out_bf[j:j + 1]] if l % 2 == 0 else [pool_w_bf[j:j + 1], pool_bs[j:j + 1]]
        mix_gather[l] = _exchange_start(f"gather_mixer{l}", *_gather_plan(shards), dep=last)
        ffn_gather[l] = _exchange_start(f"gather_ffn{l}", *_gather_plan([gu_bf[l:l + 1], dn_bf[l:l + 1]]),
                                        dep=mix_gather[l]["token"])
        last = ffn_gather[l]["token"]
    started = last[:1, :1]
    w_gu_g, w_dn = [None] * depth, [None] * depth
    w_in, w_out = [None] * n_fox, [None] * n_fox
    w_pool, pool_b_full, pool_s_full = [None] * n_pool, [None] * n_pool, [None] * n_pool
    b_pad =[jnp.pad(fox_b_f[j], (0, LANES - n_heads))[None] for j in range(n_fox)]

    saved = []
    cur = xs
    for i in range(depth):
        j = i // 2
        gm = mix_norm_g[i][None]
        if i == 0:
            gm = gm + started
        if i % 2 == 0:
            w_in_g, w_out_g = _exchange_wait(mix_gather[i], last if i == 0 else cur)
            w_in[j] = jnp.pad(jnp.transpose(w_in_g, (1, 0, 2)).reshape(d, n_in), ((0, 0), (0, n_pad - n_in)))
            w_out[j] = w_out_g.reshape(d, d)
            h = _rms_fwd(f"norm_mix{i}", cur, gm)
            proj =_mm_nn(f"proj_in{i}", h, w_in[j], F32, tn=896)
            gq, gk = fox_q_norm_g[j][None], fox_k_norm_g[j][None]
            qn, kn, vb = _qkv_fwd(f"qk_norm{i}", proj, gq, gk, d)
            flog = proj[:, 3 * d:]
            c = _gate_fwd(f"gate{i}", flog, b_pad[j])
            c_t = c[:, :n_heads].T
            c_col, c_row = c_t[:, :, None], c_t[:, None, :]
            o, lse = _attn_fwd(f"attn{i}", qn, kn, vb.T, c_row, c_col)
            mid = _mm_nn(f"proj_out{i}", o, w_out[j], F32, add=cur)
            mix_saved = (cur, h, proj, flog, qn, kn, vb, c_col, c_row, o, lse)
        else:
            pw_g, pbs_g = _exchange_wait(mix_gather[i], cur)
            w_pool[j] = jnp.transpose(pw_g, (1, 0, 2, 3)).reshape(n_groups, d // n_groups, d // n_groups)
            pbs_full = jnp.transpose(pbs_g, (1, 0, 2)).reshape(2, 1, d)
            pool_b_full[j], pool_s_full[j] = pbs_full[0], pbs_full[1]
            mid, y, zb = _pool_fwd(f"pool{i}", cur, gm, w_pool[j], pool_b_full[j], pool_s_full[j])
            mix_saved = (cur, y, zb)
        h2 = _rms_fwd(f"norm_ffn{i}", mid, ffn_norm_g[i][None])
        w_gu_g[i], dn_g = _exchange_wait(ffn_gather[i], h2)
        w_dn[i] = dn_g.reshape(-1, d)
        silu, usilu, act = _ffn_up(f"ffn_up{i}", h2, w_gu_g[i])
        nxt = _mm_nn(f"ffn_down{i}", act, w_dn[i], F32, add=mid, tk=2816)
        saved.append((mix_saved, mid, h2, silu, usilu, act))
        cur = nxt

    dcur, lpart = _loss_head("loss_head", cur, tgt)
    loss = lax.psum(0.5 * lpart[0, 0], axes)

    d_mix, d_ffn = [None] * depth, [None] * depth
    d_bf, d_gq, d_gk = [None] * n_fox, [None] * n_fox, [None] * n_fox
    mix_scatter, ffn_scatter = [None] * depth, [None] * depth
    pending = jnp.zeros((1, 1), F32)
    for i in reversed(range(depth)):
        j = i // 2
        mix_saved, mid, h2, silu, usilu, act = saved[i]
        dgu = _ffn_dact(f"ffn_dact{i}", dcur, w_dn[i].reshape(half, -1, d), silu, usilu)
        g_dn = _mm_tn(f"ffn_dw_down{i}", act, dcur, BF16, tm=1408).reshape(N_DEV, -1, d)
        sc_dn = _exchange_start(f"scatter_down{i}", *_scatter_plan([g_dn]))
        g_gu = _ffn_dw_gu(f"ffn_dw_up{i}", h2, dgu, dep=sc_dn["token"])
        sc_gu = _exchange_start(f"scatter_up{i}", *_scatter_plan([g_gu]))
        ffn_scatter[i] = (sc_gu, sc_dn)
        g_ffn = ffn_norm_g[i][None] + pending
        dmid, d_ffn[i] = _ffn_dh(f"ffn_dh{i}", dgu, w_gu_g[i], mid, g_ffn, dcur, dep=sc_gu["token"])
        gm = mix_norm_g[i][None]
        if i % 2 == 0:
            xin, h, proj, flog, qn, kn, vb, c_col, c_row, o, lse = mix_saved
            g_out = _mm_tn(f"proj_out_dw{i}", o, dmid, BF16).reshape(N_DEV, dsh, d)
            sc_out = _exchange_start(f"scatter_out{i}", *_scatter_plan([g_out]))
            do = _mm_nt(f"proj_out_dx{i}", dmid, w_out[j], BF16, dep=sc_out["token"])
            delta = _attn_delta(f"attn_delta{i}", o, do, n_heads)
            delta_row = delta[:, :n_heads].T[:, None, :]
            dqn, dkn, dv, dck, dcq = _attn_bwd(f"attn_bwd{i}", qn, kn, vb, do, c_row,
                                               lse, delta_row, c_col)
            lane_pad = ((0, 0), (0, LANES - n_heads))
            dflog, d_bf[j] = _gate_bwd(f"gate_bwd{i}", jnp.pad(dck[:, :, 0].T, lane_pad),
                                       jnp.pad(dcq[:, 0, :].T, lane_pad), flog, b_pad[j], n_heads)
            gq, gk = fox_q_norm_g[j][None], fox_k_norm_g[j][None]
            dproj, d_gq[j], d_gk[j] = _qkv_bwd(f"qk_norm_bwd{i}", proj, dqn, dkn, dv, dflog, gq, gk, d, n_pad)
            dw_in = _mm_tn(f"proj_in_dw{i}", h, dproj, BF16, tn=896)
            g_in = jnp.transpose(dw_in[:, :n_in].reshape(d, N_DEV, n_in // N_DEV), (1, 0, 2))
            sc_in = _exchange_start(f"scatter_in{i}", *_scatter_plan([g_in]))
            mix_scatter[i] = (sc_in, sc_out)
            dcur, d_mix[i] = _proj_in_dx(f"proj_in_dx{i}", dproj, w_in[j], xin, gm, dmid, dep=sc_in["token"])
        else:
            xin, y, zb = mix_saved
            dcur, dz, d_mix[i], dsc, db = _pool_bwd(f"pool_bwd{i}", dmid, xin, zb, gm, w_pool[j], pool_s_full[j])
            dwp = _pool_dw(f"pool_dw{i}", y, dz, n_groups)
            dg = d // n_groups
            g_pw = jnp.transpose(dwp.reshape(n_groups, N_DEV, dg // N_DEV, dg), (1, 0, 2, 3)).astype(BF16)
            g_pbs = jnp.stack([db.reshape(N_DEV, dsh), dsc.reshape(N_DEV, dsh)], axis=1)
            sc_pool = _exchange_start(f"scatter_pool{i}", *_scatter_plan([g_pw, g_pbs]))
            mix_scatter[i] = (sc_pool,)
            pending = sc_pool["token"][:1, :1]
    grad_x = dcur[None]

    mix_landed = [sum((_exchange_wait(hd, dcur) for hd in mix_scatter[l]), []) for l in range(depth)]
    landed = [sum((_exchange_wait(hd, dcur) for hd in ffn_scatter[l]), []) for l in range(depth)]
    r_in, r_out = [t[0] for t in mix_landed[0::2]], [t[1] for t in mix_landed[0::2]]
    r_pw = [t[0].reshape(N_DEV, -1, t[0].shape[-1]) for t in mix_landed[1::2]]
    r_pbs = [t[1] for t in mix_landed[1::2]]
    r_gu, r_dn = [t[0] for t in landed], [t[1] for t in landed]
    upd = {}
    upd["fox_w_in"] = _adamw_layers("adamw_w_in", fox_w_in, m_fox_w_in, v_fox_w_in, r_in)
    upd["fox_w_out"] = _adamw_layers("adamw_w_out", fox_w_out, m_fox_w_out, v_fox_w_out, r_out)
    fold = lambda a: a.reshape(n_pool, -1, a.shape[-1])
    upd["pool_w"] = [o.reshape(pool_w.shape) for o in
                     _adamw_layers("adamw_pool_w", fold(pool_w), fold(m_pool_w), fold(v_pool_w), r_pw)]
    pbs = _adamw_layers("adamw_pool_bs", pool_bs, jnp.stack([m_pool_b, m_pool_scale], axis=1),
                        jnp.stack([v_pool_b, v_pool_scale], axis=1), r_pbs)
    upd["pool_b"] = [o[:, 0] for o in pbs]
    upd["pool_scale"] = [o[:, 1] for o in pbs]
    upd["ffn_w_gate_up"] = _adamw_layers("adamw_gate_up", ffn_w_gate_up, m_ffn_w_gate_up, v_ffn_w_gate_up, r_gu)
    upd["ffn_w_down"] = _adamw_layers("adamw_down", ffn_w_down, m_ffn_w_down, v_ffn_w_down, r_dn)

    small_w = (mix_norm_g, ffn_norm_g, fox_b_f, fox_q_norm_g, fox_k_norm_g)
    small_g = _pack_small(jnp.concatenate(d_mix), jnp.concatenate(d_ffn),
                          jnp.concatenate(d_bf)[:, :n_heads], jnp.concatenate(d_gq), jnp.concatenate(d_gk))
    (small_pieces,), = _all_gather_layers("gather_small", [small_g[None]])
    small = _adamw("adamw_small", _pack_small(*small_w),
                   _pack_small(m_mix_norm_g, m_ffn_norm_g, m_fox_b_f, m_fox_q_norm_g, m_fox_k_norm_g),
                   _pack_small(v_mix_norm_g, v_ffn_norm_g, v_fox_b_f, v_fox_q_norm_g, v_fox_k_norm_g),
                   small_pieces)
    small = [_unpack_small(o, *small_w) for o in small]
    for n, name in enumerate(("mix_norm_g", "ffn_norm_g", "fox_b_f", "fox_q_norm_g", "fox_k_norm_g")):
        upd[name] = [o[n] for o in small]

    order = ("mix_norm_g", "ffn_norm_g", "fox_w_in", "fox_b_f", "fox_q_norm_g", "fox_k_norm_g", "fox_w_out",
             "pool_w", "pool_b", "pool_scale", "ffn_w_gate_up", "ffn_w_down")
    return (loss, grad_x) + tuple(upd[name][q] for q in range(4) for name in order)
```
